```python
import jax, jax.numpy as jnp
from jax import lax
import numpy as np

D_MODEL = 1024
BATCH = 8
SEQ = 8192
DEPTH = 2

CHUNK = 64
Q_BLOCK = 128
CONV_WIDTH = 31
N_HEADS = 16
HEAD_DIM = D_MODEL // N_HEADS
ATTN_WIDTH = N_HEADS * HEAD_DIM
D_FF = -(-(8 * D_MODEL) // (3 * 256)) * 256
N_A_LAYERS = DEPTH // 2
N_B_LAYERS = DEPTH - N_A_LAYERS
EPS = 1e-6
FORGET_BIAS_MEAN = 2.0

kernel_name = "yoco_conformer_fox_adaln_trunk"


def _rmsnorm(x, g):
    x32 = x.astype(jnp.float32)
    y = x32 * lax.rsqrt(jnp.mean(x32 * x32, axis=-1, keepdims=True) + EPS)
    return (y * g.astype(jnp.float32)).astype(x.dtype)


def _layernorm(x, g, b):
    x32 = x.astype(jnp.float32)
    mu = jnp.mean(x32, axis=-1, keepdims=True)
    var = jnp.mean(jnp.square(x32 - mu), axis=-1, keepdims=True)
    y = (x32 - mu) * lax.rsqrt(var + EPS)
    return (y * g.astype(jnp.float32) + b.astype(jnp.float32)).astype(x.dtype)


def _modulate(h, shift, scale):
    return h * (1.0 + scale[:, None, :]) + shift[:, None, :]


def _conformer_conv(h, w_in, b_in, w_dw, b_dw, ln_g, ln_b, w_out, b_out):
    u = h @ w_in + b_in
    a, g = jnp.split(u, 2, axis=-1)
    u = a * jax.nn.sigmoid(g)
    d = u.shape[-1]
    u = lax.conv_general_dilated(
        u, w_dw[:, None, :].astype(u.dtype),
        window_strides=(1,), padding=[(CONV_WIDTH - 1, 0)],
        dimension_numbers=("NWC", "WIO", "NWC"),
        feature_group_count=d) + b_dw
    u = jax.nn.silu(_layernorm(u, ln_g, ln_b))
    return u @ w_out + b_out


def _forgetting_attention(q, k, v, cum):
    b, s, h, hd = q.shape
    n_blk = s // Q_BLOCK
    scale = hd ** -0.5
    qb = q.reshape(b, n_blk, Q_BLOCK, h, hd).transpose(1, 0, 2, 3, 4)
    cum_h = cum.transpose(0, 2, 1)
    cqb = cum_h.reshape(b, h, n_blk, Q_BLOCK).transpose(2, 0, 1, 3)
    key_pos = jnp.arange(s)

    def one_block(args):
        q_i, cq_i, i = args
        logits = jnp.einsum("bqhd,bkhd->bhqk", q_i, k).astype(jnp.float32) * scale
        logits = logits + (cq_i[..., :, None] - cum_h[:, :, None, :])
        q_pos = i * Q_BLOCK + jnp.arange(Q_BLOCK)
        mask = key_pos[None, :] <= q_pos[:, None]
        logits = jnp.where(mask[None, None], logits, -jnp.inf)
        p = jax.nn.softmax(logits, axis=-1)
        return jnp.einsum("bhqk,bkhd->bqhd", p.astype(v.dtype), v)

    out = lax.map(one_block, (qb, cqb, jnp.arange(n_blk)))
    return out.transpose(1, 0, 2, 3, 4).reshape(b, s, h, hd)


def _fwd_setup_inputs(seed: int = 0) -> dict:
    key = jax.random.key(seed)
    ks = iter(jax.random.split(key, 32))
    D, F, K, H = D_MODEL, D_FF, CONV_WIDTH, N_HEADS

    def nrm(shape, std):
        return jax.random.normal(next(ks), shape, jnp.float32) * std

    def gain(shape):
        return 1.0 + nrm(shape, 0.02)

    return {
        "x": nrm((BATCH, SEQ, D), 1.0),
        "c": nrm((BATCH, D), 1.0),
        "mix_norm_g": gain((DEPTH, D)),
        "mix_ada_w": nrm((DEPTH, D, 3 * D), 0.5 * D ** -0.5),
        "mix_ada_b": nrm((DEPTH, 3 * D), 0.02),
        "ffn_norm_g": gain((DEPTH, D)),
        "ffn_ada_w": nrm((DEPTH, D, 3 * D), 0.5 * D ** -0.5),
        "ffn_ada_b": nrm((DEPTH, 3 * D), 0.02),
        "ffn_w_in": nrm((DEPTH, D, 2 * F), D ** -0.5),
        "ffn_w_out": nrm((DEPTH, F, D), F ** -0.5),
        "conv_w_in": nrm((N_A_LAYERS, D, 2 * D), D ** -0.5),
        "conv_b_in": nrm((N_A_LAYERS, 2 * D), 0.02),
        "conv_w_dw": nrm((N_A_LAYERS, K, D), K ** -0.5),
        "conv_b_dw": nrm((N_A_LAYERS, D), 0.02),
        "conv_ln_g": gain((N_A_LAYERS, D)),
        "conv_ln_b": nrm((N_A_LAYERS, D), 0.02),
        "conv_w_out": nrm((N_A_LAYERS, D, D), D ** -0.5),
        "conv_b_out": nrm((N_A_LAYERS, D), 0.02),
        "kv_norm_g": gain((D,)),
        "kv_ada_w": nrm((D, 2 * D), 0.5 * D ** -0.5),
        "kv_ada_b": nrm((2 * D,), 0.02),
        "kv_w": nrm((D, 2 * ATTN_WIDTH + H), D ** -0.5),
        "forget_b": FORGET_BIAS_MEAN + nrm((H,), 0.1),
        "attn_w_q": nrm((N_B_LAYERS, D, ATTN_WIDTH), D ** -0.5),
        "attn_w_o": nrm((N_B_LAYERS, ATTN_WIDTH, D), ATTN_WIDTH ** -0.5),
        "final_norm_g": gain((D,)),
    }


def _fwd_reference(x, c, mix_norm_g, mix_ada_w, mix_ada_b, ffn_norm_g, ffn_ada_w, ffn_ada_b,
              ffn_w_in, ffn_w_out, conv_w_in, conv_b_in, conv_w_dw, conv_b_dw, conv_ln_g,
              conv_ln_b, conv_w_out, conv_b_out, kv_norm_g, kv_ada_w, kv_ada_b, kv_w,
              forget_b, attn_w_q, attn_w_o, final_norm_g):
    b, s, _ = x.shape
    c_act = jax.nn.silu(c)

    def ada(w, bias, n):
        return jnp.split(c_act @ w + bias, n, axis=-1)

    k_sh = v_sh = cum_sh = None
    for layer in range(DEPTH):
        shift, scale, gate = ada(mix_ada_w[layer], mix_ada_b[layer], 3)
        h = _modulate(_rmsnorm(x, mix_norm_g[layer]), shift, scale)
        if layer < N_A_LAYERS:
            i = layer
            y = _conformer_conv(h, conv_w_in[i], conv_b_in[i], conv_w_dw[i], conv_b_dw[i],
                                conv_ln_g[i], conv_ln_b[i], conv_w_out[i], conv_b_out[i])
        else:
            j = layer - N_A_LAYERS
            q = (h @ attn_w_q[j]).reshape(b, s, N_HEADS, HEAD_DIM)
            o = _forgetting_attention(q, k_sh, v_sh, cum_sh)
            y = o.reshape(b, s, ATTN_WIDTH) @ attn_w_o[j]
        x = x + gate[:, None, :] * y

        shift, scale, gate = ada(ffn_ada_w[layer], ffn_ada_b[layer], 3)
        h = _modulate(_rmsnorm(x, ffn_norm_g[layer]), shift, scale)
        u_gate, u_up = jnp.split(h @ ffn_w_in[layer], 2, axis=-1)
        x = x + gate[:, None, :] * ((jax.nn.silu(u_gate) * u_up) @ ffn_w_out[layer])

        if layer == N_A_LAYERS - 1:
            shift, scale = ada(kv_ada_w, kv_ada_b, 2)
            hk = _modulate(_rmsnorm(x, kv_norm_g), shift, scale)
            kvf = hk @ kv_w
            k_sh = kvf[..., :ATTN_WIDTH].reshape(b, s, N_HEADS, HEAD_DIM)
            v_sh = kvf[..., ATTN_WIDTH:2 * ATTN_WIDTH].reshape(b, s, N_HEADS, HEAD_DIM)
            f_logit = (kvf[..., 2 * ATTN_WIDTH:] + forget_b).astype(jnp.float32)
            cum_sh = jnp.cumsum(jax.nn.log_sigmoid(f_logit), axis=1)

    return _rmsnorm(x, final_norm_g)


import jax as _jax
import jax.numpy as _jnp

TWIN_FORMAT = 'train_step'
FWD_PARAMS = ['x', 'c', 'mix_norm_g', 'mix_ada_w', 'mix_ada_b', 'ffn_norm_g', 'ffn_ada_w', 'ffn_ada_b', 'ffn_w_in', 'ffn_w_out', 'conv_w_in', 'conv_b_in', 'conv_w_dw', 'conv_b_dw', 'conv_ln_g', 'conv_ln_b', 'conv_w_out', 'conv_b_out', 'kv_norm_g', 'kv_ada_w', 'kv_ada_b', 'kv_w', 'forget_b', 'attn_w_q', 'attn_w_o', 'final_norm_g']
TWIN_WEIGHTS = ['mix_norm_g', 'mix_ada_w', 'mix_ada_b', 'ffn_norm_g', 'ffn_ada_w', 'ffn_ada_b', 'ffn_w_in', 'ffn_w_out', 'conv_w_in', 'conv_b_in', 'conv_w_dw', 'conv_b_dw', 'conv_ln_g', 'conv_ln_b', 'conv_w_out', 'conv_b_out', 'kv_norm_g', 'kv_ada_w', 'kv_ada_b', 'kv_w', 'forget_b', 'attn_w_q', 'attn_w_o', 'final_norm_g']
TWIN_DIFF_INPUT = 'x'
TWIN_INPUTS = ['x', 'c', 'mix_norm_g', 'mix_ada_w', 'mix_ada_b', 'ffn_norm_g', 'ffn_ada_w', 'ffn_ada_b', 'ffn_w_in', 'ffn_w_out', 'conv_w_in', 'conv_b_in', 'conv_w_dw', 'conv_b_dw', 'conv_ln_g', 'conv_ln_b', 'conv_w_out', 'conv_b_out', 'kv_norm_g', 'kv_ada_w', 'kv_ada_b', 'kv_w', 'forget_b', 'attn_w_q', 'attn_w_o', 'final_norm_g', 'loss_target', 'm_mix_norm_g', 'm_mix_ada_w', 'm_mix_ada_b', 'm_ffn_norm_g', 'm_ffn_ada_w', 'm_ffn_ada_b', 'm_ffn_w_in', 'm_ffn_w_out', 'm_conv_w_in', 'm_conv_b_in', 'm_conv_w_dw', 'm_conv_b_dw', 'm_conv_ln_g', 'm_conv_ln_b', 'm_conv_w_out', 'm_conv_b_out', 'm_kv_norm_g', 'm_kv_ada_w', 'm_kv_ada_b', 'm_kv_w', 'm_forget_b', 'm_attn_w_q', 'm_attn_w_o', 'm_final_norm_g', 'v_mix_norm_g', 'v_mix_ada_w', 'v_mix_ada_b', 'v_ffn_norm_g', 'v_ffn_ada_w', 'v_ffn_ada_b', 'v_ffn_w_in', 'v_ffn_w_out', 'v_conv_w_in', 'v_conv_b_in', 'v_conv_w_dw', 'v_conv_b_dw', 'v_conv_ln_g', 'v_conv_ln_b', 'v_conv_w_out', 'v_conv_b_out', 'v_kv_norm_g', 'v_kv_ada_w', 'v_kv_ada_b', 'v_kv_w', 'v_forget_b', 'v_attn_w_q', 'v_attn_w_o', 'v_final_norm_g']
TWIN_OUTPUTS = ['loss', 'grad_x', 'grad_mix_norm_g', 'grad_mix_ada_w', 'grad_mix_ada_b', 'grad_ffn_norm_g', 'grad_ffn_ada_w', 'grad_ffn_ada_b', 'grad_ffn_w_in', 'grad_ffn_w_out', 'grad_conv_w_in', 'grad_conv_b_in', 'grad_conv_w_dw', 'grad_conv_b_dw', 'grad_conv_ln_g', 'grad_conv_ln_b', 'grad_conv_w_out', 'grad_conv_b_out', 'grad_kv_norm_g', 'grad_kv_ada_w', 'grad_kv_ada_b', 'grad_kv_w', 'grad_forget_b', 'grad_attn_w_q', 'grad_attn_w_o', 'grad_final_norm_g', 'delta_mix_norm_g', 'delta_mix_ada_w', 'delta_mix_ada_b', 'delta_ffn_norm_g', 'delta_ffn_ada_w', 'delta_ffn_ada_b', 'delta_ffn_w_in', 'delta_ffn_w_out', 'delta_conv_w_in', 'delta_conv_b_in', 'delta_conv_w_dw', 'delta_conv_b_dw', 'delta_conv_ln_g', 'delta_conv_ln_b', 'delta_conv_w_out', 'delta_conv_b_out', 'delta_kv_norm_g', 'delta_kv_ada_w', 'delta_kv_ada_b', 'delta_kv_w', 'delta_forget_b', 'delta_attn_w_q', 'delta_attn_w_o', 'delta_final_norm_g', 'new_m_mix_norm_g', 'new_m_mix_ada_w', 'new_m_mix_ada_b', 'new_m_ffn_norm_g', 'new_m_ffn_ada_w', 'new_m_ffn_ada_b', 'new_m_ffn_w_in', 'new_m_ffn_w_out', 'new_m_conv_w_in', 'new_m_conv_b_in', 'new_m_conv_w_dw', 'new_m_conv_b_dw', 'new_m_conv_ln_g', 'new_m_conv_ln_b', 'new_m_conv_w_out', 'new_m_conv_b_out', 'new_m_kv_norm_g', 'new_m_kv_ada_w', 'new_m_kv_ada_b', 'new_m_kv_w', 'new_m_forget_b', 'new_m_attn_w_q', 'new_m_attn_w_o', 'new_m_final_norm_g', 'new_v_mix_norm_g', 'new_v_mix_ada_w', 'new_v_mix_ada_b', 'new_v_ffn_norm_g', 'new_v_ffn_ada_w', 'new_v_ffn_ada_b', 'new_v_ffn_w_in', 'new_v_ffn_w_out', 'new_v_conv_w_in', 'new_v_conv_b_in', 'new_v_conv_w_dw', 'new_v_conv_b_dw', 'new_v_conv_ln_g', 'new_v_conv_ln_b', 'new_v_conv_w_out', 'new_v_conv_b_out', 'new_v_kv_norm_g', 'new_v_kv_ada_w', 'new_v_kv_ada_b', 'new_v_kv_w', 'new_v_forget_b', 'new_v_attn_w_q', 'new_v_attn_w_o', 'new_v_final_norm_g']
TWIN_LEAF_KINDS = {'loss': 'loss', 'grad_x': 'grad_x', 'grad_mix_norm_g': 'grad_w', 'grad_mix_ada_w': 'grad_w', 'grad_mix_ada_b': 'grad_w', 'grad_ffn_norm_g': 'grad_w', 'grad_ffn_ada_w': 'grad_w', 'grad_ffn_ada_b': 'grad_w', 'grad_ffn_w_in': 'grad_w', 'grad_ffn_w_out': 'grad_w', 'grad_conv_w_in': 'grad_w', 'grad_conv_b_in': 'grad_w', 'grad_conv_w_dw': 'grad_w', 'grad_conv_b_dw': 'grad_w', 'grad_conv_ln_g': 'grad_w', 'grad_conv_ln_b': 'grad_w', 'grad_conv_w_out': 'grad_w', 'grad_conv_b_out': 'grad_w', 'grad_kv_norm_g': 'grad_w', 'grad_kv_ada_w': 'grad_w', 'grad_kv_ada_b': 'grad_w', 'grad_kv_w': 'grad_w', 'grad_forget_b': 'grad_w', 'grad_attn_w_q': 'grad_w', 'grad_attn_w_o': 'grad_w', 'grad_final_norm_g': 'grad_w', 'delta_mix_norm_g': 'delta_w', 'delta_mix_ada_w': 'delta_w', 'delta_mix_ada_b': 'delta_w', 'delta_ffn_norm_g': 'delta_w', 'delta_ffn_ada_w': 'delta_w', 'delta_ffn_ada_b': 'delta_w', 'delta_ffn_w_in': 'delta_w', 'delta_ffn_w_out': 'delta_w', 'delta_conv_w_in': 'delta_w', 'delta_conv_b_in': 'delta_w', 'delta_conv_w_dw': 'delta_w', 'delta_conv_b_dw': 'delta_w', 'delta_conv_ln_g': 'delta_w', 'delta_conv_ln_b': 'delta_w', 'delta_conv_w_out': 'delta_w', 'delta_conv_b_out': 'delta_w', 'delta_kv_norm_g': 'delta_w', 'delta_kv_ada_w': 'delta_w', 'delta_kv_ada_b': 'delta_w', 'delta_kv_w': 'delta_w', 'delta_forget_b': 'delta_w', 'delta_attn_w_q': 'delta_w', 'delta_attn_w_o': 'delta_w', 'delta_final_norm_g': 'delta_w', 'new_m_mix_norm_g': 'new_m', 'new_m_mix_ada_w': 'new_m', 'new_m_mix_ada_b': 'new_m', 'new_m_ffn_norm_g': 'new_m', 'new_m_ffn_ada_w': 'new_m', 'new_m_ffn_ada_b': 'new_m', 'new_m_ffn_w_in': 'new_m', 'new_m_ffn_w_out': 'new_m', 'new_m_conv_w_in': 'new_m', 'new_m_conv_b_in': 'new_m', 'new_m_conv_w_dw': 'new_m', 'new_m_conv_b_dw': 'new_m', 'new_m_conv_ln_g': 'new_m', 'new_m_conv_ln_b': 'new_m', 'new_m_conv_w_out': 'new_m', 'new_m_conv_b_out': 'new_m', 'new_m_kv_norm_g': 'new_m', 'new_m_kv_ada_w': 'new_m', 'new_m_kv_ada_b': 'new_m', 'new_m_kv_w': 'new_m', 'new_m_forget_b': 'new_m', 'new_m_attn_w_q': 'new_m', 'new_m_attn_w_o': 'new_m', 'new_m_final_norm_g': 'new_m', 'new_v_mix_norm_g': 'new_v', 'new_v_mix_ada_w': 'new_v', 'new_v_mix_ada_b': 'new_v', 'new_v_ffn_norm_g': 'new_v', 'new_v_ffn_ada_w': 'new_v', 'new_v_ffn_ada_b': 'new_v', 'new_v_ffn_w_in': 'new_v', 'new_v_ffn_w_out': 'new_v', 'new_v_conv_w_in': 'new_v', 'new_v_conv_b_in': 'new_v', 'new_v_conv_w_dw': 'new_v', 'new_v_conv_b_dw': 'new_v', 'new_v_conv_ln_g': 'new_v', 'new_v_conv_ln_b': 'new_v', 'new_v_conv_w_out': 'new_v', 'new_v_conv_b_out': 'new_v', 'new_v_kv_norm_g': 'new_v', 'new_v_kv_ada_w': 'new_v', 'new_v_kv_ada_b': 'new_v', 'new_v_kv_w': 'new_v', 'new_v_forget_b': 'new_v', 'new_v_attn_w_q': 'new_v', 'new_v_attn_w_o': 'new_v', 'new_v_final_norm_g': 'new_v'}


def _forward(args):
    return _fwd_reference(*[args[k] for k in FWD_PARAMS])


def _output_shape():
    def fwd():
        inp = _fwd_setup_inputs(0)
        return _fwd_reference(*[inp[k] for k in FWD_PARAMS])
    out = _jax.eval_shape(fwd)
    return out.shape, out.dtype

N_MICROBATCH = 1
ADAM_LR = 0.001
ADAM_B1 = 0.9
ADAM_B2 = 0.999
ADAM_EPS = 1e-08
ADAM_WD = 0.01
ADAM_STEP = 10
PER_EXAMPLE_BATCH_AXIS = {'x': 0, 'c': 0, 'loss_target': 0}
SHARED_INPUTS = []
_WEIGHT_DTYPES = {'mix_norm_g': _jnp.float32, 'mix_ada_w': _jnp.float32, 'mix_ada_b': _jnp.float32, 'ffn_norm_g': _jnp.float32, 'ffn_ada_w': _jnp.float32, 'ffn_ada_b': _jnp.float32, 'ffn_w_in': _jnp.float32, 'ffn_w_out': _jnp.float32, 'conv_w_in': _jnp.float32, 'conv_b_in': _jnp.float32, 'conv_w_dw': _jnp.float32, 'conv_b_dw': _jnp.float32, 'conv_ln_g': _jnp.float32, 'conv_ln_b': _jnp.float32, 'conv_w_out': _jnp.float32, 'conv_b_out': _jnp.float32, 'kv_norm_g': _jnp.float32, 'kv_ada_w': _jnp.float32, 'kv_ada_b': _jnp.float32, 'kv_w': _jnp.float32, 'forget_b': _jnp.float32, 'attn_w_q': _jnp.float32, 'attn_w_o': _jnp.float32, 'final_norm_g': _jnp.float32}
MOMENT_SCALE = {'mix_norm_g': 4.211443e-02, 'mix_ada_w': 5.502733e-02, 'mix_ada_b': 9.632894e-02, 'ffn_norm_g': 7.270260e-02, 'ffn_ada_w': 7.350999e-02, 'ffn_ada_b': 1.245682e-01, 'ffn_w_in': 3.196636e-02, 'ffn_w_out': 5.216948e-02, 'conv_w_in': 3.750966e-02, 'conv_b_in': 3.990539e-02, 'conv_w_dw': 4.885489e-02, 'conv_b_dw': 1.051114e-01, 'conv_ln_g': 5.852083e-02, 'conv_ln_b': 5.428048e-02, 'conv_w_out': 4.791286e-02, 'conv_b_out': 8.530933e-02, 'kv_norm_g': 4.553477e-02, 'kv_ada_w': 3.883459e-02, 'kv_ada_b': 6.833176e-02, 'kv_w': 3.545621e-02, 'forget_b': 1.595700e-01, 'attn_w_q': 2.778082e-02, 'attn_w_o': 4.095840e-02, 'final_norm_g': 6.401812e+01}


def _to_microbatches(a, axis):
    t = _jnp.moveaxis(a, axis, 0)
    t = t.reshape((N_MICROBATCH, t.shape[0] // N_MICROBATCH) + t.shape[1:])
    return _jnp.moveaxis(t, 1, axis + 1)


def setup_inputs(seed: int = 0) -> dict:
    inp = _fwd_setup_inputs(seed)
    key = _jax.random.fold_in(_jax.random.key(seed), 7919)
    shape, _ = _output_shape()
    out = dict(inp)
    out["loss_target"] = _jax.random.normal(_jax.random.fold_in(key, 0), shape, _jnp.float32)
    for i, name in enumerate(TWIN_WEIGHTS):
        w = inp[name].astype(_jnp.float32)
        if MOMENT_SCALE is None:
            s = _jnp.sqrt(_jnp.mean(_jnp.square(w)) + 1e-30)
        else:
            s = MOMENT_SCALE[name]
        km, kv = _jax.random.split(_jax.random.fold_in(key, i + 1))
        out[name] = w
        out["m_" + name] = s * _jax.random.normal(km, w.shape, _jnp.float32)
        out["v_" + name] = (s * s) * _jax.random.uniform(kv, w.shape, _jnp.float32, 0.5, 1.5)
    if N_MICROBATCH > 1:
        for name, axis in PER_EXAMPLE_BATCH_AXIS.items():
            out[name] = _to_microbatches(out[name], axis)
    return {'x': out['x'], 'c': out['c'], 'mix_norm_g': out['mix_norm_g'], 'mix_ada_w': out['mix_ada_w'], 'mix_ada_b': out['mix_ada_b'], 'ffn_norm_g': out['ffn_norm_g'], 'ffn_ada_w': out['ffn_ada_w'], 'ffn_ada_b': out['ffn_ada_b'], 'ffn_w_in': out['ffn_w_in'], 'ffn_w_out': out['ffn_w_out'], 'conv_w_in': out['conv_w_in'], 'conv_b_in': out['conv_b_in'], 'conv_w_dw': out['conv_w_dw'], 'conv_b_dw': out['conv_b_dw'], 'conv_ln_g': out['conv_ln_g'], 'conv_ln_b': out['conv_ln_b'], 'conv_w_out': out['conv_w_out'], 'conv_b_out': out['conv_b_out'], 'kv_norm_g': out['kv_norm_g'], 'kv_ada_w': out['kv_ada_w'], 'kv_ada_b': out['kv_ada_b'], 'kv_w': out['kv_w'], 'forget_b': out['forget_b'], 'attn_w_q': out['attn_w_q'], 'attn_w_o': out['attn_w_o'], 'final_norm_g': out['final_norm_g'], 'loss_target': out['loss_target'], 'm_mix_norm_g': out['m_mix_norm_g'], 'm_mix_ada_w': out['m_mix_ada_w'], 'm_mix_ada_b': out['m_mix_ada_b'], 'm_ffn_norm_g': out['m_ffn_norm_g'], 'm_ffn_ada_w': out['m_ffn_ada_w'], 'm_ffn_ada_b': out['m_ffn_ada_b'], 'm_ffn_w_in': out['m_ffn_w_in'], 'm_ffn_w_out': out['m_ffn_w_out'], 'm_conv_w_in': out['m_conv_w_in'], 'm_conv_b_in': out['m_conv_b_in'], 'm_conv_w_dw': out['m_conv_w_dw'], 'm_conv_b_dw': out['m_conv_b_dw'], 'm_conv_ln_g': out['m_conv_ln_g'], 'm_conv_ln_b': out['m_conv_ln_b'], 'm_conv_w_out': out['m_conv_w_out'], 'm_conv_b_out': out['m_conv_b_out'], 'm_kv_norm_g': out['m_kv_norm_g'], 'm_kv_ada_w': out['m_kv_ada_w'], 'm_kv_ada_b': out['m_kv_ada_b'], 'm_kv_w': out['m_kv_w'], 'm_forget_b': out['m_forget_b'], 'm_attn_w_q': out['m_attn_w_q'], 'm_attn_w_o': out['m_attn_w_o'], 'm_final_norm_g': out['m_final_norm_g'], 'v_mix_norm_g': out['v_mix_norm_g'], 'v_mix_ada_w': out['v_mix_ada_w'], 'v_mix_ada_b': out['v_mix_ada_b'], 'v_ffn_norm_g': out['v_ffn_norm_g'], 'v_ffn_ada_w': out['v_ffn_ada_w'], 'v_ffn_ada_b': out['v_ffn_ada_b'], 'v_ffn_w_in': out['v_ffn_w_in'], 'v_ffn_w_out': out['v_ffn_w_out'], 'v_conv_w_in': out['v_conv_w_in'], 'v_conv_b_in': out['v_conv_b_in'], 'v_conv_w_dw': out['v_conv_w_dw'], 'v_conv_b_dw': out['v_conv_b_dw'], 'v_conv_ln_g': out['v_conv_ln_g'], 'v_conv_ln_b': out['v_conv_ln_b'], 'v_conv_w_out': out['v_conv_w_out'], 'v_conv_b_out': out['v_conv_b_out'], 'v_kv_norm_g': out['v_kv_norm_g'], 'v_kv_ada_w': out['v_kv_ada_w'], 'v_kv_ada_b': out['v_kv_ada_b'], 'v_kv_w': out['v_kv_w'], 'v_forget_b': out['v_forget_b'], 'v_attn_w_q': out['v_attn_w_q'], 'v_attn_w_o': out['v_attn_w_o'], 'v_final_norm_g': out['v_final_norm_g']}


def _loss(weights, diff, rest, loss_target):
    with _jax.named_scope("forward"):
        args = {**rest, TWIN_DIFF_INPUT: diff, **{k: w.astype(_WEIGHT_DTYPES[k]) for k, w in weights.items()}}
        y = _forward(args)
    with _jax.named_scope("loss_head"):
        err = _jnp.square(y.astype(_jnp.float32) - loss_target)
        return 0.5 * _jnp.sum(_jnp.mean(err, axis=-1)) if err.ndim else 0.5 * err


def _adamw(w, g, m, v):
    m = ADAM_B1 * m + (1.0 - ADAM_B1) * g
    v = ADAM_B2 * v + (1.0 - ADAM_B2) * _jnp.square(g)
    m_hat = m / (1.0 - ADAM_B1 ** ADAM_STEP)
    v_hat = v / (1.0 - ADAM_B2 ** ADAM_STEP)
    delta = -ADAM_LR * (m_hat / (_jnp.sqrt(v_hat) + ADAM_EPS) + ADAM_WD * w)
    return delta, m, v


def reference(x, c, mix_norm_g, mix_ada_w, mix_ada_b, ffn_norm_g, ffn_ada_w, ffn_ada_b, ffn_w_in, ffn_w_out, conv_w_in, conv_b_in, conv_w_dw, conv_b_dw, conv_ln_g, conv_ln_b, conv_w_out, conv_b_out, kv_norm_g, kv_ada_w, kv_ada_b, kv_w, forget_b, attn_w_q, attn_w_o, final_norm_g, loss_target, m_mix_norm_g, m_mix_ada_w, m_mix_ada_b, m_ffn_norm_g, m_ffn_ada_w, m_ffn_ada_b, m_ffn_w_in, m_ffn_w_out, m_conv_w_in, m_conv_b_in, m_conv_w_dw, m_conv_b_dw, m_conv_ln_g, m_conv_ln_b, m_conv_w_out, m_conv_b_out, m_kv_norm_g, m_kv_ada_w, m_kv_ada_b, m_kv_w, m_forget_b, m_attn_w_q, m_attn_w_o, m_final_norm_g, v_mix_norm_g, v_mix_ada_w, v_mix_ada_b, v_ffn_norm_g, v_ffn_ada_w, v_ffn_ada_b, v_ffn_w_in, v_ffn_w_out, v_conv_w_in, v_conv_b_in, v_conv_w_dw, v_conv_b_dw, v_conv_ln_g, v_conv_ln_b, v_conv_w_out, v_conv_b_out, v_kv_norm_g, v_kv_ada_w, v_kv_ada_b, v_kv_w, v_forget_b, v_attn_w_q, v_attn_w_o, v_final_norm_g):
    given = dict(x=x, c=c, mix_norm_g=mix_norm_g, mix_ada_w=mix_ada_w, mix_ada_b=mix_ada_b, ffn_norm_g=ffn_norm_g, ffn_ada_w=ffn_ada_w, ffn_ada_b=ffn_ada_b, ffn_w_in=ffn_w_in, ffn_w_out=ffn_w_out, conv_w_in=conv_w_in, conv_b_in=conv_b_in, conv_w_dw=conv_w_dw, conv_b_dw=conv_b_dw, conv_ln_g=conv_ln_g, conv_ln_b=conv_ln_b, conv_w_out=conv_w_out, conv_b_out=conv_b_out, kv_norm_g=kv_norm_g, kv_ada_w=kv_ada_w, kv_ada_b=kv_ada_b, kv_w=kv_w, forget_b=forget_b, attn_w_q=attn_w_q, attn_w_o=attn_w_o, final_norm_g=final_norm_g, loss_target=loss_target, m_mix_norm_g=m_mix_norm_g, m_mix_ada_w=m_mix_ada_w, m_mix_ada_b=m_mix_ada_b, m_ffn_norm_g=m_ffn_norm_g, m_ffn_ada_w=m_ffn_ada_w, m_ffn_ada_b=m_ffn_ada_b, m_ffn_w_in=m_ffn_w_in, m_ffn_w_out=m_ffn_w_out, m_conv_w_in=m_conv_w_in, m_conv_b_in=m_conv_b_in, m_conv_w_dw=m_conv_w_dw, m_conv_b_dw=m_conv_b_dw, m_conv_ln_g=m_conv_ln_g, m_conv_ln_b=m_conv_ln_b, m_conv_w_out=m_conv_w_out, m_conv_b_out=m_conv_b_out, m_kv_norm_g=m_kv_norm_g, m_kv_ada_w=m_kv_ada_w, m_kv_ada_b=m_kv_ada_b, m_kv_w=m_kv_w, m_forget_b=m_forget_b, m_attn_w_q=m_attn_w_q, m_attn_w_o=m_attn_w_o, m_final_norm_g=m_final_norm_g, v_mix_norm_g=v_mix_norm_g, v_mix_ada_w=v_mix_ada_w, v_mix_ada_b=v_mix_ada_b, v_ffn_norm_g=v_ffn_norm_g, v_ffn_ada_w=v_ffn_ada_w, v_ffn_ada_b=v_ffn_ada_b, v_ffn_w_in=v_ffn_w_in, v_ffn_w_out=v_ffn_w_out, v_conv_w_in=v_conv_w_in, v_conv_b_in=v_conv_b_in, v_conv_w_dw=v_conv_w_dw, v_conv_b_dw=v_conv_b_dw, v_conv_ln_g=v_conv_ln_g, v_conv_ln_b=v_conv_ln_b, v_conv_w_out=v_conv_w_out, v_conv_b_out=v_conv_b_out, v_kv_norm_g=v_kv_norm_g, v_kv_ada_w=v_kv_ada_w, v_kv_ada_b=v_kv_ada_b, v_kv_w=v_kv_w, v_forget_b=v_forget_b, v_attn_w_q=v_attn_w_q, v_attn_w_o=v_attn_w_o, v_final_norm_g=v_final_norm_g)
    weights = {n: given[n] for n in TWIN_WEIGHTS}
    shared = {n: given[n] for n in SHARED_INPUTS}
    per_example = {n: given[n] for n in ['x', 'c']}
    grad_fn = _jax.value_and_grad(_loss, argnums=(0, 1))

    def one_microbatch(ex, loss_target):
        ex = dict(ex)
        diff = ex.pop(TWIN_DIFF_INPUT)
        return grad_fn(weights, diff, {**shared, **ex}, loss_target)

    if N_MICROBATCH == 1:
        loss, (grad_w, grad_x) = one_microbatch(per_example, given["loss_target"])
    else:
        def body(carry, xs):
            loss_sum, grad_sum = carry
            l_k, (gw_k, gx_k) = one_microbatch(xs[0], xs[1])
            with _jax.named_scope("update"):
                return (loss_sum + l_k, _jax.tree.map(_jnp.add, grad_sum, gw_k)), gx_k

        init = (_jnp.zeros((), _jnp.float32), _jax.tree.map(_jnp.zeros_like, weights))
        (loss, grad_w), grad_x = _jax.lax.scan(body, init, (per_example, given["loss_target"]))
    with _jax.named_scope("update"):
        delta_w, new_m, new_v = {}, {}, {}
        for n in TWIN_WEIGHTS:
            delta_w[n], new_m[n], new_v[n] = _adamw(weights[n], grad_w[n], given["m_" + n], given["v_" + n])
    return (loss, grad_x, *[grad_w[n] for n in TWIN_WEIGHTS], *[delta_w[n] for n in TWIN_WEIGHTS],
            *[new_m[n] for n in TWIN_WEIGHTS], *[new_v[n] for n in TWIN_WEIGHTS])
```

```python
import functools

import jax
import jax.numpy as jnp
from jax import lax
from jax.experimental import pallas as pl
from jax.experimental.pallas import tpu as pltpu

f32, bf16 = jnp.float32, jnp.bfloat16
SDS = jax.ShapeDtypeStruct

D = 1024
F = 2816
H = 16
HD = 64
NP = H // 2
KW = 31
HALO = 32
NDEV = 8
EPS = 1e-6
NEG = -1e30
LANES = 128

ADAM_LR, ADAM_B1, ADAM_B2, ADAM_EPS, ADAM_WD, ADAM_STEP = 0.001, 0.9, 0.999, 1e-08, 0.01, 10

TM = 512
TC = 256
TQ = 512
VMEM_LIMIT = 56 << 20

MAIN = ("ffn_w_in", "ffn_w_out", "conv_w_in", "conv_w_out", "kv_w", "attn_w_q", "attn_w_o")
ADA_SEG = (("mix", 0, 3 * D), ("mix", 1, 3 * D), ("ffn", 0, 3 * D), ("ffn", 1, 3 * D), ("kv", 0, 2 * D))
ADA_LOC = tuple(n // NDEV for _, _, n in ADA_SEG)
ADA_COLS = sum(ADA_LOC)
ADA_TOT = sum(n for _, _, n in ADA_SEG)


def _cparams(n_axes):
    return pltpu.CompilerParams(dimension_semantics=("arbitrary",) * n_axes, vmem_limit_bytes=VMEM_LIMIT)


def _mesh_pos():
    return lax.axis_index("x"), lax.axis_index("y"), lax.axis_index("c")


def _my_index():
    mx, my, mc = _mesh_pos()
    return 4 * mx + 2 * my + mc


def _peer(k, mx, my, mc):
    px = (1 - mx) if k & 4 else mx
    py = (1 - my) if k & 2 else my
    pc = (1 - mc) if k & 1 else mc
    return (px, py, pc), 4 * px + 2 * py + pc


def _exchange(x, gather, name):
    blk = x.shape if gather else x.shape[1:]

    def body(x_ref, o_ref, send_sems, recv_sems, local_sem):
        mx, my, mc = _mesh_pos()
        me = 4 * mx + 2 * my + mc
        own = x_ref if gather else x_ref.at[me]
        mine = pltpu.make_async_copy(own, o_ref.at[me], local_sem)
        mine.start()
        copies = []
        for k in range(1, NDEV):
            peer, pidx = _peer(k, mx, my, mc)
            cp = pltpu.make_async_remote_copy(
                src_ref=x_ref if gather else x_ref.at[pidx], dst_ref=o_ref.at[me],
                send_sem=send_sems.at[k - 1], recv_sem=recv_sems.at[k - 1],
                device_id=peer, device_id_type=pl.DeviceIdType.MESH)
            cp.start()
            copies.append(cp)
        for cp in copies:
            cp.wait()
        mine.wait()

    return pl.pallas_call(
        body, out_shape=SDS((NDEV,) + tuple(blk), x.dtype),
        in_specs=[pl.BlockSpec(memory_space=pl.ANY)], out_specs=pl.BlockSpec(memory_space=pl.ANY),
        scratch_shapes=[pltpu.SemaphoreType.DMA((NDEV - 1,)), pltpu.SemaphoreType.DMA((NDEV - 1,)), pltpu.SemaphoreType.DMA],
        name=name)(x)


def _row(i):
    return (i, 0)


def _fix(i):
    return (0, 0)


def _normmod(x, g, shift, scale, name):
    T = x.shape[0]
    tm = min(TM, T)

    def body(x_ref, g_ref, sh_ref, sc_ref, h_ref):
        xv = x_ref[...]
        r = lax.rsqrt(jnp.mean(xv * xv, axis=-1, keepdims=True) + EPS)
        hn = (xv * r) * g_ref[...]
        h_ref[...] = (hn * (1.0 + sc_ref[...]) + sh_ref[...]).astype(bf16)

    vec = pl.BlockSpec((1, D), _fix)
    return pl.pallas_call(
        body, grid=(T // tm,), in_specs=[pl.BlockSpec((tm, D), _row), vec, vec, vec],
        out_specs=pl.BlockSpec((tm, D), _row), out_shape=SDS((T, D), bf16),
        compiler_params=_cparams(1), name=name)(x, g, shift, scale)


def _normmod_bwd(dh, x, g, scale, dx_res, name):
    T = x.shape[0]
    tm = min(TM, T)

    def body(dh_ref, x_ref, g_ref, sc_ref, res_ref, dx_ref, acc_ref):
        @pl.when(pl.program_id(0) == 0)
        def _():
            acc_ref[...] = jnp.zeros_like(acc_ref)
        xv = x_ref[...]
        dhv = dh_ref[...]
        gv = g_ref[...]
        r = lax.rsqrt(jnp.mean(xv * xv, axis=-1, keepdims=True) + EPS)
        xn = xv * r
        dhn = dhv * (1.0 + sc_ref[...])
        dxn = dhn * gv
        dx_ref[...] = res_ref[...] + r * (dxn - xn * jnp.mean(dxn * xn, axis=-1, keepdims=True))
        acc_ref[0:1, :] += jnp.sum(dhv, axis=0, keepdims=True)
        acc_ref[1:2, :] += jnp.sum(dhv * (xn * gv), axis=0, keepdims=True)
        acc_ref[2:3, :] += jnp.sum(dhn * xn, axis=0, keepdims=True)

    vec = pl.BlockSpec((1, D), _fix)
    til = pl.BlockSpec((tm, D), _row)
    return pl.pallas_call(
        body, grid=(T // tm,), in_specs=[til, til, vec, vec, til],
        out_specs=[til, pl.BlockSpec((8, D), _fix)], out_shape=[SDS((T, D), f32), SDS((8, D), f32)],
        compiler_params=_cparams(1), name=name)(dh, x, g, scale, dx_res)


def _res_in(dx, y, gate, name):
    T = dx.shape[0]
    tm = min(TM, T)

    def body(dx_ref, y_ref, gt_ref, dy_ref, acc_ref):
        @pl.when(pl.program_id(0) == 0)
        def _():
            acc_ref[...] = jnp.zeros_like(acc_ref)
        dxv = dx_ref[...]
        dy = dxv * gt_ref[...]
        dy_ref[...] = dy.astype(bf16)
        acc_ref[0:1, :] += jnp.sum(dxv * y_ref[...].astype(f32), axis=0, keepdims=True)
        acc_ref[1:2, :] += jnp.sum(dy, axis=0, keepdims=True)

    til = pl.BlockSpec((tm, D), _row)
    return pl.pallas_call(
        body, grid=(T // tm,), in_specs=[til, til, pl.BlockSpec((1, D), _fix)],
        out_specs=[til, pl.BlockSpec((8, D), _fix)], out_shape=[SDS((T, D), bf16), SDS((8, D), f32)],
        compiler_params=_cparams(1), name=name)(dx, y, gate)


def _final_bwd(x, g, tgt, name):
    T = x.shape[0]
    tm = min(TM, T)

    def body(x_ref, g_ref, t_ref, dx_ref, acc_ref):
        @pl.when(pl.program_id(0) == 0)
        def _():
            acc_ref[...] = jnp.zeros_like(acc_ref)
        xv = x_ref[...]
        gv = g_ref[...]
        r = lax.rsqrt(jnp.mean(xv * xv, axis=-1, keepdims=True) + EPS)
        xn = xv * r
        err = xn * gv - t_ref[...]
        dy = err * (1.0 / D)
        dxn = dy * gv
        dx_ref[...] = r * (dxn - xn * jnp.mean(dxn * xn, axis=-1, keepdims=True))
        acc_ref[0:1, :] += jnp.sum(dy * xn, axis=0, keepdims=True)
        acc_ref[1:2, :] += 0.5 * jnp.sum(jnp.mean(err * err, axis=-1, keepdims=True))

    til = pl.BlockSpec((tm, D), _row)
    return pl.pallas_call(
        body, grid=(T // tm,), in_specs=[til, pl.BlockSpec((1, D), _fix), til],
        out_specs=[til, pl.BlockSpec((8, D), _fix)], out_shape=[SDS((T, D), f32), SDS((8, D), f32)],
        compiler_params=_cparams(1), name=name)(x, g, tgt)


def _col_tile(n):
    if n <= 1024:
        return min(n, 512) if n % 512 == 0 else n
    return 1408 if n % 1408 == 0 else 1024


def _mm_gated(h, wa, wb, ba, bb, swiglu, name):
    T, K = h.shape
    N = wa.shape[1]
    tm, tn = min(TM, T), _col_tile(N)

    def body(h_ref, wa_ref, wb_ref, ba_ref, bb_ref, u_ref, w_ref, p_ref):
        hv = h_ref[...]
        u = jnp.dot(hv, wa_ref[...], preferred_element_type=f32) + ba_ref[...]
        w = jnp.dot(hv, wb_ref[...], preferred_element_type=f32) + bb_ref[...]
        u_ref[...] = u
        w_ref[...] = w
        if swiglu:
            p_ref[...] = ((u * jax.nn.sigmoid(u)) * w).astype(p_ref.dtype)
        else:
            p_ref[...] = (u * jax.nn.sigmoid(w)).astype(p_ref.dtype)

    wsp = pl.BlockSpec((K, tn), lambda i, j: (0, j))
    bsp = pl.BlockSpec((1, tn), lambda i, j: (0, j))
    osp = pl.BlockSpec((tm, tn), lambda i, j: (i, j))
    return pl.pallas_call(
        body, grid=(T // tm, N // tn), in_specs=[pl.BlockSpec((tm, K), lambda i, j: (i, 0)), wsp, wsp, bsp, bsp],
        out_specs=[osp, osp, osp], out_shape=[SDS((T, N), f32), SDS((T, N), f32), SDS((T, N), bf16)],
        compiler_params=_cparams(2), name=name)(h, wa, wb, ba, bb)


def _mm_res(a, w, b, x_in, gate, name):
    T, K = a.shape
    N = w.shape[1]
    tm, tn = min(TM, T), _col_tile(N)

    def body(a_ref, w_ref, b_ref, x_ref, gt_ref, xo_ref, y_ref):
        y = jnp.dot(a_ref[...], w_ref[...], preferred_element_type=f32) + b_ref[...]
        y_ref[...] = y.astype(bf16)
        xo_ref[...] = x_ref[...] + gt_ref[...] * y

    vsp = pl.BlockSpec((1, tn), lambda i, j: (0, j))
    osp = pl.BlockSpec((tm, tn), lambda i, j: (i, j))
    return pl.pallas_call(
        body, grid=(T // tm, N // tn),
        in_specs=[pl.BlockSpec((tm, K), lambda i, j: (i, 0)), pl.BlockSpec((K, tn), lambda i, j: (0, j)), vsp, osp, vsp],
        out_specs=[osp, osp], out_shape=[SDS((T, N), f32), SDS((T, N), bf16)],
        compiler_params=_cparams(2), name=name)(a, w, b, x_in, gate)


def _mm(a, w, out_dtype, out_scale, name):
    T, K = a.shape
    N = w.shape[1]
    tm, tn = min(TM, T), _col_tile(N)

    def body(a_ref, w_ref, o_ref):
        y = jnp.dot(a_ref[...], w_ref[...], preferred_element_type=f32)
        if out_scale != 1.0:
            y = y * out_scale
        o_ref[...] = y.astype(out_dtype)

    return pl.pallas_call(
        body, grid=(T // tm, N // tn),
        in_specs=[pl.BlockSpec((tm, K), lambda i, j: (i, 0)), pl.BlockSpec((K, tn), lambda i, j: (0, j))],
        out_specs=pl.BlockSpec((tm, tn), lambda i, j: (i, j)), out_shape=SDS((T, N), out_dtype),
        compiler_params=_cparams(2), name=name)(a, w)


def _dot_nt(a, b):
    return lax.dot_general(a, b, (((1,), (1,)), ((), ())), preferred_element_type=f32)


def _dot_tn(a, b):
    return lax.dot_general(a, b, (((0,), (0,)), ((), ())), preferred_element_type=f32)


def _mm_nt(pairs, out_dtype, name):
    T = pairs[0][0].shape[0]
    K = pairs[0][1].shape[0]
    tm, tk = min(TM, T), _col_tile(K)
    n = len(pairs)

    def body(*refs):
        o_ref = refs[2 * n]
        acc = None
        for i in range(n):
            part = _dot_nt(refs[2 * i][...].astype(bf16), refs[2 * i + 1][...])
            acc = part if acc is None else acc + part
        o_ref[...] = acc.astype(out_dtype)

    in_specs, args = [], []
    for dy, w in pairs:
        ni = dy.shape[1]
        in_specs += [pl.BlockSpec((tm, ni), lambda i, j: (i, 0)), pl.BlockSpec((tk, ni), lambda i, j: (j, 0))]
        args += [dy, w]
    return pl.pallas_call(
        body, grid=(T // tm, K // tk), in_specs=in_specs,
        out_specs=pl.BlockSpec((tm, tk), lambda i, j: (i, j)), out_shape=SDS((T, K), out_dtype),
        compiler_params=_cparams(2), name=name)(*args)


def _mm_nt_swiglu(dy, w, ug, uu, name):
    T, N = dy.shape
    K = w.shape[0]
    tm, tk = min(TM, T), _col_tile(K)

    def body(dy_ref, w_ref, ug_ref, uu_ref, dug_ref, duu_ref):
        dact = _dot_nt(dy_ref[...], w_ref[...])
        g = ug_ref[...].astype(f32)
        u = uu_ref[...].astype(f32)
        sg = jax.nn.sigmoid(g)
        duu_ref[...] = (dact * (g * sg)).astype(bf16)
        dug_ref[...] = (dact * u * (sg * (1.0 + g * (1.0 - sg)))).astype(bf16)

    osp = pl.BlockSpec((tm, tk), lambda i, j: (i, j))
    return pl.pallas_call(
        body, grid=(T // tm, K // tk),
        in_specs=[pl.BlockSpec((tm, N), lambda i, j: (i, 0)), pl.BlockSpec((tk, N), lambda i, j: (j, 0)), osp, osp],
        out_specs=[osp, osp], out_shape=[SDS((T, K), bf16), SDS((T, K), bf16)],
        compiler_params=_cparams(2), name=name)(dy, w, ug, uu)


def _mm_tn(a, b, name):
    T, K = a.shape
    N = b.shape[1]
    tt = min(TM, T)
    tk = K if K <= 1024 else _col_tile(K)
    tn = N if N <= 1024 else _col_tile(N)
    nt = T // tt

    def body(a_ref, b_ref, o_ref):
        @pl.when(pl.program_id(2) == 0)
        def _():
            o_ref[...] = jnp.zeros_like(o_ref)
        o_ref[...] += _dot_tn(a_ref[...].astype(bf16), b_ref[...].astype(bf16))

    return pl.pallas_call(
        body, grid=(K // tk, N // tn, nt),
        in_specs=[pl.BlockSpec((tt, tk), lambda i, j, t: (t, i)), pl.BlockSpec((tt, tn), lambda i, j, t: (t, j))],
        out_specs=pl.BlockSpec((tk, tn), lambda i, j, t: (i, j)), out_shape=SDS((K, N), f32),
        compiler_params=_cparams(3), name=name)(a, b)


def _layernorm_parts(qv, g, b):
    mu = jnp.mean(qv, axis=-1, keepdims=True)
    cen = qv - mu
    rstd = lax.rsqrt(jnp.mean(cen * cen, axis=-1, keepdims=True) + EPS)
    z = cen * rstd
    return z, rstd, z * g + b


def _conv_fwd(p, w_dw, b_dw, ln_g, ln_b, name):
    T = p.shape[0]
    tc = min(TC, T)

    def body(p_ref, w_ref, b_ref, g_ref, bb_ref, q_ref, s_ref, ext):
        i = pl.program_id(0)

        @pl.when(i == 0)
        def _():
            ext[0:HALO, :] = jnp.zeros((HALO, D), f32)

        @pl.when(i > 0)
        def _():
            ext[0:HALO, :] = ext[tc:tc + HALO, :]

        ext[HALO:HALO + tc, :] = p_ref[...].astype(f32)
        for cb in range(D // LANES):
            cols = slice(cb * LANES, (cb + 1) * LANES)
            acc = jnp.zeros((tc, LANES), f32)
            for k in range(KW):
                acc = acc + w_ref[k:k + 1, cols] * ext[pl.ds(HALO - (KW - 1) + k, tc), cols]
            q_ref[:, cols] = acc + b_ref[:, cols]
        _, _, l = _layernorm_parts(q_ref[...], g_ref[...], bb_ref[...])
        s_ref[...] = (l * jax.nn.sigmoid(l)).astype(bf16)

    vec = pl.BlockSpec((1, D), _fix)
    til = pl.BlockSpec((tc, D), _row)
    return pl.pallas_call(
        body, grid=(T // tc,), in_specs=[til, pl.BlockSpec((HALO, D), _fix), vec, vec, vec],
        out_specs=[til, til], out_shape=[SDS((T, D), f32), SDS((T, D), bf16)],
        scratch_shapes=[pltpu.VMEM((tc + HALO, D), f32)], compiler_params=_cparams(1), name=name)(p, w_dw, b_dw, ln_g, ln_b)


def _conv_bwd(ds, q, p, a, gl, w_dw, ln_g, ln_b, name):
    T = q.shape[0]
    tc = min(TC, T)
    n = T // tc

    def body(ds_ref, q_ref, p_ref, a_ref, gl_ref, w_ref, g_ref, bb_ref, da_ref, dgl_ref, acc_ref, dw_ref, ext):
        i = pl.program_id(0)

        @pl.when(i == 0)
        def _():
            acc_ref[...] = jnp.zeros_like(acc_ref)
            dw_ref[...] = jnp.zeros_like(dw_ref)
            ext[tc:tc + HALO, :] = jnp.zeros((HALO, D), f32)

        @pl.when(i > 0)
        def _():
            ext[tc:tc + HALO, :] = ext[0:HALO, :]

        gv = g_ref[...]
        z, rstd, l = _layernorm_parts(q_ref[...], gv, bb_ref[...])
        sg = jax.nn.sigmoid(l)
        dl = ds_ref[...] * (sg * (1.0 + l * (1.0 - sg)))
        dz = dl * gv
        dq = rstd * (dz - jnp.mean(dz, axis=-1, keepdims=True) - z * jnp.mean(dz * z, axis=-1, keepdims=True))
        ext[0:tc, :] = dq.astype(bf16).astype(f32)
        acc_ref[0:1, :] += jnp.sum(dl * z, axis=0, keepdims=True)
        acc_ref[1:2, :] += jnp.sum(dl, axis=0, keepdims=True)
        acc_ref[2:3, :] += jnp.sum(dq, axis=0, keepdims=True)
        for cb in range(D // LANES):
            cols = slice(cb * LANES, (cb + 1) * LANES)
            pc = p_ref[:, cols].astype(f32)
            dp = jnp.zeros((tc, LANES), f32)
            for k in range(KW):
                sl = ext[pl.ds(KW - 1 - k, tc), cols]
                dp = dp + w_ref[k:k + 1, cols] * sl
                dw_ref[k:k + 1, cols] += jnp.sum(sl * pc, axis=0, keepdims=True)
            av = a_ref[:, cols].astype(f32)
            sgl = jax.nn.sigmoid(gl_ref[:, cols].astype(f32))
            da = dp * sgl
            dgl = dp * av * (sgl * (1.0 - sgl))
            da_ref[:, cols] = da.astype(bf16)
            dgl_ref[:, cols] = dgl.astype(bf16)
            acc_ref[3:4, cols] += jnp.sum(da, axis=0, keepdims=True)
            acc_ref[4:5, cols] += jnp.sum(dgl, axis=0, keepdims=True)

    rev = lambda i: (n - 1 - i, 0)
    til = pl.BlockSpec((tc, D), rev)
    vec = pl.BlockSpec((1, D), _fix)
    return pl.pallas_call(
        body, grid=(n,), in_specs=[til, til, til, til, til, pl.BlockSpec((HALO, D), _fix), vec, vec],
        out_specs=[til, til, pl.BlockSpec((8, D), _fix), pl.BlockSpec((HALO, D), _fix)],
        out_shape=[SDS((T, D), bf16), SDS((T, D), bf16), SDS((8, D), f32), SDS((HALO, D), f32)],
        scratch_shapes=[pltpu.VMEM((tc + HALO, D), f32)], compiler_params=_cparams(1), name=name)(ds, q, p, a, gl, w_dw, ln_g, ln_b)


def _tri(n, upper):
    r = lax.broadcasted_iota(jnp.int32, (n, n), 0)
    c = lax.broadcasted_iota(jnp.int32, (n, n), 1)
    return ((c >= r) if upper else (r >= c)).astype(f32)


def _forget_fwd(fl, fb, name):
    T = fl.shape[0]
    tc = min(TC, T)

    def body(fl_ref, fb_ref, cum_ref, carry):
        @pl.when(pl.program_id(0) == 0)
        def _():
            carry[...] = jnp.zeros_like(carry)
        xv = fl_ref[...] + fb_ref[...]
        lf = jnp.minimum(xv, 0.0) - jnp.log(1.0 + jnp.exp(-jnp.abs(xv)))
        cs = jnp.dot(_tri(tc, False), lf, preferred_element_type=f32, precision=lax.Precision.HIGHEST) + carry[0:1, :]
        cum_ref[...] = cs
        carry[0:1, :] = cs[tc - 1:tc, :]

    til = pl.BlockSpec((tc, LANES), _row)
    return pl.pallas_call(
        body, grid=(T // tc,), in_specs=[til, pl.BlockSpec((1, LANES), _fix)], out_specs=til,
        out_shape=SDS((T, LANES), f32), scratch_shapes=[pltpu.VMEM((8, LANES), f32)],
        compiler_params=_cparams(1), name=name)(fl, fb)


def _forget_bwd(dcum, fl, fb, name):
    T = fl.shape[0]
    tc = min(TC, T)
    n = T // tc

    def body(dc_ref, fl_ref, fb_ref, dfl_ref, acc_ref, carry):
        @pl.when(pl.program_id(0) == 0)
        def _():
            carry[...] = jnp.zeros_like(carry)
            acc_ref[...] = jnp.zeros_like(acc_ref)
        dlf = jnp.dot(_tri(tc, True), dc_ref[...], preferred_element_type=f32, precision=lax.Precision.HIGHEST) + carry[0:1, :]
        carry[0:1, :] = dlf[0:1, :]
        dfl = dlf * (1.0 - jax.nn.sigmoid(fl_ref[...] + fb_ref[...]))
        dfl_ref[...] = dfl.astype(bf16)
        acc_ref[0:1, :] += jnp.sum(dfl, axis=0, keepdims=True)

    rev = lambda i: (n - 1 - i, 0)
    til = pl.BlockSpec((tc, LANES), rev)
    return pl.pallas_call(
        body, grid=(n,), in_specs=[til, til, pl.BlockSpec((1, LANES), _fix)],
        out_specs=[til, pl.BlockSpec((8, LANES), _fix)], out_shape=[SDS((T, LANES), bf16), SDS((8, LANES), f32)],
        scratch_shapes=[pltpu.VMEM((8, LANES), f32)], compiler_params=_cparams(1), name=name)(dcum, fl, fb)


def _causal(s, n):
    r = lax.broadcasted_iota(jnp.int32, (n, n), 0)
    c = lax.broadcasted_iota(jnp.int32, (n, n), 1)
    return jnp.where(c <= r, s, NEG)


def _attn_fwd(q, k, v, cum_t, name):
    T = q.shape[0]
    tq = min(TQ, T)

    def body(q_ref, k_ref, v_ref, ck_ref, o_ref, o32_ref, st_ref, m_sc, l_sc, acc_sc, res_sc):
        i = pl.program_id(1)
        lane = lax.broadcasted_iota(jnp.int32, (1, LANES), 1)
        lo = lane < HD
        q2 = q_ref[...]
        zero = jnp.zeros_like(q2)
        qa = (jnp.where(lo, q2, zero), jnp.where(lo, zero, q2))
        m_sc[...] = jnp.full(m_sc.shape, NEG, f32)
        l_sc[...] = jnp.zeros_like(l_sc)
        acc_sc[...] = jnp.zeros_like(acc_sc)
        res_sc[...] = jnp.zeros_like(res_sc)

        def block(j, masked):
            off = pl.multiple_of(j * tq, tq)
            k2 = k_ref[pl.ds(off, tq), :]
            v2 = v_ref[pl.ds(off, tq), :]
            for a in range(2):
                s = _dot_nt(qa[a], k2) - ck_ref[0, a:a + 1, pl.ds(off, tq)]
                if masked:
                    s = _causal(s, tq)
                m_old = m_sc[a]
                m_new = jnp.maximum(m_old, jnp.max(s, axis=1, keepdims=True))
                alpha = jnp.exp(m_old - m_new)
                pm = jnp.exp(s - m_new)
                pb = pm.astype(bf16)
                pr = (pm - pb.astype(f32)).astype(bf16)
                l_sc[a] = alpha * l_sc[a] + jnp.sum(pm, axis=1, keepdims=True)
                acc_sc[a] = alpha * acc_sc[a] + jnp.dot(pb, v2, preferred_element_type=f32)
                res_sc[a] = alpha * res_sc[a] + jnp.dot(pr, v2, preferred_element_type=f32)
                m_sc[a] = m_new

        def step(j, carry):
            block(j, False)
            return carry

        lax.fori_loop(0, i, step, 0)
        block(i, True)
        o_ref[...] = jnp.where(lo, acc_sc[0] / l_sc[0], acc_sc[1] / l_sc[1]).astype(bf16)
        o32_ref[...] = jnp.where(lo, (acc_sc[0] + res_sc[0]) / l_sc[0], (acc_sc[1] + res_sc[1]) / l_sc[1])
        lse0 = m_sc[0] + jnp.log(l_sc[0])
        lse1 = m_sc[1] + jnp.log(l_sc[1])
        st_ref[0] = jnp.where(lane == 0, lse0, jnp.where(lane == 1, lse1, 0.0))

    full = lambda blk: pl.BlockSpec((T, LANES), blk)
    return pl.pallas_call(
        body, grid=(NP, T // tq),
        in_specs=[pl.BlockSpec((tq, LANES), lambda p, i: (i, p)), full(lambda p, i: (0, p)), full(lambda p, i: (0, p)),
                  pl.BlockSpec((1, 2, T), lambda p, i: (p, 0, 0))],
        out_specs=[pl.BlockSpec((tq, LANES), lambda p, i: (i, p)), pl.BlockSpec((tq, LANES), lambda p, i: (i, p)),
                   pl.BlockSpec((1, tq, LANES), lambda p, i: (p, i, 0))],
        out_shape=[SDS((T, H * HD), bf16), SDS((T, H * HD), f32), SDS((NP, T, LANES), f32)],
        scratch_shapes=[pltpu.VMEM((2, tq, 1), f32), pltpu.VMEM((2, tq, 1), f32), pltpu.VMEM((2, tq, LANES), f32),
                        pltpu.VMEM((2, tq, LANES), f32)],
        compiler_params=_cparams(2), name=name)(q, k, v, cum_t)


def _attn_stats(do, o, lse, name):
    T = do.shape[0]
    tm = min(TM, T)

    def body(do_ref, o_ref, lse_ref, st_ref):
        lane = lax.broadcasted_iota(jnp.int32, (1, LANES), 1)
        prod = do_ref[...].astype(f32) * o_ref[...].astype(f32)
        d0 = jnp.sum(jnp.where(lane < HD, prod, 0.0), axis=1, keepdims=True)
        d1 = jnp.sum(jnp.where(lane < HD, 0.0, prod), axis=1, keepdims=True)
        st_ref[0] = jnp.where(lane < 2, lse_ref[0], jnp.where(lane == 2, d0, jnp.where(lane == 3, d1, 0.0)))

    til = pl.BlockSpec((tm, LANES), lambda p, i: (i, p))
    stt = pl.BlockSpec((1, tm, LANES), lambda p, i: (p, i, 0))
    return pl.pallas_call(
        body, grid=(NP, T // tm), in_specs=[til, til, stt], out_specs=stt, out_shape=SDS((NP, T, LANES), f32),
        compiler_params=_cparams(2), name=name)(do, o, lse)


def _attn_bwd(q, k, v, do, st, cum_t, name):
    T = q.shape[0]
    tq = min(TQ, T)
    n = T // tq

    def body(q_ref, k_ref, v_ref, do_ref, st_ref, ck_ref, dq_ref, dk_ref, dv_ref, dck_ref, dk_sc, dv_sc, dck_sc):
        j = pl.program_id(1)
        lane = lax.broadcasted_iota(jnp.int32, (1, LANES), 1)
        lo = lane < HD

        @pl.when(j == 0)
        def _():
            dq_ref[...] = jnp.zeros_like(dq_ref)

        k2 = k_ref[...]
        v2 = v_ref[...]
        zero = jnp.zeros_like(k2)
        ka = (jnp.where(lo, k2, zero), jnp.where(lo, zero, k2))
        va = (jnp.where(lo, v2, zero), jnp.where(lo, zero, v2))
        dk_sc[...] = jnp.zeros_like(dk_sc)
        dv_sc[...] = jnp.zeros_like(dv_sc)
        dck_sc[...] = jnp.zeros_like(dck_sc)

        def block(i, masked):
            off = pl.multiple_of(i * tq, tq)
            q2 = q_ref[pl.ds(off, tq), :]
            do2 = do_ref[pl.ds(off, tq), :]
            stt = st_ref[0, pl.ds(off, tq), :]
            parts = []
            for a in range(2):
                s = _dot_nt(q2, ka[a]) - ck_ref[0, a:a + 1, :]
                if masked:
                    s = _causal(s, tq)
                pm = jnp.exp(s - stt[:, a:a + 1])
                dp = _dot_nt(do2, va[a])
                dsm = pm * (dp - stt[:, 2 + a:3 + a])
                dsb = dsm.astype(bf16)
                dv_sc[a] += _dot_tn(pm.astype(bf16), do2)
                dk_sc[a] += _dot_tn(dsb, q2)
                dck_sc[a:a + 1, :] -= jnp.sum(dsm, axis=0, keepdims=True)
                parts.append(jnp.dot(dsb, k2, preferred_element_type=f32))
            dq_ref[pl.ds(off, tq), :] += jnp.where(lo, parts[0], parts[1])

        block(j, True)

        def step(i, carry):
            block(i, False)
            return carry

        lax.fori_loop(j + 1, n, step, 0)
        dk_ref[...] = jnp.where(lo, dk_sc[0], dk_sc[1]).astype(bf16)
        dv_ref[...] = jnp.where(lo, dv_sc[0], dv_sc[1]).astype(bf16)
        dck_ref[0] = dck_sc[0:2, :]

        @pl.when(j == n - 1)
        def _():
            dq_ref[...] = dq_ref[...] * (HD ** -0.5)

    full = lambda: pl.BlockSpec((T, LANES), lambda p, j: (0, p))
    kvb = lambda: pl.BlockSpec((tq, LANES), lambda p, j: (j, p))
    ckb = lambda: pl.BlockSpec((1, 2, tq), lambda p, j: (p, 0, j))
    return pl.pallas_call(
        body, grid=(NP, n),
        in_specs=[full(), kvb(), kvb(), full(), pl.BlockSpec((1, T, LANES), lambda p, j: (p, 0, 0)), ckb()],
        out_specs=[full(), kvb(), kvb(), ckb()],
        out_shape=[SDS((T, H * HD), f32), SDS((T, H * HD), bf16), SDS((T, H * HD), bf16), SDS((NP, 2, T), f32)],
        scratch_shapes=[pltpu.VMEM((2, tq, LANES), f32), pltpu.VMEM((2, tq, LANES), f32), pltpu.VMEM((8, tq), f32)],
        compiler_params=_cparams(2), name=name)(q, k, v, do, st, cum_t)


def _ada_fwd(c_all, w_cat, name):
    n = w_cat.shape[1]
    tn = 256

    def body(c_ref, w_ref, o_ref):
        cv = c_ref[...]
        o_ref[...] = jnp.dot((cv * jax.nn.sigmoid(cv)).astype(bf16), w_ref[...].astype(bf16), preferred_element_type=f32)

    return pl.pallas_call(
        body, grid=(n // tn,), in_specs=[pl.BlockSpec((NDEV, D), _fix), pl.BlockSpec((D, tn), lambda i: (0, i))],
        out_specs=pl.BlockSpec((NDEV, tn), lambda i: (0, i)), out_shape=SDS((NDEV, n), f32),
        compiler_params=_cparams(1), name=name)(c_all, w_cat)


def _ada_bwd(c_all_t, dsel, name):
    n = dsel.shape[1]
    tn = 256

    def body(c_ref, d_ref, o_ref):
        cv = c_ref[...]
        ca = cv * jax.nn.sigmoid(cv)
        acc = ca[:, 0:1] * d_ref[0:1, :]
        for b in range(1, NDEV):
            acc = acc + ca[:, b:b + 1] * d_ref[b:b + 1, :]
        o_ref[...] = acc

    return pl.pallas_call(
        body, grid=(n // tn,), in_specs=[pl.BlockSpec((D, NDEV), _fix), pl.BlockSpec((NDEV, tn), lambda i: (0, i))],
        out_specs=pl.BlockSpec((D, tn), lambda i: (0, i)), out_shape=SDS((D, n), f32),
        compiler_params=_cparams(1), name=name)(c_all_t, dsel)


def _sum_parts(parts, name):
    R = parts.shape[1]

    def body(p_ref, o_ref):
        acc = p_ref[0]
        for j in range(1, NDEV):
            acc = acc + p_ref[j]
        o_ref[...] = acc

    return pl.pallas_call(body, out_shape=SDS((R, LANES), f32), name=name)(parts)


def _adamw(g_parts, w, m, v, name):
    n_parts, R, C = g_parts.shape
    tr = 256
    c1 = 1.0 / (1.0 - ADAM_B1 ** ADAM_STEP)
    c2 = 1.0 / (1.0 - ADAM_B2 ** ADAM_STEP)

    def body(g_ref, w_ref, m_ref, v_ref, go_ref, d_ref, mo_ref, vo_ref):
        g = g_ref[0].astype(f32)
        for j in range(1, n_parts):
            g = g + g_ref[j].astype(f32)
        mn = ADAM_B1 * m_ref[...] + (1.0 - ADAM_B1) * g
        vn = ADAM_B2 * v_ref[...] + (1.0 - ADAM_B2) * (g * g)
        go_ref[...] = g
        mo_ref[...] = mn
        vo_ref[...] = vn
        d_ref[...] = -ADAM_LR * ((mn * c1) / (jnp.sqrt(vn * c2) + ADAM_EPS) + ADAM_WD * w_ref[...])

    til = pl.BlockSpec((tr, C), _row)
    out = SDS((R, C), f32)
    return pl.pallas_call(
        body, grid=(R // tr,), in_specs=[pl.BlockSpec((n_parts, tr, C), lambda i: (0, i, 0)), til, til, til],
        out_specs=[til, til, til, til], out_shape=[out, out, out, out],
        compiler_params=_cparams(1), name=name)(g_parts, w, m, v)


def _pad_rows(flat, cols, mult):
    n = flat.shape[-1]
    rows = -(-n // cols)
    rows = -(-rows // mult) * mult
    pad = [(0, 0)] * (flat.ndim - 1) + [(0, rows * cols - n)]
    return jnp.pad(flat, pad).reshape(flat.shape[:-1] + (rows, cols))


def _shards_of(name, full):
    if name in ("ffn_w_in",):
        return full.reshape(2, D, NDEV, -1).transpose(2, 0, 1, 3).reshape(NDEV, -1)
    if name == "ffn_w_out":
        return full.reshape(2, NDEV, -1, D).transpose(1, 0, 2, 3).reshape(NDEV, -1)
    if name in ("conv_w_in", "kv_w"):
        return full.reshape(D, NDEV, -1).transpose(1, 0, 2).reshape(NDEV, -1)
    return full.reshape(NDEV, -1)


def _full_of(name, shards, local_shape):
    s = shards.reshape((NDEV,) + tuple(local_shape))
    if name == "ffn_w_in":
        return s.transpose(1, 2, 0, 3).reshape(2, D, -1)
    if name == "ffn_w_out":
        return s.transpose(1, 0, 2, 3).reshape(2, -1, D)
    if name in ("conv_w_in", "kv_w"):
        return s.transpose(1, 0, 2).reshape(D, -1)
    return s.reshape(-1, D)


def _split_flat(flat, shapes):
    out, off = {}, 0
    for name, shp in shapes:
        n = 1
        for d in shp:
            n *= d
        out[name] = flat[off:off + n].reshape(shp)
        off += n
    return out


def kernel(x, c, mix_norm_g, mix_ada_w, mix_ada_b, ffn_norm_g, ffn_ada_w, ffn_ada_b, ffn_w_in, ffn_w_out, conv_w_in, conv_b_in, conv_w_dw, conv_b_dw, conv_ln_g, conv_ln_b, conv_w_out, conv_b_out, kv_norm_g, kv_ada_w, kv_ada_b, kv_w, forget_b, attn_w_q, attn_w_o, final_norm_g, loss_target, m_mix_norm_g, m_mix_ada_w, m_mix_ada_b, m_ffn_norm_g, m_ffn_ada_w, m_ffn_ada_b, m_ffn_w_in, m_ffn_w_out, m_conv_w_in, m_conv_b_in, m_conv_w_dw, m_conv_b_dw, m_conv_ln_g, m_conv_ln_b, m_conv_w_out, m_conv_b_out, m_kv_norm_g, m_kv_ada_w, m_kv_ada_b, m_kv_w, m_forget_b, m_attn_w_q, m_attn_w_o, m_final_norm_g, v_mix_norm_g, v_mix_ada_w, v_mix_ada_b, v_ffn_norm_g, v_ffn_ada_w, v_ffn_ada_b, v_ffn_w_in, v_ffn_w_out, v_conv_w_in, v_conv_b_in, v_conv_w_dw, v_conv_b_dw, v_conv_ln_g, v_conv_ln_b, v_conv_w_out, v_conv_b_out, v_kv_norm_g, v_kv_ada_w, v_kv_ada_b, v_kv_w, v_forget_b, v_attn_w_q, v_attn_w_o, v_final_norm_g):
    W = dict(mix_norm_g=mix_norm_g, mix_ada_w=mix_ada_w, mix_ada_b=mix_ada_b, ffn_norm_g=ffn_norm_g, ffn_ada_w=ffn_ada_w, ffn_ada_b=ffn_ada_b, ffn_w_in=ffn_w_in, ffn_w_out=ffn_w_out, conv_w_in=conv_w_in, conv_b_in=conv_b_in, conv_w_dw=conv_w_dw, conv_b_dw=conv_b_dw, conv_ln_g=conv_ln_g, conv_ln_b=conv_ln_b, conv_w_out=conv_w_out, conv_b_out=conv_b_out, kv_norm_g=kv_norm_g, kv_ada_w=kv_ada_w, kv_ada_b=kv_ada_b, kv_w=kv_w, forget_b=forget_b, attn_w_q=attn_w_q, attn_w_o=attn_w_o, final_norm_g=final_norm_g)
    M = dict(mix_norm_g=m_mix_norm_g, mix_ada_w=m_mix_ada_w, mix_ada_b=m_mix_ada_b, ffn_norm_g=m_ffn_norm_g, ffn_ada_w=m_ffn_ada_w, ffn_ada_b=m_ffn_ada_b, ffn_w_in=m_ffn_w_in, ffn_w_out=m_ffn_w_out, conv_w_in=m_conv_w_in, conv_b_in=m_conv_b_in, conv_w_dw=m_conv_w_dw, conv_b_dw=m_conv_b_dw, conv_ln_g=m_conv_ln_g, conv_ln_b=m_conv_ln_b, conv_w_out=m_conv_w_out, conv_b_out=m_conv_b_out, kv_norm_g=m_kv_norm_g, kv_ada_w=m_kv_ada_w, kv_ada_b=m_kv_ada_b, kv_w=m_kv_w, forget_b=m_forget_b, attn_w_q=m_attn_w_q, attn_w_o=m_attn_w_o, final_norm_g=m_final_norm_g)
    V = dict(mix_norm_g=v_mix_norm_g, mix_ada_w=v_mix_ada_w, mix_ada_b=v_mix_ada_b, ffn_norm_g=v_ffn_norm_g, ffn_ada_w=v_ffn_ada_w, ffn_ada_b=v_ffn_ada_b, ffn_w_in=v_ffn_w_in, ffn_w_out=v_ffn_w_out, conv_w_in=v_conv_w_in, conv_b_in=v_conv_b_in, conv_w_dw=v_conv_w_dw, conv_b_dw=v_conv_b_dw, conv_ln_g=v_conv_ln_g, conv_ln_b=v_conv_ln_b, conv_w_out=v_conv_w_out, conv_b_out=v_conv_b_out, kv_norm_g=v_kv_norm_g, kv_ada_w=v_kv_ada_w, kv_ada_b=v_kv_ada_b, kv_w=v_kv_w, forget_b=v_forget_b, attn_w_q=v_attn_w_q, attn_w_o=v_attn_w_o, final_norm_g=v_final_norm_g)
    names = list(W)
    T = x.shape[1]
    me = _my_index()
    x0 = x[0]
    tgt = loss_target[0]
    row = lambda vct: vct.reshape(1, -1)

    small_names = ("conv_b_in", "conv_w_dw", "conv_b_dw", "conv_ln_g", "conv_ln_b", "conv_b_out")
    small_loc = jnp.concatenate([c.reshape(-1)] + [W[n].reshape(-1) for n in small_names])
    sg = _exchange(_pad_rows(small_loc, LANES, 8), True, "gather_small").reshape(NDEV, -1)
    c_all = sg[:, :D]
    off = D
    b_in = sg[:, off:off + 2 * D // NDEV].reshape(1, 2 * D); off += 2 * D // NDEV
    cl = D // NDEV
    w_dw = sg[:, off:off + KW * cl].reshape(NDEV, KW, cl).transpose(1, 0, 2).reshape(KW, D); off += KW * cl
    w_dw = jnp.pad(w_dw, ((0, HALO - KW), (0, 0))).astype(bf16).astype(f32)
    b_dw = sg[:, off:off + cl].reshape(1, D); off += cl
    ln_g = sg[:, off:off + cl].reshape(1, D); off += cl
    ln_b = sg[:, off:off + cl].reshape(1, D); off += cl
    b_out = sg[:, off:off + cl].reshape(1, D)

    w_cat = jnp.concatenate([mix_ada_w[0], mix_ada_w[1], ffn_ada_w[0], ffn_ada_w[1], kv_ada_w], axis=1)
    ada_loc = _ada_fwd(c_all, w_cat, "ada_fwd")
    ada_all = _exchange(ada_loc, True, "gather_ada")
    ada_me = lax.dynamic_index_in_dim(ada_all, me, axis=1, keepdims=False)
    ada_bias = (mix_ada_b[0], mix_ada_b[1], ffn_ada_b[0], ffn_ada_b[1], kv_ada_b)
    ada, off = [], 0
    for nl, bias in zip(ADA_LOC, ada_bias):
        full = ada_me[:, off:off + nl].reshape(-1) + bias
        ada.append([row(t) for t in jnp.split(full, full.shape[0] // D)])
        off += nl
    (sh_m0, sc_m0, gt_m0), (sh_m1, sc_m1, gt_m1), (sh_f0, sc_f0, gt_f0), (sh_f1, sc_f1, gt_f1), (sh_kv, sc_kv) = ada

    main_loc = jnp.concatenate([W[n].reshape(-1) for n in MAIN]).astype(bf16)
    main_all = _exchange(_pad_rows(main_loc, D, 256), True, "gather_weights").reshape(NDEV, -1)
    full_w, off = {}, 0
    for n in MAIN:
        sz = W[n].size
        full_w[n] = _full_of(n, main_all[:, off:off + sz], W[n].shape[1:] if W[n].shape[0] == 1 else W[n].shape)
        off += sz
    w_in_g = [full_w["ffn_w_in"][l][:, :F] for l in range(2)]
    w_in_u = [full_w["ffn_w_in"][l][:, F:] for l in range(2)]
    w_out = [full_w["ffn_w_out"][l] for l in range(2)]
    wc_a, wc_g = full_w["conv_w_in"][:, :D], full_w["conv_w_in"][:, D:]
    wc_o = full_w["conv_w_out"]
    w_k, w_v = full_w["kv_w"][:, :D], full_w["kv_w"][:, D:2 * D]
    w_f = jnp.pad(full_w["kv_w"][:, 2 * D:], ((0, 0), (0, LANES - H)))
    w_q, w_o = full_w["attn_w_q"], full_w["attn_w_o"]
    zeros_f = jnp.zeros((1, F), f32)
    zeros_d = jnp.zeros((1, D), f32)
    fb = jnp.pad(forget_b, (0, LANES - H)).reshape(1, LANES)

    h0 = _normmod(x0, row(mix_norm_g[0]), sh_m0, sc_m0, "norm_mix0")
    a0, gl0, p0 = _mm_gated(h0, wc_a, wc_g, b_in[:, :D], b_in[:, D:], False, "conv_in")
    q0, s0 = _conv_fwd(p0, w_dw, b_dw, ln_g, ln_b, "conv_dw")
    x1, y0 = _mm_res(s0, wc_o, b_out, x0, gt_m0, "conv_out")

    def ffn_fwd(xin, l, sh, sc, gt):
        h = _normmod(xin, row(ffn_norm_g[l]), sh, sc, f"norm_ffn{l}")
        ug, uu, act = _mm_gated(h, w_in_g[l], w_in_u[l], zeros_f, zeros_f, True, f"ffn_in{l}")
        xo, y = _mm_res(act, w_out[l], zeros_d, xin, gt, f"ffn_out{l}")
        return xo, (h, ug, uu, act, y)

    x2, ffn0 = ffn_fwd(x1, 0, sh_f0, sc_f0, gt_f0)

    hk = _normmod(x2, row(kv_norm_g), sh_kv, sc_kv, "norm_kv")
    k_sh = _mm(hk, w_k, bf16, 1.0, "proj_k")
    v_sh = _mm(hk, w_v, bf16, 1.0, "proj_v")
    fl = _mm(hk, w_f, f32, 1.0, "proj_f")
    cum = _forget_fwd(fl, fb, "forget_fwd")
    cum_t = cum[:, :H].T.reshape(NP, 2, T)

    h2 = _normmod(x2, row(mix_norm_g[1]), sh_m1, sc_m1, "norm_mix1")
    qh = _mm(h2, w_q, bf16, HD ** -0.5, "proj_q")
    o, o32, lse = _attn_fwd(qh, k_sh, v_sh, cum_t, "attn_fwd")
    x3, y1 = _mm_res(o, w_o, zeros_d, x2, gt_m1, "attn_out")

    x4, ffn1 = ffn_fwd(x3, 1, sh_f1, sc_f1, gt_f1)

    dx4, acc_fin = _final_bwd(x4, row(final_norm_g), tgt, "final_bwd")

    G = {}
    d_ada = {}

    def ffn_bwd(dx_out, xin, l, sc, gt, saved):
        h, ug, uu, act, y = saved
        dyb, acc_r = _res_in(dx_out, y, gt, f"ffn_res_bwd{l}")
        dug, duu = _mm_nt_swiglu(dyb, w_out[l], ug, uu, f"ffn_dact{l}")
        g_out = _mm_tn(act, dyb, f"ffn_dw_out{l}")
        g_in = jnp.concatenate([_mm_tn(h, dug, f"ffn_dw_gate{l}"), _mm_tn(h, duu, f"ffn_dw_up{l}")], axis=1)
        dh = _mm_nt([(dug, w_in_g[l]), (duu, w_in_u[l])], f32, f"ffn_dh{l}")
        dxi, acc_n = _normmod_bwd(dh, xin, row(ffn_norm_g[l]), sc, dx_out, f"norm_ffn_bwd{l}")
        return dxi, g_in, g_out, [acc_n[0:1], acc_n[1:2], acc_r[0:1]], acc_n[2]

    dx3, g_in1, g_out1, d_ada[("ffn", 1)], dg_ffn1 = ffn_bwd(dx4, x3, 1, sc_f1, gt_f1, ffn1)

    dyb, acc_r = _res_in(dx3, y1, gt_m1, "attn_res_bwd")
    do = _mm_nt([(dyb, w_o)], bf16, "attn_do")
    G["attn_w_o"] = _mm_tn(o, dyb, "attn_dw_o")
    st = _attn_stats(do, o32, lse, "attn_stats")
    dq, dk, dv, dck = _attn_bwd(qh, k_sh, v_sh, do, st, cum_t, "attn_bwd")
    G["attn_w_q"] = _mm_tn(h2, dq, "attn_dw_q")
    dh2 = _mm_nt([(dq, w_q)], f32, "attn_dh")
    dx2, acc_n = _normmod_bwd(dh2, x2, row(mix_norm_g[1]), sc_m1, dx3, "norm_mix_bwd1")
    d_ada[("mix", 1)] = [acc_n[0:1], acc_n[1:2], acc_r[0:1]]
    dg_mix1 = acc_n[2]

    dcum = jnp.pad(dck.reshape(H, T).T, ((0, 0), (0, LANES - H)))
    dfl, acc_f = _forget_bwd(dcum, fl, fb, "forget_bwd")
    G["kv_w"] = jnp.concatenate([_mm_tn(hk, dk, "kv_dw_k"), _mm_tn(hk, dv, "kv_dw_v"), _mm_tn(hk, dfl, "kv_dw_f")[:, :H]], axis=1)
    dhk = _mm_nt([(dk, w_k), (dv, w_v), (dfl, w_f)], f32, "kv_dh")
    dx2, acc_n = _normmod_bwd(dhk, x2, row(kv_norm_g), sc_kv, dx2, "norm_kv_bwd")
    d_ada[("kv", 0)] = [acc_n[0:1], acc_n[1:2]]
    dg_kv = acc_n[2]

    dx1, g_in0, g_out0, d_ada[("ffn", 0)], dg_ffn0 = ffn_bwd(dx2, x1, 0, sc_f0, gt_f0, ffn0)
    G["ffn_w_in"] = jnp.stack([g_in0, g_in1])
    G["ffn_w_out"] = jnp.stack([g_out0, g_out1])

    dyb, acc_r = _res_in(dx1, y0, gt_m0, "conv_res_bwd")
    dsw = _mm_nt([(dyb, wc_o)], f32, "conv_ds")
    G["conv_w_out"] = _mm_tn(s0, dyb, "conv_dw_out")
    da, dgl, acc_c, dw_dw = _conv_bwd(dsw, q0, p0, a0, gl0, w_dw, ln_g, ln_b, "conv_bwd")
    G["conv_w_in"] = jnp.concatenate([_mm_tn(h0, da, "conv_dw_a"), _mm_tn(h0, dgl, "conv_dw_g")], axis=1)
    dh0 = _mm_nt([(da, wc_a), (dgl, wc_g)], f32, "conv_dh")
    dx0, acc_n = _normmod_bwd(dh0, x0, row(mix_norm_g[0]), sc_m0, dx1, "norm_mix_bwd0")
    d_ada[("mix", 0)] = [acc_n[0:1], acc_n[1:2], acc_r[0:1]]
    dg_mix0 = acc_n[2]

    vec = [t.reshape(-1) for key in [(s[0], s[1]) for s in ADA_SEG] for t in d_ada[key]]
    vec += [dg_mix0, dg_mix1, dg_ffn0, dg_ffn1, dg_kv, acc_fin[0]]
    vec += [acc_f[0], acc_fin[1, :LANES]]
    vec += [acc_c[3], acc_c[4], dw_dw[:KW].reshape(-1), acc_c[2], acc_c[0], acc_c[1], acc_r[1]]
    small_parts = _exchange(_pad_rows(jnp.concatenate(vec), LANES, 8), True, "gather_partials")
    small_sum = _sum_parts(small_parts, "sum_partials").reshape(-1)
    d_ada_all = small_parts.reshape(NDEV, -1)[:, :ADA_TOT]
    off = 0
    gsm = {}
    ada_b_sum = []
    for _, _, n in ADA_SEG:
        ada_b_sum.append(small_sum[off:off + n]); off += n
    gsm["mix_ada_b"] = jnp.stack(ada_b_sum[0:2])
    gsm["ffn_ada_b"] = jnp.stack(ada_b_sum[2:4])
    gsm["kv_ada_b"] = ada_b_sum[4]
    gsm["mix_norm_g"] = small_sum[off:off + 2 * D].reshape(2, D); off += 2 * D
    gsm["ffn_norm_g"] = small_sum[off:off + 2 * D].reshape(2, D); off += 2 * D
    gsm["kv_norm_g"] = small_sum[off:off + D]; off += D
    gsm["final_norm_g"] = small_sum[off:off + D]; off += D
    gsm["forget_b"] = small_sum[off:off + H]; off += LANES
    loss = small_sum[off]; off += LANES
    sl = lambda full, width: lax.dynamic_slice_in_dim(full, me * width, width, axis=full.ndim - 1)
    gsm["conv_b_in"] = sl(small_sum[off:off + 2 * D].reshape(1, 2 * D), 2 * D // NDEV); off += 2 * D
    gsm["conv_w_dw"] = sl(small_sum[off:off + KW * D].reshape(1, KW, D), cl); off += KW * D
    for n in ("conv_b_dw", "conv_ln_g", "conv_ln_b", "conv_b_out"):
        gsm[n] = sl(small_sum[off:off + D].reshape(1, D), cl); off += D

    dsel, off = [], 0
    for (_, _, n), nl in zip(ADA_SEG, ADA_LOC):
        dsel.append(lax.dynamic_slice_in_dim(d_ada_all[:, off:off + n], me * nl, nl, axis=1)); off += n
    g_ada = _ada_bwd(c_all.T, jnp.concatenate(dsel, axis=1), "ada_bwd")
    gsm["mix_ada_w"] = jnp.stack([g_ada[:, 0:ADA_LOC[0]], g_ada[:, ADA_LOC[0]:2 * ADA_LOC[0]]])
    o2 = 2 * ADA_LOC[0]
    gsm["ffn_ada_w"] = jnp.stack([g_ada[:, o2:o2 + ADA_LOC[2]], g_ada[:, o2 + ADA_LOC[2]:o2 + 2 * ADA_LOC[2]]])
    gsm["kv_ada_w"] = g_ada[:, o2 + 2 * ADA_LOC[2]:]

    g_pack = jnp.concatenate([_shards_of(n, G[n]) for n in MAIN], axis=1).astype(bf16)
    g_recv = _exchange(_pad_rows(g_pack, D, 256), False, "scatter_grads")
    pack_main = lambda src: _pad_rows(jnp.concatenate([src[n].reshape(-1) for n in MAIN]), D, 256)
    res_main = _adamw(g_recv, pack_main(W), pack_main(M), pack_main(V), "adamw_main")
    rest = [n for n in names if n not in MAIN]
    pack_rest = lambda src: _pad_rows(jnp.concatenate([src[n].reshape(-1) for n in rest]), D, 256)
    res_rest = _adamw(pack_rest(gsm)[None], pack_rest(W), pack_rest(M), pack_rest(V), "adamw_rest")

    outs = []
    for k in range(4):
        um = _split_flat(res_main[k].reshape(-1), [(n, W[n].shape) for n in MAIN])
        ur = _split_flat(res_rest[k].reshape(-1), [(n, W[n].shape) for n in rest])
        outs.append({**um, **ur})
    grads, deltas, new_m, new_v = outs
    return (loss, dx0[None], *[grads[n] for n in names], *[deltas[n] for n in names],
            *[new_m[n] for n in names], *[new_v[n] for n in names])
```

```python
import functools

import jax
import jax.numpy as jnp
from jax import lax
from jax.experimental import pallas as pl
from jax.experimental.pallas import tpu as pltpu

f32, bf16 = jnp.float32, jnp.bfloat16
SDS = jax.ShapeDtypeStruct

D = 1024
F = 2816
H = 16
HD = 64
NP = H // 2
KW = 31
HALO = 32
NDEV = 8
FS = 2 * F // NDEV
FSP = 768
FP = 4 * FSP
EPS = 1e-6
NEG = -1e30
LANES = 128

ADAM_LR, ADAM_B1, ADAM_B2, ADAM_EPS, ADAM_WD, ADAM_STEP = 0.001, 0.9, 0.999, 1e-08, 0.01, 10

TM = 512
TC = 256
TQ = 512
VMEM_LIMIT = 56 << 20

MAIN = ("ffn_w_in", "ffn_w_out", "conv_w_in", "conv_w_out", "kv_w", "attn_w_q", "attn_w_o")
ADA_SEG = (("mix", 0, 3 * D), ("mix", 1, 3 * D), ("ffn", 0, 3 * D), ("ffn", 1, 3 * D), ("kv", 0, 2 * D))
ADA_LOC = tuple(n // NDEV for _, _, n in ADA_SEG)
ADA_COLS = sum(ADA_LOC)
ADA_TOT = sum(n for _, _, n in ADA_SEG)


def _cparams(n_axes):
    return pltpu.CompilerParams(dimension_semantics=("arbitrary",) * n_axes, vmem_limit_bytes=VMEM_LIMIT)


def _mesh_pos():
    return lax.axis_index("x"), lax.axis_index("y"), lax.axis_index("c")


def _my_index():
    mx, my, mc = _mesh_pos()
    return 4 * mx + 2 * my + mc


def _peer(k, mx, my, mc):
    px = (1 - mx) if k & 4 else mx
    py = (1 - my) if k & 2 else my
    pc = (1 - mc) if k & 1 else mc
    return (px, py, pc), 4 * px + 2 * py + pc


def _exchange(xs, gather, name):
    n = len(xs)

    def body(*refs):
        x_refs, o_refs = refs[:n], refs[n:2 * n]
        send_sems, recv_sems, local_sems = refs[2 * n:]
        mx, my, mc = _mesh_pos()
        me = 4 * mx + 2 * my + mc
        copies = []
        for a in range(n):
            x_ref, o_ref = x_refs[a], o_refs[a]
            mine = pltpu.make_async_copy(x_ref if gather else x_ref.at[me], o_ref.at[me], local_sems.at[a])
            mine.start()
            copies.append(mine)
            for k in range(1, NDEV):
                peer, pidx = _peer(k, mx, my, mc)
                sem = a * (NDEV - 1) + k - 1
                cp = pltpu.make_async_remote_copy(
                    src_ref=x_ref if gather else x_ref.at[pidx], dst_ref=o_ref.at[me],
                    send_sem=send_sems.at[sem], recv_sem=recv_sems.at[sem],
                    device_id=peer, device_id_type=pl.DeviceIdType.MESH)
                cp.start()
                copies.append(cp)
        for cp in copies:
            cp.wait()

    hbm = pl.BlockSpec(memory_space=pl.ANY)
    return pl.pallas_call(
        body, out_shape=[SDS((NDEV,) + tuple(x.shape if gather else x.shape[1:]), x.dtype) for x in xs],
        in_specs=[hbm] * n, out_specs=[hbm] * n,
        scratch_shapes=[pltpu.SemaphoreType.DMA((n * (NDEV - 1),)), pltpu.SemaphoreType.DMA((n * (NDEV - 1),)),
                        pltpu.SemaphoreType.DMA((n,))],
        name=name)(*xs)


def _row(i):
    return (i, 0)


def _fix(i):
    return (0, 0)


def _normmod(x, g, shift, scale, name):
    T = x.shape[0]
    tm = min(TM, T)

    def body(x_ref, g_ref, sh_ref, sc_ref, h_ref):
        xv = x_ref[...]
        r = lax.rsqrt(jnp.mean(xv * xv, axis=-1, keepdims=True) + EPS)
        hn = (xv * r) * g_ref[...]
        h_ref[...] = (hn * (1.0 + sc_ref[...]) + sh_ref[...]).astype(bf16)

    vec = pl.BlockSpec((1, D), _fix)
    return pl.pallas_call(
        body, grid=(T // tm,), in_specs=[pl.BlockSpec((tm, D), _row), vec, vec, vec],
        out_specs=pl.BlockSpec((tm, D), _row), out_shape=SDS((T, D), bf16),
        compiler_params=_cparams(1), name=name)(x, g, shift, scale)


def _normmod_bwd(dh, x, g, scale, dx_res, name):
    T = x.shape[0]
    tm = min(TM, T)

    def body(dh_ref, x_ref, g_ref, sc_ref, res_ref, dx_ref, acc_ref):
        @pl.when(pl.program_id(0) == 0)
        def _():
            acc_ref[...] = jnp.zeros_like(acc_ref)
        xv = x_ref[...]
        dhv = dh_ref[...]
        gv = g_ref[...]
        r = lax.rsqrt(jnp.mean(xv * xv, axis=-1, keepdims=True) + EPS)
        xn = xv * r
        dhn = dhv * (1.0 + sc_ref[...])
        dxn = dhn * gv
        dx_ref[...] = res_ref[...] + r * (dxn - xn * jnp.mean(dxn * xn, axis=-1, keepdims=True))
        acc_ref[0:1, :] += jnp.sum(dhv, axis=0, keepdims=True)
        acc_ref[1:2, :] += jnp.sum(dhv * (xn * gv), axis=0, keepdims=True)
        acc_ref[2:3, :] += jnp.sum(dhn * xn, axis=0, keepdims=True)

    vec = pl.BlockSpec((1, D), _fix)
    til = pl.BlockSpec((tm, D), _row)
    return pl.pallas_call(
        body, grid=(T // tm,), in_specs=[til, til, vec, vec, til],
        out_specs=[til, pl.BlockSpec((8, D), _fix)], out_shape=[SDS((T, D), f32), SDS((8, D), f32)],
        compiler_params=_cparams(1), name=name)(dh, x, g, scale, dx_res)


def _res_in(dx, y, gate, name):
    T = dx.shape[0]
    tm = min(TM, T)

    def body(dx_ref, y_ref, gt_ref, dy_ref, acc_ref):
        @pl.when(pl.program_id(0) == 0)
        def _():
            acc_ref[...] = jnp.zeros_like(acc_ref)
        dxv = dx_ref[...]
        dy = dxv * gt_ref[...]
        dy_ref[...] = dy.astype(bf16)
        acc_ref[0:1, :] += jnp.sum(dxv * y_ref[...].astype(f32), axis=0, keepdims=True)
        acc_ref[1:2, :] += jnp.sum(dy, axis=0, keepdims=True)

    til = pl.BlockSpec((tm, D), _row)
    return pl.pallas_call(
        body, grid=(T // tm,), in_specs=[til, til, pl.BlockSpec((1, D), _fix)],
        out_specs=[til, pl.BlockSpec((8, D), _fix)], out_shape=[SDS((T, D), bf16), SDS((8, D), f32)],
        compiler_params=_cparams(1), name=name)(dx, y, gate)


def _final_bwd(x, g, tgt, name):
    T = x.shape[0]
    tm = min(TM, T)

    def body(x_ref, g_ref, t_ref, dx_ref, acc_ref):
        @pl.when(pl.program_id(0) == 0)
        def _():
            acc_ref[...] = jnp.zeros_like(acc_ref)
        xv = x_ref[...]
        gv = g_ref[...]
        r = lax.rsqrt(jnp.mean(xv * xv, axis=-1, keepdims=True) + EPS)
        xn = xv * r
        err = xn * gv - t_ref[...]
        dy = err * (1.0 / D)
        dxn = dy * gv
        dx_ref[...] = r * (dxn - xn * jnp.mean(dxn * xn, axis=-1, keepdims=True))
        acc_ref[0:1, :] += jnp.sum(dy * xn, axis=0, keepdims=True)
        acc_ref[1:2, :] += 0.5 * jnp.sum(jnp.mean(err * err, axis=-1, keepdims=True))

    til = pl.BlockSpec((tm, D), _row)
    return pl.pallas_call(
        body, grid=(T // tm,), in_specs=[til, pl.BlockSpec((1, D), _fix), til],
        out_specs=[til, pl.BlockSpec((8, D), _fix)], out_shape=[SDS((T, D), f32), SDS((8, D), f32)],
        compiler_params=_cparams(1), name=name)(x, g, tgt)


def _col_tile(n):
    if n <= 1024:
        return min(n, 512) if n % 512 == 0 else n
    return 1408 if n % 1408 == 0 else 1024


def _mm_gated(h, wa, wb, ba, bb, swiglu, name):
    T, K = h.shape
    N = wa.shape[1]
    tm, tn = min(TM, T), _col_tile(N)

    def body(h_ref, wa_ref, wb_ref, ba_ref, bb_ref, u_ref, w_ref, p_ref):
        hv = h_ref[...]
        u = jnp.dot(hv, wa_ref[...], preferred_element_type=f32) + ba_ref[...]
        w = jnp.dot(hv, wb_ref[...], preferred_element_type=f32) + bb_ref[...]
        u_ref[...] = u
        w_ref[...] = w
        if swiglu:
            p_ref[...] = ((u * jax.nn.sigmoid(u)) * w).astype(p_ref.dtype)
        else:
            p_ref[...] = (u * jax.nn.sigmoid(w)).astype(p_ref.dtype)

    wsp = pl.BlockSpec((K, tn), lambda i, j: (0, j))
    bsp = pl.BlockSpec((1, tn), lambda i, j: (0, j))
    osp = pl.BlockSpec((tm, tn), lambda i, j: (i, j))
    return pl.pallas_call(
        body, grid=(T // tm, N // tn), in_specs=[pl.BlockSpec((tm, K), lambda i, j: (i, 0)), wsp, wsp, bsp, bsp],
        out_specs=[osp, osp, osp], out_shape=[SDS((T, N), f32), SDS((T, N), f32), SDS((T, N), bf16)],
        compiler_params=_cparams(2), name=name)(h, wa, wb, ba, bb)


def _mm_res(a, w, b, x_in, gate, name):
    T, K = a.shape
    N = w.shape[1]
    tm, tn = min(TM, T), _col_tile(N)

    def body(a_ref, w_ref, b_ref, x_ref, gt_ref, xo_ref, y_ref):
        y = jnp.dot(a_ref[...], w_ref[...], preferred_element_type=f32) + b_ref[...]
        y_ref[...] = y.astype(bf16)
        xo_ref[...] = x_ref[...] + gt_ref[...] * y

    vsp = pl.BlockSpec((1, tn), lambda i, j: (0, j))
    osp = pl.BlockSpec((tm, tn), lambda i, j: (i, j))
    return pl.pallas_call(
        body, grid=(T // tm, N // tn),
        in_specs=[pl.BlockSpec((tm, K), lambda i, j: (i, 0)), pl.BlockSpec((K, tn), lambda i, j: (0, j)), vsp, osp, vsp],
        out_specs=[osp, osp], out_shape=[SDS((T, N), f32), SDS((T, N), bf16)],
        compiler_params=_cparams(2), name=name)(a, w, b, x_in, gate)


def _mm(a, w, out_dtype, out_scale, name):
    T, K = a.shape
    N = w.shape[1]
    tm, tn = min(TM, T), _col_tile(N)

    def body(a_ref, w_ref, o_ref):
        y = jnp.dot(a_ref[...], w_ref[...], preferred_element_type=f32)
        if out_scale != 1.0:
            y = y * out_scale
        o_ref[...] = y.astype(out_dtype)

    return pl.pallas_call(
        body, grid=(T // tm, N // tn),
        in_specs=[pl.BlockSpec((tm, K), lambda i, j: (i, 0)), pl.BlockSpec((K, tn), lambda i, j: (0, j))],
        out_specs=pl.BlockSpec((tm, tn), lambda i, j: (i, j)), out_shape=SDS((T, N), out_dtype),
        compiler_params=_cparams(2), name=name)(a, w)


def _dot_nt(a, b):
    return lax.dot_general(a, b, (((1,), (1,)), ((), ())), preferred_element_type=f32)


def _dot_tn(a, b):
    return lax.dot_general(a, b, (((0,), (0,)), ((), ())), preferred_element_type=f32)


def _mm_nt(pairs, out_dtype, name):
    T = pairs[0][0].shape[0]
    K = pairs[0][1].shape[0]
    tm, tk = min(TM, T), _col_tile(K)
    n = len(pairs)

    def body(*refs):
        o_ref = refs[2 * n]
        acc = None
        for i in range(n):
            part = _dot_nt(refs[2 * i][...].astype(bf16), refs[2 * i + 1][...])
            acc = part if acc is None else acc + part
        o_ref[...] = acc.astype(out_dtype)

    in_specs, args = [], []
    for dy, w in pairs:
        ni = dy.shape[1]
        in_specs += [pl.BlockSpec((tm, ni), lambda i, j: (i, 0)), pl.BlockSpec((tk, ni), lambda i, j: (j, 0))]
        args += [dy, w]
    return pl.pallas_call(
        body, grid=(T // tm, K // tk), in_specs=in_specs,
        out_specs=pl.BlockSpec((tm, tk), lambda i, j: (i, j)), out_shape=SDS((T, K), out_dtype),
        compiler_params=_cparams(2), name=name)(*args)


def _mm_nt_swiglu(dy, w, ug, uu, name):
    T, N = dy.shape
    K = w.shape[0]
    tm, tk = min(TM, T), _col_tile(K)

    def body(dy_ref, w_ref, ug_ref, uu_ref, dug_ref, duu_ref):
        dact = _dot_nt(dy_ref[...], w_ref[...])
        g = ug_ref[...].astype(f32)
        u = uu_ref[...].astype(f32)
        sg = jax.nn.sigmoid(g)
        duu_ref[...] = (dact * (g * sg)).astype(bf16)
        dug_ref[...] = (dact * u * (sg * (1.0 + g * (1.0 - sg)))).astype(bf16)

    osp = pl.BlockSpec((tm, tk), lambda i, j: (i, j))
    return pl.pallas_call(
        body, grid=(T // tm, K // tk),
        in_specs=[pl.BlockSpec((tm, N), lambda i, j: (i, 0)), pl.BlockSpec((tk, N), lambda i, j: (j, 0)), osp, osp],
        out_specs=[osp, osp], out_shape=[SDS((T, K), bf16), SDS((T, K), bf16)],
        compiler_params=_cparams(2), name=name)(dy, w, ug, uu)


def _mm_tn(a, b, name, shard_cols=0):
    T, K = a.shape
    N = b.shape[1]
    tt = min(TM, T)
    tk = K if K <= 1024 else _col_tile(K)
    tn = shard_cols or (N if N <= 1024 else _col_tile(N))
    nt = T // tt

    def body(a_ref, b_ref, o_ref):
        @pl.when(pl.program_id(2) == 0)
        def _():
            o_ref[...] = jnp.zeros_like(o_ref)
        o_ref[...] += _dot_tn(a_ref[...].astype(bf16), b_ref[...].astype(bf16))

    if shard_cols:
        out_spec, out_shape = pl.BlockSpec((None, tk, tn), lambda i, j, t: (j, i, 0)), SDS((N // tn, K, tn), f32)
    else:
        out_spec, out_shape = pl.BlockSpec((tk, tn), lambda i, j, t: (i, j)), SDS((K, N), f32)
    return pl.pallas_call(
        body, grid=(K // tk, N // tn, nt),
        in_specs=[pl.BlockSpec((tt, tk), lambda i, j, t: (t, i)), pl.BlockSpec((tt, tn), lambda i, j, t: (t, j))],
        out_specs=out_spec, out_shape=out_shape,
        compiler_params=_cparams(3), name=name)(a, b)


def _ffn_in(h, w_sh, layer, name):
    T = h.shape[0]
    tm = min(TM, T)

    def body(h_ref, wg_ref, wu_ref, ug_ref, uu_ref, act_ref):
        hv = h_ref[...]
        ug = jnp.dot(hv, wg_ref[...], preferred_element_type=f32)
        uu = jnp.dot(hv, wu_ref[...], preferred_element_type=f32)
        ug_ref[...] = ug
        uu_ref[...] = uu
        act_ref[...] = ((ug * jax.nn.sigmoid(ug)) * uu).astype(bf16)

    osp = pl.BlockSpec((tm, FSP), lambda i, j: (i, j))
    return pl.pallas_call(
        body, grid=(T // tm, NDEV // 2),
        in_specs=[pl.BlockSpec((tm, D), lambda i, j: (i, 0)),
                  pl.BlockSpec((None, None, D, FSP), lambda i, j: (j, layer, 0, 0)),
                  pl.BlockSpec((None, None, D, FSP), lambda i, j: (j + NDEV // 2, layer, 0, 0))],
        out_specs=[osp, osp, osp], out_shape=[SDS((T, FP), f32), SDS((T, FP), f32), SDS((T, FP), bf16)],
        compiler_params=_cparams(2), name=name)(h, w_sh, w_sh)


def _ffn_dh(dug, duu, w_sh, layer, name):
    T = dug.shape[0]
    tm, tk = min(TM, T), 512
    half = NDEV // 2

    def body(dg_ref, du_ref, wg_ref, wu_ref, o_ref):
        acc = None
        for s in range(half):
            cols = slice(s * FSP, (s + 1) * FSP)
            part = _dot_nt(dg_ref[:, cols], wg_ref[s]) + _dot_nt(du_ref[:, cols], wu_ref[s])
            acc = part if acc is None else acc + part
        o_ref[...] = acc

    dsp = pl.BlockSpec((tm, FP), lambda i, j: (i, 0))
    return pl.pallas_call(
        body, grid=(T // tm, D // tk),
        in_specs=[dsp, dsp, pl.BlockSpec((half, None, tk, FSP), lambda i, j: (0, layer, j, 0)),
                  pl.BlockSpec((half, None, tk, FSP), lambda i, j: (1, layer, j, 0))],
        out_specs=pl.BlockSpec((tm, tk), lambda i, j: (i, j)), out_shape=SDS((T, D), f32),
        compiler_params=_cparams(2), name=name)(dug, duu, w_sh, w_sh)


def _layernorm_parts(qv, g, b):
    mu = jnp.mean(qv, axis=-1, keepdims=True)
    cen = qv - mu
    rstd = lax.rsqrt(jnp.mean(cen * cen, axis=-1, keepdims=True) + EPS)
    z = cen * rstd
    return z, rstd, z * g + b


def _conv_fwd(p, w_dw, b_dw, ln_g, ln_b, name):
    T = p.shape[0]
    tc = min(TC, T)

    def body(p_ref, w_ref, b_ref, g_ref, bb_ref, q_ref, s_ref, ext):
        i = pl.program_id(0)

        @pl.when(i == 0)
        def _():
            ext[0:HALO, :] = jnp.zeros((HALO, D), f32)

        @pl.when(i > 0)
        def _():
            ext[0:HALO, :] = ext[tc:tc + HALO, :]

        ext[HALO:HALO + tc, :] = p_ref[...].astype(f32)
        for cb in range(D // LANES):
            cols = slice(cb * LANES, (cb + 1) * LANES)
            acc = jnp.zeros((tc, LANES), f32)
            for k in range(KW):
                acc = acc + w_ref[k:k + 1, cols] * ext[pl.ds(HALO - (KW - 1) + k, tc), cols]
            q_ref[:, cols] = acc + b_ref[:, cols]
        _, _, l = _layernorm_parts(q_ref[...], g_ref[...], bb_ref[...])
        s_ref[...] = (l * jax.nn.sigmoid(l)).astype(bf16)

    vec = pl.BlockSpec((1, D), _fix)
    til = pl.BlockSpec((tc, D), _row)
    return pl.pallas_call(
        body, grid=(T // tc,), in_specs=[til, pl.BlockSpec((HALO, D), _fix), vec, vec, vec],
        out_specs=[til, til], out_shape=[SDS((T, D), f32), SDS((T, D), bf16)],
        scratch_shapes=[pltpu.VMEM((tc + HALO, D), f32)], compiler_params=_cparams(1), name=name)(p, w_dw, b_dw, ln_g, ln_b)


def _conv_bwd(ds, q, p, a, gl, w_dw, ln_g, ln_b, name):
    T = q.shape[0]
    tc = min(TC, T)
    n = T // tc

    def body(ds_ref, q_ref, p_ref, a_ref, gl_ref, w_ref, g_ref, bb_ref, da_ref, dgl_ref, acc_ref, dw_ref, ext):
        i = pl.program_id(0)

        @pl.when(i == 0)
        def _():
            acc_ref[...] = jnp.zeros_like(acc_ref)
            dw_ref[...] = jnp.zeros_like(dw_ref)
            ext[tc:tc + HALO, :] = jnp.zeros((HALO, D), f32)

        @pl.when(i > 0)
        def _():
            ext[tc:tc + HALO, :] = ext[0:HALO, :]

        gv = g_ref[...]
        z, rstd, l = _layernorm_parts(q_ref[...], gv, bb_ref[...])
        sg = jax.nn.sigmoid(l)
        dl = ds_ref[...] * (sg * (1.0 + l * (1.0 - sg)))
        dz = dl * gv
        dq = rstd * (dz - jnp.mean(dz, axis=-1, keepdims=True) - z * jnp.mean(dz * z, axis=-1, keepdims=True))
        ext[0:tc, :] = dq.astype(bf16).astype(f32)
        acc_ref[0:1, :] += jnp.sum(dl * z, axis=0, keepdims=True)
        acc_ref[1:2, :] += jnp.sum(dl, axis=0, keepdims=True)
        acc_ref[2:3, :] += jnp.sum(dq, axis=0, keepdims=True)
        for cb in range(D // LANES):
            cols = slice(cb * LANES, (cb + 1) * LANES)
            pc = p_ref[:, cols].astype(f32)
            dp = jnp.zeros((tc, LANES), f32)
            for k in range(KW):
                sl = ext[pl.ds(KW - 1 - k, tc), cols]
                dp = dp + w_ref[k:k + 1, cols] * sl
                dw_ref[k:k + 1, cols] += jnp.sum(sl * pc, axis=0, keepdims=True)
            av = a_ref[:, cols].astype(f32)
            sgl = jax.nn.sigmoid(gl_ref[:, cols].astype(f32))
            da = dp * sgl
            dgl = dp * av * (sgl * (1.0 - sgl))
            da_ref[:, cols] = da.astype(bf16)
            dgl_ref[:, cols] = dgl.astype(bf16)
            acc_ref[3:4, cols] += jnp.sum(da, axis=0, keepdims=True)
            acc_ref[4:5, cols] += jnp.sum(dgl, axis=0, keepdims=True)

    rev = lambda i: (n - 1 - i, 0)
    til = pl.BlockSpec((tc, D), rev)
    vec = pl.BlockSpec((1, D), _fix)
    return pl.pallas_call(
        body, grid=(n,), in_specs=[til, til, til, til, til, pl.BlockSpec((HALO, D), _fix), vec, vec],
        out_specs=[til, til, pl.BlockSpec((8, D), _fix), pl.BlockSpec((HALO, D), _fix)],
        out_shape=[SDS((T, D), bf16), SDS((T, D), bf16), SDS((8, D), f32), SDS((HALO, D), f32)],
        scratch_shapes=[pltpu.VMEM((tc + HALO, D), f32)], compiler_params=_cparams(1), name=name)(ds, q, p, a, gl, w_dw, ln_g, ln_b)


def _tri(n, upper):
    r = lax.broadcasted_iota(jnp.int32, (n, n), 0)
    c = lax.broadcasted_iota(jnp.int32, (n, n), 1)
    return ((c >= r) if upper else (r >= c)).astype(f32)


def _forget_fwd(fl, fb, name):
    T = fl.shape[0]
    tc = min(TC, T)

    def body(fl_ref, fb_ref, cum_ref, carry):
        @pl.when(pl.program_id(0) == 0)
        def _():
            carry[...] = jnp.zeros_like(carry)
        xv = fl_ref[...] + fb_ref[...]
        lf = jnp.minimum(xv, 0.0) - jnp.log(1.0 + jnp.exp(-jnp.abs(xv)))
        cs = jnp.dot(_tri(tc, False), lf, preferred_element_type=f32, precision=lax.Precision.HIGHEST) + carry[0:1, :]
        cum_ref[...] = cs
        carry[0:1, :] = cs[tc - 1:tc, :]

    til = pl.BlockSpec((tc, LANES), _row)
    return pl.pallas_call(
        body, grid=(T // tc,), in_specs=[til, pl.BlockSpec((1, LANES), _fix)], out_specs=til,
        out_shape=SDS((T, LANES), f32), scratch_shapes=[pltpu.VMEM((8, LANES), f32)],
        compiler_params=_cparams(1), name=name)(fl, fb)


def _forget_bwd(dcum, fl, fb, name):
    T = fl.shape[0]
    tc = min(TC, T)
    n = T // tc

    def body(dc_ref, fl_ref, fb_ref, dfl_ref, acc_ref, carry):
        @pl.when(pl.program_id(0) == 0)
        def _():
            carry[...] = jnp.zeros_like(carry)
            acc_ref[...] = jnp.zeros_like(acc_ref)
        dlf = jnp.dot(_tri(tc, True), dc_ref[...], preferred_element_type=f32, precision=lax.Precision.HIGHEST) + carry[0:1, :]
        carry[0:1, :] = dlf[0:1, :]
        dfl = dlf * (1.0 - jax.nn.sigmoid(fl_ref[...] + fb_ref[...]))
        dfl_ref[...] = dfl.astype(bf16)
        acc_ref[0:1, :] += jnp.sum(dfl, axis=0, keepdims=True)

    rev = lambda i: (n - 1 - i, 0)
    til = pl.BlockSpec((tc, LANES), rev)
    return pl.pallas_call(
        body, grid=(n,), in_specs=[til, til, pl.BlockSpec((1, LANES), _fix)],
        out_specs=[til, pl.BlockSpec((8, LANES), _fix)], out_shape=[SDS((T, LANES), bf16), SDS((8, LANES), f32)],
        scratch_shapes=[pltpu.VMEM((8, LANES), f32)], compiler_params=_cparams(1), name=name)(dcum, fl, fb)


def _causal(s, n):
    r = lax.broadcasted_iota(jnp.int32, (n, n), 0)
    c = lax.broadcasted_iota(jnp.int32, (n, n), 1)
    return jnp.where(c <= r, s, NEG)


def _attn_fwd(q, k, v, cum_t, name):
    T = q.shape[0]
    tq = min(TQ, T)

    def body(q_ref, k_ref, v_ref, ck_ref, o_ref, o32_ref, st_ref, m_sc, l_sc, acc_sc, res_sc):
        i = pl.program_id(1)
        lane = lax.broadcasted_iota(jnp.int32, (1, LANES), 1)
        lo = lane < HD
        q2 = q_ref[...]
        zero = jnp.zeros_like(q2)
        qa = (jnp.where(lo, q2, zero), jnp.where(lo, zero, q2))
        m_sc[...] = jnp.full(m_sc.shape, NEG, f32)
        l_sc[...] = jnp.zeros_like(l_sc)
        acc_sc[...] = jnp.zeros_like(acc_sc)
        res_sc[...] = jnp.zeros_like(res_sc)

        def block(j, masked):
            off = pl.multiple_of(j * tq, tq)
            k2 = k_ref[pl.ds(off, tq), :]
            v2 = v_ref[pl.ds(off, tq), :]
            for a in range(2):
                s = _dot_nt(qa[a], k2) - ck_ref[0, a:a + 1, pl.ds(off, tq)]
                if masked:
                    s = _causal(s, tq)
                m_old = m_sc[a]
                m_new = jnp.maximum(m_old, jnp.max(s, axis=1, keepdims=True))
                alpha = jnp.exp(m_old - m_new)
                pm = jnp.exp(s - m_new)
                pb = pm.astype(bf16)
                pr = (pm - pb.astype(f32)).astype(bf16)
                l_sc[a] = alpha * l_sc[a] + jnp.sum(pm, axis=1, keepdims=True)
                acc_sc[a] = alpha * acc_sc[a] + jnp.dot(pb, v2, preferred_element_type=f32)
                res_sc[a] = alpha * res_sc[a] + jnp.dot(pr, v2, preferred_element_type=f32)
                m_sc[a] = m_new

        def step(j, carry):
            block(j, False)
            return carry

        lax.fori_loop(0, i, step, 0)
        block(i, True)
        o_ref[...] = jnp.where(lo, acc_sc[0] / l_sc[0], acc_sc[1] / l_sc[1]).astype(bf16)
        o32_ref[...] = jnp.where(lo, (acc_sc[0] + res_sc[0]) / l_sc[0], (acc_sc[1] + res_sc[1]) / l_sc[1])
        lse0 = m_sc[0] + jnp.log(l_sc[0])
        lse1 = m_sc[1] + jnp.log(l_sc[1])
        st_ref[0] = jnp.where(lane == 0, lse0, jnp.where(lane == 1, lse1, 0.0))

    full = lambda blk: pl.BlockSpec((T, LANES), blk)
    return pl.pallas_call(
        body, grid=(NP, T // tq),
        in_specs=[pl.BlockSpec((tq, LANES), lambda p, i: (i, p)), full(lambda p, i: (0, p)), full(lambda p, i: (0, p)),
                  pl.BlockSpec((1, 2, T), lambda p, i: (p, 0, 0))],
        out_specs=[pl.BlockSpec((tq, LANES), lambda p, i: (i, p)), pl.BlockSpec((tq, LANES), lambda p, i: (i, p)),
                   pl.BlockSpec((1, tq, LANES), lambda p, i: (p, i, 0))],
        out_shape=[SDS((T, H * HD), bf16), SDS((T, H * HD), f32), SDS((NP, T, LANES), f32)],
        scratch_shapes=[pltpu.VMEM((2, tq, 1), f32), pltpu.VMEM((2, tq, 1), f32), pltpu.VMEM((2, tq, LANES), f32),
                        pltpu.VMEM((2, tq, LANES), f32)],
        compiler_params=_cparams(2), name=name)(q, k, v, cum_t)


def _attn_stats(do, o, lse, name):
    T = do.shape[0]
    tm = min(TM, T)

    def body(do_ref, o_ref, lse_ref, st_ref):
        lane = lax.broadcasted_iota(jnp.int32, (1, LANES), 1)
        prod = do_ref[...].astype(f32) * o_ref[...].astype(f32)
        d0 = jnp.sum(jnp.where(lane < HD, prod, 0.0), axis=1, keepdims=True)
        d1 = jnp.sum(jnp.where(lane < HD, 0.0, prod), axis=1, keepdims=True)
        st_ref[0] = jnp.where(lane < 2, lse_ref[0], jnp.where(lane == 2, d0, jnp.where(lane == 3, d1, 0.0)))

    til = pl.BlockSpec((tm, LANES), lambda p, i: (i, p))
    stt = pl.BlockSpec((1, tm, LANES), lambda p, i: (p, i, 0))
    return pl.pallas_call(
        body, grid=(NP, T // tm), in_specs=[til, til, stt], out_specs=stt, out_shape=SDS((NP, T, LANES), f32),
        compiler_params=_cparams(2), name=name)(do, o, lse)


def _attn_bwd(q, k, v, do, st, cum_t, name):
    T = q.shape[0]
    tq = min(TQ, T)
    n = T // tq

    def body(q_ref, k_ref, v_ref, do_ref, st_ref, ck_ref, dq_ref, dk_ref, dv_ref, dck_ref, dk_sc, dv_sc, dck_sc):
        j = pl.program_id(1)
        lane = lax.broadcasted_iota(jnp.int32, (1, LANES), 1)
        lo = lane < HD

        @pl.when(j == 0)
        def _():
            dq_ref[...] = jnp.zeros_like(dq_ref)

        k2 = k_ref[...]
        v2 = v_ref[...]
        zero = jnp.zeros_like(k2)
        ka = (jnp.where(lo, k2, zero), jnp.where(lo, zero, k2))
        va = (jnp.where(lo, v2, zero), jnp.where(lo, zero, v2))
        dk_sc[...] = jnp.zeros_like(dk_sc)
        dv_sc[...] = jnp.zeros_like(dv_sc)
        dck_sc[...] = jnp.zeros_like(dck_sc)

        def block(i, masked):
            off = pl.multiple_of(i * tq, tq)
            q2 = q_ref[pl.ds(off, tq), :]
            do2 = do_ref[pl.ds(off, tq), :]
            stt = st_ref[0, pl.ds(off, tq), :]
            parts = []
            for a in range(2):
                s = _dot_nt(q2, ka[a]) - ck_ref[0, a:a + 1, :]
                if masked:
                    s = _causal(s, tq)
                pm = jnp.exp(s - stt[:, a:a + 1])
                dp = _dot_nt(do2, va[a])
                dsm = pm * (dp - stt[:, 2 + a:3 + a])
                dsb = dsm.astype(bf16)
                dv_sc[a] += _dot_tn(pm.astype(bf16), do2)
                dk_sc[a] += _dot_tn(dsb, q2)
                dck_sc[a:a + 1, :] -= jnp.sum(dsm, axis=0, keepdims=True)
                parts.append(jnp.dot(dsb, k2, preferred_element_type=f32))
            dq_ref[pl.ds(off, tq), :] += jnp.where(lo, parts[0], parts[1])

        block(j, True)

        def step(i, carry):
            block(i, False)
            return carry

        lax.fori_loop(j + 1, n, step, 0)
        dk_ref[...] = jnp.where(lo, dk_sc[0], dk_sc[1]).astype(bf16)
        dv_ref[...] = jnp.where(lo, dv_sc[0], dv_sc[1]).astype(bf16)
        dck_ref[0] = dck_sc[0:2, :]

        @pl.when(j == n - 1)
        def _():
            dq_ref[...] = dq_ref[...] * (HD ** -0.5)

    full = lambda: pl.BlockSpec((T, LANES), lambda p, j: (0, p))
    kvb = lambda: pl.BlockSpec((tq, LANES), lambda p, j: (j, p))
    ckb = lambda: pl.BlockSpec((1, 2, tq), lambda p, j: (p, 0, j))
    return pl.pallas_call(
        body, grid=(NP, n),
        in_specs=[full(), kvb(), kvb(), full(), pl.BlockSpec((1, T, LANES), lambda p, j: (p, 0, 0)), ckb()],
        out_specs=[full(), kvb(), kvb(), ckb()],
        out_shape=[SDS((T, H * HD), f32), SDS((T, H * HD), bf16), SDS((T, H * HD), bf16), SDS((NP, 2, T), f32)],
        scratch_shapes=[pltpu.VMEM((2, tq, LANES), f32), pltpu.VMEM((2, tq, LANES), f32), pltpu.VMEM((8, tq), f32)],
        compiler_params=_cparams(2), name=name)(q, k, v, do, st, cum_t)


def _ada_fwd(c_all, w_cat, name):
    n = w_cat.shape[1]
    tn = 256

    def body(c_ref, w_ref, o_ref):
        cv = c_ref[...]
        o_ref[...] = jnp.dot((cv * jax.nn.sigmoid(cv)).astype(bf16), w_ref[...].astype(bf16), preferred_element_type=f32)

    return pl.pallas_call(
        body, grid=(n // tn,), in_specs=[pl.BlockSpec((NDEV, D), _fix), pl.BlockSpec((D, tn), lambda i: (0, i))],
        out_specs=pl.BlockSpec((NDEV, tn), lambda i: (0, i)), out_shape=SDS((NDEV, n), f32),
        compiler_params=_cparams(1), name=name)(c_all, w_cat)


def _ada_bwd(c_all_t, dsel, name):
    n = dsel.shape[1]
    tn = 256

    def body(c_ref, d_ref, o_ref):
        cv = c_ref[...]
        ca = cv * jax.nn.sigmoid(cv)
        acc = ca[:, 0:1] * d_ref[0:1, :]
        for b in range(1, NDEV):
            acc = acc + ca[:, b:b + 1] * d_ref[b:b + 1, :]
        o_ref[...] = acc

    return pl.pallas_call(
        body, grid=(n // tn,), in_specs=[pl.BlockSpec((D, NDEV), _fix), pl.BlockSpec((NDEV, tn), lambda i: (0, i))],
        out_specs=pl.BlockSpec((D, tn), lambda i: (0, i)), out_shape=SDS((D, n), f32),
        compiler_params=_cparams(1), name=name)(c_all_t, dsel)


def _sum_parts(parts, name):
    R = parts.shape[1]

    def body(p_ref, o_ref):
        acc = p_ref[0]
        for j in range(1, NDEV):
            acc = acc + p_ref[j]
        o_ref[...] = acc

    return pl.pallas_call(body, out_shape=SDS((R, LANES), f32), name=name)(parts)


def _adamw(g_parts, w, m, v, name):
    n_parts, R, C = g_parts.shape
    tr = next(t for t in (256, 128, 64, 32, 16, 8) if R % t == 0)
    c1 = 1.0 / (1.0 - ADAM_B1 ** ADAM_STEP)
    c2 = 1.0 / (1.0 - ADAM_B2 ** ADAM_STEP)

    def body(g_ref, w_ref, m_ref, v_ref, go_ref, d_ref, mo_ref, vo_ref):
        g = g_ref[0].astype(f32)
        for j in range(1, n_parts):
            g = g + g_ref[j].astype(f32)
        mn = ADAM_B1 * m_ref[...] + (1.0 - ADAM_B1) * g
        vn = ADAM_B2 * v_ref[...] + (1.0 - ADAM_B2) * (g * g)
        go_ref[...] = g
        mo_ref[...] = mn
        vo_ref[...] = vn
        d_ref[...] = -ADAM_LR * ((mn * c1) / (jnp.sqrt(vn * c2) + ADAM_EPS) + ADAM_WD * w_ref[...])

    til = pl.BlockSpec((tr, C), _row)
    out = SDS((R, C), f32)
    return pl.pallas_call(
        body, grid=(R // tr,), in_specs=[pl.BlockSpec((n_parts, tr, C), lambda i: (0, i, 0)), til, til, til],
        out_specs=[til, til, til, til], out_shape=[out, out, out, out],
        compiler_params=_cparams(1), name=name)(g_parts, w, m, v)


def _pad_rows(flat, cols, mult):
    n = flat.shape[-1]
    rows = -(-n // cols)
    rows = -(-rows // mult) * mult
    pad = [(0, 0)] * (flat.ndim - 1) + [(0, rows * cols - n)]
    return jnp.pad(flat, pad).reshape(flat.shape[:-1] + (rows, cols))


def _split_flat(flat, shapes):
    out, off = {}, 0
    for name, shp in shapes:
        n = 1
        for d in shp:
            n *= d
        out[name] = flat[off:off + n].reshape(shp)
        off += n
    return out


def kernel(x, c, mix_norm_g, mix_ada_w, mix_ada_b, ffn_norm_g, ffn_ada_w, ffn_ada_b, ffn_w_in, ffn_w_out, conv_w_in, conv_b_in, conv_w_dw, conv_b_dw, conv_ln_g, conv_ln_b, conv_w_out, conv_b_out, kv_norm_g, kv_ada_w, kv_ada_b, kv_w, forget_b, attn_w_q, attn_w_o, final_norm_g, loss_target, m_mix_norm_g, m_mix_ada_w, m_mix_ada_b, m_ffn_norm_g, m_ffn_ada_w, m_ffn_ada_b, m_ffn_w_in, m_ffn_w_out, m_conv_w_in, m_conv_b_in, m_conv_w_dw, m_conv_b_dw, m_conv_ln_g, m_conv_ln_b, m_conv_w_out, m_conv_b_out, m_kv_norm_g, m_kv_ada_w, m_kv_ada_b, m_kv_w, m_forget_b, m_attn_w_q, m_attn_w_o, m_final_norm_g, v_mix_norm_g, v_mix_ada_w, v_mix_ada_b, v_ffn_norm_g, v_ffn_ada_w, v_ffn_ada_b, v_ffn_w_in, v_ffn_w_out, v_conv_w_in, v_conv_b_in, v_conv_w_dw, v_conv_b_dw, v_conv_ln_g, v_conv_ln_b, v_conv_w_out, v_conv_b_out, v_kv_norm_g, v_kv_ada_w, v_kv_ada_b, v_kv_w, v_forget_b, v_attn_w_q, v_attn_w_o, v_final_norm_g):
    W = dict(mix_norm_g=mix_norm_g, mix_ada_w=mix_ada_w, mix_ada_b=mix_ada_b, ffn_norm_g=ffn_norm_g, ffn_ada_w=ffn_ada_w, ffn_ada_b=ffn_ada_b, ffn_w_in=ffn_w_in, ffn_w_out=ffn_w_out, conv_w_in=conv_w_in, conv_b_in=conv_b_in, conv_w_dw=conv_w_dw, conv_b_dw=conv_b_dw, conv_ln_g=conv_ln_g, conv_ln_b=conv_ln_b, conv_w_out=conv_w_out, conv_b_out=conv_b_out, kv_norm_g=kv_norm_g, kv_ada_w=kv_ada_w, kv_ada_b=kv_ada_b, kv_w=kv_w, forget_b=forget_b, attn_w_q=attn_w_q, attn_w_o=attn_w_o, final_norm_g=final_norm_g)
    M = dict(mix_norm_g=m_mix_norm_g, mix_ada_w=m_mix_ada_w, mix_ada_b=m_mix_ada_b, ffn_norm_g=m_ffn_norm_g, ffn_ada_w=m_ffn_ada_w, ffn_ada_b=m_ffn_ada_b, ffn_w_in=m_ffn_w_in, ffn_w_out=m_ffn_w_out, conv_w_in=m_conv_w_in, conv_b_in=m_conv_b_in, conv_w_dw=m_conv_w_dw, conv_b_dw=m_conv_b_dw, conv_ln_g=m_conv_ln_g, conv_ln_b=m_conv_ln_b, conv_w_out=m_conv_w_out, conv_b_out=m_conv_b_out, kv_norm_g=m_kv_norm_g, kv_ada_w=m_kv_ada_w, kv_ada_b=m_kv_ada_b, kv_w=m_kv_w, forget_b=m_forget_b, attn_w_q=m_attn_w_q, attn_w_o=m_attn_w_o, final_norm_g=m_final_norm_g)
    V = dict(mix_norm_g=v_mix_norm_g, mix_ada_w=v_mix_ada_w, mix_ada_b=v_mix_ada_b, ffn_norm_g=v_ffn_norm_g, ffn_ada_w=v_ffn_ada_w, ffn_ada_b=v_ffn_ada_b, ffn_w_in=v_ffn_w_in, ffn_w_out=v_ffn_w_out, conv_w_in=v_conv_w_in, conv_b_in=v_conv_b_in, conv_w_dw=v_conv_w_dw, conv_b_dw=v_conv_b_dw, conv_ln_g=v_conv_ln_g, conv_ln_b=v_conv_ln_b, conv_w_out=v_conv_w_out, conv_b_out=v_conv_b_out, kv_norm_g=v_kv_norm_g, kv_ada_w=v_kv_ada_w, kv_ada_b=v_kv_ada_b, kv_w=v_kv_w, forget_b=v_forget_b, attn_w_q=v_attn_w_q, attn_w_o=v_attn_w_o, final_norm_g=v_final_norm_g)
    names = list(W)
    T = x.shape[1]
    me = _my_index()
    x0 = x[0]
    tgt = loss_target[0]
    row = lambda vct: vct.reshape(1, -1)

    small_names = ("conv_b_in", "conv_w_dw", "conv_b_dw", "conv_ln_g", "conv_ln_b", "conv_b_out")
    small_loc = jnp.concatenate([c.reshape(-1)] + [W[n].reshape(-1) for n in small_names])
    sg = _exchange([_pad_rows(small_loc, LANES, 8)], True, "gather_small")[0].reshape(NDEV, -1)
    c_all = sg[:, :D]
    off = D
    b_in = sg[:, off:off + 2 * D // NDEV].reshape(1, 2 * D); off += 2 * D // NDEV
    cl = D // NDEV
    w_dw = sg[:, off:off + KW * cl].reshape(NDEV, KW, cl).transpose(1, 0, 2).reshape(KW, D); off += KW * cl
    w_dw = jnp.pad(w_dw, ((0, HALO - KW), (0, 0))).astype(bf16).astype(f32)
    b_dw = sg[:, off:off + cl].reshape(1, D); off += cl
    ln_g = sg[:, off:off + cl].reshape(1, D); off += cl
    ln_b = sg[:, off:off + cl].reshape(1, D); off += cl
    b_out = sg[:, off:off + cl].reshape(1, D)

    w_cat = jnp.concatenate([mix_ada_w[0], mix_ada_w[1], ffn_ada_w[0], ffn_ada_w[1], kv_ada_w], axis=1)
    ada_loc = _ada_fwd(c_all, w_cat, "ada_fwd")
    ada_all = _exchange([ada_loc], True, "gather_ada")[0]
    ada_me = lax.dynamic_index_in_dim(ada_all, me, axis=1, keepdims=False)
    ada_bias = (mix_ada_b[0], mix_ada_b[1], ffn_ada_b[0], ffn_ada_b[1], kv_ada_b)
    ada, off = [], 0
    for nl, bias in zip(ADA_LOC, ada_bias):
        full = ada_me[:, off:off + nl].reshape(-1) + bias
        ada.append([row(t) for t in jnp.split(full, full.shape[0] // D)])
        off += nl
    (sh_m0, sc_m0, gt_m0), (sh_m1, sc_m1, gt_m1), (sh_f0, sc_f0, gt_f0), (sh_f1, sc_f1, gt_f1), (sh_kv, sc_kv) = ada

    pad_in = lambda src: jnp.pad(src["ffn_w_in"], ((0, 0), (0, 0), (0, FSP - FS)))
    rows_of = lambda src: jnp.concatenate([src["ffn_w_out"].reshape(-1, D), src["conv_w_out"][0], src["attn_w_q"][0], src["attn_w_o"][0]])
    fo_rows = 2 * F // NDEV
    sq_rows = D // NDEV
    w_sh, g_ci, g_rows, g_kv = _exchange(
        [pad_in(W).astype(bf16), conv_w_in[0].astype(bf16), rows_of(W).astype(bf16), kv_w.astype(bf16)], True, "gather_weights")
    conv_in_full = g_ci.transpose(1, 0, 2).reshape(D, 2 * D)
    wc_a, wc_g = conv_in_full[:, :D], conv_in_full[:, D:]
    w_out = g_rows[:, :fo_rows].reshape(NDEV, 2, F // NDEV, D).transpose(1, 0, 2, 3).reshape(2, NDEV // 2, FS, D)
    w_out = jnp.pad(w_out, ((0, 0), (0, 0), (0, FSP - FS), (0, 0))).reshape(2, FP, D)
    wc_o = g_rows[:, fo_rows:fo_rows + sq_rows].reshape(D, D)
    w_q = g_rows[:, fo_rows + sq_rows:fo_rows + 2 * sq_rows].reshape(D, D)
    w_o = g_rows[:, fo_rows + 2 * sq_rows:].reshape(D, D)
    kv_full = g_kv.transpose(1, 0, 2).reshape(D, -1)
    w_k, w_v = kv_full[:, :D], kv_full[:, D:2 * D]
    w_f = jnp.pad(kv_full[:, 2 * D:], ((0, 0), (0, LANES - H)))
    zeros_d = jnp.zeros((1, D), f32)
    fb = jnp.pad(forget_b, (0, LANES - H)).reshape(1, LANES)

    h0 = _normmod(x0, row(mix_norm_g[0]), sh_m0, sc_m0, "norm_mix0")
    a0, gl0, p0 = _mm_gated(h0, wc_a, wc_g, b_in[:, :D], b_in[:, D:], False, "conv_in")
    q0, s0 = _conv_fwd(p0, w_dw, b_dw, ln_g, ln_b, "conv_dw")
    x1, y0 = _mm_res(s0, wc_o, b_out, x0, gt_m0, "conv_out")

    def ffn_fwd(xin, l, sh, sc, gt):
        h = _normmod(xin, row(ffn_norm_g[l]), sh, sc, f"norm_ffn{l}")
        ug, uu, act = _ffn_in(h, w_sh, l, f"ffn_in{l}")
        xo, y = _mm_res(act, w_out[l], zeros_d, xin, gt, f"ffn_out{l}")
        return xo, (h, ug, uu, act, y)

    x2, ffn0 = ffn_fwd(x1, 0, sh_f0, sc_f0, gt_f0)

    hk = _normmod(x2, row(kv_norm_g), sh_kv, sc_kv, "norm_kv")
    k_sh = _mm(hk, w_k, bf16, 1.0, "proj_k")
    v_sh = _mm(hk, w_v, bf16, 1.0, "proj_v")
    fl = _mm(hk, w_f, f32, 1.0, "proj_f")
    cum = _forget_fwd(fl, fb, "forget_fwd")
    cum_t = cum[:, :H].T.reshape(NP, 2, T)

    h2 = _normmod(x2, row(mix_norm_g[1]), sh_m1, sc_m1, "norm_mix1")
    qh = _mm(h2, w_q, bf16, HD ** -0.5, "proj_q")
    o, o32, lse = _attn_fwd(qh, k_sh, v_sh, cum_t, "attn_fwd")
    x3, y1 = _mm_res(o, w_o, zeros_d, x2, gt_m1, "attn_out")

    x4, ffn1 = ffn_fwd(x3, 1, sh_f1, sc_f1, gt_f1)

    dx4, acc_fin = _final_bwd(x4, row(final_norm_g), tgt, "final_bwd")

    G = {}
    d_ada = {}

    def ffn_bwd(dx_out, xin, l, sc, gt, saved):
        h, ug, uu, act, y = saved
        dyb, acc_r = _res_in(dx_out, y, gt, f"ffn_res_bwd{l}")
        dug, duu = _mm_nt_swiglu(dyb, w_out[l], ug, uu, f"ffn_dact{l}")
        g_out = _mm_tn(act, dyb, f"ffn_dw_out{l}").reshape(NDEV // 2, FSP, D)[:, :FS].reshape(NDEV, F // NDEV, D)
        g_in = jnp.concatenate([_mm_tn(h, dug, f"ffn_dw_gate{l}", FSP), _mm_tn(h, duu, f"ffn_dw_up{l}", FSP)])
        dh = _ffn_dh(dug, duu, w_sh, l, f"ffn_dh{l}")
        dxi, acc_n = _normmod_bwd(dh, xin, row(ffn_norm_g[l]), sc, dx_out, f"norm_ffn_bwd{l}")
        return dxi, g_in, g_out, [acc_n[0:1], acc_n[1:2], acc_r[0:1]], acc_n[2]

    dx3, g_in1, g_out1, d_ada[("ffn", 1)], dg_ffn1 = ffn_bwd(dx4, x3, 1, sc_f1, gt_f1, ffn1)

    dyb, acc_r = _res_in(dx3, y1, gt_m1, "attn_res_bwd")
    do = _mm_nt([(dyb, w_o)], bf16, "attn_do")
    G["attn_w_o"] = _mm_tn(o, dyb, "attn_dw_o")
    st = _attn_stats(do, o32, lse, "attn_stats")
    dq, dk, dv, dck = _attn_bwd(qh, k_sh, v_sh, do, st, cum_t, "attn_bwd")
    G["attn_w_q"] = _mm_tn(h2, dq, "attn_dw_q")
    dh2 = _mm_nt([(dq, w_q)], f32, "attn_dh")
    dx2, acc_n = _normmod_bwd(dh2, x2, row(mix_norm_g[1]), sc_m1, dx3, "norm_mix_bwd1")
    d_ada[("mix", 1)] = [acc_n[0:1], acc_n[1:2], acc_r[0:1]]
    dg_mix1 = acc_n[2]

    dcum = jnp.pad(dck.reshape(H, T).T, ((0, 0), (0, LANES - H)))
    dfl, acc_f = _forget_bwd(dcum, fl, fb, "forget_bwd")
    G["kv_w"] = jnp.concatenate([_mm_tn(hk, dk, "kv_dw_k"), _mm_tn(hk, dv, "kv_dw_v"), _mm_tn(hk, dfl, "kv_dw_f")[:, :H]], axis=1)
    dhk = _mm_nt([(dk, w_k), (dv, w_v), (dfl, w_f)], f32, "kv_dh")
    dx2, acc_n = _normmod_bwd(dhk, x2, row(kv_norm_g), sc_kv, dx2, "norm_kv_bwd")
    d_ada[("kv", 0)] = [acc_n[0:1], acc_n[1:2]]
    dg_kv = acc_n[2]

    dx1, g_in0, g_out0, d_ada[("ffn", 0)], dg_ffn0 = ffn_bwd(dx2, x1, 0, sc_f0, gt_f0, ffn0)
    G["ffn_w_in"] = jnp.stack([g_in0, g_in1], axis=1)
    G["ffn_w_out"] = jnp.stack([g_out0, g_out1], axis=1).reshape(NDEV, fo_rows, D)

    dyb, acc_r = _res_in(dx1, y0, gt_m0, "conv_res_bwd")
    dsw = _mm_nt([(dyb, wc_o)], f32, "conv_ds")
    G["conv_w_out"] = _mm_tn(s0, dyb, "conv_dw_out")
    da, dgl, acc_c, dw_dw = _conv_bwd(dsw, q0, p0, a0, gl0, w_dw, ln_g, ln_b, "conv_bwd")
    cs = 2 * D // NDEV
    G["conv_w_in"] = jnp.concatenate([_mm_tn(h0, da, "conv_dw_a", cs), _mm_tn(h0, dgl, "conv_dw_g", cs)])
    dh0 = _mm_nt([(da, wc_a), (dgl, wc_g)], f32, "conv_dh")
    dx0, acc_n = _normmod_bwd(dh0, x0, row(mix_norm_g[0]), sc_m0, dx1, "norm_mix_bwd0")
    d_ada[("mix", 0)] = [acc_n[0:1], acc_n[1:2], acc_r[0:1]]
    dg_mix0 = acc_n[2]

    vec = [t.reshape(-1) for key in [(s[0], s[1]) for s in ADA_SEG] for t in d_ada[key]]
    vec += [dg_mix0, dg_mix1, dg_ffn0, dg_ffn1, dg_kv, acc_fin[0]]
    vec += [acc_f[0], acc_fin[1, :LANES]]
    vec += [acc_c[3], acc_c[4], dw_dw[:KW].reshape(-1), acc_c[2], acc_c[0], acc_c[1], acc_r[1]]
    small_parts = _exchange([_pad_rows(jnp.concatenate(vec), LANES, 8)], True, "gather_partials")[0]
    small_sum = _sum_parts(small_parts, "sum_partials").reshape(-1)
    d_ada_all = small_parts.reshape(NDEV, -1)[:, :ADA_TOT]
    off = 0
    gsm = {}
    ada_b_sum = []
    for _, _, n in ADA_SEG:
        ada_b_sum.append(small_sum[off:off + n]); off += n
    gsm["mix_ada_b"] = jnp.stack(ada_b_sum[0:2])
    gsm["ffn_ada_b"] = jnp.stack(ada_b_sum[2:4])
    gsm["kv_ada_b"] = ada_b_sum[4]
    gsm["mix_norm_g"] = small_sum[off:off + 2 * D].reshape(2, D); off += 2 * D
    gsm["ffn_norm_g"] = small_sum[off:off + 2 * D].reshape(2, D); off += 2 * D
    gsm["kv_norm_g"] = small_sum[off:off + D]; off += D
    gsm["final_norm_g"] = small_sum[off:off + D]; off += D
    gsm["forget_b"] = small_sum[off:off + H]; off += LANES
    loss = small_sum[off]; off += LANES
    sl = lambda full, width: lax.dynamic_slice_in_dim(full, me * width, width, axis=full.ndim - 1)
    gsm["conv_b_in"] = sl(small_sum[off:off + 2 * D].reshape(1, 2 * D), 2 * D // NDEV); off += 2 * D
    gsm["conv_w_dw"] = sl(small_sum[off:off + KW * D].reshape(1, KW, D), cl); off += KW * D
    for n in ("conv_b_dw", "conv_ln_g", "conv_ln_b", "conv_b_out"):
        gsm[n] = sl(small_sum[off:off + D].reshape(1, D), cl); off += D

    dsel, off = [], 0
    for (_, _, n), nl in zip(ADA_SEG, ADA_LOC):
        dsel.append(lax.dynamic_slice_in_dim(d_ada_all[:, off:off + n], me * nl, nl, axis=1)); off += n
    g_ada = _ada_bwd(c_all.T, jnp.concatenate(dsel, axis=1), "ada_bwd")
    gsm["mix_ada_w"] = jnp.stack([g_ada[:, 0:ADA_LOC[0]], g_ada[:, ADA_LOC[0]:2 * ADA_LOC[0]]])
    o2 = 2 * ADA_LOC[0]
    gsm["ffn_ada_w"] = jnp.stack([g_ada[:, o2:o2 + ADA_LOC[2]], g_ada[:, o2 + ADA_LOC[2]:o2 + 2 * ADA_LOC[2]]])
    gsm["kv_ada_w"] = g_ada[:, o2 + 2 * ADA_LOC[2]:]

    by_rows = lambda g: g.reshape(NDEV, sq_rows, D)
    g_rows_out = jnp.concatenate([G["ffn_w_out"], by_rows(G["conv_w_out"]), by_rows(G["attn_w_q"]), by_rows(G["attn_w_o"])], axis=1)
    g_kv_out = G["kv_w"].reshape(D, NDEV, -1).transpose(1, 0, 2)
    r_in, r_ci, r_rows, r_kv = _exchange(
        [G["ffn_w_in"].astype(bf16), G["conv_w_in"].astype(bf16), g_rows_out.astype(bf16), g_kv_out.astype(bf16)], False, "scatter_grads")
    res_in = _adamw(r_in.reshape(NDEV, 2 * D, FSP), *[pad_in(s).reshape(2 * D, FSP) for s in (W, M, V)], "adamw_ffn_in")
    res_ci = _adamw(r_ci, *[s["conv_w_in"][0] for s in (W, M, V)], "adamw_conv_in")
    res_rows = _adamw(r_rows, *[rows_of(s) for s in (W, M, V)], "adamw_rows")
    res_kv = _adamw(r_kv, *[s["kv_w"] for s in (W, M, V)], "adamw_kv")
    rest = [n for n in names if n not in MAIN]
    pack_rest = lambda src: _pad_rows(jnp.concatenate([src[n].reshape(-1) for n in rest]), D, 256)
    res_rest = _adamw(pack_rest(gsm)[None], pack_rest(W), pack_rest(M), pack_rest(V), "adamw_rest")

    outs = []
    for k in range(4):
        ur = _split_flat(res_rest[k].reshape(-1), [(n, W[n].shape) for n in rest])
        ur["ffn_w_in"] = res_in[k].reshape(2, D, FSP)[:, :, :FS]
        ur["conv_w_in"] = res_ci[k][None]
        ur["kv_w"] = res_kv[k]
        ur["ffn_w_out"] = res_rows[k][:fo_rows].reshape(W["ffn_w_out"].shape)
        ur["conv_w_out"] = res_rows[k][fo_rows:fo_rows + sq_rows][None]
        ur["attn_w_q"] = res_rows[k][fo_rows + sq_rows:fo_rows + 2 * sq_rows][None]
        ur["attn_w_o"] = res_rows[k][fo_rows + 2 * sq_rows:][None]
        outs.append(ur)
    grads, deltas, new_m, new_v = outs
    return (loss, dx0[None], *[grads[n] for n in names], *[deltas[n] for n in names],
            *[new_m[n] for n in names], *[new_v[n] for n in names])
```

```python
import functools

import jax
import jax.numpy as jnp
from jax import lax
from jax.experimental import pallas as pl
from jax.experimental.pallas import tpu as pltpu

f32, bf16 = jnp.float32, jnp.bfloat16
SDS = jax.ShapeDtypeStruct

D = 1024
F = 2816
H = 16
HD = 64
NP = H // 2
KW = 31
HALO = 32
NDEV = 8
FS = 2 * F // NDEV
FSP = 768
FP = 4 * FSP
EPS = 1e-6
NEG = -1e30
LANES = 128

ADAM_LR, ADAM_B1, ADAM_B2, ADAM_EPS, ADAM_WD, ADAM_STEP = 0.001, 0.9, 0.999, 1e-08, 0.01, 10

TM = 512
TC = 256
TQ = 512
VMEM_LIMIT = 56 << 20

MAIN = ("ffn_w_in", "ffn_w_out", "conv_w_in", "conv_w_out", "kv_w", "attn_w_q", "attn_w_o")
ADA_SEG = (("mix", 0, 3 * D), ("mix", 1, 3 * D), ("ffn", 0, 3 * D), ("ffn", 1, 3 * D), ("kv", 0, 2 * D))
ADA_LOC = tuple(n // NDEV for _, _, n in ADA_SEG)
ADA_COLS = sum(ADA_LOC)
ADA_TOT = sum(n for _, _, n in ADA_SEG)


def _cparams(n_axes):
    return pltpu.CompilerParams(dimension_semantics=("arbitrary",) * n_axes, vmem_limit_bytes=VMEM_LIMIT)


def _mesh_pos():
    return lax.axis_index("x"), lax.axis_index("y"), lax.axis_index("c")


def _my_index():
    mx, my, mc = _mesh_pos()
    return 4 * mx + 2 * my + mc


def _peer(k, mx, my, mc):
    px = (1 - mx) if k & 4 else mx
    py = (1 - my) if k & 2 else my
    pc = (1 - mc) if k & 1 else mc
    return (px, py, pc), 4 * px + 2 * py + pc


def _exchange_copies(x_refs, o_refs, sems, gather):
    send_sems, recv_sems, local_sems = sems
    mx, my, mc = _mesh_pos()
    me = 4 * mx + 2 * my + mc
    copies = []
    for a, (x_ref, o_ref) in enumerate(zip(x_refs, o_refs)):
        copies.append(pltpu.make_async_copy(x_ref if gather else x_ref.at[me], o_ref.at[me], local_sems.at[a]))
        for k in range(1, NDEV):
            peer, pidx = _peer(k, mx, my, mc)
            sem = a * (NDEV - 1) + k - 1
            copies.append(pltpu.make_async_remote_copy(
                src_ref=x_ref if gather else x_ref.at[pidx], dst_ref=o_ref.at[me],
                send_sem=send_sems.at[sem], recv_sem=recv_sems.at[sem],
                device_id=peer, device_id_type=pl.DeviceIdType.MESH))
    return copies


def _exchange_shapes(xs, gather):
    n = len(xs)
    hbm = pl.BlockSpec(memory_space=pl.ANY)
    outs = [SDS((NDEV,) + tuple(x.shape if gather else x.shape[1:]), x.dtype) for x in xs]
    sems = [pltpu.SemaphoreType.DMA((n * (NDEV - 1),)), pltpu.SemaphoreType.DMA((n * (NDEV - 1),)), pltpu.SemaphoreType.DMA((n,))]
    return [hbm] * n, [hbm] * n, outs, sems


def _exchange_start(x_refs, o_refs, sems, gather, first):
    @pl.when(first)
    def _():
        for cp in _exchange_copies(x_refs, o_refs, sems, gather):
            cp.start()


def _exchange_wait(x_refs, o_refs, sems, gather, last):
    @pl.when(last)
    def _():
        for cp in _exchange_copies(x_refs, o_refs, sems, gather):
            cp.wait()


def _exchange(xs, gather, name):
    n = len(xs)

    def body(*refs):
        copies = _exchange_copies(refs[:n], refs[n:2 * n], refs[2 * n:], gather)
        for cp in copies:
            cp.start()
        for cp in copies:
            cp.wait()

    in_specs, out_specs, outs, sems = _exchange_shapes(xs, gather)
    return pl.pallas_call(body, out_shape=outs, in_specs=in_specs, out_specs=out_specs, scratch_shapes=sems, name=name)(*xs)


def _row(i):
    return (i, 0)


def _fix(i):
    return (0, 0)


def _normmod(x, g, shift, scale, name):
    T = x.shape[0]
    tm = min(TM, T)

    def body(x_ref, g_ref, sh_ref, sc_ref, h_ref):
        xv = x_ref[...]
        r = lax.rsqrt(jnp.mean(xv * xv, axis=-1, keepdims=True) + EPS)
        hn = (xv * r) * g_ref[...]
        h_ref[...] = (hn * (1.0 + sc_ref[...]) + sh_ref[...]).astype(bf16)

    vec = pl.BlockSpec((1, D), _fix)
    return pl.pallas_call(
        body, grid=(T // tm,), in_specs=[pl.BlockSpec((tm, D), _row), vec, vec, vec],
        out_specs=pl.BlockSpec((tm, D), _row), out_shape=SDS((T, D), bf16),
        compiler_params=_cparams(1), name=name)(x, g, shift, scale)


def _normmod_bwd(dh, x, g, scale, dx_res, name):
    T = x.shape[0]
    tm = min(TM, T)

    def body(dh_ref, x_ref, g_ref, sc_ref, res_ref, dx_ref, acc_ref):
        @pl.when(pl.program_id(0) == 0)
        def _():
            acc_ref[...] = jnp.zeros_like(acc_ref)
        xv = x_ref[...]
        dhv = dh_ref[...]
        gv = g_ref[...]
        r = lax.rsqrt(jnp.mean(xv * xv, axis=-1, keepdims=True) + EPS)
        xn = xv * r
        dhn = dhv * (1.0 + sc_ref[...])
        dxn = dhn * gv
        dx_ref[...] = res_ref[...] + r * (dxn - xn * jnp.mean(dxn * xn, axis=-1, keepdims=True))
        acc_ref[0:1, :] += jnp.sum(dhv, axis=0, keepdims=True)
        acc_ref[1:2, :] += jnp.sum(dhv * (xn * gv), axis=0, keepdims=True)
        acc_ref[2:3, :] += jnp.sum(dhn * xn, axis=0, keepdims=True)

    vec = pl.BlockSpec((1, D), _fix)
    til = pl.BlockSpec((tm, D), _row)
    return pl.pallas_call(
        body, grid=(T // tm,), in_specs=[til, til, vec, vec, til],
        out_specs=[til, pl.BlockSpec((8, D), _fix)], out_shape=[SDS((T, D), f32), SDS((8, D), f32)],
        compiler_params=_cparams(1), name=name)(dh, x, g, scale, dx_res)


def _res_in(dx, y, gate, name):
    T = dx.shape[0]
    tm = min(TM, T)

    def body(dx_ref, y_ref, gt_ref, dy_ref, acc_ref):
        @pl.when(pl.program_id(0) == 0)
        def _():
            acc_ref[...] = jnp.zeros_like(acc_ref)
        dxv = dx_ref[...]
        dy = dxv * gt_ref[...]
        dy_ref[...] = dy.astype(bf16)
        acc_ref[0:1, :] += jnp.sum(dxv * y_ref[...].astype(f32), axis=0, keepdims=True)
        acc_ref[1:2, :] += jnp.sum(dy, axis=0, keepdims=True)

    til = pl.BlockSpec((tm, D), _row)
    return pl.pallas_call(
        body, grid=(T // tm,), in_specs=[til, til, pl.BlockSpec((1, D), _fix)],
        out_specs=[til, pl.BlockSpec((8, D), _fix)], out_shape=[SDS((T, D), bf16), SDS((8, D), f32)],
        compiler_params=_cparams(1), name=name)(dx, y, gate)


def _final_bwd(x, g, tgt, name):
    T = x.shape[0]
    tm = min(TM, T)

    def body(x_ref, g_ref, t_ref, dx_ref, acc_ref):
        @pl.when(pl.program_id(0) == 0)
        def _():
            acc_ref[...] = jnp.zeros_like(acc_ref)
        xv = x_ref[...]
        gv = g_ref[...]
        r = lax.rsqrt(jnp.mean(xv * xv, axis=-1, keepdims=True) + EPS)
        xn = xv * r
        err = xn * gv - t_ref[...]
        dy = err * (1.0 / D)
        dxn = dy * gv
        dx_ref[...] = r * (dxn - xn * jnp.mean(dxn * xn, axis=-1, keepdims=True))
        acc_ref[0:1, :] += jnp.sum(dy * xn, axis=0, keepdims=True)
        acc_ref[1:2, :] += 0.5 * jnp.sum(jnp.mean(err * err, axis=-1, keepdims=True))

    til = pl.BlockSpec((tm, D), _row)
    return pl.pallas_call(
        body, grid=(T // tm,), in_specs=[til, pl.BlockSpec((1, D), _fix), til],
        out_specs=[til, pl.BlockSpec((8, D), _fix)], out_shape=[SDS((T, D), f32), SDS((8, D), f32)],
        compiler_params=_cparams(1), name=name)(x, g, tgt)


def _col_tile(n):
    if n <= 1024:
        return min(n, 512) if n % 512 == 0 else n
    return 1408 if n % 1408 == 0 else 1024


def _mm_gated(h, wa, wb, ba, bb, swiglu, name):
    T, K = h.shape
    N = wa.shape[1]
    tm, tn = min(TM, T), _col_tile(N)

    def body(h_ref, wa_ref, wb_ref, ba_ref, bb_ref, u_ref, w_ref, p_ref):
        hv = h_ref[...]
        u = jnp.dot(hv, wa_ref[...], preferred_element_type=f32) + ba_ref[...]
        w = jnp.dot(hv, wb_ref[...], preferred_element_type=f32) + bb_ref[...]
        u_ref[...] = u
        w_ref[...] = w
        if swiglu:
            p_ref[...] = ((u * jax.nn.sigmoid(u)) * w).astype(p_ref.dtype)
        else:
            p_ref[...] = (u * jax.nn.sigmoid(w)).astype(p_ref.dtype)

    wsp = pl.BlockSpec((K, tn), lambda i, j: (0, j))
    bsp = pl.BlockSpec((1, tn), lambda i, j: (0, j))
    osp = pl.BlockSpec((tm, tn), lambda i, j: (i, j))
    return pl.pallas_call(
        body, grid=(T // tm, N // tn), in_specs=[pl.BlockSpec((tm, K), lambda i, j: (i, 0)), wsp, wsp, bsp, bsp],
        out_specs=[osp, osp, osp], out_shape=[SDS((T, N), f32), SDS((T, N), f32), SDS((T, N), bf16)],
        compiler_params=_cparams(2), name=name)(h, wa, wb, ba, bb)


def _mm_res(a, w, b, x_in, gate, name):
    T, K = a.shape
    N = w.shape[1]
    tm, tn = min(TM, T), _col_tile(N)

    def body(a_ref, w_ref, b_ref, x_ref, gt_ref, xo_ref, y_ref):
        y = jnp.dot(a_ref[...], w_ref[...], preferred_element_type=f32) + b_ref[...]
        y_ref[...] = y.astype(bf16)
        xo_ref[...] = x_ref[...] + gt_ref[...] * y

    vsp = pl.BlockSpec((1, tn), lambda i, j: (0, j))
    osp = pl.BlockSpec((tm, tn), lambda i, j: (i, j))
    return pl.pallas_call(
        body, grid=(T // tm, N // tn),
        in_specs=[pl.BlockSpec((tm, K), lambda i, j: (i, 0)), pl.BlockSpec((K, tn), lambda i, j: (0, j)), vsp, osp, vsp],
        out_specs=[osp, osp], out_shape=[SDS((T, N), f32), SDS((T, N), bf16)],
        compiler_params=_cparams(2), name=name)(a, w, b, x_in, gate)


def _mm(a, w, out_dtype, out_scale, name):
    T, K = a.shape
    N = w.shape[1]
    tm, tn = min(TM, T), _col_tile(N)

    def body(a_ref, w_ref, o_ref):
        y = jnp.dot(a_ref[...], w_ref[...], preferred_element_type=f32)
        if out_scale != 1.0:
            y = y * out_scale
        o_ref[...] = y.astype(out_dtype)

    return pl.pallas_call(
        body, grid=(T // tm, N // tn),
        in_specs=[pl.BlockSpec((tm, K), lambda i, j: (i, 0)), pl.BlockSpec((K, tn), lambda i, j: (0, j))],
        out_specs=pl.BlockSpec((tm, tn), lambda i, j: (i, j)), out_shape=SDS((T, N), out_dtype),
        compiler_params=_cparams(2), name=name)(a, w)


def _dot_nt(a, b):
    return lax.dot_general(a, b, (((1,), (1,)), ((), ())), preferred_element_type=f32)


def _dot_tn(a, b):
    return lax.dot_general(a, b, (((0,), (0,)), ((), ())), preferred_element_type=f32)


def _mm_nt(pairs, out_dtype, name):
    T = pairs[0][0].shape[0]
    K = pairs[0][1].shape[0]
    tm, tk = min(TM, T), _col_tile(K)
    n = len(pairs)

    def body(*refs):
        o_ref = refs[2 * n]
        acc = None
        for i in range(n):
            part = _dot_nt(refs[2 * i][...].astype(bf16), refs[2 * i + 1][...])
            acc = part if acc is None else acc + part
        o_ref[...] = acc.astype(out_dtype)

    in_specs, args = [], []
    for dy, w in pairs:
        ni = dy.shape[1]
        in_specs += [pl.BlockSpec((tm, ni), lambda i, j: (i, 0)), pl.BlockSpec((tk, ni), lambda i, j: (j, 0))]
        args += [dy, w]
    return pl.pallas_call(
        body, grid=(T // tm, K // tk), in_specs=in_specs,
        out_specs=pl.BlockSpec((tm, tk), lambda i, j: (i, j)), out_shape=SDS((T, K), out_dtype),
        compiler_params=_cparams(2), name=name)(*args)


def _mm_nt_swiglu(dy, w, ug, uu, name):
    T, N = dy.shape
    K = w.shape[0]
    tm, tk = min(TM, T), _col_tile(K)

    def body(dy_ref, w_ref, ug_ref, uu_ref, dug_ref, duu_ref):
        dact = _dot_nt(dy_ref[...], w_ref[...])
        g = ug_ref[...].astype(f32)
        u = uu_ref[...].astype(f32)
        sg = jax.nn.sigmoid(g)
        duu_ref[...] = (dact * (g * sg)).astype(bf16)
        dug_ref[...] = (dact * u * (sg * (1.0 + g * (1.0 - sg)))).astype(bf16)

    osp = pl.BlockSpec((tm, tk), lambda i, j: (i, j))
    return pl.pallas_call(
        body, grid=(T // tm, K // tk),
        in_specs=[pl.BlockSpec((tm, N), lambda i, j: (i, 0)), pl.BlockSpec((tk, N), lambda i, j: (j, 0)), osp, osp],
        out_specs=[osp, osp], out_shape=[SDS((T, K), bf16), SDS((T, K), bf16)],
        compiler_params=_cparams(2), name=name)(dy, w, ug, uu)


def _mm_tn(a, b, name, shard_cols=0):
    T, K = a.shape
    N = b.shape[1]
    tt = min(TM, T)
    tk = K if K <= 1024 else _col_tile(K)
    tn = shard_cols or (N if N <= 1024 else _col_tile(N))
    nt = T // tt

    def body(a_ref, b_ref, o_ref):
        @pl.when(pl.program_id(2) == 0)
        def _():
            o_ref[...] = jnp.zeros_like(o_ref)
        o_ref[...] += _dot_tn(a_ref[...].astype(bf16), b_ref[...].astype(bf16))

    if shard_cols:
        out_spec, out_shape = pl.BlockSpec((None, tk, tn), lambda i, j, t: (j, i, 0)), SDS((N // tn, K, tn), f32)
    else:
        out_spec, out_shape = pl.BlockSpec((tk, tn), lambda i, j, t: (i, j)), SDS((K, N), f32)
    return pl.pallas_call(
        body, grid=(K // tk, N // tn, nt),
        in_specs=[pl.BlockSpec((tt, tk), lambda i, j, t: (t, i)), pl.BlockSpec((tt, tn), lambda i, j, t: (t, j))],
        out_specs=out_spec, out_shape=out_shape,
        compiler_params=_cparams(3), name=name)(a, b)


def _ffn_in(h, w_sh, name):
    T = h.shape[0]
    tm = min(TM, T)

    def body(h_ref, wg_ref, wu_ref, ug_ref, uu_ref, act_ref):
        hv = h_ref[...]
        ug = jnp.dot(hv, wg_ref[...], preferred_element_type=f32)
        uu = jnp.dot(hv, wu_ref[...], preferred_element_type=f32)
        ug_ref[...] = ug
        uu_ref[...] = uu
        act_ref[...] = ((ug * jax.nn.sigmoid(ug)) * uu).astype(bf16)

    osp = pl.BlockSpec((tm, FSP), lambda i, j: (i, j))
    return pl.pallas_call(
        body, grid=(T // tm, NDEV // 2),
        in_specs=[pl.BlockSpec((tm, D), lambda i, j: (i, 0)),
                  pl.BlockSpec((None, D, FSP), lambda i, j: (j, 0, 0)),
                  pl.BlockSpec((None, D, FSP), lambda i, j: (j + NDEV // 2, 0, 0))],
        out_specs=[osp, osp, osp], out_shape=[SDS((T, FP), f32), SDS((T, FP), f32), SDS((T, FP), bf16)],
        compiler_params=_cparams(2), name=name)(h, w_sh, w_sh)


def _ffn_dh(dug, duu, w_sh, name):
    T = dug.shape[0]
    tm, tk = min(TM, T), 512
    half = NDEV // 2

    def body(dg_ref, du_ref, wg_ref, wu_ref, o_ref):
        acc = None
        for s in range(half):
            cols = slice(s * FSP, (s + 1) * FSP)
            part = _dot_nt(dg_ref[:, cols], wg_ref[s]) + _dot_nt(du_ref[:, cols], wu_ref[s])
            acc = part if acc is None else acc + part
        o_ref[...] = acc

    dsp = pl.BlockSpec((tm, FP), lambda i, j: (i, 0))
    return pl.pallas_call(
        body, grid=(T // tm, D // tk),
        in_specs=[dsp, dsp, pl.BlockSpec((half, tk, FSP), lambda i, j: (0, j, 0)),
                  pl.BlockSpec((half, tk, FSP), lambda i, j: (1, j, 0))],
        out_specs=pl.BlockSpec((tm, tk), lambda i, j: (i, j)), out_shape=SDS((T, D), f32),
        compiler_params=_cparams(2), name=name)(dug, duu, w_sh, w_sh)


def _layernorm_parts(qv, g, b):
    mu = jnp.mean(qv, axis=-1, keepdims=True)
    cen = qv - mu
    rstd = lax.rsqrt(jnp.mean(cen * cen, axis=-1, keepdims=True) + EPS)
    z = cen * rstd
    return z, rstd, z * g + b


def _conv_fwd(p, w_dw, b_dw, ln_g, ln_b, name):
    T = p.shape[0]
    tc = min(TC, T)

    def body(p_ref, w_ref, b_ref, g_ref, bb_ref, q_ref, s_ref, ext):
        i = pl.program_id(0)

        @pl.when(i == 0)
        def _():
            ext[0:HALO, :] = jnp.zeros((HALO, D), f32)

        @pl.when(i > 0)
        def _():
            ext[0:HALO, :] = ext[tc:tc + HALO, :]

        ext[HALO:HALO + tc, :] = p_ref[...].astype(f32)
        for cb in range(D // LANES):
            cols = slice(cb * LANES, (cb + 1) * LANES)
            acc = jnp.zeros((tc, LANES), f32)
            for k in range(KW):
                acc = acc + w_ref[k:k + 1, cols] * ext[pl.ds(HALO - (KW - 1) + k, tc), cols]
            q_ref[:, cols] = acc + b_ref[:, cols]
        _, _, l = _layernorm_parts(q_ref[...], g_ref[...], bb_ref[...])
        s_ref[...] = (l * jax.nn.sigmoid(l)).astype(bf16)

    vec = pl.BlockSpec((1, D), _fix)
    til = pl.BlockSpec((tc, D), _row)
    return pl.pallas_call(
        body, grid=(T // tc,), in_specs=[til, pl.BlockSpec((HALO, D), _fix), vec, vec, vec],
        out_specs=[til, til], out_shape=[SDS((T, D), f32), SDS((T, D), bf16)],
        scratch_shapes=[pltpu.VMEM((tc + HALO, D), f32)], compiler_params=_cparams(1), name=name)(p, w_dw, b_dw, ln_g, ln_b)


def _conv_bwd(ds, q, p, a, gl, w_dw, ln_g, ln_b, name, ride=()):
    T = q.shape[0]
    tc = min(TC, T)
    n = T // tc
    nx = len(ride)

    def body(ds_ref, q_ref, p_ref, a_ref, gl_ref, w_ref, g_ref, bb_ref, *rest):
        x_refs, (da_ref, dgl_ref, acc_ref, dw_ref), xo_refs = rest[:nx], rest[nx:nx + 4], rest[nx + 4:2 * nx + 4]
        ext, sems = rest[2 * nx + 4], rest[2 * nx + 5:]
        i = pl.program_id(0)
        if nx:
            _exchange_start(x_refs, xo_refs, sems, False, i == 0)

        @pl.when(i == 0)
        def _():
            acc_ref[...] = jnp.zeros_like(acc_ref)
            dw_ref[...] = jnp.zeros_like(dw_ref)
            ext[tc:tc + HALO, :] = jnp.zeros((HALO, D), f32)

        @pl.when(i > 0)
        def _():
            ext[tc:tc + HALO, :] = ext[0:HALO, :]

        gv = g_ref[...]
        z, rstd, l = _layernorm_parts(q_ref[...], gv, bb_ref[...])
        sg = jax.nn.sigmoid(l)
        dl = ds_ref[...] * (sg * (1.0 + l * (1.0 - sg)))
        dz = dl * gv
        dq = rstd * (dz - jnp.mean(dz, axis=-1, keepdims=True) - z * jnp.mean(dz * z, axis=-1, keepdims=True))
        ext[0:tc, :] = dq.astype(bf16).astype(f32)
        acc_ref[0:1, :] += jnp.sum(dl * z, axis=0, keepdims=True)
        acc_ref[1:2, :] += jnp.sum(dl, axis=0, keepdims=True)
        acc_ref[2:3, :] += jnp.sum(dq, axis=0, keepdims=True)
        for cb in range(D // LANES):
            cols = slice(cb * LANES, (cb + 1) * LANES)
            pc = p_ref[:, cols].astype(f32)
            dp = jnp.zeros((tc, LANES), f32)
            for k in range(KW):
                sl = ext[pl.ds(KW - 1 - k, tc), cols]
                dp = dp + w_ref[k:k + 1, cols] * sl
                dw_ref[k:k + 1, cols] += jnp.sum(sl * pc, axis=0, keepdims=True)
            av = a_ref[:, cols].astype(f32)
            sgl = jax.nn.sigmoid(gl_ref[:, cols].astype(f32))
            da = dp * sgl
            dgl = dp * av * (sgl * (1.0 - sgl))
            da_ref[:, cols] = da.astype(bf16)
            dgl_ref[:, cols] = dgl.astype(bf16)
            acc_ref[3:4, cols] += jnp.sum(da, axis=0, keepdims=True)
            acc_ref[4:5, cols] += jnp.sum(dgl, axis=0, keepdims=True)
        if nx:
            _exchange_wait(x_refs, xo_refs, sems, False, i == n - 1)

    rev = lambda i: (n - 1 - i, 0)
    til = pl.BlockSpec((tc, D), rev)
    vec = pl.BlockSpec((1, D), _fix)
    x_in, x_out, x_shape, x_sems = _exchange_shapes(ride, False) if nx else ([], [], [], [])
    return pl.pallas_call(
        body, grid=(n,), in_specs=[til, til, til, til, til, pl.BlockSpec((HALO, D), _fix), vec, vec] + x_in,
        out_specs=[til, til, pl.BlockSpec((8, D), _fix), pl.BlockSpec((HALO, D), _fix)] + x_out,
        out_shape=[SDS((T, D), bf16), SDS((T, D), bf16), SDS((8, D), f32), SDS((HALO, D), f32)] + x_shape,
        scratch_shapes=[pltpu.VMEM((tc + HALO, D), f32)] + x_sems, compiler_params=_cparams(1),
        name=name)(ds, q, p, a, gl, w_dw, ln_g, ln_b, *ride)


def _tri(n, upper):
    r = lax.broadcasted_iota(jnp.int32, (n, n), 0)
    c = lax.broadcasted_iota(jnp.int32, (n, n), 1)
    return ((c >= r) if upper else (r >= c)).astype(f32)


def _forget_fwd(fl, fb, name):
    T = fl.shape[0]
    tc = min(TC, T)

    def body(fl_ref, fb_ref, cum_ref, carry):
        @pl.when(pl.program_id(0) == 0)
        def _():
            carry[...] = jnp.zeros_like(carry)
        xv = fl_ref[...] + fb_ref[...]
        lf = jnp.minimum(xv, 0.0) - jnp.log(1.0 + jnp.exp(-jnp.abs(xv)))
        cs = jnp.dot(_tri(tc, False), lf, preferred_element_type=f32, precision=lax.Precision.HIGHEST) + carry[0:1, :]
        cum_ref[...] = cs
        carry[0:1, :] = cs[tc - 1:tc, :]

    til = pl.BlockSpec((tc, LANES), _row)
    return pl.pallas_call(
        body, grid=(T // tc,), in_specs=[til, pl.BlockSpec((1, LANES), _fix)], out_specs=til,
        out_shape=SDS((T, LANES), f32), scratch_shapes=[pltpu.VMEM((8, LANES), f32)],
        compiler_params=_cparams(1), name=name)(fl, fb)


def _forget_bwd(dcum, fl, fb, name):
    T = fl.shape[0]
    tc = min(TC, T)
    n = T // tc

    def body(dc_ref, fl_ref, fb_ref, dfl_ref, acc_ref, carry):
        @pl.when(pl.program_id(0) == 0)
        def _():
            carry[...] = jnp.zeros_like(carry)
            acc_ref[...] = jnp.zeros_like(acc_ref)
        dlf = jnp.dot(_tri(tc, True), dc_ref[...], preferred_element_type=f32, precision=lax.Precision.HIGHEST) + carry[0:1, :]
        carry[0:1, :] = dlf[0:1, :]
        dfl = dlf * (1.0 - jax.nn.sigmoid(fl_ref[...] + fb_ref[...]))
        dfl_ref[...] = dfl.astype(bf16)
        acc_ref[0:1, :] += jnp.sum(dfl, axis=0, keepdims=True)

    rev = lambda i: (n - 1 - i, 0)
    til = pl.BlockSpec((tc, LANES), rev)
    return pl.pallas_call(
        body, grid=(n,), in_specs=[til, til, pl.BlockSpec((1, LANES), _fix)],
        out_specs=[til, pl.BlockSpec((8, LANES), _fix)], out_shape=[SDS((T, LANES), bf16), SDS((8, LANES), f32)],
        scratch_shapes=[pltpu.VMEM((8, LANES), f32)], compiler_params=_cparams(1), name=name)(dcum, fl, fb)


def _causal(s, n):
    r = lax.broadcasted_iota(jnp.int32, (n, n), 0)
    c = lax.broadcasted_iota(jnp.int32, (n, n), 1)
    return jnp.where(c <= r, s, NEG)


def _attn_fwd(q, k, v, cum_t, name, ride=()):
    T = q.shape[0]
    tq = min(TQ, T)
    n = T // tq
    nx = len(ride)

    def body(q_ref, k_ref, v_ref, ck_ref, *rest):
        x_refs, (o_ref, o32_ref, st_ref), xo_refs = rest[:nx], rest[nx:nx + 3], rest[nx + 3:2 * nx + 3]
        (m_sc, l_sc, acc_sc, res_sc), sems = rest[2 * nx + 3:2 * nx + 7], rest[2 * nx + 7:]
        i = pl.program_id(1)
        if nx:
            _exchange_start(x_refs, xo_refs, sems, True, (pl.program_id(0) == 0) & (i == 0))
        lane = lax.broadcasted_iota(jnp.int32, (1, LANES), 1)
        lo = lane < HD
        q2 = q_ref[...]
        zero = jnp.zeros_like(q2)
        qa = (jnp.where(lo, q2, zero), jnp.where(lo, zero, q2))
        m_sc[...] = jnp.full(m_sc.shape, NEG, f32)
        l_sc[...] = jnp.zeros_like(l_sc)
        acc_sc[...] = jnp.zeros_like(acc_sc)
        res_sc[...] = jnp.zeros_like(res_sc)

        def block(j, masked):
            off = pl.multiple_of(j * tq, tq)
            k2 = k_ref[pl.ds(off, tq), :]
            v2 = v_ref[pl.ds(off, tq), :]
            for a in range(2):
                s = _dot_nt(qa[a], k2) - ck_ref[0, a:a + 1, pl.ds(off, tq)]
                if masked:
                    s = _causal(s, tq)
                m_old = m_sc[a]
                m_new = jnp.maximum(m_old, jnp.max(s, axis=1, keepdims=True))
                alpha = jnp.exp(m_old - m_new)
                pm = jnp.exp(s - m_new)
                pb = pm.astype(bf16)
                pr = (pm - pb.astype(f32)).astype(bf16)
                l_sc[a] = alpha * l_sc[a] + jnp.sum(pm, axis=1, keepdims=True)
                acc_sc[a] = alpha * acc_sc[a] + jnp.dot(pb, v2, preferred_element_type=f32)
                res_sc[a] = alpha * res_sc[a] + jnp.dot(pr, v2, preferred_element_type=f32)
                m_sc[a] = m_new

        def step(j, carry):
            block(j, False)
            return carry

        lax.fori_loop(0, i, step, 0)
        block(i, True)
        o_ref[...] = jnp.where(lo, acc_sc[0] / l_sc[0], acc_sc[1] / l_sc[1]).astype(bf16)
        o32_ref[...] = jnp.where(lo, (acc_sc[0] + res_sc[0]) / l_sc[0], (acc_sc[1] + res_sc[1]) / l_sc[1])
        lse0 = m_sc[0] + jnp.log(l_sc[0])
        lse1 = m_sc[1] + jnp.log(l_sc[1])
        st_ref[0] = jnp.where(lane == 0, lse0, jnp.where(lane == 1, lse1, 0.0))
        if nx:
            _exchange_wait(x_refs, xo_refs, sems, True, (pl.program_id(0) == NP - 1) & (i == n - 1))

    full = lambda blk: pl.BlockSpec((T, LANES), blk)
    x_in, x_out, x_shape, x_sems = _exchange_shapes(ride, True) if nx else ([], [], [], [])
    return pl.pallas_call(
        body, grid=(NP, n),
        in_specs=[pl.BlockSpec((tq, LANES), lambda p, i: (i, p)), full(lambda p, i: (0, p)), full(lambda p, i: (0, p)),
                  pl.BlockSpec((1, 2, T), lambda p, i: (p, 0, 0))] + x_in,
        out_specs=[pl.BlockSpec((tq, LANES), lambda p, i: (i, p)), pl.BlockSpec((tq, LANES), lambda p, i: (i, p)),
                   pl.BlockSpec((1, tq, LANES), lambda p, i: (p, i, 0))] + x_out,
        out_shape=[SDS((T, H * HD), bf16), SDS((T, H * HD), f32), SDS((NP, T, LANES), f32)] + x_shape,
        scratch_shapes=[pltpu.VMEM((2, tq, 1), f32), pltpu.VMEM((2, tq, 1), f32), pltpu.VMEM((2, tq, LANES), f32),
                        pltpu.VMEM((2, tq, LANES), f32)] + x_sems,
        compiler_params=_cparams(2), name=name)(q, k, v, cum_t, *ride)


def _attn_stats(do, o, lse, name):
    T = do.shape[0]
    tm = min(TM, T)

    def body(do_ref, o_ref, lse_ref, st_ref):
        lane = lax.broadcasted_iota(jnp.int32, (1, LANES), 1)
        prod = do_ref[...].astype(f32) * o_ref[...].astype(f32)
        d0 = jnp.sum(jnp.where(lane < HD, prod, 0.0), axis=1, keepdims=True)
        d1 = jnp.sum(jnp.where(lane < HD, 0.0, prod), axis=1, keepdims=True)
        st_ref[0] = jnp.where(lane < 2, lse_ref[0], jnp.where(lane == 2, d0, jnp.where(lane == 3, d1, 0.0)))

    til = pl.BlockSpec((tm, LANES), lambda p, i: (i, p))
    stt = pl.BlockSpec((1, tm, LANES), lambda p, i: (p, i, 0))
    return pl.pallas_call(
        body, grid=(NP, T // tm), in_specs=[til, til, stt], out_specs=stt, out_shape=SDS((NP, T, LANES), f32),
        compiler_params=_cparams(2), name=name)(do, o, lse)


def _attn_bwd(q, k, v, do, st, cum_t, name, ride=()):
    T = q.shape[0]
    tq = min(TQ, T)
    n = T // tq
    nx = len(ride)

    def body(q_ref, k_ref, v_ref, do_ref, st_ref, ck_ref, *rest):
        x_refs, (dq_ref, dk_ref, dv_ref, dck_ref), xo_refs = rest[:nx], rest[nx:nx + 4], rest[nx + 4:2 * nx + 4]
        (dk_sc, dv_sc, dck_sc), sems = rest[2 * nx + 4:2 * nx + 7], rest[2 * nx + 7:]
        j = pl.program_id(1)
        if nx:
            _exchange_start(x_refs, xo_refs, sems, False, (pl.program_id(0) == 0) & (j == 0))
        lane = lax.broadcasted_iota(jnp.int32, (1, LANES), 1)
        lo = lane < HD

        @pl.when(j == 0)
        def _():
            dq_ref[...] = jnp.zeros_like(dq_ref)

        k2 = k_ref[...]
        v2 = v_ref[...]
        zero = jnp.zeros_like(k2)
        ka = (jnp.where(lo, k2, zero), jnp.where(lo, zero, k2))
        va = (jnp.where(lo, v2, zero), jnp.where(lo, zero, v2))
        dk_sc[...] = jnp.zeros_like(dk_sc)
        dv_sc[...] = jnp.zeros_like(dv_sc)
        dck_sc[...] = jnp.zeros_like(dck_sc)

        def block(i, masked):
            off = pl.multiple_of(i * tq, tq)
            q2 = q_ref[pl.ds(off, tq), :]
            do2 = do_ref[pl.ds(off, tq), :]
            stt = st_ref[0, pl.ds(off, tq), :]
            parts = []
            for a in range(2):
                s = _dot_nt(q2, ka[a]) - ck_ref[0, a:a + 1, :]
                if masked:
                    s = _causal(s, tq)
                pm = jnp.exp(s - stt[:, a:a + 1])
                dp = _dot_nt(do2, va[a])
                dsm = pm * (dp - stt[:, 2 + a:3 + a])
                dsb = dsm.astype(bf16)
                dv_sc[a] += _dot_tn(pm.astype(bf16), do2)
                dk_sc[a] += _dot_tn(dsb, q2)
                dck_sc[a:a + 1, :] -= jnp.sum(dsm, axis=0, keepdims=True)
                parts.append(jnp.dot(dsb, k2, preferred_element_type=f32))
            dq_ref[pl.ds(off, tq), :] += jnp.where(lo, parts[0], parts[1])

        block(j, True)

        def step(i, carry):
            block(i, False)
            return carry

        lax.fori_loop(j + 1, n, step, 0)
        dk_ref[...] = jnp.where(lo, dk_sc[0], dk_sc[1]).astype(bf16)
        dv_ref[...] = jnp.where(lo, dv_sc[0], dv_sc[1]).astype(bf16)
        dck_ref[0] = dck_sc[0:2, :]

        @pl.when(j == n - 1)
        def _():
            dq_ref[...] = dq_ref[...] * (HD ** -0.5)

        if nx:
            _exchange_wait(x_refs, xo_refs, sems, False, (pl.program_id(0) == NP - 1) & (j == n - 1))

    full = lambda: pl.BlockSpec((T, LANES), lambda p, j: (0, p))
    kvb = lambda: pl.BlockSpec((tq, LANES), lambda p, j: (j, p))
    ckb = lambda: pl.BlockSpec((1, 2, tq), lambda p, j: (p, 0, j))
    x_in, x_out, x_shape, x_sems = _exchange_shapes(ride, False) if nx else ([], [], [], [])
    return pl.pallas_call(
        body, grid=(NP, n),
        in_specs=[full(), kvb(), kvb(), full(), pl.BlockSpec((1, T, LANES), lambda p, j: (p, 0, 0)), ckb()] + x_in,
        out_specs=[full(), kvb(), kvb(), ckb()] + x_out,
        out_shape=[SDS((T, H * HD), f32), SDS((T, H * HD), bf16), SDS((T, H * HD), bf16), SDS((NP, 2, T), f32)] + x_shape,
        scratch_shapes=[pltpu.VMEM((2, tq, LANES), f32), pltpu.VMEM((2, tq, LANES), f32), pltpu.VMEM((8, tq), f32)] + x_sems,
        compiler_params=_cparams(2), name=name)(q, k, v, do, st, cum_t, *ride)


def _ada_fwd(c_all, w_cat, name):
    n = w_cat.shape[1]
    tn = 256

    def body(c_ref, w_ref, o_ref):
        cv = c_ref[...]
        o_ref[...] = jnp.dot((cv * jax.nn.sigmoid(cv)).astype(bf16), w_ref[...].astype(bf16), preferred_element_type=f32)

    return pl.pallas_call(
        body, grid=(n // tn,), in_specs=[pl.BlockSpec((NDEV, D), _fix), pl.BlockSpec((D, tn), lambda i: (0, i))],
        out_specs=pl.BlockSpec((NDEV, tn), lambda i: (0, i)), out_shape=SDS((NDEV, n), f32),
        compiler_params=_cparams(1), name=name)(c_all, w_cat)


def _ada_bwd(c_all_t, dsel, name):
    n = dsel.shape[1]
    tn = 256

    def body(c_ref, d_ref, o_ref):
        cv = c_ref[...]
        ca = cv * jax.nn.sigmoid(cv)
        acc = ca[:, 0:1] * d_ref[0:1, :]
        for b in range(1, NDEV):
            acc = acc + ca[:, b:b + 1] * d_ref[b:b + 1, :]
        o_ref[...] = acc

    return pl.pallas_call(
        body, grid=(n // tn,), in_specs=[pl.BlockSpec((D, NDEV), _fix), pl.BlockSpec((NDEV, tn), lambda i: (0, i))],
        out_specs=pl.BlockSpec((D, tn), lambda i: (0, i)), out_shape=SDS((D, n), f32),
        compiler_params=_cparams(1), name=name)(c_all_t, dsel)


def _sum_parts(parts, name):
    R = parts.shape[1]

    def body(p_ref, o_ref):
        acc = p_ref[0]
        for j in range(1, NDEV):
            acc = acc + p_ref[j]
        o_ref[...] = acc

    return pl.pallas_call(body, out_shape=SDS((R, LANES), f32), name=name)(parts)


def _adamw(g_parts, w, m, v, name):
    n_parts, R, C = g_parts.shape
    tr = next(t for t in (256, 128, 64, 32, 16, 8) if R % t == 0)
    c1 = 1.0 / (1.0 - ADAM_B1 ** ADAM_STEP)
    c2 = 1.0 / (1.0 - ADAM_B2 ** ADAM_STEP)

    def body(g_ref, w_ref, m_ref, v_ref, go_ref, d_ref, mo_ref, vo_ref):
        g = g_ref[0].astype(f32)
        for j in range(1, n_parts):
            g = g + g_ref[j].astype(f32)
        mn = ADAM_B1 * m_ref[...] + (1.0 - ADAM_B1) * g
        vn = ADAM_B2 * v_ref[...] + (1.0 - ADAM_B2) * (g * g)
        go_ref[...] = g
        mo_ref[...] = mn
        vo_ref[...] = vn
        d_ref[...] = -ADAM_LR * ((mn * c1) / (jnp.sqrt(vn * c2) + ADAM_EPS) + ADAM_WD * w_ref[...])

    til = pl.BlockSpec((tr, C), _row)
    out = SDS((R, C), f32)
    return pl.pallas_call(
        body, grid=(R // tr,), in_specs=[pl.BlockSpec((n_parts, tr, C), lambda i: (0, i, 0)), til, til, til],
        out_specs=[til, til, til, til], out_shape=[out, out, out, out],
        compiler_params=_cparams(1), name=name)(g_parts, w, m, v)


def _pad_rows(flat, cols, mult):
    n = flat.shape[-1]
    rows = -(-n // cols)
    rows = -(-rows // mult) * mult
    pad = [(0, 0)] * (flat.ndim - 1) + [(0, rows * cols - n)]
    return jnp.pad(flat, pad).reshape(flat.shape[:-1] + (rows, cols))


def _split_flat(flat, shapes):
    out, off = {}, 0
    for name, shp in shapes:
        n = 1
        for d in shp:
            n *= d
        out[name] = flat[off:off + n].reshape(shp)
        off += n
    return out


def kernel(x, c, mix_norm_g, mix_ada_w, mix_ada_b, ffn_norm_g, ffn_ada_w, ffn_ada_b, ffn_w_in, ffn_w_out, conv_w_in, conv_b_in, conv_w_dw, conv_b_dw, conv_ln_g, conv_ln_b, conv_w_out, conv_b_out, kv_norm_g, kv_ada_w, kv_ada_b, kv_w, forget_b, attn_w_q, attn_w_o, final_norm_g, loss_target, m_mix_norm_g, m_mix_ada_w, m_mix_ada_b, m_ffn_norm_g, m_ffn_ada_w, m_ffn_ada_b, m_ffn_w_in, m_ffn_w_out, m_conv_w_in, m_conv_b_in, m_conv_w_dw, m_conv_b_dw, m_conv_ln_g, m_conv_ln_b, m_conv_w_out, m_conv_b_out, m_kv_norm_g, m_kv_ada_w, m_kv_ada_b, m_kv_w, m_forget_b, m_attn_w_q, m_attn_w_o, m_final_norm_g, v_mix_norm_g, v_mix_ada_w, v_mix_ada_b, v_ffn_norm_g, v_ffn_ada_w, v_ffn_ada_b, v_ffn_w_in, v_ffn_w_out, v_conv_w_in, v_conv_b_in, v_conv_w_dw, v_conv_b_dw, v_conv_ln_g, v_conv_ln_b, v_conv_w_out, v_conv_b_out, v_kv_norm_g, v_kv_ada_w, v_kv_ada_b, v_kv_w, v_forget_b, v_attn_w_q, v_attn_w_o, v_final_norm_g):
    W = dict(mix_norm_g=mix_norm_g, mix_ada_w=mix_ada_w, mix_ada_b=mix_ada_b, ffn_norm_g=ffn_norm_g, ffn_ada_w=ffn_ada_w, ffn_ada_b=ffn_ada_b, ffn_w_in=ffn_w_in, ffn_w_out=ffn_w_out, conv_w_in=conv_w_in, conv_b_in=conv_b_in, conv_w_dw=conv_w_dw, conv_b_dw=conv_b_dw, conv_ln_g=conv_ln_g, conv_ln_b=conv_ln_b, conv_w_out=conv_w_out, conv_b_out=conv_b_out, kv_norm_g=kv_norm_g, kv_ada_w=kv_ada_w, kv_ada_b=kv_ada_b, kv_w=kv_w, forget_b=forget_b, attn_w_q=attn_w_q, attn_w_o=attn_w_o, final_norm_g=final_norm_g)
    M = dict(mix_norm_g=m_mix_norm_g, mix_ada_w=m_mix_ada_w, mix_ada_b=m_mix_ada_b, ffn_norm_g=m_ffn_norm_g, ffn_ada_w=m_ffn_ada_w, ffn_ada_b=m_ffn_ada_b, ffn_w_in=m_ffn_w_in, ffn_w_out=m_ffn_w_out, conv_w_in=m_conv_w_in, conv_b_in=m_conv_b_in, conv_w_dw=m_conv_w_dw, conv_b_dw=m_conv_b_dw, conv_ln_g=m_conv_ln_g, conv_ln_b=m_conv_ln_b, conv_w_out=m_conv_w_out, conv_b_out=m_conv_b_out, kv_norm_g=m_kv_norm_g, kv_ada_w=m_kv_ada_w, kv_ada_b=m_kv_ada_b, kv_w=m_kv_w, forget_b=m_forget_b, attn_w_q=m_attn_w_q, attn_w_o=m_attn_w_o, final_norm_g=m_final_norm_g)
    V = dict(mix_norm_g=v_mix_norm_g, mix_ada_w=v_mix_ada_w, mix_ada_b=v_mix_ada_b, ffn_norm_g=v_ffn_norm_g, ffn_ada_w=v_ffn_ada_w, ffn_ada_b=v_ffn_ada_b, ffn_w_in=v_ffn_w_in, ffn_w_out=v_ffn_w_out, conv_w_in=v_conv_w_in, conv_b_in=v_conv_b_in, conv_w_dw=v_conv_w_dw, conv_b_dw=v_conv_b_dw, conv_ln_g=v_conv_ln_g, conv_ln_b=v_conv_ln_b, conv_w_out=v_conv_w_out, conv_b_out=v_conv_b_out, kv_norm_g=v_kv_norm_g, kv_ada_w=v_kv_ada_w, kv_ada_b=v_kv_ada_b, kv_w=v_kv_w, forget_b=v_forget_b, attn_w_q=v_attn_w_q, attn_w_o=v_attn_w_o, final_norm_g=v_final_norm_g)
    names = list(W)
    T = x.shape[1]
    me = _my_index()
    x0 = x[0]
    tgt = loss_target[0]
    row = lambda vct: vct.reshape(1, -1)

    small_names = ("conv_b_in", "conv_w_dw", "conv_b_dw", "conv_ln_g", "conv_ln_b", "conv_b_out")
    small_loc = jnp.concatenate([c.reshape(-1)] + [W[n].reshape(-1) for n in small_names])
    sg = _exchange([_pad_rows(small_loc, LANES, 8)], True, "gather_small")[0].reshape(NDEV, -1)
    c_all = sg[:, :D]
    off = D
    b_in = sg[:, off:off + 2 * D // NDEV].reshape(1, 2 * D); off += 2 * D // NDEV
    cl = D // NDEV
    w_dw = sg[:, off:off + KW * cl].reshape(NDEV, KW, cl).transpose(1, 0, 2).reshape(KW, D); off += KW * cl
    w_dw = jnp.pad(w_dw, ((0, HALO - KW), (0, 0))).astype(bf16).astype(f32)
    b_dw = sg[:, off:off + cl].reshape(1, D); off += cl
    ln_g = sg[:, off:off + cl].reshape(1, D); off += cl
    ln_b = sg[:, off:off + cl].reshape(1, D); off += cl
    b_out = sg[:, off:off + cl].reshape(1, D)

    w_cat = jnp.concatenate([mix_ada_w[0], mix_ada_w[1], ffn_ada_w[0], ffn_ada_w[1], kv_ada_w], axis=1)
    ada_loc = _ada_fwd(c_all, w_cat, "ada_fwd")
    ada_all = _exchange([ada_loc], True, "gather_ada")[0]
    ada_me = lax.dynamic_index_in_dim(ada_all, me, axis=1, keepdims=False)
    ada_bias = (mix_ada_b[0], mix_ada_b[1], ffn_ada_b[0], ffn_ada_b[1], kv_ada_b)
    ada, off = [], 0
    for nl, bias in zip(ADA_LOC, ada_bias):
        full = ada_me[:, off:off + nl].reshape(-1) + bias
        ada.append([row(t) for t in jnp.split(full, full.shape[0] // D)])
        off += nl
    (sh_m0, sc_m0, gt_m0), (sh_m1, sc_m1, gt_m1), (sh_f0, sc_f0, gt_f0), (sh_f1, sc_f1, gt_f1), (sh_kv, sc_kv) = ada

    pad_in = lambda src, l: jnp.pad(src["ffn_w_in"][l], ((0, 0), (0, FSP - FS)))
    rows_a = lambda src: jnp.concatenate([src["ffn_w_out"][0], src["attn_w_q"][0]])
    rows_b = lambda src: jnp.concatenate([src["ffn_w_out"][1], src["attn_w_o"][0]])
    as_bf = lambda arrs: [t.astype(bf16) for t in arrs]
    fo, sq_rows = F // NDEV, D // NDEV
    w_sh0, g_ci, g_ra, g_co, g_kv = _exchange(
        as_bf([pad_in(W, 0), conv_w_in[0], rows_a(W), conv_w_out[0], kv_w]), True, "gather_weights")
    late = as_bf([pad_in(W, 1), rows_b(W)])

    def w_out_of(g_r):
        t = g_r[:, :fo].reshape(NDEV // 2, FS, D)
        return jnp.pad(t, ((0, 0), (0, FSP - FS), (0, 0))).reshape(FP, D)

    conv_in_full = g_ci.transpose(1, 0, 2).reshape(D, 2 * D)
    wc_a, wc_g = conv_in_full[:, :D], conv_in_full[:, D:]
    wc_o = g_co.reshape(D, D)
    w_out0, w_q = w_out_of(g_ra), g_ra[:, fo:].reshape(D, D)
    kv_full = g_kv.transpose(1, 0, 2).reshape(D, -1)
    w_k, w_v = kv_full[:, :D], kv_full[:, D:2 * D]
    w_f = jnp.pad(kv_full[:, 2 * D:], ((0, 0), (0, LANES - H)))
    zeros_d = jnp.zeros((1, D), f32)
    fb = jnp.pad(forget_b, (0, LANES - H)).reshape(1, LANES)

    h0 = _normmod(x0, row(mix_norm_g[0]), sh_m0, sc_m0, "norm_mix0")
    a0, gl0, p0 = _mm_gated(h0, wc_a, wc_g, b_in[:, :D], b_in[:, D:], False, "conv_in")
    q0, s0 = _conv_fwd(p0, w_dw, b_dw, ln_g, ln_b, "conv_dw")
    x1, y0 = _mm_res(s0, wc_o, b_out, x0, gt_m0, "conv_out")

    def ffn_fwd(xin, l, sh, sc, gt, w_sh, w_out):
        h = _normmod(xin, row(ffn_norm_g[l]), sh, sc, f"norm_ffn{l}")
        ug, uu, act = _ffn_in(h, w_sh, f"ffn_in{l}")
        xo, y = _mm_res(act, w_out, zeros_d, xin, gt, f"ffn_out{l}")
        return xo, (h, ug, uu, act, y, w_sh, w_out)

    x2, ffn0 = ffn_fwd(x1, 0, sh_f0, sc_f0, gt_f0, w_sh0, w_out0)

    hk = _normmod(x2, row(kv_norm_g), sh_kv, sc_kv, "norm_kv")
    k_sh = _mm(hk, w_k, bf16, 1.0, "proj_k")
    v_sh = _mm(hk, w_v, bf16, 1.0, "proj_v")
    fl = _mm(hk, w_f, f32, 1.0, "proj_f")
    cum = _forget_fwd(fl, fb, "forget_fwd")
    cum_t = cum[:, :H].T.reshape(NP, 2, T)

    h2 = _normmod(x2, row(mix_norm_g[1]), sh_m1, sc_m1, "norm_mix1")
    qh = _mm(h2, w_q, bf16, HD ** -0.5, "proj_q")
    o, o32, lse, w_sh1, g_rb = _attn_fwd(qh, k_sh, v_sh, cum_t, "attn_fwd", ride=late)
    w_out1, w_o = w_out_of(g_rb), g_rb[:, fo:].reshape(D, D)
    x3, y1 = _mm_res(o, w_o, zeros_d, x2, gt_m1, "attn_out")

    x4, ffn1 = ffn_fwd(x3, 1, sh_f1, sc_f1, gt_f1, w_sh1, w_out1)

    dx4, acc_fin = _final_bwd(x4, row(final_norm_g), tgt, "final_bwd")

    d_ada = {}
    by_rows = lambda g: g.reshape(NDEV, sq_rows, D)

    def ffn_bwd(dx_out, xin, l, sc, gt, saved):
        h, ug, uu, act, y, w_sh, w_out = saved
        dyb, acc_r = _res_in(dx_out, y, gt, f"ffn_res_bwd{l}")
        dug, duu = _mm_nt_swiglu(dyb, w_out, ug, uu, f"ffn_dact{l}")
        g_out = _mm_tn(act, dyb, f"ffn_dw_out{l}").reshape(NDEV // 2, FSP, D)[:, :FS].reshape(NDEV, fo, D)
        g_in = jnp.concatenate([_mm_tn(h, dug, f"ffn_dw_gate{l}", FSP), _mm_tn(h, duu, f"ffn_dw_up{l}", FSP)])
        dh = _ffn_dh(dug, duu, w_sh, f"ffn_dh{l}")
        dxi, acc_n = _normmod_bwd(dh, xin, row(ffn_norm_g[l]), sc, dx_out, f"norm_ffn_bwd{l}")
        return dxi, g_in, g_out, [acc_n[0:1], acc_n[1:2], acc_r[0:1]], acc_n[2]

    dx3, g_in1, g_out1, d_ada[("ffn", 1)], dg_ffn1 = ffn_bwd(dx4, x3, 1, sc_f1, gt_f1, ffn1)

    dyb, acc_r = _res_in(dx3, y1, gt_m1, "attn_res_bwd")
    do = _mm_nt([(dyb, w_o)], bf16, "attn_do")
    g_wo = _mm_tn(o, dyb, "attn_dw_o")
    st = _attn_stats(do, o32, lse, "attn_stats")
    leave_b = as_bf([g_in1, jnp.concatenate([g_out1, by_rows(g_wo)], axis=1)])
    dq, dk, dv, dck, r_in1, r_rb = _attn_bwd(qh, k_sh, v_sh, do, st, cum_t, "attn_bwd", ride=leave_b)
    g_wq = _mm_tn(h2, dq, "attn_dw_q")
    dh2 = _mm_nt([(dq, w_q)], f32, "attn_dh")
    dx2, acc_n = _normmod_bwd(dh2, x2, row(mix_norm_g[1]), sc_m1, dx3, "norm_mix_bwd1")
    d_ada[("mix", 1)] = [acc_n[0:1], acc_n[1:2], acc_r[0:1]]
    dg_mix1 = acc_n[2]

    dcum = jnp.pad(dck.reshape(H, T).T, ((0, 0), (0, LANES - H)))
    dfl, acc_f = _forget_bwd(dcum, fl, fb, "forget_bwd")
    g_kvw = jnp.concatenate([_mm_tn(hk, dk, "kv_dw_k"), _mm_tn(hk, dv, "kv_dw_v"), _mm_tn(hk, dfl, "kv_dw_f")[:, :H]], axis=1)
    dhk = _mm_nt([(dk, w_k), (dv, w_v), (dfl, w_f)], f32, "kv_dh")
    dx2, acc_n = _normmod_bwd(dhk, x2, row(kv_norm_g), sc_kv, dx2, "norm_kv_bwd")
    d_ada[("kv", 0)] = [acc_n[0:1], acc_n[1:2]]
    dg_kv = acc_n[2]

    dx1, g_in0, g_out0, d_ada[("ffn", 0)], dg_ffn0 = ffn_bwd(dx2, x1, 0, sc_f0, gt_f0, ffn0)

    dyb, acc_r = _res_in(dx1, y0, gt_m0, "conv_res_bwd")
    dsw = _mm_nt([(dyb, wc_o)], f32, "conv_ds")
    g_co_out = _mm_tn(s0, dyb, "conv_dw_out")
    leave_a = as_bf([g_in0, jnp.concatenate([g_out0, by_rows(g_wq)], axis=1), g_kvw.reshape(D, NDEV, -1).transpose(1, 0, 2)])
    da, dgl, acc_c, dw_dw, r_in0, r_ra, r_kv = _conv_bwd(dsw, q0, p0, a0, gl0, w_dw, ln_g, ln_b, "conv_bwd", ride=leave_a)
    cs = 2 * D // NDEV
    g_ci_out = jnp.concatenate([_mm_tn(h0, da, "conv_dw_a", cs), _mm_tn(h0, dgl, "conv_dw_g", cs)])
    dh0 = _mm_nt([(da, wc_a), (dgl, wc_g)], f32, "conv_dh")
    dx0, acc_n = _normmod_bwd(dh0, x0, row(mix_norm_g[0]), sc_m0, dx1, "norm_mix_bwd0")
    d_ada[("mix", 0)] = [acc_n[0:1], acc_n[1:2], acc_r[0:1]]
    dg_mix0 = acc_n[2]

    vec = [t.reshape(-1) for key in [(s[0], s[1]) for s in ADA_SEG] for t in d_ada[key]]
    vec += [dg_mix0, dg_mix1, dg_ffn0, dg_ffn1, dg_kv, acc_fin[0]]
    vec += [acc_f[0], acc_fin[1, :LANES]]
    vec += [acc_c[3], acc_c[4], dw_dw[:KW].reshape(-1), acc_c[2], acc_c[0], acc_c[1], acc_r[1]]
    small_parts = _exchange([_pad_rows(jnp.concatenate(vec), LANES, 8)], True, "gather_partials")[0]
    small_sum = _sum_parts(small_parts, "sum_partials").reshape(-1)
    d_ada_all = small_parts.reshape(NDEV, -1)[:, :ADA_TOT]
    off = 0
    gsm = {}
    ada_b_sum = []
    for _, _, n in ADA_SEG:
        ada_b_sum.append(small_sum[off:off + n]); off += n
    gsm["mix_ada_b"] = jnp.stack(ada_b_sum[0:2])
    gsm["ffn_ada_b"] = jnp.stack(ada_b_sum[2:4])
    gsm["kv_ada_b"] = ada_b_sum[4]
    gsm["mix_norm_g"] = small_sum[off:off + 2 * D].reshape(2, D); off += 2 * D
    gsm["ffn_norm_g"] = small_sum[off:off + 2 * D].reshape(2, D); off += 2 * D
    gsm["kv_norm_g"] = small_sum[off:off + D]; off += D
    gsm["final_norm_g"] = small_sum[off:off + D]; off += D
    gsm["forget_b"] = small_sum[off:off + H]; off += LANES
    loss = small_sum[off]; off += LANES
    sl = lambda full, width: lax.dynamic_slice_in_dim(full, me * width, width, axis=full.ndim - 1)
    gsm["conv_b_in"] = sl(small_sum[off:off + 2 * D].reshape(1, 2 * D), 2 * D // NDEV); off += 2 * D
    gsm["conv_w_dw"] = sl(small_sum[off:off + KW * D].reshape(1, KW, D), cl); off += KW * D
    for n in ("conv_b_dw", "conv_ln_g", "conv_ln_b", "conv_b_out"):
        gsm[n] = sl(small_sum[off:off + D].reshape(1, D), cl); off += D

    dsel, off = [], 0
    for (_, _, n), nl in zip(ADA_SEG, ADA_LOC):
        dsel.append(lax.dynamic_slice_in_dim(d_ada_all[:, off:off + n], me * nl, nl, axis=1)); off += n
    g_ada = _ada_bwd(c_all.T, jnp.concatenate(dsel, axis=1), "ada_bwd")
    gsm["mix_ada_w"] = jnp.stack([g_ada[:, 0:ADA_LOC[0]], g_ada[:, ADA_LOC[0]:2 * ADA_LOC[0]]])
    o2 = 2 * ADA_LOC[0]
    gsm["ffn_ada_w"] = jnp.stack([g_ada[:, o2:o2 + ADA_LOC[2]], g_ada[:, o2 + ADA_LOC[2]:o2 + 2 * ADA_LOC[2]]])
    gsm["kv_ada_w"] = g_ada[:, o2 + 2 * ADA_LOC[2]:]

    r_ci, r_co = _exchange(as_bf([g_ci_out, by_rows(g_co_out)]), False, "scatter_grads")
    res_in0 = _adamw(r_in0, *[pad_in(s, 0) for s in (W, M, V)], "adamw_ffn_in0")
    res_in1 = _adamw(r_in1, *[pad_in(s, 1) for s in (W, M, V)], "adamw_ffn_in1")
    res_ra = _adamw(r_ra, *[rows_a(s) for s in (W, M, V)], "adamw_rows_a")
    res_rb = _adamw(r_rb, *[rows_b(s) for s in (W, M, V)], "adamw_rows_b")
    res_ci = _adamw(r_ci, *[s["conv_w_in"][0] for s in (W, M, V)], "adamw_conv_in")
    res_co = _adamw(r_co, *[s["conv_w_out"][0] for s in (W, M, V)], "adamw_conv_out")
    res_kv = _adamw(r_kv, *[s["kv_w"] for s in (W, M, V)], "adamw_kv")
    rest = [n for n in names if n not in MAIN]
    pack_rest = lambda src: _pad_rows(jnp.concatenate([src[n].reshape(-1) for n in rest]), D, 256)
    res_rest = _adamw(pack_rest(gsm)[None], pack_rest(W), pack_rest(M), pack_rest(V), "adamw_rest")

    outs = []
    for k in range(4):
        ur = _split_flat(res_rest[k].reshape(-1), [(n, W[n].shape) for n in rest])
        ur["ffn_w_in"] = jnp.stack([res_in0[k][:, :FS], res_in1[k][:, :FS]])
        ur["conv_w_in"] = res_ci[k][None]
        ur["conv_w_out"] = res_co[k][None]
        ur["kv_w"] = res_kv[k]
        ur["ffn_w_out"] = jnp.stack([res_ra[k][:fo], res_rb[k][:fo]])
        ur["attn_w_q"] = res_ra[k][fo:][None]
        ur["attn_w_o"] = res_rb[k][fo:][None]
        outs.append(ur)
    grads, deltas, new_m, new_v = outs
    return (loss, dx0[None], *[grads[n] for n in names], *[deltas[n] for n in names],
            *[new_m[n] for n in names], *[new_v[n] for n in names])
```

```python
import functools

import jax
import jax.numpy as jnp
from jax import lax
from jax.experimental import pallas as pl
from jax.experimental.pallas import tpu as pltpu

f32, bf16 = jnp.float32, jnp.bfloat16
SDS = jax.ShapeDtypeStruct

D = 1024
F = 2816
H = 16
HD = 64
NP = H // 2
KW = 31
HALO = 32
NDEV = 8
FS = 2 * F // NDEV
FSP = 768
FP = 4 * FSP
EPS = 1e-6
NEG = -1e30
LANES = 128

ADAM_LR, ADAM_B1, ADAM_B2, ADAM_EPS, ADAM_WD, ADAM_STEP = 0.001, 0.9, 0.999, 1e-08, 0.01, 10

TM = 512
TC = 256
TQ = 1024
VMEM_LIMIT = 56 << 20

MAIN = ("ffn_w_in", "ffn_w_out", "conv_w_in", "conv_w_out", "kv_w", "attn_w_q", "attn_w_o")
ADA_SEG = (("mix", 0, 3 * D), ("mix", 1, 3 * D), ("ffn", 0, 3 * D), ("ffn", 1, 3 * D), ("kv", 0, 2 * D))
ADA_LOC = tuple(n // NDEV for _, _, n in ADA_SEG)
ADA_COLS = sum(ADA_LOC)
ADA_TOT = sum(n for _, _, n in ADA_SEG)


def _cparams(n_axes):
    return pltpu.CompilerParams(dimension_semantics=("arbitrary",) * n_axes, vmem_limit_bytes=VMEM_LIMIT)


def _mesh_pos():
    return lax.axis_index("x"), lax.axis_index("y"), lax.axis_index("c")


def _my_index():
    mx, my, mc = _mesh_pos()
    return 4 * mx + 2 * my + mc


def _peer(k, mx, my, mc):
    px = (1 - mx) if k & 4 else mx
    py = (1 - my) if k & 2 else my
    pc = (1 - mc) if k & 1 else mc
    return (px, py, pc), 4 * px + 2 * py + pc


def _exchange_copies(x_refs, o_refs, sems, gather):
    send_sems, recv_sems, local_sems = sems
    mx, my, mc = _mesh_pos()
    me = 4 * mx + 2 * my + mc
    copies = []
    for a, (x_ref, o_ref) in enumerate(zip(x_refs, o_refs)):
        copies.append(pltpu.make_async_copy(x_ref if gather else x_ref.at[me], o_ref.at[me], local_sems.at[a]))
        for k in range(1, NDEV):
            peer, pidx = _peer(k, mx, my, mc)
            sem = a * (NDEV - 1) + k - 1
            copies.append(pltpu.make_async_remote_copy(
                src_ref=x_ref if gather else x_ref.at[pidx], dst_ref=o_ref.at[me],
                send_sem=send_sems.at[sem], recv_sem=recv_sems.at[sem],
                device_id=peer, device_id_type=pl.DeviceIdType.MESH))
    return copies


def _exchange_shapes(xs, gather):
    n = len(xs)
    hbm = pl.BlockSpec(memory_space=pl.ANY)
    outs = [SDS((NDEV,) + tuple(x.shape if gather else x.shape[1:]), x.dtype) for x in xs]
    sems = [pltpu.SemaphoreType.DMA((n * (NDEV - 1),)), pltpu.SemaphoreType.DMA((n * (NDEV - 1),)), pltpu.SemaphoreType.DMA((n,))]
    return [hbm] * n, [hbm] * n, outs, sems


def _exchange_start(x_refs, o_refs, sems, gather, first):
    @pl.when(first)
    def _():
        for cp in _exchange_copies(x_refs, o_refs, sems, gather):
            cp.start()


def _exchange_wait(x_refs, o_refs, sems, gather, last):
    @pl.when(last)
    def _():
        for cp in _exchange_copies(x_refs, o_refs, sems, gather):
            cp.wait()


def _exchange(xs, gather, name):
    n = len(xs)

    def body(*refs):
        copies = _exchange_copies(refs[:n], refs[n:2 * n], refs[2 * n:], gather)
        for cp in copies:
            cp.start()
        for cp in copies:
            cp.wait()

    in_specs, out_specs, outs, sems = _exchange_shapes(xs, gather)
    return pl.pallas_call(body, out_shape=outs, in_specs=in_specs, out_specs=out_specs, scratch_shapes=sems, name=name)(*xs)


def _row(i):
    return (i, 0)


def _fix(i):
    return (0, 0)


def _normmod(x, g, shift, scale, name):
    T = x.shape[0]
    tm = min(TM, T)

    def body(x_ref, g_ref, sh_ref, sc_ref, h_ref):
        xv = x_ref[...]
        r = lax.rsqrt(jnp.mean(xv * xv, axis=-1, keepdims=True) + EPS)
        hn = (xv * r) * g_ref[...]
        h_ref[...] = (hn * (1.0 + sc_ref[...]) + sh_ref[...]).astype(bf16)

    vec = pl.BlockSpec((1, D), _fix)
    return pl.pallas_call(
        body, grid=(T // tm,), in_specs=[pl.BlockSpec((tm, D), _row), vec, vec, vec],
        out_specs=pl.BlockSpec((tm, D), _row), out_shape=SDS((T, D), bf16),
        compiler_params=_cparams(1), name=name)(x, g, shift, scale)


def _normmod_bwd(dh, x, g, scale, dx_res, name):
    T = x.shape[0]
    tm = min(TM, T)

    def body(dh_ref, x_ref, g_ref, sc_ref, res_ref, dx_ref, acc_ref):
        @pl.when(pl.program_id(0) == 0)
        def _():
            acc_ref[...] = jnp.zeros_like(acc_ref)
        xv = x_ref[...]
        dhv = dh_ref[...]
        gv = g_ref[...]
        r = lax.rsqrt(jnp.mean(xv * xv, axis=-1, keepdims=True) + EPS)
        xn = xv * r
        dhn = dhv * (1.0 + sc_ref[...])
        dxn = dhn * gv
        dx_ref[...] = res_ref[...] + r * (dxn - xn * jnp.mean(dxn * xn, axis=-1, keepdims=True))
        acc_ref[0:1, :] += jnp.sum(dhv, axis=0, keepdims=True)
        acc_ref[1:2, :] += jnp.sum(dhv * (xn * gv), axis=0, keepdims=True)
        acc_ref[2:3, :] += jnp.sum(dhn * xn, axis=0, keepdims=True)

    vec = pl.BlockSpec((1, D), _fix)
    til = pl.BlockSpec((tm, D), _row)
    return pl.pallas_call(
        body, grid=(T // tm,), in_specs=[til, til, vec, vec, til],
        out_specs=[til, pl.BlockSpec((8, D), _fix)], out_shape=[SDS((T, D), f32), SDS((8, D), f32)],
        compiler_params=_cparams(1), name=name)(dh, x, g, scale, dx_res)


def _res_in(dx, y, gate, name):
    T = dx.shape[0]
    tm = min(TM, T)

    def body(dx_ref, y_ref, gt_ref, dy_ref, acc_ref):
        @pl.when(pl.program_id(0) == 0)
        def _():
            acc_ref[...] = jnp.zeros_like(acc_ref)
        dxv = dx_ref[...]
        dy = dxv * gt_ref[...]
        dy_ref[...] = dy.astype(bf16)
        acc_ref[0:1, :] += jnp.sum(dxv * y_ref[...].astype(f32), axis=0, keepdims=True)
        acc_ref[1:2, :] += jnp.sum(dy, axis=0, keepdims=True)

    til = pl.BlockSpec((tm, D), _row)
    return pl.pallas_call(
        body, grid=(T // tm,), in_specs=[til, til, pl.BlockSpec((1, D), _fix)],
        out_specs=[til, pl.BlockSpec((8, D), _fix)], out_shape=[SDS((T, D), bf16), SDS((8, D), f32)],
        compiler_params=_cparams(1), name=name)(dx, y, gate)


def _final_bwd(x, g, tgt, name):
    T = x.shape[0]
    tm = min(TM, T)

    def body(x_ref, g_ref, t_ref, dx_ref, acc_ref):
        @pl.when(pl.program_id(0) == 0)
        def _():
            acc_ref[...] = jnp.zeros_like(acc_ref)
        xv = x_ref[...]
        gv = g_ref[...]
        r = lax.rsqrt(jnp.mean(xv * xv, axis=-1, keepdims=True) + EPS)
        xn = xv * r
        err = xn * gv - t_ref[...]
        dy = err * (1.0 / D)
        dxn = dy * gv
        dx_ref[...] = r * (dxn - xn * jnp.mean(dxn * xn, axis=-1, keepdims=True))
        acc_ref[0:1, :] += jnp.sum(dy * xn, axis=0, keepdims=True)
        acc_ref[1:2, :] += 0.5 * jnp.sum(jnp.mean(err * err, axis=-1, keepdims=True))

    til = pl.BlockSpec((tm, D), _row)
    return pl.pallas_call(
        body, grid=(T // tm,), in_specs=[til, pl.BlockSpec((1, D), _fix), til],
        out_specs=[til, pl.BlockSpec((8, D), _fix)], out_shape=[SDS((T, D), f32), SDS((8, D), f32)],
        compiler_params=_cparams(1), name=name)(x, g, tgt)


def _col_tile(n):
    if n <= 1024:
        return min(n, 512) if n % 512 == 0 else n
    return 1408 if n % 1408 == 0 else 1024


def _mm_gated(h, wa, wb, ba, bb, swiglu, name):
    T, K = h.shape
    N = wa.shape[1]
    tm, tn = min(TM, T), _col_tile(N)

    def body(h_ref, wa_ref, wb_ref, ba_ref, bb_ref, u_ref, w_ref, p_ref):
        hv = h_ref[...]
        u = jnp.dot(hv, wa_ref[...], preferred_element_type=f32) + ba_ref[...]
        w = jnp.dot(hv, wb_ref[...], preferred_element_type=f32) + bb_ref[...]
        u_ref[...] = u
        w_ref[...] = w
        if swiglu:
            p_ref[...] = ((u * jax.nn.sigmoid(u)) * w).astype(p_ref.dtype)
        else:
            p_ref[...] = (u * jax.nn.sigmoid(w)).astype(p_ref.dtype)

    wsp = pl.BlockSpec((K, tn), lambda i, j: (0, j))
    bsp = pl.BlockSpec((1, tn), lambda i, j: (0, j))
    osp = pl.BlockSpec((tm, tn), lambda i, j: (i, j))
    return pl.pallas_call(
        body, grid=(T // tm, N // tn), in_specs=[pl.BlockSpec((tm, K), lambda i, j: (i, 0)), wsp, wsp, bsp, bsp],
        out_specs=[osp, osp, osp], out_shape=[SDS((T, N), f32), SDS((T, N), f32), SDS((T, N), bf16)],
        compiler_params=_cparams(2), name=name)(h, wa, wb, ba, bb)


def _mm_res(a, w, b, x_in, gate, name):
    T, K = a.shape
    N = w.shape[1]
    tm, tn = min(TM, T), _col_tile(N)

    def body(a_ref, w_ref, b_ref, x_ref, gt_ref, xo_ref, y_ref):
        y = jnp.dot(a_ref[...], w_ref[...], preferred_element_type=f32) + b_ref[...]
        y_ref[...] = y.astype(bf16)
        xo_ref[...] = x_ref[...] + gt_ref[...] * y

    vsp = pl.BlockSpec((1, tn), lambda i, j: (0, j))
    osp = pl.BlockSpec((tm, tn), lambda i, j: (i, j))
    return pl.pallas_call(
        body, grid=(T // tm, N // tn),
        in_specs=[pl.BlockSpec((tm, K), lambda i, j: (i, 0)), pl.BlockSpec((K, tn), lambda i, j: (0, j)), vsp, osp, vsp],
        out_specs=[osp, osp], out_shape=[SDS((T, N), f32), SDS((T, N), bf16)],
        compiler_params=_cparams(2), name=name)(a, w, b, x_in, gate)


def _mm(a, w, out_dtype, out_scale, name):
    T, K = a.shape
    N = w.shape[1]
    tm, tn = min(TM, T), _col_tile(N)

    def body(a_ref, w_ref, o_ref):
        y = jnp.dot(a_ref[...], w_ref[...], preferred_element_type=f32)
        if out_scale != 1.0:
            y = y * out_scale
        o_ref[...] = y.astype(out_dtype)

    return pl.pallas_call(
        body, grid=(T // tm, N // tn),
        in_specs=[pl.BlockSpec((tm, K), lambda i, j: (i, 0)), pl.BlockSpec((K, tn), lambda i, j: (0, j))],
        out_specs=pl.BlockSpec((tm, tn), lambda i, j: (i, j)), out_shape=SDS((T, N), out_dtype),
        compiler_params=_cparams(2), name=name)(a, w)


def _dot_nt(a, b):
    return lax.dot_general(a, b, (((1,), (1,)), ((), ())), preferred_element_type=f32)


def _dot_tn(a, b):
    return lax.dot_general(a, b, (((0,), (0,)), ((), ())), preferred_element_type=f32)


def _mm_nt(pairs, out_dtype, name):
    T = pairs[0][0].shape[0]
    K = pairs[0][1].shape[0]
    tm, tk = min(TM, T), _col_tile(K)
    n = len(pairs)

    def body(*refs):
        o_ref = refs[2 * n]
        acc = None
        for i in range(n):
            part = _dot_nt(refs[2 * i][...].astype(bf16), refs[2 * i + 1][...])
            acc = part if acc is None else acc + part
        o_ref[...] = acc.astype(out_dtype)

    in_specs, args = [], []
    for dy, w in pairs:
        ni = dy.shape[1]
        in_specs += [pl.BlockSpec((tm, ni), lambda i, j: (i, 0)), pl.BlockSpec((tk, ni), lambda i, j: (j, 0))]
        args += [dy, w]
    return pl.pallas_call(
        body, grid=(T // tm, K // tk), in_specs=in_specs,
        out_specs=pl.BlockSpec((tm, tk), lambda i, j: (i, j)), out_shape=SDS((T, K), out_dtype),
        compiler_params=_cparams(2), name=name)(*args)


def _mm_nt_swiglu(dy, w, ug, uu, name):
    T, N = dy.shape
    K = w.shape[0]
    tm, tk = min(TM, T), _col_tile(K)

    def body(dy_ref, w_ref, ug_ref, uu_ref, dug_ref, duu_ref):
        dact = _dot_nt(dy_ref[...], w_ref[...])
        g = ug_ref[...].astype(f32)
        u = uu_ref[...].astype(f32)
        sg = jax.nn.sigmoid(g)
        duu_ref[...] = (dact * (g * sg)).astype(bf16)
        dug_ref[...] = (dact * u * (sg * (1.0 + g * (1.0 - sg)))).astype(bf16)

    osp = pl.BlockSpec((tm, tk), lambda i, j: (i, j))
    return pl.pallas_call(
        body, grid=(T // tm, K // tk),
        in_specs=[pl.BlockSpec((tm, N), lambda i, j: (i, 0)), pl.BlockSpec((tk, N), lambda i, j: (j, 0)), osp, osp],
        out_specs=[osp, osp], out_shape=[SDS((T, K), bf16), SDS((T, K), bf16)],
        compiler_params=_cparams(2), name=name)(dy, w, ug, uu)


def _mm_tn(a, b, name):
    T, K = a.shape
    N = b.shape[1]
    tt = min(TM, T)
    tk = K if K <= 1024 else _col_tile(K)
    tn = N if N <= 1024 else _col_tile(N)

    def body(a_ref, b_ref, o_ref):
        @pl.when(pl.program_id(2) == 0)
        def _():
            o_ref[...] = jnp.zeros_like(o_ref)
        o_ref[...] += _dot_tn(a_ref[...].astype(bf16), b_ref[...].astype(bf16))

    return pl.pallas_call(
        body, grid=(K // tk, N // tn, T // tt),
        in_specs=[pl.BlockSpec((tt, tk), lambda i, j, t: (t, i)), pl.BlockSpec((tt, tn), lambda i, j, t: (t, j))],
        out_specs=pl.BlockSpec((tk, tn), lambda i, j, t: (i, j)), out_shape=SDS((K, N), f32),
        compiler_params=_cparams(3), name=name)(a, b)


def _mm_tn_shards(a, b, c, name, into=None):
    T, K = a.shape
    half = NDEV // 2
    assert b.shape[1] == half * c
    tt = min(TM, T)
    nt = T // tt
    first = into is None

    def body(a_ref, b_ref, *rest):
        o_ref, acc = rest[-2], rest[-1]
        t = pl.program_id(1)

        @pl.when(t == 0)
        def _():
            acc[...] = jnp.zeros_like(acc)
        acc[...] += _dot_tn(a_ref[...].astype(bf16), b_ref[...].astype(bf16))

        @pl.when(t == nt - 1)
        def _():
            o_ref[...] = acc[...].astype(bf16)

    in_specs = [pl.BlockSpec((tt, K), lambda j, t: (t, 0)), pl.BlockSpec((tt, c), lambda j, t: (t, j))]
    args = [a, b]
    if not first:
        in_specs.append(pl.BlockSpec(memory_space=pl.ANY))
        args.append(into)
    base = 0 if first else half
    return pl.pallas_call(
        body, grid=(half, nt), in_specs=in_specs,
        out_specs=pl.BlockSpec((None, K, c), lambda j, t: (j + base, 0, 0)), out_shape=SDS((NDEV, K, c), bf16),
        scratch_shapes=[pltpu.VMEM((K, c), f32)], input_output_aliases={} if first else {2: 0},
        compiler_params=_cparams(2), name=name)(*args)


def _ffn_in(h, w_sh, name):
    T = h.shape[0]
    tm = min(TM, T)

    def body(h_ref, wg_ref, wu_ref, ug_ref, uu_ref, act_ref):
        hv = h_ref[...]
        ug = jnp.dot(hv, wg_ref[...], preferred_element_type=f32)
        uu = jnp.dot(hv, wu_ref[...], preferred_element_type=f32)
        ug_ref[...] = ug
        uu_ref[...] = uu
        act_ref[...] = ((ug * jax.nn.sigmoid(ug)) * uu).astype(bf16)

    osp = pl.BlockSpec((tm, FSP), lambda i, j: (i, j))
    return pl.pallas_call(
        body, grid=(T // tm, NDEV // 2),
        in_specs=[pl.BlockSpec((tm, D), lambda i, j: (i, 0)),
                  pl.BlockSpec((None, D, FSP), lambda i, j: (j, 0, 0)),
                  pl.BlockSpec((None, D, FSP), lambda i, j: (j + NDEV // 2, 0, 0))],
        out_specs=[osp, osp, osp], out_shape=[SDS((T, FP), f32), SDS((T, FP), f32), SDS((T, FP), bf16)],
        compiler_params=_cparams(2), name=name)(h, w_sh, w_sh)


def _ffn_dh(dug, duu, w_sh, name):
    T = dug.shape[0]
    tm, tk = min(TM, T), 512
    half = NDEV // 2

    def body(dg_ref, du_ref, wg_ref, wu_ref, o_ref):
        acc = None
        for s in range(half):
            cols = slice(s * FSP, (s + 1) * FSP)
            part = _dot_nt(dg_ref[:, cols], wg_ref[s]) + _dot_nt(du_ref[:, cols], wu_ref[s])
            acc = part if acc is None else acc + part
        o_ref[...] = acc

    dsp = pl.BlockSpec((tm, FP), lambda i, j: (i, 0))
    return pl.pallas_call(
        body, grid=(T // tm, D // tk),
        in_specs=[dsp, dsp, pl.BlockSpec((half, tk, FSP), lambda i, j: (0, j, 0)),
                  pl.BlockSpec((half, tk, FSP), lambda i, j: (1, j, 0))],
        out_specs=pl.BlockSpec((tm, tk), lambda i, j: (i, j)), out_shape=SDS((T, D), f32),
        compiler_params=_cparams(2), name=name)(dug, duu, w_sh, w_sh)


def _layernorm_parts(qv, g, b):
    mu = jnp.mean(qv, axis=-1, keepdims=True)
    cen = qv - mu
    rstd = lax.rsqrt(jnp.mean(cen * cen, axis=-1, keepdims=True) + EPS)
    z = cen * rstd
    return z, rstd, z * g + b


def _conv_fwd(p, w_dw, b_dw, ln_g, ln_b, name, ride=()):
    T = p.shape[0]
    tc = min(TC, T)
    n = T // tc
    nx = len(ride)

    def body(p_ref, w_ref, b_ref, g_ref, bb_ref, *rest):
        x_refs, (q_ref, s_ref), xo_refs = rest[:nx], rest[nx:nx + 2], rest[nx + 2:2 * nx + 2]
        ext, sems = rest[2 * nx + 2], rest[2 * nx + 3:]
        i = pl.program_id(0)
        if nx:
            _exchange_start(x_refs, xo_refs, sems, True, i == 0)

        @pl.when(i == 0)
        def _():
            ext[0:HALO, :] = jnp.zeros((HALO, D), f32)

        @pl.when(i > 0)
        def _():
            ext[0:HALO, :] = ext[tc:tc + HALO, :]

        ext[HALO:HALO + tc, :] = p_ref[...].astype(f32)
        for cb in range(D // LANES):
            cols = slice(cb * LANES, (cb + 1) * LANES)
            acc = jnp.zeros((tc, LANES), f32)
            for k in range(KW):
                acc = acc + w_ref[k:k + 1, cols] * ext[pl.ds(HALO - (KW - 1) + k, tc), cols]
            q_ref[:, cols] = acc + b_ref[:, cols]
        _, _, l = _layernorm_parts(q_ref[...], g_ref[...], bb_ref[...])
        s_ref[...] = (l * jax.nn.sigmoid(l)).astype(bf16)
        if nx:
            _exchange_wait(x_refs, xo_refs, sems, True, i == n - 1)

    vec = pl.BlockSpec((1, D), _fix)
    til = pl.BlockSpec((tc, D), _row)
    x_in, x_out, x_shape, x_sems = _exchange_shapes(ride, True) if nx else ([], [], [], [])
    return pl.pallas_call(
        body, grid=(n,), in_specs=[til, pl.BlockSpec((HALO, D), _fix), vec, vec, vec] + x_in,
        out_specs=[til, til] + x_out, out_shape=[SDS((T, D), f32), SDS((T, D), bf16)] + x_shape,
        scratch_shapes=[pltpu.VMEM((tc + HALO, D), f32)] + x_sems, compiler_params=_cparams(1),
        name=name)(p, w_dw, b_dw, ln_g, ln_b, *ride)


def _conv_bwd(ds, q, p, a, gl, w_dw, ln_g, ln_b, name, ride=()):
    T = q.shape[0]
    tc = min(TC, T)
    n = T // tc
    nx = len(ride)

    def body(ds_ref, q_ref, p_ref, a_ref, gl_ref, w_ref, g_ref, bb_ref, *rest):
        x_refs, (da_ref, dgl_ref, acc_ref, dw_ref), xo_refs = rest[:nx], rest[nx:nx + 4], rest[nx + 4:2 * nx + 4]
        ext, sems = rest[2 * nx + 4], rest[2 * nx + 5:]
        i = pl.program_id(0)
        if nx:
            _exchange_start(x_refs, xo_refs, sems, False, i == 0)

        @pl.when(i == 0)
        def _():
            acc_ref[...] = jnp.zeros_like(acc_ref)
            dw_ref[...] = jnp.zeros_like(dw_ref)
            ext[tc:tc + HALO, :] = jnp.zeros((HALO, D), f32)

        @pl.when(i > 0)
        def _():
            ext[tc:tc + HALO, :] = ext[0:HALO, :]

        gv = g_ref[...]
        z, rstd, l = _layernorm_parts(q_ref[...], gv, bb_ref[...])
        sg = jax.nn.sigmoid(l)
        dl = ds_ref[...] * (sg * (1.0 + l * (1.0 - sg)))
        dz = dl * gv
        dq = rstd * (dz - jnp.mean(dz, axis=-1, keepdims=True) - z * jnp.mean(dz * z, axis=-1, keepdims=True))
        ext[0:tc, :] = dq.astype(bf16).astype(f32)
        acc_ref[0:1, :] += jnp.sum(dl * z, axis=0, keepdims=True)
        acc_ref[1:2, :] += jnp.sum(dl, axis=0, keepdims=True)
        acc_ref[2:3, :] += jnp.sum(dq, axis=0, keepdims=True)
        for cb in range(D // LANES):
            cols = slice(cb * LANES, (cb + 1) * LANES)
            pc = p_ref[:, cols].astype(f32)
            dp = jnp.zeros((tc, LANES), f32)
            for k in range(KW):
                sl = ext[pl.ds(KW - 1 - k, tc), cols]
                dp = dp + w_ref[k:k + 1, cols] * sl
                dw_ref[k:k + 1, cols] += jnp.sum(sl * pc, axis=0, keepdims=True)
            av = a_ref[:, cols].astype(f32)
            sgl = jax.nn.sigmoid(gl_ref[:, cols].astype(f32))
            da = dp * sgl
            dgl = dp * av * (sgl * (1.0 - sgl))
            da_ref[:, cols] = da.astype(bf16)
            dgl_ref[:, cols] = dgl.astype(bf16)
            acc_ref[3:4, cols] += jnp.sum(da, axis=0, keepdims=True)
            acc_ref[4:5, cols] += jnp.sum(dgl, axis=0, keepdims=True)
        if nx:
            _exchange_wait(x_refs, xo_refs, sems, False, i == n - 1)

    rev = lambda i: (n - 1 - i, 0)
    til = pl.BlockSpec((tc, D), rev)
    vec = pl.BlockSpec((1, D), _fix)
    x_in, x_out, x_shape, x_sems = _exchange_shapes(ride, False) if nx else ([], [], [], [])
    return pl.pallas_call(
        body, grid=(n,), in_specs=[til, til, til, til, til, pl.BlockSpec((HALO, D), _fix), vec, vec] + x_in,
        out_specs=[til, til, pl.BlockSpec((8, D), _fix), pl.BlockSpec((HALO, D), _fix)] + x_out,
        out_shape=[SDS((T, D), bf16), SDS((T, D), bf16), SDS((8, D), f32), SDS((HALO, D), f32)] + x_shape,
        scratch_shapes=[pltpu.VMEM((tc + HALO, D), f32)] + x_sems, compiler_params=_cparams(1),
        name=name)(ds, q, p, a, gl, w_dw, ln_g, ln_b, *ride)


def _tri(n, upper):
    r = lax.broadcasted_iota(jnp.int32, (n, n), 0)
    c = lax.broadcasted_iota(jnp.int32, (n, n), 1)
    return ((c >= r) if upper else (r >= c)).astype(f32)


def _forget_fwd(fl, fb, name):
    T = fl.shape[0]
    tc = min(TC, T)

    def body(fl_ref, fb_ref, cum_ref, carry):
        @pl.when(pl.program_id(0) == 0)
        def _():
            carry[...] = jnp.zeros_like(carry)
        xv = fl_ref[...] + fb_ref[...]
        lf = jnp.minimum(xv, 0.0) - jnp.log(1.0 + jnp.exp(-jnp.abs(xv)))
        cs = jnp.dot(_tri(tc, False), lf, preferred_element_type=f32, precision=lax.Precision.HIGHEST) + carry[0:1, :]
        cum_ref[...] = cs
        carry[0:1, :] = cs[tc - 1:tc, :]

    til = pl.BlockSpec((tc, LANES), _row)
    return pl.pallas_call(
        body, grid=(T // tc,), in_specs=[til, pl.BlockSpec((1, LANES), _fix)], out_specs=til,
        out_shape=SDS((T, LANES), f32), scratch_shapes=[pltpu.VMEM((8, LANES), f32)],
        compiler_params=_cparams(1), name=name)(fl, fb)


def _forget_bwd(dcum, fl, fb, name):
    T = fl.shape[0]
    tc = min(TC, T)
    n = T // tc

    def body(dc_ref, fl_ref, fb_ref, dfl_ref, acc_ref, carry):
        @pl.when(pl.program_id(0) == 0)
        def _():
            carry[...] = jnp.zeros_like(carry)
            acc_ref[...] = jnp.zeros_like(acc_ref)
        dlf = jnp.dot(_tri(tc, True), dc_ref[...], preferred_element_type=f32, precision=lax.Precision.HIGHEST) + carry[0:1, :]
        carry[0:1, :] = dlf[0:1, :]
        dfl = dlf * (1.0 - jax.nn.sigmoid(fl_ref[...] + fb_ref[...]))
        dfl_ref[...] = dfl.astype(bf16)
        acc_ref[0:1, :] += jnp.sum(dfl, axis=0, keepdims=True)

    rev = lambda i: (n - 1 - i, 0)
    til = pl.BlockSpec((tc, LANES), rev)
    return pl.pallas_call(
        body, grid=(n,), in_specs=[til, til, pl.BlockSpec((1, LANES), _fix)],
        out_specs=[til, pl.BlockSpec((8, LANES), _fix)], out_shape=[SDS((T, LANES), bf16), SDS((8, LANES), f32)],
        scratch_shapes=[pltpu.VMEM((8, LANES), f32)], compiler_params=_cparams(1), name=name)(dcum, fl, fb)


def _causal(s, n):
    r = lax.broadcasted_iota(jnp.int32, (n, n), 0)
    c = lax.broadcasted_iota(jnp.int32, (n, n), 1)
    return jnp.where(c <= r, s, NEG)


def _attn_fwd(q, k, v, cum_t, name, ride=()):
    T = q.shape[0]
    tq = min(TQ, T)
    n = T // tq
    nx = len(ride)

    def body(q_ref, k_ref, v_ref, ck_ref, *rest):
        x_refs, (o_ref, o32_ref, st_ref), xo_refs = rest[:nx], rest[nx:nx + 3], rest[nx + 3:2 * nx + 3]
        (m_sc, l_sc, acc_sc, res_sc), sems = rest[2 * nx + 3:2 * nx + 7], rest[2 * nx + 7:]
        i = pl.program_id(1)
        if nx:
            _exchange_start(x_refs, xo_refs, sems, True, (pl.program_id(0) == 0) & (i == 0))
        lane = lax.broadcasted_iota(jnp.int32, (1, LANES), 1)
        lo = lane < HD
        q2 = q_ref[...]
        zero = jnp.zeros_like(q2)
        qa = (jnp.where(lo, q2, zero), jnp.where(lo, zero, q2))
        m_sc[...] = jnp.full(m_sc.shape, NEG, f32)
        l_sc[...] = jnp.zeros_like(l_sc)
        acc_sc[...] = jnp.zeros_like(acc_sc)
        res_sc[...] = jnp.zeros_like(res_sc)

        def block(j, masked):
            off = pl.multiple_of(j * tq, tq)
            k2 = k_ref[pl.ds(off, tq), :]
            v2 = v_ref[pl.ds(off, tq), :]
            for a in range(2):
                s = _dot_nt(qa[a], k2) - ck_ref[0, a:a + 1, pl.ds(off, tq)]
                if masked:
                    s = _causal(s, tq)
                m_old = m_sc[a]
                m_new = jnp.maximum(m_old, jnp.max(s, axis=1, keepdims=True))
                alpha = jnp.exp(m_old - m_new)
                pm = jnp.exp(s - m_new)
                pb = pm.astype(bf16)
                pr = (pm - pb.astype(f32)).astype(bf16)
                l_sc[a] = alpha * l_sc[a] + jnp.sum(pm, axis=1, keepdims=True)
                acc_sc[a] = alpha * acc_sc[a] + jnp.dot(pb, v2, preferred_element_type=f32)
                res_sc[a] = alpha * res_sc[a] + jnp.dot(pr, v2, preferred_element_type=f32)
                m_sc[a] = m_new

        def step(j, carry):
            block(j, False)
            return carry

        lax.fori_loop(0, i, step, 0)
        block(i, True)
        o_ref[...] = jnp.where(lo, acc_sc[0] / l_sc[0], acc_sc[1] / l_sc[1]).astype(bf16)
        o32_ref[...] = jnp.where(lo, (acc_sc[0] + res_sc[0]) / l_sc[0], (acc_sc[1] + res_sc[1]) / l_sc[1])
        lse0 = m_sc[0] + jnp.log(l_sc[0])
        lse1 = m_sc[1] + jnp.log(l_sc[1])
        st_ref[0] = jnp.where(lane == 0, lse0, jnp.where(lane == 1, lse1, 0.0))
        if nx:
            _exchange_wait(x_refs, xo_refs, sems, True, (pl.program_id(0) == NP - 1) & (i == n - 1))

    full = lambda blk: pl.BlockSpec((T, LANES), blk)
    x_in, x_out, x_shape, x_sems = _exchange_shapes(ride, True) if nx else ([], [], [], [])
    return pl.pallas_call(
        body, grid=(NP, n),
        in_specs=[pl.BlockSpec((tq, LANES), lambda p, i: (i, p)), full(lambda p, i: (0, p)), full(lambda p, i: (0, p)),
                  pl.BlockSpec((1, 2, T), lambda p, i: (p, 0, 0))] + x_in,
        out_specs=[pl.BlockSpec((tq, LANES), lambda p, i: (i, p)), pl.BlockSpec((tq, LANES), lambda p, i: (i, p)),
                   pl.BlockSpec((1, tq, LANES), lambda p, i: (p, i, 0))] + x_out,
        out_shape=[SDS((T, H * HD), bf16), SDS((T, H * HD), f32), SDS((NP, T, LANES), f32)] + x_shape,
        scratch_shapes=[pltpu.VMEM((2, tq, 1), f32), pltpu.VMEM((2, tq, 1), f32), pltpu.VMEM((2, tq, LANES), f32),
                        pltpu.VMEM((2, tq, LANES), f32)] + x_sems,
        compiler_params=_cparams(2), name=name)(q, k, v, cum_t, *ride)


def _attn_stats(do, o, lse, name):
    T = do.shape[0]
    tm = min(TM, T)

    def body(do_ref, o_ref, lse_ref, st_ref):
        lane = lax.broadcasted_iota(jnp.int32, (1, LANES), 1)
        prod = do_ref[...].astype(f32) * o_ref[...].astype(f32)
        d0 = jnp.sum(jnp.where(lane < HD, prod, 0.0), axis=1, keepdims=True)
        d1 = jnp.sum(jnp.where(lane < HD, 0.0, prod), axis=1, keepdims=True)
        st_ref[0] = jnp.where(lane < 2, lse_ref[0], jnp.where(lane == 2, d0, jnp.where(lane == 3, d1, 0.0)))

    til = pl.BlockSpec((tm, LANES), lambda p, i: (i, p))
    stt = pl.BlockSpec((1, tm, LANES), lambda p, i: (p, i, 0))
    return pl.pallas_call(
        body, grid=(NP, T // tm), in_specs=[til, til, stt], out_specs=stt, out_shape=SDS((NP, T, LANES), f32),
        compiler_params=_cparams(2), name=name)(do, o, lse)


def _attn_bwd(q, k, v, do, st, cum_t, name, ride=()):
    T = q.shape[0]
    tq = min(TQ, T)
    n = T // tq
    nx = len(ride)

    def body(q_ref, k_ref, v_ref, do_ref, st_ref, ck_ref, *rest):
        x_refs, (dq_ref, dk_ref, dv_ref, dck_ref), xo_refs = rest[:nx], rest[nx:nx + 4], rest[nx + 4:2 * nx + 4]
        (dk_sc, dv_sc, dck_sc), sems = rest[2 * nx + 4:2 * nx + 7], rest[2 * nx + 7:]
        j = pl.program_id(1)
        if nx:
            _exchange_start(x_refs, xo_refs, sems, False, (pl.program_id(0) == 0) & (j == 0))
        lane = lax.broadcasted_iota(jnp.int32, (1, LANES), 1)
        lo = lane < HD

        @pl.when(j == 0)
        def _():
            dq_ref[...] = jnp.zeros_like(dq_ref)

        k2 = k_ref[...]
        v2 = v_ref[...]
        zero = jnp.zeros_like(k2)
        ka = (jnp.where(lo, k2, zero), jnp.where(lo, zero, k2))
        va = (jnp.where(lo, v2, zero), jnp.where(lo, zero, v2))
        dk_sc[...] = jnp.zeros_like(dk_sc)
        dv_sc[...] = jnp.zeros_like(dv_sc)
        dck_sc[...] = jnp.zeros_like(dck_sc)

        def block(i, masked):
            off = pl.multiple_of(i * tq, tq)
            q2 = q_ref[pl.ds(off, tq), :]
            do2 = do_ref[pl.ds(off, tq), :]
            stt = st_ref[0, pl.ds(off, tq), :]
            parts = []
            for a in range(2):
                s = _dot_nt(q2, ka[a]) - ck_ref[0, a:a + 1, :]
                if masked:
                    s = _causal(s, tq)
                pm = jnp.exp(s - stt[:, a:a + 1])
                dp = _dot_nt(do2, va[a])
                dsm = pm * (dp - stt[:, 2 + a:3 + a])
                dsb = dsm.astype(bf16)
                dv_sc[a] += _dot_tn(pm.astype(bf16), do2)
                dk_sc[a] += _dot_tn(dsb, q2)
                dck_sc[a:a + 1, :] -= jnp.sum(dsm, axis=0, keepdims=True)
                parts.append(jnp.dot(dsb, k2, preferred_element_type=f32))
            dq_ref[pl.ds(off, tq), :] += jnp.where(lo, parts[0], parts[1])

        block(j, True)

        def step(i, carry):
            block(i, False)
            return carry

        lax.fori_loop(j + 1, n, step, 0)
        dk_ref[...] = jnp.where(lo, dk_sc[0], dk_sc[1]).astype(bf16)
        dv_ref[...] = jnp.where(lo, dv_sc[0], dv_sc[1]).astype(bf16)
        dck_ref[0] = dck_sc[0:2, :]

        @pl.when(j == n - 1)
        def _():
            dq_ref[...] = dq_ref[...] * (HD ** -0.5)

        if nx:
            _exchange_wait(x_refs, xo_refs, sems, False, (pl.program_id(0) == NP - 1) & (j == n - 1))

    full = lambda: pl.BlockSpec((T, LANES), lambda p, j: (0, p))
    kvb = lambda: pl.BlockSpec((tq, LANES), lambda p, j: (j, p))
    ckb = lambda: pl.BlockSpec((1, 2, tq), lambda p, j: (p, 0, j))
    x_in, x_out, x_shape, x_sems = _exchange_shapes(ride, False) if nx else ([], [], [], [])
    return pl.pallas_call(
        body, grid=(NP, n),
        in_specs=[full(), kvb(), kvb(), full(), pl.BlockSpec((1, T, LANES), lambda p, j: (p, 0, 0)), ckb()] + x_in,
        out_specs=[full(), kvb(), kvb(), ckb()] + x_out,
        out_shape=[SDS((T, H * HD), f32), SDS((T, H * HD), bf16), SDS((T, H * HD), bf16), SDS((NP, 2, T), f32)] + x_shape,
        scratch_shapes=[pltpu.VMEM((2, tq, LANES), f32), pltpu.VMEM((2, tq, LANES), f32), pltpu.VMEM((8, tq), f32)] + x_sems,
        compiler_params=_cparams(2), name=name)(q, k, v, do, st, cum_t, *ride)


def _ada_fwd(c_all, w_cat, name):
    n = w_cat.shape[1]
    tn = 256

    def body(c_ref, w_ref, o_ref):
        cv = c_ref[...]
        o_ref[...] = jnp.dot((cv * jax.nn.sigmoid(cv)).astype(bf16), w_ref[...].astype(bf16), preferred_element_type=f32)

    return pl.pallas_call(
        body, grid=(n // tn,), in_specs=[pl.BlockSpec((NDEV, D), _fix), pl.BlockSpec((D, tn), lambda i: (0, i))],
        out_specs=pl.BlockSpec((NDEV, tn), lambda i: (0, i)), out_shape=SDS((NDEV, n), f32),
        compiler_params=_cparams(1), name=name)(c_all, w_cat)


def _ada_bwd(c_all_t, dsel, name):
    n = dsel.shape[1]
    tn = 256

    def body(c_ref, d_ref, o_ref):
        cv = c_ref[...]
        ca = cv * jax.nn.sigmoid(cv)
        acc = ca[:, 0:1] * d_ref[0:1, :]
        for b in range(1, NDEV):
            acc = acc + ca[:, b:b + 1] * d_ref[b:b + 1, :]
        o_ref[...] = acc

    return pl.pallas_call(
        body, grid=(n // tn,), in_specs=[pl.BlockSpec((D, NDEV), _fix), pl.BlockSpec((NDEV, tn), lambda i: (0, i))],
        out_specs=pl.BlockSpec((D, tn), lambda i: (0, i)), out_shape=SDS((D, n), f32),
        compiler_params=_cparams(1), name=name)(c_all_t, dsel)


def _sum_parts(parts, name):
    R = parts.shape[1]

    def body(p_ref, o_ref):
        acc = p_ref[0]
        for j in range(1, NDEV):
            acc = acc + p_ref[j]
        o_ref[...] = acc

    return pl.pallas_call(body, out_shape=SDS((R, LANES), f32), name=name)(parts)


def _adamw(g_parts, w, m, v, name):
    n_parts, R, C = g_parts.shape
    tr = next(t for t in (256, 128, 64, 32, 16, 8) if R % t == 0)
    c1 = 1.0 / (1.0 - ADAM_B1 ** ADAM_STEP)
    c2 = 1.0 / (1.0 - ADAM_B2 ** ADAM_STEP)

    def body(g_ref, w_ref, m_ref, v_ref, go_ref, d_ref, mo_ref, vo_ref):
        g = g_ref[0].astype(f32)
        for j in range(1, n_parts):
            g = g + g_ref[j].astype(f32)
        mn = ADAM_B1 * m_ref[...] + (1.0 - ADAM_B1) * g
        vn = ADAM_B2 * v_ref[...] + (1.0 - ADAM_B2) * (g * g)
        go_ref[...] = g
        mo_ref[...] = mn
        vo_ref[...] = vn
        d_ref[...] = -ADAM_LR * ((mn * c1) / (jnp.sqrt(vn * c2) + ADAM_EPS) + ADAM_WD * w_ref[...])

    til = pl.BlockSpec((tr, C), _row)
    out = SDS((R, C), f32)
    return pl.pallas_call(
        body, grid=(R // tr,), in_specs=[pl.BlockSpec((n_parts, tr, C), lambda i: (0, i, 0)), til, til, til],
        out_specs=[til, til, til, til], out_shape=[out, out, out, out],
        compiler_params=_cparams(1), name=name)(g_parts, w, m, v)


def _pad_rows(flat, cols, mult):
    n = flat.shape[-1]
    rows = -(-n // cols)
    rows = -(-rows // mult) * mult
    pad = [(0, 0)] * (flat.ndim - 1) + [(0, rows * cols - n)]
    return jnp.pad(flat, pad).reshape(flat.shape[:-1] + (rows, cols))


def _split_flat(flat, shapes):
    out, off = {}, 0
    for name, shp in shapes:
        n = 1
        for d in shp:
            n *= d
        out[name] = flat[off:off + n].reshape(shp)
        off += n
    return out


def kernel(x, c, mix_norm_g, mix_ada_w, mix_ada_b, ffn_norm_g, ffn_ada_w, ffn_ada_b, ffn_w_in, ffn_w_out, conv_w_in, conv_b_in, conv_w_dw, conv_b_dw, conv_ln_g, conv_ln_b, conv_w_out, conv_b_out, kv_norm_g, kv_ada_w, kv_ada_b, kv_w, forget_b, attn_w_q, attn_w_o, final_norm_g, loss_target, m_mix_norm_g, m_mix_ada_w, m_mix_ada_b, m_ffn_norm_g, m_ffn_ada_w, m_ffn_ada_b, m_ffn_w_in, m_ffn_w_out, m_conv_w_in, m_conv_b_in, m_conv_w_dw, m_conv_b_dw, m_conv_ln_g, m_conv_ln_b, m_conv_w_out, m_conv_b_out, m_kv_norm_g, m_kv_ada_w, m_kv_ada_b, m_kv_w, m_forget_b, m_attn_w_q, m_attn_w_o, m_final_norm_g, v_mix_norm_g, v_mix_ada_w, v_mix_ada_b, v_ffn_norm_g, v_ffn_ada_w, v_ffn_ada_b, v_ffn_w_in, v_ffn_w_out, v_conv_w_in, v_conv_b_in, v_conv_w_dw, v_conv_b_dw, v_conv_ln_g, v_conv_ln_b, v_conv_w_out, v_conv_b_out, v_kv_norm_g, v_kv_ada_w, v_kv_ada_b, v_kv_w, v_forget_b, v_attn_w_q, v_attn_w_o, v_final_norm_g):
    W = dict(mix_norm_g=mix_norm_g, mix_ada_w=mix_ada_w, mix_ada_b=mix_ada_b, ffn_norm_g=ffn_norm_g, ffn_ada_w=ffn_ada_w, ffn_ada_b=ffn_ada_b, ffn_w_in=ffn_w_in, ffn_w_out=ffn_w_out, conv_w_in=conv_w_in, conv_b_in=conv_b_in, conv_w_dw=conv_w_dw, conv_b_dw=conv_b_dw, conv_ln_g=conv_ln_g, conv_ln_b=conv_ln_b, conv_w_out=conv_w_out, conv_b_out=conv_b_out, kv_norm_g=kv_norm_g, kv_ada_w=kv_ada_w, kv_ada_b=kv_ada_b, kv_w=kv_w, forget_b=forget_b, attn_w_q=attn_w_q, attn_w_o=attn_w_o, final_norm_g=final_norm_g)
    M = dict(mix_norm_g=m_mix_norm_g, mix_ada_w=m_mix_ada_w, mix_ada_b=m_mix_ada_b, ffn_norm_g=m_ffn_norm_g, ffn_ada_w=m_ffn_ada_w, ffn_ada_b=m_ffn_ada_b, ffn_w_in=m_ffn_w_in, ffn_w_out=m_ffn_w_out, conv_w_in=m_conv_w_in, conv_b_in=m_conv_b_in, conv_w_dw=m_conv_w_dw, conv_b_dw=m_conv_b_dw, conv_ln_g=m_conv_ln_g, conv_ln_b=m_conv_ln_b, conv_w_out=m_conv_w_out, conv_b_out=m_conv_b_out, kv_norm_g=m_kv_norm_g, kv_ada_w=m_kv_ada_w, kv_ada_b=m_kv_ada_b, kv_w=m_kv_w, forget_b=m_forget_b, attn_w_q=m_attn_w_q, attn_w_o=m_attn_w_o, final_norm_g=m_final_norm_g)
    V = dict(mix_norm_g=v_mix_norm_g, mix_ada_w=v_mix_ada_w, mix_ada_b=v_mix_ada_b, ffn_norm_g=v_ffn_norm_g, ffn_ada_w=v_ffn_ada_w, ffn_ada_b=v_ffn_ada_b, ffn_w_in=v_ffn_w_in, ffn_w_out=v_ffn_w_out, conv_w_in=v_conv_w_in, conv_b_in=v_conv_b_in, conv_w_dw=v_conv_w_dw, conv_b_dw=v_conv_b_dw, conv_ln_g=v_conv_ln_g, conv_ln_b=v_conv_ln_b, conv_w_out=v_conv_w_out, conv_b_out=v_conv_b_out, kv_norm_g=v_kv_norm_g, kv_ada_w=v_kv_ada_w, kv_ada_b=v_kv_ada_b, kv_w=v_kv_w, forget_b=v_forget_b, attn_w_q=v_attn_w_q, attn_w_o=v_attn_w_o, final_norm_g=v_final_norm_g)
    names = list(W)
    T = x.shape[1]
    me = _my_index()
    x0 = x[0]
    tgt = loss_target[0]
    row = lambda vct: vct.reshape(1, -1)

    small_names = ("conv_b_in", "conv_w_dw", "conv_b_dw", "conv_ln_g", "conv_ln_b", "conv_b_out")
    small_loc = jnp.concatenate([c.reshape(-1)] + [W[n].reshape(-1) for n in small_names])
    sg = _exchange([_pad_rows(small_loc, LANES, 8)], True, "gather_small")[0].reshape(NDEV, -1)
    c_all = sg[:, :D]
    off = D
    b_in = sg[:, off:off + 2 * D // NDEV].reshape(1, 2 * D); off += 2 * D // NDEV
    cl = D // NDEV
    w_dw = sg[:, off:off + KW * cl].reshape(NDEV, KW, cl).transpose(1, 0, 2).reshape(KW, D); off += KW * cl
    w_dw = jnp.pad(w_dw, ((0, HALO - KW), (0, 0))).astype(bf16).astype(f32)
    b_dw = sg[:, off:off + cl].reshape(1, D); off += cl
    ln_g = sg[:, off:off + cl].reshape(1, D); off += cl
    ln_b = sg[:, off:off + cl].reshape(1, D); off += cl
    b_out = sg[:, off:off + cl].reshape(1, D)

    w_cat = jnp.concatenate([mix_ada_w[0], mix_ada_w[1], ffn_ada_w[0], ffn_ada_w[1], kv_ada_w], axis=1)
    ada_loc = _ada_fwd(c_all, w_cat, "ada_fwd")
    ada_all = _exchange([ada_loc], True, "gather_ada")[0]
    ada_me = lax.dynamic_index_in_dim(ada_all, me, axis=1, keepdims=False)
    ada_bias = (mix_ada_b[0], mix_ada_b[1], ffn_ada_b[0], ffn_ada_b[1], kv_ada_b)
    ada, off = [], 0
    for nl, bias in zip(ADA_LOC, ada_bias):
        full = ada_me[:, off:off + nl].reshape(-1) + bias
        ada.append([row(t) for t in jnp.split(full, full.shape[0] // D)])
        off += nl
    (sh_m0, sc_m0, gt_m0), (sh_m1, sc_m1, gt_m1), (sh_f0, sc_f0, gt_f0), (sh_f1, sc_f1, gt_f1), (sh_kv, sc_kv) = ada

    pad_in = lambda src, l: jnp.pad(src["ffn_w_in"][l], ((0, 0), (0, FSP - FS)))
    rows_a = lambda src: jnp.concatenate([src["ffn_w_out"][0], src["attn_w_q"][0]])
    rows_b = lambda src: jnp.concatenate([src["ffn_w_out"][1], src["attn_w_o"][0]])
    as_bf = lambda arrs: [t.astype(bf16) for t in arrs]
    fo, sq_rows = F // NDEV, D // NDEV
    g_ci, g_co = _exchange(as_bf([conv_w_in[0], conv_w_out[0]]), True, "gather_weights")
    soon = as_bf([pad_in(W, 0), rows_a(W), kv_w])
    late = as_bf([pad_in(W, 1), rows_b(W)])

    def w_out_of(g_r):
        t = g_r[:, :fo].reshape(NDEV // 2, FS, D)
        return jnp.pad(t, ((0, 0), (0, FSP - FS), (0, 0))).reshape(FP, D)

    conv_in_full = g_ci.transpose(1, 0, 2).reshape(D, 2 * D)
    wc_a, wc_g = conv_in_full[:, :D], conv_in_full[:, D:]
    wc_o = g_co.reshape(D, D)
    zeros_d = jnp.zeros((1, D), f32)
    fb = jnp.pad(forget_b, (0, LANES - H)).reshape(1, LANES)

    h0 = _normmod(x0, row(mix_norm_g[0]), sh_m0, sc_m0, "norm_mix0")
    a0, gl0, p0 = _mm_gated(h0, wc_a, wc_g, b_in[:, :D], b_in[:, D:], False, "conv_in")
    q0, s0, w_sh0, g_ra, g_kv = _conv_fwd(p0, w_dw, b_dw, ln_g, ln_b, "conv_dw", ride=soon)
    w_out0, w_q = w_out_of(g_ra), g_ra[:, fo:].reshape(D, D)
    kv_full = g_kv.transpose(1, 0, 2).reshape(D, -1)
    w_k, w_v = kv_full[:, :D], kv_full[:, D:2 * D]
    w_f = jnp.pad(kv_full[:, 2 * D:], ((0, 0), (0, LANES - H)))
    x1, y0 = _mm_res(s0, wc_o, b_out, x0, gt_m0, "conv_out")

    def ffn_fwd(xin, l, sh, sc, gt, w_sh, w_out):
        h = _normmod(xin, row(ffn_norm_g[l]), sh, sc, f"norm_ffn{l}")
        ug, uu, act = _ffn_in(h, w_sh, f"ffn_in{l}")
        xo, y = _mm_res(act, w_out, zeros_d, xin, gt, f"ffn_out{l}")
        return xo, (h, ug, uu, act, y, w_sh, w_out)

    x2, ffn0 = ffn_fwd(x1, 0, sh_f0, sc_f0, gt_f0, w_sh0, w_out0)

    hk = _normmod(x2, row(kv_norm_g), sh_kv, sc_kv, "norm_kv")
    k_sh = _mm(hk, w_k, bf16, 1.0, "proj_k")
    v_sh = _mm(hk, w_v, bf16, 1.0, "proj_v")
    fl = _mm(hk, w_f, f32, 1.0, "proj_f")
    cum = _forget_fwd(fl, fb, "forget_fwd")
    cum_t = cum[:, :H].T.reshape(NP, 2, T)

    h2 = _normmod(x2, row(mix_norm_g[1]), sh_m1, sc_m1, "norm_mix1")
    qh = _mm(h2, w_q, bf16, HD ** -0.5, "proj_q")
    o, o32, lse, w_sh1, g_rb = _attn_fwd(qh, k_sh, v_sh, cum_t, "attn_fwd", ride=late)
    w_out1, w_o = w_out_of(g_rb), g_rb[:, fo:].reshape(D, D)
    x3, y1 = _mm_res(o, w_o, zeros_d, x2, gt_m1, "attn_out")

    x4, ffn1 = ffn_fwd(x3, 1, sh_f1, sc_f1, gt_f1, w_sh1, w_out1)

    dx4, acc_fin = _final_bwd(x4, row(final_norm_g), tgt, "final_bwd")

    d_ada = {}
    by_rows = lambda g: g.reshape(NDEV, sq_rows, D)

    def ffn_bwd(dx_out, xin, l, sc, gt, saved):
        h, ug, uu, act, y, w_sh, w_out = saved
        dyb, acc_r = _res_in(dx_out, y, gt, f"ffn_res_bwd{l}")
        dug, duu = _mm_nt_swiglu(dyb, w_out, ug, uu, f"ffn_dact{l}")
        g_out = _mm_tn(act, dyb, f"ffn_dw_out{l}").reshape(NDEV // 2, FSP, D)[:, :FS].reshape(NDEV, fo, D)
        g_in = _mm_tn_shards(h, duu, FSP, f"ffn_dw_up{l}", into=_mm_tn_shards(h, dug, FSP, f"ffn_dw_gate{l}"))
        dh = _ffn_dh(dug, duu, w_sh, f"ffn_dh{l}")
        dxi, acc_n = _normmod_bwd(dh, xin, row(ffn_norm_g[l]), sc, dx_out, f"norm_ffn_bwd{l}")
        return dxi, g_in, g_out, [acc_n[0:1], acc_n[1:2], acc_r[0:1]], acc_n[2]

    dx3, g_in1, g_out1, d_ada[("ffn", 1)], dg_ffn1 = ffn_bwd(dx4, x3, 1, sc_f1, gt_f1, ffn1)

    dyb, acc_r = _res_in(dx3, y1, gt_m1, "attn_res_bwd")
    do = _mm_nt([(dyb, w_o)], bf16, "attn_do")
    g_wo = _mm_tn(o, dyb, "attn_dw_o")
    st = _attn_stats(do, o32, lse, "attn_stats")
    leave_b = as_bf([g_in1, jnp.concatenate([g_out1, by_rows(g_wo)], axis=1)])
    dq, dk, dv, dck, r_in1, r_rb = _attn_bwd(qh, k_sh, v_sh, do, st, cum_t, "attn_bwd", ride=leave_b)
    g_wq = _mm_tn(h2, dq, "attn_dw_q")
    dh2 = _mm_nt([(dq, w_q)], f32, "attn_dh")
    dx2, acc_n = _normmod_bwd(dh2, x2, row(mix_norm_g[1]), sc_m1, dx3, "norm_mix_bwd1")
    d_ada[("mix", 1)] = [acc_n[0:1], acc_n[1:2], acc_r[0:1]]
    dg_mix1 = acc_n[2]

    dcum = jnp.pad(dck.reshape(H, T).T, ((0, 0), (0, LANES - H)))
    dfl, acc_f = _forget_bwd(dcum, fl, fb, "forget_bwd")
    g_kvw = jnp.concatenate([_mm_tn(hk, dk, "kv_dw_k"), _mm_tn(hk, dv, "kv_dw_v"), _mm_tn(hk, dfl, "kv_dw_f")[:, :H]], axis=1)
    dhk = _mm_nt([(dk, w_k), (dv, w_v), (dfl, w_f)], f32, "kv_dh")
    dx2, acc_n = _normmod_bwd(dhk, x2, row(kv_norm_g), sc_kv, dx2, "norm_kv_bwd")
    d_ada[("kv", 0)] = [acc_n[0:1], acc_n[1:2]]
    dg_kv = acc_n[2]

    dx1, g_in0, g_out0, d_ada[("ffn", 0)], dg_ffn0 = ffn_bwd(dx2, x1, 0, sc_f0, gt_f0, ffn0)

    dyb, acc_r = _res_in(dx1, y0, gt_m0, "conv_res_bwd")
    dsw = _mm_nt([(dyb, wc_o)], f32, "conv_ds")
    g_co_out = _mm_tn(s0, dyb, "conv_dw_out")
    leave_a = as_bf([g_in0, jnp.concatenate([g_out0, by_rows(g_wq)], axis=1), g_kvw.reshape(D, NDEV, -1).transpose(1, 0, 2)])
    da, dgl, acc_c, dw_dw, r_in0, r_ra, r_kv = _conv_bwd(dsw, q0, p0, a0, gl0, w_dw, ln_g, ln_b, "conv_bwd", ride=leave_a)
    cs = 2 * D // NDEV
    g_ci_out = _mm_tn_shards(h0, dgl, cs, "conv_dw_g", into=_mm_tn_shards(h0, da, cs, "conv_dw_a"))
    dh0 = _mm_nt([(da, wc_a), (dgl, wc_g)], f32, "conv_dh")
    dx0, acc_n = _normmod_bwd(dh0, x0, row(mix_norm_g[0]), sc_m0, dx1, "norm_mix_bwd0")
    d_ada[("mix", 0)] = [acc_n[0:1], acc_n[1:2], acc_r[0:1]]
    dg_mix0 = acc_n[2]

    vec = [t.reshape(-1) for key in [(s[0], s[1]) for s in ADA_SEG] for t in d_ada[key]]
    vec += [dg_mix0, dg_mix1, dg_ffn0, dg_ffn1, dg_kv, acc_fin[0]]
    vec += [acc_f[0], acc_fin[1, :LANES]]
    vec += [acc_c[3], acc_c[4], dw_dw[:KW].reshape(-1), acc_c[2], acc_c[0], acc_c[1], acc_r[1]]
    small_parts = _exchange([_pad_rows(jnp.concatenate(vec), LANES, 8)], True, "gather_partials")[0]
    small_sum = _sum_parts(small_parts, "sum_partials").reshape(-1)
    d_ada_all = small_parts.reshape(NDEV, -1)[:, :ADA_TOT]
    off = 0
    gsm = {}
    ada_b_sum = []
    for _, _, n in ADA_SEG:
        ada_b_sum.append(small_sum[off:off + n]); off += n
    gsm["mix_ada_b"] = jnp.stack(ada_b_sum[0:2])
    gsm["ffn_ada_b"] = jnp.stack(ada_b_sum[2:4])
    gsm["kv_ada_b"] = ada_b_sum[4]
    gsm["mix_norm_g"] = small_sum[off:off + 2 * D].reshape(2, D); off += 2 * D
    gsm["ffn_norm_g"] = small_sum[off:off + 2 * D].reshape(2, D); off += 2 * D
    gsm["kv_norm_g"] = small_sum[off:off + D]; off += D
    gsm["final_norm_g"] = small_sum[off:off + D]; off += D
    gsm["forget_b"] = small_sum[off:off + H]; off += LANES
    loss = small_sum[off]; off += LANES
    sl = lambda full, width: lax.dynamic_slice_in_dim(full, me * width, width, axis=full.ndim - 1)
    gsm["conv_b_in"] = sl(small_sum[off:off + 2 * D].reshape(1, 2 * D), 2 * D // NDEV); off += 2 * D
    gsm["conv_w_dw"] = sl(small_sum[off:off + KW * D].reshape(1, KW, D), cl); off += KW * D
    for n in ("conv_b_dw", "conv_ln_g", "conv_ln_b", "conv_b_out"):
        gsm[n] = sl(small_sum[off:off + D].reshape(1, D), cl); off += D

    dsel, off = [], 0
    for (_, _, n), nl in zip(ADA_SEG, ADA_LOC):
        dsel.append(lax.dynamic_slice_in_dim(d_ada_all[:, off:off + n], me * nl, nl, axis=1)); off += n
    g_ada = _ada_bwd(c_all.T, jnp.concatenate(dsel, axis=1), "ada_bwd")
    gsm["mix_ada_w"] = jnp.stack([g_ada[:, 0:ADA_LOC[0]], g_ada[:, ADA_LOC[0]:2 * ADA_LOC[0]]])
    o2 = 2 * ADA_LOC[0]
    gsm["ffn_ada_w"] = jnp.stack([g_ada[:, o2:o2 + ADA_LOC[2]], g_ada[:, o2 + ADA_LOC[2]:o2 + 2 * ADA_LOC[2]]])
    gsm["kv_ada_w"] = g_ada[:, o2 + 2 * ADA_LOC[2]:]

    r_ci, r_co = _exchange(as_bf([g_ci_out, by_rows(g_co_out)]), False, "scatter_grads")
    res_in0 = _adamw(r_in0, *[pad_in(s, 0) for s in (W, M, V)], "adamw_ffn_in0")
    res_in1 = _adamw(r_in1, *[pad_in(s, 1) for s in (W, M, V)], "adamw_ffn_in1")
    res_ra = _adamw(r_ra, *[rows_a(s) for s in (W, M, V)], "adamw_rows_a")
    res_rb = _adamw(r_rb, *[rows_b(s) for s in (W, M, V)], "adamw_rows_b")
    res_ci = _adamw(r_ci, *[s["conv_w_in"][0] for s in (W, M, V)], "adamw_conv_in")
    res_co = _adamw(r_co, *[s["conv_w_out"][0] for s in (W, M, V)], "adamw_conv_out")
    res_kv = _adamw(r_kv, *[s["kv_w"] for s in (W, M, V)], "adamw_kv")
    rest = [n for n in names if n not in MAIN]
    pack_rest = lambda src: _pad_rows(jnp.concatenate([src[n].reshape(-1) for n in rest]), D, 256)
    res_rest = _adamw(pack_rest(gsm)[None], pack_rest(W), pack_rest(M), pack_rest(V), "adamw_rest")

    outs = []
    for k in range(4):
        ur = _split_flat(res_rest[k].reshape(-1), [(n, W[n].shape) for n in rest])
        ur["ffn_w_in"] = jnp.stack([res_in0[k][:, :FS], res_in1[k][:, :FS]])
        ur["conv_w_in"] = res_ci[k][None]
        ur["conv_w_out"] = res_co[k][None]
        ur["kv_w"] = res_kv[k]
        ur["ffn_w_out"] = jnp.stack([res_ra[k][:fo], res_rb[k][:fo]])
        ur["attn_w_q"] = res_ra[k][fo:][None]
        ur["attn_w_o"] = res_rb[k][fo:][None]
        outs.append(ur)
    grads, deltas, new_m, new_v = outs
    return (loss, dx0[None], *[grads[n] for n in names], *[deltas[n] for n in names],
            *[new_m[n] for n in names], *[new_v[n] for n in names])
```

```python
import functools

import jax
import jax.numpy as jnp
from jax import lax
from jax.experimental import pallas as pl
from jax.experimental.pallas import tpu as pltpu

f32, bf16 = jnp.float32, jnp.bfloat16
SDS = jax.ShapeDtypeStruct

D = 1024
F = 2816
H = 16
HD = 64
NP = H // 2
KW = 31
HALO = 32
NDEV = 8
FS = 2 * F // NDEV
FSP = 768
FP = 4 * FSP
EPS = 1e-6
NEG = -1e30
LANES = 128

ADAM_LR, ADAM_B1, ADAM_B2, ADAM_EPS, ADAM_WD, ADAM_STEP = 0.001, 0.9, 0.999, 1e-08, 0.01, 10

TM = 512
TC = 256
TQ = 1024
VMEM_LIMIT = 56 << 20

MAIN = ("ffn_w_in", "ffn_w_out", "conv_w_in", "conv_w_out", "kv_w", "attn_w_q", "attn_w_o")
ADA_SEG = (("mix", 0, 3 * D), ("mix", 1, 3 * D), ("ffn", 0, 3 * D), ("ffn", 1, 3 * D), ("kv", 0, 2 * D))
ADA_LOC = tuple(n // NDEV for _, _, n in ADA_SEG)
ADA_COLS = sum(ADA_LOC)
ADA_TOT = sum(n for _, _, n in ADA_SEG)


def _cparams(n_axes):
    return pltpu.CompilerParams(dimension_semantics=("arbitrary",) * n_axes, vmem_limit_bytes=VMEM_LIMIT)


def _mesh_pos():
    return lax.axis_index("x"), lax.axis_index("y"), lax.axis_index("c")


def _my_index():
    mx, my, mc = _mesh_pos()
    return 4 * mx + 2 * my + mc


def _peer(k, mx, my, mc):
    px = (1 - mx) if k & 4 else mx
    py = (1 - my) if k & 2 else my
    pc = (1 - mc) if k & 1 else mc
    return (px, py, pc), 4 * px + 2 * py + pc


def _exchange_copies(x_refs, o_refs, sems, gather):
    send_sems, recv_sems, local_sems = sems
    mx, my, mc = _mesh_pos()
    me = 4 * mx + 2 * my + mc
    copies = []
    for a, (x_ref, o_ref) in enumerate(zip(x_refs, o_refs)):
        copies.append(pltpu.make_async_copy(x_ref if gather else x_ref.at[me], o_ref.at[me], local_sems.at[a]))
        for k in range(1, NDEV):
            peer, pidx = _peer(k, mx, my, mc)
            sem = a * (NDEV - 1) + k - 1
            copies.append(pltpu.make_async_remote_copy(
                src_ref=x_ref if gather else x_ref.at[pidx], dst_ref=o_ref.at[me],
                send_sem=send_sems.at[sem], recv_sem=recv_sems.at[sem],
                device_id=peer, device_id_type=pl.DeviceIdType.MESH))
    return copies


def _exchange_shapes(xs, gather):
    n = len(xs)
    hbm = pl.BlockSpec(memory_space=pl.ANY)
    outs = [SDS((NDEV,) + tuple(x.shape if gather else x.shape[1:]), x.dtype) for x in xs]
    sems = [pltpu.SemaphoreType.DMA((n * (NDEV - 1),)), pltpu.SemaphoreType.DMA((n * (NDEV - 1),)), pltpu.SemaphoreType.DMA((n,))]
    return [hbm] * n, [hbm] * n, outs, sems


def _exchange_start(x_refs, o_refs, sems, gather, first):
    @pl.when(first)
    def _():
        for cp in _exchange_copies(x_refs, o_refs, sems, gather):
            cp.start()


def _exchange_wait(x_refs, o_refs, sems, gather, last):
    @pl.when(last)
    def _():
        for cp in _exchange_copies(x_refs, o_refs, sems, gather):
            cp.wait()


def _exchange(xs, gather, name):
    n = len(xs)

    def body(*refs):
        copies = _exchange_copies(refs[:n], refs[n:2 * n], refs[2 * n:], gather)
        for cp in copies:
            cp.start()
        for cp in copies:
            cp.wait()

    in_specs, out_specs, outs, sems = _exchange_shapes(xs, gather)
    return pl.pallas_call(body, out_shape=outs, in_specs=in_specs, out_specs=out_specs, scratch_shapes=sems, name=name)(*xs)


def _row(i):
    return (i, 0)


def _fix(i):
    return (0, 0)


def _normmod(x, g, shift, scale, name):
    T = x.shape[0]
    tm = min(TM, T)

    def body(x_ref, g_ref, sh_ref, sc_ref, h_ref):
        xv = x_ref[...]
        r = lax.rsqrt(jnp.mean(xv * xv, axis=-1, keepdims=True) + EPS)
        hn = (xv * r) * g_ref[...]
        h_ref[...] = (hn * (1.0 + sc_ref[...]) + sh_ref[...]).astype(bf16)

    vec = pl.BlockSpec((1, D), _fix)
    return pl.pallas_call(
        body, grid=(T // tm,), in_specs=[pl.BlockSpec((tm, D), _row), vec, vec, vec],
        out_specs=pl.BlockSpec((tm, D), _row), out_shape=SDS((T, D), bf16),
        compiler_params=_cparams(1), name=name)(x, g, shift, scale)


def _normmod_bwd(dh, x, g, scale, dx_res, name):
    T = x.shape[0]
    tm = min(TM, T)

    def body(dh_ref, x_ref, g_ref, sc_ref, res_ref, dx_ref, acc_ref):
        @pl.when(pl.program_id(0) == 0)
        def _():
            acc_ref[...] = jnp.zeros_like(acc_ref)
        xv = x_ref[...]
        dhv = dh_ref[...]
        gv = g_ref[...]
        r = lax.rsqrt(jnp.mean(xv * xv, axis=-1, keepdims=True) + EPS)
        xn = xv * r
        dhn = dhv * (1.0 + sc_ref[...])
        dxn = dhn * gv
        dx_ref[...] = res_ref[...] + r * (dxn - xn * jnp.mean(dxn * xn, axis=-1, keepdims=True))
        acc_ref[0:1, :] += jnp.sum(dhv, axis=0, keepdims=True)
        acc_ref[1:2, :] += jnp.sum(dhv * (xn * gv), axis=0, keepdims=True)
        acc_ref[2:3, :] += jnp.sum(dhn * xn, axis=0, keepdims=True)

    vec = pl.BlockSpec((1, D), _fix)
    til = pl.BlockSpec((tm, D), _row)
    return pl.pallas_call(
        body, grid=(T // tm,), in_specs=[til, til, vec, vec, til],
        out_specs=[til, pl.BlockSpec((8, D), _fix)], out_shape=[SDS((T, D), f32), SDS((8, D), f32)],
        compiler_params=_cparams(1), name=name)(dh, x, g, scale, dx_res)


def _res_in(dx, y, gate, name):
    T = dx.shape[0]
    tm = min(TM, T)

    def body(dx_ref, y_ref, gt_ref, dy_ref, acc_ref):
        @pl.when(pl.program_id(0) == 0)
        def _():
            acc_ref[...] = jnp.zeros_like(acc_ref)
        dxv = dx_ref[...]
        dy = dxv * gt_ref[...]
        dy_ref[...] = dy.astype(bf16)
        acc_ref[0:1, :] += jnp.sum(dxv * y_ref[...].astype(f32), axis=0, keepdims=True)
        acc_ref[1:2, :] += jnp.sum(dy, axis=0, keepdims=True)

    til = pl.BlockSpec((tm, D), _row)
    return pl.pallas_call(
        body, grid=(T // tm,), in_specs=[til, til, pl.BlockSpec((1, D), _fix)],
        out_specs=[til, pl.BlockSpec((8, D), _fix)], out_shape=[SDS((T, D), bf16), SDS((8, D), f32)],
        compiler_params=_cparams(1), name=name)(dx, y, gate)


def _final_bwd(x, g, tgt, name):
    T = x.shape[0]
    tm = min(TM, T)

    def body(x_ref, g_ref, t_ref, dx_ref, acc_ref):
        @pl.when(pl.program_id(0) == 0)
        def _():
            acc_ref[...] = jnp.zeros_like(acc_ref)
        xv = x_ref[...]
        gv = g_ref[...]
        r = lax.rsqrt(jnp.mean(xv * xv, axis=-1, keepdims=True) + EPS)
        xn = xv * r
        err = xn * gv - t_ref[...]
        dy = err * (1.0 / D)
        dxn = dy * gv
        dx_ref[...] = r * (dxn - xn * jnp.mean(dxn * xn, axis=-1, keepdims=True))
        acc_ref[0:1, :] += jnp.sum(dy * xn, axis=0, keepdims=True)
        acc_ref[1:2, :] += 0.5 * jnp.sum(jnp.mean(err * err, axis=-1, keepdims=True))

    til = pl.BlockSpec((tm, D), _row)
    return pl.pallas_call(
        body, grid=(T // tm,), in_specs=[til, pl.BlockSpec((1, D), _fix), til],
        out_specs=[til, pl.BlockSpec((8, D), _fix)], out_shape=[SDS((T, D), f32), SDS((8, D), f32)],
        compiler_params=_cparams(1), name=name)(x, g, tgt)


def _col_tile(n):
    if n <= 1024:
        return min(n, 512) if n % 512 == 0 else n
    return 1408 if n % 1408 == 0 else 1024


def _mm_gated(h, wa, wb, ba, bb, swiglu, name):
    T, K = h.shape
    N = wa.shape[1]
    tm, tn = min(TM, T), _col_tile(N)

    def body(h_ref, wa_ref, wb_ref, ba_ref, bb_ref, u_ref, w_ref, p_ref):
        hv = h_ref[...]
        u = jnp.dot(hv, wa_ref[...], preferred_element_type=f32) + ba_ref[...]
        w = jnp.dot(hv, wb_ref[...], preferred_element_type=f32) + bb_ref[...]
        u_ref[...] = u
        w_ref[...] = w
        if swiglu:
            p_ref[...] = ((u * jax.nn.sigmoid(u)) * w).astype(p_ref.dtype)
        else:
            p_ref[...] = (u * jax.nn.sigmoid(w)).astype(p_ref.dtype)

    wsp = pl.BlockSpec((K, tn), lambda i, j: (0, j))
    bsp = pl.BlockSpec((1, tn), lambda i, j: (0, j))
    osp = pl.BlockSpec((tm, tn), lambda i, j: (i, j))
    return pl.pallas_call(
        body, grid=(T // tm, N // tn), in_specs=[pl.BlockSpec((tm, K), lambda i, j: (i, 0)), wsp, wsp, bsp, bsp],
        out_specs=[osp, osp, osp], out_shape=[SDS((T, N), f32), SDS((T, N), f32), SDS((T, N), bf16)],
        compiler_params=_cparams(2), name=name)(h, wa, wb, ba, bb)


def _mm_res(a, w, b, x_in, gate, name):
    T, K = a.shape
    N = w.shape[1]
    tm, tn = min(TM, T), _col_tile(N)

    def body(a_ref, w_ref, b_ref, x_ref, gt_ref, xo_ref, y_ref):
        y = jnp.dot(a_ref[...], w_ref[...], preferred_element_type=f32) + b_ref[...]
        y_ref[...] = y.astype(bf16)
        xo_ref[...] = x_ref[...] + gt_ref[...] * y

    vsp = pl.BlockSpec((1, tn), lambda i, j: (0, j))
    osp = pl.BlockSpec((tm, tn), lambda i, j: (i, j))
    return pl.pallas_call(
        body, grid=(T // tm, N // tn),
        in_specs=[pl.BlockSpec((tm, K), lambda i, j: (i, 0)), pl.BlockSpec((K, tn), lambda i, j: (0, j)), vsp, osp, vsp],
        out_specs=[osp, osp], out_shape=[SDS((T, N), f32), SDS((T, N), bf16)],
        compiler_params=_cparams(2), name=name)(a, w, b, x_in, gate)


def _mm(a, w, out_dtype, out_scale, name):
    T, K = a.shape
    N = w.shape[1]
    tm, tn = min(TM, T), _col_tile(N)

    def body(a_ref, w_ref, o_ref):
        y = jnp.dot(a_ref[...], w_ref[...], preferred_element_type=f32)
        if out_scale != 1.0:
            y = y * out_scale
        o_ref[...] = y.astype(out_dtype)

    return pl.pallas_call(
        body, grid=(T // tm, N // tn),
        in_specs=[pl.BlockSpec((tm, K), lambda i, j: (i, 0)), pl.BlockSpec((K, tn), lambda i, j: (0, j))],
        out_specs=pl.BlockSpec((tm, tn), lambda i, j: (i, j)), out_shape=SDS((T, N), out_dtype),
        compiler_params=_cparams(2), name=name)(a, w)


def _dot_nt(a, b):
    return lax.dot_general(a, b, (((1,), (1,)), ((), ())), preferred_element_type=f32)


def _dot_tn(a, b):
    return lax.dot_general(a, b, (((0,), (0,)), ((), ())), preferred_element_type=f32)


def _mm_nt(pairs, out_dtype, name):
    T = pairs[0][0].shape[0]
    K = pairs[0][1].shape[0]
    tm, tk = min(TM, T), _col_tile(K)
    n = len(pairs)

    def body(*refs):
        o_ref = refs[2 * n]
        acc = None
        for i in range(n):
            part = _dot_nt(refs[2 * i][...].astype(bf16), refs[2 * i + 1][...])
            acc = part if acc is None else acc + part
        o_ref[...] = acc.astype(out_dtype)

    in_specs, args = [], []
    for dy, w in pairs:
        ni = dy.shape[1]
        in_specs += [pl.BlockSpec((tm, ni), lambda i, j: (i, 0)), pl.BlockSpec((tk, ni), lambda i, j: (j, 0))]
        args += [dy, w]
    return pl.pallas_call(
        body, grid=(T // tm, K // tk), in_specs=in_specs,
        out_specs=pl.BlockSpec((tm, tk), lambda i, j: (i, j)), out_shape=SDS((T, K), out_dtype),
        compiler_params=_cparams(2), name=name)(*args)


def _mm_nt_swiglu(dy, w, ug, uu, name):
    T, N = dy.shape
    K = w.shape[0]
    tm, tk = min(TM, T), _col_tile(K)

    def body(dy_ref, w_ref, ug_ref, uu_ref, dug_ref, duu_ref):
        dact = _dot_nt(dy_ref[...], w_ref[...])
        g = ug_ref[...].astype(f32)
        u = uu_ref[...].astype(f32)
        sg = jax.nn.sigmoid(g)
        duu_ref[...] = (dact * (g * sg)).astype(bf16)
        dug_ref[...] = (dact * u * (sg * (1.0 + g * (1.0 - sg)))).astype(bf16)

    osp = pl.BlockSpec((tm, tk), lambda i, j: (i, j))
    return pl.pallas_call(
        body, grid=(T // tm, K // tk),
        in_specs=[pl.BlockSpec((tm, N), lambda i, j: (i, 0)), pl.BlockSpec((tk, N), lambda i, j: (j, 0)), osp, osp],
        out_specs=[osp, osp], out_shape=[SDS((T, K), bf16), SDS((T, K), bf16)],
        compiler_params=_cparams(2), name=name)(dy, w, ug, uu)


def _mm_tn(a, b, name):
    T, K = a.shape
    N = b.shape[1]
    tt = min(TM, T)
    tk = K if K <= 1024 else _col_tile(K)
    tn = N if N <= 1024 else _col_tile(N)

    def body(a_ref, b_ref, o_ref):
        @pl.when(pl.program_id(2) == 0)
        def _():
            o_ref[...] = jnp.zeros_like(o_ref)
        o_ref[...] += _dot_tn(a_ref[...].astype(bf16), b_ref[...].astype(bf16))

    return pl.pallas_call(
        body, grid=(K // tk, N // tn, T // tt),
        in_specs=[pl.BlockSpec((tt, tk), lambda i, j, t: (t, i)), pl.BlockSpec((tt, tn), lambda i, j, t: (t, j))],
        out_specs=pl.BlockSpec((tk, tn), lambda i, j, t: (i, j)), out_shape=SDS((K, N), f32),
        compiler_params=_cparams(3), name=name)(a, b)


def _mm_tn_shards(a, b, c, name, into=None):
    T, K = a.shape
    half = NDEV // 2
    assert b.shape[1] == half * c
    tt = min(TM, T)
    nt = T // tt
    first = into is None

    def body(a_ref, b_ref, *rest):
        o_ref, acc = rest[-2], rest[-1]
        t = pl.program_id(1)

        @pl.when(t == 0)
        def _():
            acc[...] = jnp.zeros_like(acc)
        acc[...] += _dot_tn(a_ref[...].astype(bf16), b_ref[...].astype(bf16))

        @pl.when(t == nt - 1)
        def _():
            o_ref[...] = acc[...].astype(bf16)

    in_specs = [pl.BlockSpec((tt, K), lambda j, t: (t, 0)), pl.BlockSpec((tt, c), lambda j, t: (t, j))]
    args = [a, b]
    if not first:
        in_specs.append(pl.BlockSpec(memory_space=pl.ANY))
        args.append(into)
    base = 0 if first else half
    return pl.pallas_call(
        body, grid=(half, nt), in_specs=in_specs,
        out_specs=pl.BlockSpec((None, K, c), lambda j, t: (j + base, 0, 0)), out_shape=SDS((NDEV, K, c), bf16),
        scratch_shapes=[pltpu.VMEM((K, c), f32)], input_output_aliases={} if first else {2: 0},
        compiler_params=_cparams(2), name=name)(*args)


def _ffn_in(h, w_sh, name):
    T = h.shape[0]
    tm = min(TM, T)

    def body(h_ref, wg_ref, wu_ref, ug_ref, uu_ref, act_ref):
        hv = h_ref[...]
        ug = jnp.dot(hv, wg_ref[...], preferred_element_type=f32)
        uu = jnp.dot(hv, wu_ref[...], preferred_element_type=f32)
        ug_ref[...] = ug
        uu_ref[...] = uu
        act_ref[...] = ((ug * jax.nn.sigmoid(ug)) * uu).astype(bf16)

    osp = pl.BlockSpec((tm, FSP), lambda i, j: (i, j))
    return pl.pallas_call(
        body, grid=(T // tm, NDEV // 2),
        in_specs=[pl.BlockSpec((tm, D), lambda i, j: (i, 0)),
                  pl.BlockSpec((None, D, FSP), lambda i, j: (j, 0, 0)),
                  pl.BlockSpec((None, D, FSP), lambda i, j: (j + NDEV // 2, 0, 0))],
        out_specs=[osp, osp, osp], out_shape=[SDS((T, FP), f32), SDS((T, FP), f32), SDS((T, FP), bf16)],
        compiler_params=_cparams(2), name=name)(h, w_sh, w_sh)


def _ffn_dh(dug, duu, w_sh, name):
    T = dug.shape[0]
    tm, tk = min(TM, T), 512
    half = NDEV // 2

    def body(dg_ref, du_ref, wg_ref, wu_ref, o_ref):
        acc = None
        for s in range(half):
            cols = slice(s * FSP, (s + 1) * FSP)
            part = _dot_nt(dg_ref[:, cols], wg_ref[s]) + _dot_nt(du_ref[:, cols], wu_ref[s])
            acc = part if acc is None else acc + part
        o_ref[...] = acc

    dsp = pl.BlockSpec((tm, FP), lambda i, j: (i, 0))
    return pl.pallas_call(
        body, grid=(T // tm, D // tk),
        in_specs=[dsp, dsp, pl.BlockSpec((half, tk, FSP), lambda i, j: (0, j, 0)),
                  pl.BlockSpec((half, tk, FSP), lambda i, j: (1, j, 0))],
        out_specs=pl.BlockSpec((tm, tk), lambda i, j: (i, j)), out_shape=SDS((T, D), f32),
        compiler_params=_cparams(2), name=name)(dug, duu, w_sh, w_sh)


def _layernorm_parts(qv, g, b):
    mu = jnp.mean(qv, axis=-1, keepdims=True)
    cen = qv - mu
    rstd = lax.rsqrt(jnp.mean(cen * cen, axis=-1, keepdims=True) + EPS)
    z = cen * rstd
    return z, rstd, z * g + b


def _tap_groups(offsets):
    groups = {}
    for k, o in enumerate(offsets):
        groups.setdefault(o % 8, []).append((k, o - o % 8))
    return groups


def _conv_fwd(p, w_dw, b_dw, ln_g, ln_b, name, ride=()):
    T = p.shape[0]
    tc = min(TC, T)
    n = T // tc
    nx = len(ride)

    def body(p_ref, w_ref, b_ref, g_ref, bb_ref, *rest):
        x_refs, (q_ref, s_ref), xo_refs = rest[:nx], rest[nx:nx + 2], rest[nx + 2:2 * nx + 2]
        (ext, sh), sems = rest[2 * nx + 2:2 * nx + 4], rest[2 * nx + 4:]
        i = pl.program_id(0)
        if nx:
            _exchange_start(x_refs, xo_refs, sems, True, i == 0)

        @pl.when(i == 0)
        def _():
            ext[0:HALO, :] = jnp.zeros((HALO, D), f32)

        @pl.when(i > 0)
        def _():
            ext[0:HALO, :] = ext[tc:tc + HALO, :]

        ext[HALO:HALO + tc, :] = p_ref[...].astype(f32)
        groups = _tap_groups([HALO - (KW - 1) + k for k in range(KW)])
        for cb in range(D // LANES):
            cols = slice(cb * LANES, (cb + 1) * LANES)
            acc = jnp.zeros((tc, LANES), f32)
            for r, taps in groups.items():
                span = max(base for _, base in taps) + tc
                sh[0:span, :] = ext[r:r + span, cols]
                for k, base in taps:
                    acc = acc + w_ref[k:k + 1, cols] * sh[base:base + tc, :]
            q_ref[:, cols] = acc + b_ref[:, cols]
        _, _, l = _layernorm_parts(q_ref[...], g_ref[...], bb_ref[...])
        s_ref[...] = (l * jax.nn.sigmoid(l)).astype(bf16)
        if nx:
            _exchange_wait(x_refs, xo_refs, sems, True, i == n - 1)

    vec = pl.BlockSpec((1, D), _fix)
    til = pl.BlockSpec((tc, D), _row)
    x_in, x_out, x_shape, x_sems = _exchange_shapes(ride, True) if nx else ([], [], [], [])
    return pl.pallas_call(
        body, grid=(n,), in_specs=[til, pl.BlockSpec((HALO, D), _fix), vec, vec, vec] + x_in,
        out_specs=[til, til] + x_out, out_shape=[SDS((T, D), f32), SDS((T, D), bf16)] + x_shape,
        scratch_shapes=[pltpu.VMEM((tc + HALO, D), f32), pltpu.VMEM((tc + HALO, LANES), f32)] + x_sems, compiler_params=_cparams(1),
        name=name)(p, w_dw, b_dw, ln_g, ln_b, *ride)


def _conv_bwd(ds, q, p, a, gl, w_dw, ln_g, ln_b, name, ride=()):
    T = q.shape[0]
    tc = min(TC, T)
    n = T // tc
    nx = len(ride)

    def body(ds_ref, q_ref, p_ref, a_ref, gl_ref, w_ref, g_ref, bb_ref, *rest):
        x_refs, (da_ref, dgl_ref, acc_ref, dw_ref), xo_refs = rest[:nx], rest[nx:nx + 4], rest[nx + 4:2 * nx + 4]
        (ext, sh), sems = rest[2 * nx + 4:2 * nx + 6], rest[2 * nx + 6:]
        i = pl.program_id(0)
        if nx:
            _exchange_start(x_refs, xo_refs, sems, False, i == 0)

        @pl.when(i == 0)
        def _():
            acc_ref[...] = jnp.zeros_like(acc_ref)
            dw_ref[...] = jnp.zeros_like(dw_ref)
            ext[tc:tc + HALO, :] = jnp.zeros((HALO, D), f32)

        @pl.when(i > 0)
        def _():
            ext[tc:tc + HALO, :] = ext[0:HALO, :]

        gv = g_ref[...]
        z, rstd, l = _layernorm_parts(q_ref[...], gv, bb_ref[...])
        sg = jax.nn.sigmoid(l)
        dl = ds_ref[...] * (sg * (1.0 + l * (1.0 - sg)))
        dz = dl * gv
        dq = rstd * (dz - jnp.mean(dz, axis=-1, keepdims=True) - z * jnp.mean(dz * z, axis=-1, keepdims=True))
        ext[0:tc, :] = dq.astype(bf16).astype(f32)
        acc_ref[0:1, :] += jnp.sum(dl * z, axis=0, keepdims=True)
        acc_ref[1:2, :] += jnp.sum(dl, axis=0, keepdims=True)
        acc_ref[2:3, :] += jnp.sum(dq, axis=0, keepdims=True)
        groups = _tap_groups([KW - 1 - k for k in range(KW)])
        for cb in range(D // LANES):
            cols = slice(cb * LANES, (cb + 1) * LANES)
            pc = p_ref[:, cols].astype(f32)
            dp = jnp.zeros((tc, LANES), f32)
            for r, taps in groups.items():
                span = max(base for _, base in taps) + tc
                sh[0:span, :] = ext[r:r + span, cols]
                for k, base in taps:
                    sl = sh[base:base + tc, :]
                    dp = dp + w_ref[k:k + 1, cols] * sl
                    dw_ref[k:k + 1, cols] += jnp.sum(sl * pc, axis=0, keepdims=True)
            av = a_ref[:, cols].astype(f32)
            sgl = jax.nn.sigmoid(gl_ref[:, cols].astype(f32))
            da = dp * sgl
            dgl = dp * av * (sgl * (1.0 - sgl))
            da_ref[:, cols] = da.astype(bf16)
            dgl_ref[:, cols] = dgl.astype(bf16)
            acc_ref[3:4, cols] += jnp.sum(da, axis=0, keepdims=True)
            acc_ref[4:5, cols] += jnp.sum(dgl, axis=0, keepdims=True)
        if nx:
            _exchange_wait(x_refs, xo_refs, sems, False, i == n - 1)

    rev = lambda i: (n - 1 - i, 0)
    til = pl.BlockSpec((tc, D), rev)
    vec = pl.BlockSpec((1, D), _fix)
    x_in, x_out, x_shape, x_sems = _exchange_shapes(ride, False) if nx else ([], [], [], [])
    return pl.pallas_call(
        body, grid=(n,), in_specs=[til, til, til, til, til, pl.BlockSpec((HALO, D), _fix), vec, vec] + x_in,
        out_specs=[til, til, pl.BlockSpec((8, D), _fix), pl.BlockSpec((HALO, D), _fix)] + x_out,
        out_shape=[SDS((T, D), bf16), SDS((T, D), bf16), SDS((8, D), f32), SDS((HALO, D), f32)] + x_shape,
        scratch_shapes=[pltpu.VMEM((tc + HALO, D), f32), pltpu.VMEM((tc + HALO, LANES), f32)] + x_sems, compiler_params=_cparams(1),
        name=name)(ds, q, p, a, gl, w_dw, ln_g, ln_b, *ride)


def _tri(n, upper):
    r = lax.broadcasted_iota(jnp.int32, (n, n), 0)
    c = lax.broadcasted_iota(jnp.int32, (n, n), 1)
    return ((c >= r) if upper else (r >= c)).astype(f32)


def _forget_fwd(fl, fb, name):
    T = fl.shape[0]
    tc = min(TC, T)

    def body(fl_ref, fb_ref, cum_ref, carry):
        @pl.when(pl.program_id(0) == 0)
        def _():
            carry[...] = jnp.zeros_like(carry)
        xv = fl_ref[...] + fb_ref[...]
        lf = jnp.minimum(xv, 0.0) - jnp.log(1.0 + jnp.exp(-jnp.abs(xv)))
        cs = jnp.dot(_tri(tc, False), lf, preferred_element_type=f32, precision=lax.Precision.HIGHEST) + carry[0:1, :]
        cum_ref[...] = cs
        carry[0:1, :] = cs[tc - 1:tc, :]

    til = pl.BlockSpec((tc, LANES), _row)
    return pl.pallas_call(
        body, grid=(T // tc,), in_specs=[til, pl.BlockSpec((1, LANES), _fix)], out_specs=til,
        out_shape=SDS((T, LANES), f32), scratch_shapes=[pltpu.VMEM((8, LANES), f32)],
        compiler_params=_cparams(1), name=name)(fl, fb)


def _forget_bwd(dcum, fl, fb, name):
    T = fl.shape[0]
    tc = min(TC, T)
    n = T // tc

    def body(dc_ref, fl_ref, fb_ref, dfl_ref, acc_ref, carry):
        @pl.when(pl.program_id(0) == 0)
        def _():
            carry[...] = jnp.zeros_like(carry)
            acc_ref[...] = jnp.zeros_like(acc_ref)
        dlf = jnp.dot(_tri(tc, True), dc_ref[...], preferred_element_type=f32, precision=lax.Precision.HIGHEST) + carry[0:1, :]
        carry[0:1, :] = dlf[0:1, :]
        dfl = dlf * (1.0 - jax.nn.sigmoid(fl_ref[...] + fb_ref[...]))
        dfl_ref[...] = dfl.astype(bf16)
        acc_ref[0:1, :] += jnp.sum(dfl, axis=0, keepdims=True)

    rev = lambda i: (n - 1 - i, 0)
    til = pl.BlockSpec((tc, LANES), rev)
    return pl.pallas_call(
        body, grid=(n,), in_specs=[til, til, pl.BlockSpec((1, LANES), _fix)],
        out_specs=[til, pl.BlockSpec((8, LANES), _fix)], out_shape=[SDS((T, LANES), bf16), SDS((8, LANES), f32)],
        scratch_shapes=[pltpu.VMEM((8, LANES), f32)], compiler_params=_cparams(1), name=name)(dcum, fl, fb)


def _causal(s, n):
    r = lax.broadcasted_iota(jnp.int32, (n, n), 0)
    c = lax.broadcasted_iota(jnp.int32, (n, n), 1)
    return jnp.where(c <= r, s, NEG)


def _attn_fwd(q, k, v, cum_t, name, ride=()):
    T = q.shape[0]
    tq = min(TQ, T)
    n = T // tq
    nx = len(ride)

    def body(q_ref, k_ref, v_ref, ck_ref, *rest):
        x_refs, (o_ref, o32_ref, st_ref), xo_refs = rest[:nx], rest[nx:nx + 3], rest[nx + 3:2 * nx + 3]
        (m_sc, l_sc, acc_sc, res_sc), sems = rest[2 * nx + 3:2 * nx + 7], rest[2 * nx + 7:]
        i = pl.program_id(1)
        if nx:
            _exchange_start(x_refs, xo_refs, sems, True, (pl.program_id(0) == 0) & (i == 0))
        lane = lax.broadcasted_iota(jnp.int32, (1, LANES), 1)
        lo = lane < HD
        q2 = q_ref[...]
        zero = jnp.zeros_like(q2)
        qa = (jnp.where(lo, q2, zero), jnp.where(lo, zero, q2))
        m_sc[...] = jnp.full(m_sc.shape, NEG, f32)
        l_sc[...] = jnp.zeros_like(l_sc)
        acc_sc[...] = jnp.zeros_like(acc_sc)
        res_sc[...] = jnp.zeros_like(res_sc)

        def block(j, masked):
            off = pl.multiple_of(j * tq, tq)
            k2 = k_ref[pl.ds(off, tq), :]
            v2 = v_ref[pl.ds(off, tq), :]
            for a in range(2):
                s = _dot_nt(qa[a], k2) - ck_ref[0, a:a + 1, pl.ds(off, tq)]
                if masked:
                    s = _causal(s, tq)
                m_old = m_sc[a]
                m_new = jnp.maximum(m_old, jnp.max(s, axis=1, keepdims=True))
                alpha = jnp.exp(m_old - m_new)
                pm = jnp.exp(s - m_new)
                pb = pm.astype(bf16)
                pr = (pm - pb.astype(f32)).astype(bf16)
                l_sc[a] = alpha * l_sc[a] + jnp.sum(pm, axis=1, keepdims=True)
                acc_sc[a] = alpha * acc_sc[a] + jnp.dot(pb, v2, preferred_element_type=f32)
                res_sc[a] = alpha * res_sc[a] + jnp.dot(pr, v2, preferred_element_type=f32)
                m_sc[a] = m_new

        def step(j, carry):
            block(j, False)
            return carry

        lax.fori_loop(0, i, step, 0)
        block(i, True)
        o_ref[...] = jnp.where(lo, acc_sc[0] / l_sc[0], acc_sc[1] / l_sc[1]).astype(bf16)
        o32_ref[...] = jnp.where(lo, (acc_sc[0] + res_sc[0]) / l_sc[0], (acc_sc[1] + res_sc[1]) / l_sc[1])
        lse0 = m_sc[0] + jnp.log(l_sc[0])
        lse1 = m_sc[1] + jnp.log(l_sc[1])
        st_ref[0] = jnp.where(lane == 0, lse0, jnp.where(lane == 1, lse1, 0.0))
        if nx:
            _exchange_wait(x_refs, xo_refs, sems, True, (pl.program_id(0) == NP - 1) & (i == n - 1))

    full = lambda blk: pl.BlockSpec((T, LANES), blk)
    x_in, x_out, x_shape, x_sems = _exchange_shapes(ride, True) if nx else ([], [], [], [])
    return pl.pallas_call(
        body, grid=(NP, n),
        in_specs=[pl.BlockSpec((tq, LANES), lambda p, i: (i, p)), full(lambda p, i: (0, p)), full(lambda p, i: (0, p)),
                  pl.BlockSpec((1, 2, T), lambda p, i: (p, 0, 0))] + x_in,
        out_specs=[pl.BlockSpec((tq, LANES), lambda p, i: (i, p)), pl.BlockSpec((tq, LANES), lambda p, i: (i, p)),
                   pl.BlockSpec((1, tq, LANES), lambda p, i: (p, i, 0))] + x_out,
        out_shape=[SDS((T, H * HD), bf16), SDS((T, H * HD), f32), SDS((NP, T, LANES), f32)] + x_shape,
        scratch_shapes=[pltpu.VMEM((2, tq, 1), f32), pltpu.VMEM((2, tq, 1), f32), pltpu.VMEM((2, tq, LANES), f32),
                        pltpu.VMEM((2, tq, LANES), f32)] + x_sems,
        compiler_params=_cparams(2), name=name)(q, k, v, cum_t, *ride)


def _attn_stats(do, o, lse, name):
    T = do.shape[0]
    tm = min(TM, T)

    def body(do_ref, o_ref, lse_ref, st_ref):
        lane = lax.broadcasted_iota(jnp.int32, (1, LANES), 1)
        prod = do_ref[...].astype(f32) * o_ref[...].astype(f32)
        d0 = jnp.sum(jnp.where(lane < HD, prod, 0.0), axis=1, keepdims=True)
        d1 = jnp.sum(jnp.where(lane < HD, 0.0, prod), axis=1, keepdims=True)
        st_ref[0] = jnp.where(lane < 2, lse_ref[0], jnp.where(lane == 2, d0, jnp.where(lane == 3, d1, 0.0)))

    til = pl.BlockSpec((tm, LANES), lambda p, i: (i, p))
    stt = pl.BlockSpec((1, tm, LANES), lambda p, i: (p, i, 0))
    return pl.pallas_call(
        body, grid=(NP, T // tm), in_specs=[til, til, stt], out_specs=stt, out_shape=SDS((NP, T, LANES), f32),
        compiler_params=_cparams(2), name=name)(do, o, lse)


def _attn_bwd(q, k, v, do, st, cum_t, name, ride=()):
    T = q.shape[0]
    tq = min(TQ, T)
    n = T // tq
    nx = len(ride)

    def body(q_ref, k_ref, v_ref, do_ref, st_ref, ck_ref, *rest):
        x_refs, (dq_ref, dk_ref, dv_ref, dck_ref), xo_refs = rest[:nx], rest[nx:nx + 4], rest[nx + 4:2 * nx + 4]
        (dk_sc, dv_sc, dck_sc), sems = rest[2 * nx + 4:2 * nx + 7], rest[2 * nx + 7:]
        j = pl.program_id(1)
        if nx:
            _exchange_start(x_refs, xo_refs, sems, False, (pl.program_id(0) == 0) & (j == 0))
        lane = lax.broadcasted_iota(jnp.int32, (1, LANES), 1)
        lo = lane < HD

        @pl.when(j == 0)
        def _():
            dq_ref[...] = jnp.zeros_like(dq_ref)

        k2 = k_ref[...]
        v2 = v_ref[...]
        zero = jnp.zeros_like(k2)
        ka = (jnp.where(lo, k2, zero), jnp.where(lo, zero, k2))
        va = (jnp.where(lo, v2, zero), jnp.where(lo, zero, v2))
        dk_sc[...] = jnp.zeros_like(dk_sc)
        dv_sc[...] = jnp.zeros_like(dv_sc)
        dck_sc[...] = jnp.zeros_like(dck_sc)

        def block(i, masked):
            off = pl.multiple_of(i * tq, tq)
            q2 = q_ref[pl.ds(off, tq), :]
            do2 = do_ref[pl.ds(off, tq), :]
            stt = st_ref[0, pl.ds(off, tq), :]
            parts = []
            for a in range(2):
                s = _dot_nt(q2, ka[a]) - ck_ref[0, a:a + 1, :]
                if masked:
                    s = _causal(s, tq)
                pm = jnp.exp(s - stt[:, a:a + 1])
                dp = _dot_nt(do2, va[a])
                dsm = pm * (dp - stt[:, 2 + a:3 + a])
                dsb = dsm.astype(bf16)
                dv_sc[a] += _dot_tn(pm.astype(bf16), do2)
                dk_sc[a] += _dot_tn(dsb, q2)
                dck_sc[a:a + 1, :] -= jnp.sum(dsm, axis=0, keepdims=True)
                parts.append(jnp.dot(dsb, k2, preferred_element_type=f32))
            dq_ref[pl.ds(off, tq), :] += jnp.where(lo, parts[0], parts[1])

        block(j, True)

        def step(i, carry):
            block(i, False)
            return carry

        lax.fori_loop(j + 1, n, step, 0)
        dk_ref[...] = jnp.where(lo, dk_sc[0], dk_sc[1]).astype(bf16)
        dv_ref[...] = jnp.where(lo, dv_sc[0], dv_sc[1]).astype(bf16)
        dck_ref[0] = dck_sc[0:2, :]

        @pl.when(j == n - 1)
        def _():
            dq_ref[...] = dq_ref[...] * (HD ** -0.5)

        if nx:
            _exchange_wait(x_refs, xo_refs, sems, False, (pl.program_id(0) == NP - 1) & (j == n - 1))

    full = lambda: pl.BlockSpec((T, LANES), lambda p, j: (0, p))
    kvb = lambda: pl.BlockSpec((tq, LANES), lambda p, j: (j, p))
    ckb = lambda: pl.BlockSpec((1, 2, tq), lambda p, j: (p, 0, j))
    x_in, x_out, x_shape, x_sems = _exchange_shapes(ride, False) if nx else ([], [], [], [])
    return pl.pallas_call(
        body, grid=(NP, n),
        in_specs=[full(), kvb(), kvb(), full(), pl.BlockSpec((1, T, LANES), lambda p, j: (p, 0, 0)), ckb()] + x_in,
        out_specs=[full(), kvb(), kvb(), ckb()] + x_out,
        out_shape=[SDS((T, H * HD), f32), SDS((T, H * HD), bf16), SDS((T, H * HD), bf16), SDS((NP, 2, T), f32)] + x_shape,
        scratch_shapes=[pltpu.VMEM((2, tq, LANES), f32), pltpu.VMEM((2, tq, LANES), f32), pltpu.VMEM((8, tq), f32)] + x_sems,
        compiler_params=_cparams(2), name=name)(q, k, v, do, st, cum_t, *ride)


def _ada_fwd(c_all, w_cat, name):
    n = w_cat.shape[1]
    tn = 256

    def body(c_ref, w_ref, o_ref):
        cv = c_ref[...]
        o_ref[...] = jnp.dot((cv * jax.nn.sigmoid(cv)).astype(bf16), w_ref[...].astype(bf16), preferred_element_type=f32)

    return pl.pallas_call(
        body, grid=(n // tn,), in_specs=[pl.BlockSpec((NDEV, D), _fix), pl.BlockSpec((D, tn), lambda i: (0, i))],
        out_specs=pl.BlockSpec((NDEV, tn), lambda i: (0, i)), out_shape=SDS((NDEV, n), f32),
        compiler_params=_cparams(1), name=name)(c_all, w_cat)


def _ada_bwd(c_all_t, dsel, name):
    n = dsel.shape[1]
    tn = 256

    def body(c_ref, d_ref, o_ref):
        cv = c_ref[...]
        ca = cv * jax.nn.sigmoid(cv)
        acc = ca[:, 0:1] * d_ref[0:1, :]
        for b in range(1, NDEV):
            acc = acc + ca[:, b:b + 1] * d_ref[b:b + 1, :]
        o_ref[...] = acc

    return pl.pallas_call(
        body, grid=(n // tn,), in_specs=[pl.BlockSpec((D, NDEV), _fix), pl.BlockSpec((NDEV, tn), lambda i: (0, i))],
        out_specs=pl.BlockSpec((D, tn), lambda i: (0, i)), out_shape=SDS((D, n), f32),
        compiler_params=_cparams(1), name=name)(c_all_t, dsel)


def _sum_parts(parts, name):
    R = parts.shape[1]

    def body(p_ref, o_ref):
        acc = p_ref[0]
        for j in range(1, NDEV):
            acc = acc + p_ref[j]
        o_ref[...] = acc

    return pl.pallas_call(body, out_shape=SDS((R, LANES), f32), name=name)(parts)


def _adamw(g_parts, w, m, v, name):
    n_parts, R, C = g_parts.shape
    tr = next(t for t in (256, 128, 64, 32, 16, 8) if R % t == 0)
    c1 = 1.0 / (1.0 - ADAM_B1 ** ADAM_STEP)
    c2 = 1.0 / (1.0 - ADAM_B2 ** ADAM_STEP)

    def body(g_ref, w_ref, m_ref, v_ref, go_ref, d_ref, mo_ref, vo_ref):
        g = g_ref[0].astype(f32)
        for j in range(1, n_parts):
            g = g + g_ref[j].astype(f32)
        mn = ADAM_B1 * m_ref[...] + (1.0 - ADAM_B1) * g
        vn = ADAM_B2 * v_ref[...] + (1.0 - ADAM_B2) * (g * g)
        go_ref[...] = g
        mo_ref[...] = mn
        vo_ref[...] = vn
        d_ref[...] = -ADAM_LR * ((mn * c1) / (jnp.sqrt(vn * c2) + ADAM_EPS) + ADAM_WD * w_ref[...])

    til = pl.BlockSpec((tr, C), _row)
    out = SDS((R, C), f32)
    return pl.pallas_call(
        body, grid=(R // tr,), in_specs=[pl.BlockSpec((n_parts, tr, C), lambda i: (0, i, 0)), til, til, til],
        out_specs=[til, til, til, til], out_shape=[out, out, out, out],
        compiler_params=_cparams(1), name=name)(g_parts, w, m, v)


def _pad_rows(flat, cols, mult):
    n = flat.shape[-1]
    rows = -(-n // cols)
    rows = -(-rows // mult) * mult
    pad = [(0, 0)] * (flat.ndim - 1) + [(0, rows * cols - n)]
    return jnp.pad(flat, pad).reshape(flat.shape[:-1] + (rows, cols))


def _split_flat(flat, shapes):
    out, off = {}, 0
    for name, shp in shapes:
        n = 1
        for d in shp:
            n *= d
        out[name] = flat[off:off + n].reshape(shp)
        off += n
    return out


def kernel(x, c, mix_norm_g, mix_ada_w, mix_ada_b, ffn_norm_g, ffn_ada_w, ffn_ada_b, ffn_w_in, ffn_w_out, conv_w_in, conv_b_in, conv_w_dw, conv_b_dw, conv_ln_g, conv_ln_b, conv_w_out, conv_b_out, kv_norm_g, kv_ada_w, kv_ada_b, kv_w, forget_b, attn_w_q, attn_w_o, final_norm_g, loss_target, m_mix_norm_g, m_mix_ada_w, m_mix_ada_b, m_ffn_norm_g, m_ffn_ada_w, m_ffn_ada_b, m_ffn_w_in, m_ffn_w_out, m_conv_w_in, m_conv_b_in, m_conv_w_dw, m_conv_b_dw, m_conv_ln_g, m_conv_ln_b, m_conv_w_out, m_conv_b_out, m_kv_norm_g, m_kv_ada_w, m_kv_ada_b, m_kv_w, m_forget_b, m_attn_w_q, m_attn_w_o, m_final_norm_g, v_mix_norm_g, v_mix_ada_w, v_mix_ada_b, v_ffn_norm_g, v_ffn_ada_w, v_ffn_ada_b, v_ffn_w_in, v_ffn_w_out, v_conv_w_in, v_conv_b_in, v_conv_w_dw, v_conv_b_dw, v_conv_ln_g, v_conv_ln_b, v_conv_w_out, v_conv_b_out, v_kv_norm_g, v_kv_ada_w, v_kv_ada_b, v_kv_w, v_forget_b, v_attn_w_q, v_attn_w_o, v_final_norm_g):
    W = dict(mix_norm_g=mix_norm_g, mix_ada_w=mix_ada_w, mix_ada_b=mix_ada_b, ffn_norm_g=ffn_norm_g, ffn_ada_w=ffn_ada_w, ffn_ada_b=ffn_ada_b, ffn_w_in=ffn_w_in, ffn_w_out=ffn_w_out, conv_w_in=conv_w_in, conv_b_in=conv_b_in, conv_w_dw=conv_w_dw, conv_b_dw=conv_b_dw, conv_ln_g=conv_ln_g, conv_ln_b=conv_ln_b, conv_w_out=conv_w_out, conv_b_out=conv_b_out, kv_norm_g=kv_norm_g, kv_ada_w=kv_ada_w, kv_ada_b=kv_ada_b, kv_w=kv_w, forget_b=forget_b, attn_w_q=attn_w_q, attn_w_o=attn_w_o, final_norm_g=final_norm_g)
    M = dict(mix_norm_g=m_mix_norm_g, mix_ada_w=m_mix_ada_w, mix_ada_b=m_mix_ada_b, ffn_norm_g=m_ffn_norm_g, ffn_ada_w=m_ffn_ada_w, ffn_ada_b=m_ffn_ada_b, ffn_w_in=m_ffn_w_in, ffn_w_out=m_ffn_w_out, conv_w_in=m_conv_w_in, conv_b_in=m_conv_b_in, conv_w_dw=m_conv_w_dw, conv_b_dw=m_conv_b_dw, conv_ln_g=m_conv_ln_g, conv_ln_b=m_conv_ln_b, conv_w_out=m_conv_w_out, conv_b_out=m_conv_b_out, kv_norm_g=m_kv_norm_g, kv_ada_w=m_kv_ada_w, kv_ada_b=m_kv_ada_b, kv_w=m_kv_w, forget_b=m_forget_b, attn_w_q=m_attn_w_q, attn_w_o=m_attn_w_o, final_norm_g=m_final_norm_g)
    V = dict(mix_norm_g=v_mix_norm_g, mix_ada_w=v_mix_ada_w, mix_ada_b=v_mix_ada_b, ffn_norm_g=v_ffn_norm_g, ffn_ada_w=v_ffn_ada_w, ffn_ada_b=v_ffn_ada_b, ffn_w_in=v_ffn_w_in, ffn_w_out=v_ffn_w_out, conv_w_in=v_conv_w_in, conv_b_in=v_conv_b_in, conv_w_dw=v_conv_w_dw, conv_b_dw=v_conv_b_dw, conv_ln_g=v_conv_ln_g, conv_ln_b=v_conv_ln_b, conv_w_out=v_conv_w_out, conv_b_out=v_conv_b_out, kv_norm_g=v_kv_norm_g, kv_ada_w=v_kv_ada_w, kv_ada_b=v_kv_ada_b, kv_w=v_kv_w, forget_b=v_forget_b, attn_w_q=v_attn_w_q, attn_w_o=v_attn_w_o, final_norm_g=v_final_norm_g)
    names = list(W)
    T = x.shape[1]
    me = _my_index()
    x0 = x[0]
    tgt = loss_target[0]
    row = lambda vct: vct.reshape(1, -1)

    small_names = ("conv_b_in", "conv_w_dw", "conv_b_dw", "conv_ln_g", "conv_ln_b", "conv_b_out")
    small_loc = jnp.concatenate([c.reshape(-1)] + [W[n].reshape(-1) for n in small_names])
    sg = _exchange([_pad_rows(small_loc, LANES, 8)], True, "gather_small")[0].reshape(NDEV, -1)
    c_all = sg[:, :D]
    off = D
    b_in = sg[:, off:off + 2 * D // NDEV].reshape(1, 2 * D); off += 2 * D // NDEV
    cl = D // NDEV
    w_dw = sg[:, off:off + KW * cl].reshape(NDEV, KW, cl).transpose(1, 0, 2).reshape(KW, D); off += KW * cl
    w_dw = jnp.pad(w_dw, ((0, HALO - KW), (0, 0))).astype(bf16).astype(f32)
    b_dw = sg[:, off:off + cl].reshape(1, D); off += cl
    ln_g = sg[:, off:off + cl].reshape(1, D); off += cl
    ln_b = sg[:, off:off + cl].reshape(1, D); off += cl
    b_out = sg[:, off:off + cl].reshape(1, D)

    cat_ada = lambda s: jnp.concatenate([s["mix_ada_w"][0], s["mix_ada_w"][1], s["ffn_ada_w"][0], s["ffn_ada_w"][1], s["kv_ada_w"]], axis=1)
    w_cat = cat_ada(W)
    ada_loc = _ada_fwd(c_all, w_cat, "ada_fwd")
    ada_all = _exchange([ada_loc], True, "gather_ada")[0]
    ada_me = lax.dynamic_index_in_dim(ada_all, me, axis=1, keepdims=False)
    ada_bias = (mix_ada_b[0], mix_ada_b[1], ffn_ada_b[0], ffn_ada_b[1], kv_ada_b)
    ada, off = [], 0
    for nl, bias in zip(ADA_LOC, ada_bias):
        full = ada_me[:, off:off + nl].reshape(-1) + bias
        ada.append([row(t) for t in jnp.split(full, full.shape[0] // D)])
        off += nl
    (sh_m0, sc_m0, gt_m0), (sh_m1, sc_m1, gt_m1), (sh_f0, sc_f0, gt_f0), (sh_f1, sc_f1, gt_f1), (sh_kv, sc_kv) = ada

    pad_in = lambda src, l: jnp.pad(src["ffn_w_in"][l], ((0, 0), (0, FSP - FS)))
    rows_a = lambda src: jnp.concatenate([src["ffn_w_out"][0], src["attn_w_q"][0]])
    rows_b = lambda src: jnp.concatenate([src["ffn_w_out"][1], src["attn_w_o"][0]])
    as_bf = lambda arrs: [t.astype(bf16) for t in arrs]
    fo, sq_rows = F // NDEV, D // NDEV
    g_ci, g_co = _exchange(as_bf([conv_w_in[0], conv_w_out[0]]), True, "gather_weights")
    soon = as_bf([pad_in(W, 0), rows_a(W), kv_w])
    late = as_bf([pad_in(W, 1), rows_b(W)])

    def w_out_of(g_r):
        t = g_r[:, :fo].reshape(NDEV // 2, FS, D)
        return jnp.pad(t, ((0, 0), (0, FSP - FS), (0, 0))).reshape(FP, D)

    conv_in_full = g_ci.transpose(1, 0, 2).reshape(D, 2 * D)
    wc_a, wc_g = conv_in_full[:, :D], conv_in_full[:, D:]
    wc_o = g_co.reshape(D, D)
    zeros_d = jnp.zeros((1, D), f32)
    fb = jnp.pad(forget_b, (0, LANES - H)).reshape(1, LANES)

    h0 = _normmod(x0, row(mix_norm_g[0]), sh_m0, sc_m0, "norm_mix0")
    a0, gl0, p0 = _mm_gated(h0, wc_a, wc_g, b_in[:, :D], b_in[:, D:], False, "conv_in")
    q0, s0, w_sh0, g_ra, g_kv = _conv_fwd(p0, w_dw, b_dw, ln_g, ln_b, "conv_dw", ride=soon)
    w_out0, w_q = w_out_of(g_ra), g_ra[:, fo:].reshape(D, D)
    kv_full = g_kv.transpose(1, 0, 2).reshape(D, -1)
    w_k, w_v = kv_full[:, :D], kv_full[:, D:2 * D]
    w_f = jnp.pad(kv_full[:, 2 * D:], ((0, 0), (0, LANES - H)))
    x1, y0 = _mm_res(s0, wc_o, b_out, x0, gt_m0, "conv_out")

    def ffn_fwd(xin, l, sh, sc, gt, w_sh, w_out):
        h = _normmod(xin, row(ffn_norm_g[l]), sh, sc, f"norm_ffn{l}")
        ug, uu, act = _ffn_in(h, w_sh, f"ffn_in{l}")
        xo, y = _mm_res(act, w_out, zeros_d, xin, gt, f"ffn_out{l}")
        return xo, (h, ug, uu, act, y, w_sh, w_out)

    x2, ffn0 = ffn_fwd(x1, 0, sh_f0, sc_f0, gt_f0, w_sh0, w_out0)

    hk = _normmod(x2, row(kv_norm_g), sh_kv, sc_kv, "norm_kv")
    k_sh = _mm(hk, w_k, bf16, 1.0, "proj_k")
    v_sh = _mm(hk, w_v, bf16, 1.0, "proj_v")
    fl = _mm(hk, w_f, f32, 1.0, "proj_f")
    cum = _forget_fwd(fl, fb, "forget_fwd")
    cum_t = cum[:, :H].T.reshape(NP, 2, T)

    h2 = _normmod(x2, row(mix_norm_g[1]), sh_m1, sc_m1, "norm_mix1")
    qh = _mm(h2, w_q, bf16, HD ** -0.5, "proj_q")
    o, o32, lse, w_sh1, g_rb = _attn_fwd(qh, k_sh, v_sh, cum_t, "attn_fwd", ride=late)
    w_out1, w_o = w_out_of(g_rb), g_rb[:, fo:].reshape(D, D)
    x3, y1 = _mm_res(o, w_o, zeros_d, x2, gt_m1, "attn_out")

    x4, ffn1 = ffn_fwd(x3, 1, sh_f1, sc_f1, gt_f1, w_sh1, w_out1)

    dx4, acc_fin = _final_bwd(x4, row(final_norm_g), tgt, "final_bwd")

    d_ada = {}
    by_rows = lambda g: g.reshape(NDEV, sq_rows, D)

    def ffn_bwd(dx_out, xin, l, sc, gt, saved):
        h, ug, uu, act, y, w_sh, w_out = saved
        dyb, acc_r = _res_in(dx_out, y, gt, f"ffn_res_bwd{l}")
        dug, duu = _mm_nt_swiglu(dyb, w_out, ug, uu, f"ffn_dact{l}")
        g_out = _mm_tn(act, dyb, f"ffn_dw_out{l}").reshape(NDEV // 2, FSP, D)[:, :FS].reshape(NDEV, fo, D)
        g_in = _mm_tn_shards(h, duu, FSP, f"ffn_dw_up{l}", into=_mm_tn_shards(h, dug, FSP, f"ffn_dw_gate{l}"))
        dh = _ffn_dh(dug, duu, w_sh, f"ffn_dh{l}")
        dxi, acc_n = _normmod_bwd(dh, xin, row(ffn_norm_g[l]), sc, dx_out, f"norm_ffn_bwd{l}")
        return dxi, g_in, g_out, [acc_n[0:1], acc_n[1:2], acc_r[0:1]], acc_n[2]

    dx3, g_in1, g_out1, d_ada[("ffn", 1)], dg_ffn1 = ffn_bwd(dx4, x3, 1, sc_f1, gt_f1, ffn1)

    dyb, acc_r = _res_in(dx3, y1, gt_m1, "attn_res_bwd")
    do = _mm_nt([(dyb, w_o)], bf16, "attn_do")
    g_wo = _mm_tn(o, dyb, "attn_dw_o")
    st = _attn_stats(do, o32, lse, "attn_stats")
    leave_b = as_bf([g_in1, jnp.concatenate([g_out1, by_rows(g_wo)], axis=1)])
    dq, dk, dv, dck, r_in1, r_rb = _attn_bwd(qh, k_sh, v_sh, do, st, cum_t, "attn_bwd", ride=leave_b)
    g_wq = _mm_tn(h2, dq, "attn_dw_q")
    dh2 = _mm_nt([(dq, w_q)], f32, "attn_dh")
    dx2, acc_n = _normmod_bwd(dh2, x2, row(mix_norm_g[1]), sc_m1, dx3, "norm_mix_bwd1")
    d_ada[("mix", 1)] = [acc_n[0:1], acc_n[1:2], acc_r[0:1]]
    dg_mix1 = acc_n[2]

    dcum = jnp.pad(dck.reshape(H, T).T, ((0, 0), (0, LANES - H)))
    dfl, acc_f = _forget_bwd(dcum, fl, fb, "forget_bwd")
    g_kvw = jnp.concatenate([_mm_tn(hk, dk, "kv_dw_k"), _mm_tn(hk, dv, "kv_dw_v"), _mm_tn(hk, dfl, "kv_dw_f")[:, :H]], axis=1)
    dhk = _mm_nt([(dk, w_k), (dv, w_v), (dfl, w_f)], f32, "kv_dh")
    dx2, acc_n = _normmod_bwd(dhk, x2, row(kv_norm_g), sc_kv, dx2, "norm_kv_bwd")
    d_ada[("kv", 0)] = [acc_n[0:1], acc_n[1:2]]
    dg_kv = acc_n[2]

    dx1, g_in0, g_out0, d_ada[("ffn", 0)], dg_ffn0 = ffn_bwd(dx2, x1, 0, sc_f0, gt_f0, ffn0)

    dyb, acc_r = _res_in(dx1, y0, gt_m0, "conv_res_bwd")
    dsw = _mm_nt([(dyb, wc_o)], f32, "conv_ds")
    g_co_out = _mm_tn(s0, dyb, "conv_dw_out")
    leave_a = as_bf([g_in0, jnp.concatenate([g_out0, by_rows(g_wq)], axis=1), g_kvw.reshape(D, NDEV, -1).transpose(1, 0, 2)])
    da, dgl, acc_c, dw_dw, r_in0, r_ra, r_kv = _conv_bwd(dsw, q0, p0, a0, gl0, w_dw, ln_g, ln_b, "conv_bwd", ride=leave_a)
    cs = 2 * D // NDEV
    g_ci_out = _mm_tn_shards(h0, dgl, cs, "conv_dw_g", into=_mm_tn_shards(h0, da, cs, "conv_dw_a"))
    dh0 = _mm_nt([(da, wc_a), (dgl, wc_g)], f32, "conv_dh")
    dx0, acc_n = _normmod_bwd(dh0, x0, row(mix_norm_g[0]), sc_m0, dx1, "norm_mix_bwd0")
    d_ada[("mix", 0)] = [acc_n[0:1], acc_n[1:2], acc_r[0:1]]
    dg_mix0 = acc_n[2]

    vec = [t.reshape(-1) for key in [(s[0], s[1]) for s in ADA_SEG] for t in d_ada[key]]
    vec += [dg_mix0, dg_mix1, dg_ffn0, dg_ffn1, dg_kv, acc_fin[0]]
    vec += [acc_f[0], acc_fin[1, :LANES]]
    vec += [acc_c[3], acc_c[4], dw_dw[:KW].reshape(-1), acc_c[2], acc_c[0], acc_c[1], acc_r[1]]
    small_parts = _exchange([_pad_rows(jnp.concatenate(vec), LANES, 8)], True, "gather_partials")[0]
    small_sum = _sum_parts(small_parts, "sum_partials").reshape(-1)
    d_ada_all = small_parts.reshape(NDEV, -1)[:, :ADA_TOT]
    off = 0
    gsm = {}
    ada_b_sum = []
    for _, _, n in ADA_SEG:
        ada_b_sum.append(small_sum[off:off + n]); off += n
    gsm["mix_ada_b"] = jnp.stack(ada_b_sum[0:2])
    gsm["ffn_ada_b"] = jnp.stack(ada_b_sum[2:4])
    gsm["kv_ada_b"] = ada_b_sum[4]
    gsm["mix_norm_g"] = small_sum[off:off + 2 * D].reshape(2, D); off += 2 * D
    gsm["ffn_norm_g"] = small_sum[off:off + 2 * D].reshape(2, D); off += 2 * D
    gsm["kv_norm_g"] = small_sum[off:off + D]; off += D
    gsm["final_norm_g"] = small_sum[off:off + D]; off += D
    gsm["forget_b"] = small_sum[off:off + H]; off += LANES
    loss = small_sum[off]; off += LANES
    sl = lambda full, width: lax.dynamic_slice_in_dim(full, me * width, width, axis=full.ndim - 1)
    gsm["conv_b_in"] = sl(small_sum[off:off + 2 * D].reshape(1, 2 * D), 2 * D // NDEV); off += 2 * D
    gsm["conv_w_dw"] = sl(small_sum[off:off + KW * D].reshape(1, KW, D), cl); off += KW * D
    for n in ("conv_b_dw", "conv_ln_g", "conv_ln_b", "conv_b_out"):
        gsm[n] = sl(small_sum[off:off + D].reshape(1, D), cl); off += D

    dsel, off = [], 0
    for (_, _, n), nl in zip(ADA_SEG, ADA_LOC):
        dsel.append(lax.dynamic_slice_in_dim(d_ada_all[:, off:off + n], me * nl, nl, axis=1)); off += n
    g_ada = _ada_bwd(c_all.T, jnp.concatenate(dsel, axis=1), "ada_bwd")
    res_ada = _adamw(g_ada[None], w_cat, cat_ada(M), cat_ada(V), "adamw_ada")

    r_ci, r_co = _exchange(as_bf([g_ci_out, by_rows(g_co_out)]), False, "scatter_grads")
    res_in0 = _adamw(r_in0, *[pad_in(s, 0) for s in (W, M, V)], "adamw_ffn_in0")
    res_in1 = _adamw(r_in1, *[pad_in(s, 1) for s in (W, M, V)], "adamw_ffn_in1")
    res_ra = _adamw(r_ra, *[rows_a(s) for s in (W, M, V)], "adamw_rows_a")
    res_rb = _adamw(r_rb, *[rows_b(s) for s in (W, M, V)], "adamw_rows_b")
    res_ci = _adamw(r_ci, *[s["conv_w_in"][0] for s in (W, M, V)], "adamw_conv_in")
    res_co = _adamw(r_co, *[s["conv_w_out"][0] for s in (W, M, V)], "adamw_conv_out")
    res_kv = _adamw(r_kv, *[s["kv_w"] for s in (W, M, V)], "adamw_kv")
    rest = [n for n in names if n not in MAIN and n not in ("mix_ada_w", "ffn_ada_w", "kv_ada_w")]
    pack_rest = lambda src: _pad_rows(jnp.concatenate([src[n].reshape(-1) for n in rest]), D, 256)
    res_rest = _adamw(pack_rest(gsm)[None], pack_rest(W), pack_rest(M), pack_rest(V), "adamw_rest")

    outs = []
    for k in range(4):
        ur = _split_flat(res_rest[k].reshape(-1), [(n, W[n].shape) for n in rest])
        ra, a0_, a2_ = res_ada[k], ADA_LOC[0], ADA_LOC[2]
        ur["mix_ada_w"] = jnp.stack([ra[:, 0:a0_], ra[:, a0_:2 * a0_]])
        ur["ffn_ada_w"] = jnp.stack([ra[:, 2 * a0_:2 * a0_ + a2_], ra[:, 2 * a0_ + a2_:2 * a0_ + 2 * a2_]])
        ur["kv_ada_w"] = ra[:, 2 * a0_ + 2 * a2_:]
        ur["ffn_w_in"] = jnp.stack([res_in0[k][:, :FS], res_in1[k][:, :FS]])
        ur["conv_w_in"] = res_ci[k][None]
        ur["conv_w_out"] = res_co[k][None]
        ur["kv_w"] = res_kv[k]
        ur["ffn_w_out"] = jnp.stack([res_ra[k][:fo], res_rb[k][:fo]])
        ur["attn_w_q"] = res_ra[k][fo:][None]
        ur["attn_w_o"] = res_rb[k][fo:][None]
        outs.append(ur)
    grads, deltas, new_m, new_v = outs
    return (loss, dx0[None], *[grads[n] for n in names], *[deltas[n] for n in names],
            *[new_m[n] for n in names], *[new_v[n] for n in names])
```

```python
import functools

import jax
import jax.numpy as jnp
from jax import lax
from jax.experimental import pallas as pl
from jax.experimental.pallas import tpu as pltpu

f32, bf16 = jnp.float32, jnp.bfloat16
SDS = jax.ShapeDtypeStruct

D = 1024
F = 2816
H = 16
HD = 64
NP = H // 2
KW = 31
HALO = 32
NDEV = 8
FS = 2 * F // NDEV
FSP = 768
FP = 4 * FSP
EPS = 1e-6
NEG = -1e30
LANES = 128

ADAM_LR, ADAM_B1, ADAM_B2, ADAM_EPS, ADAM_WD, ADAM_STEP = 0.001, 0.9, 0.999, 1e-08, 0.01, 10

TM = 512
TC = 256
TQ = 1024
VMEM_LIMIT = 56 << 20

MAIN = ("ffn_w_in", "ffn_w_out", "conv_w_in", "conv_w_out", "kv_w", "attn_w_q", "attn_w_o")
ADA_SEG = (("mix", 0, 3 * D), ("mix", 1, 3 * D), ("ffn", 0, 3 * D), ("ffn", 1, 3 * D), ("kv", 0, 2 * D))
ADA_LOC = tuple(n // NDEV for _, _, n in ADA_SEG)
ADA_COLS = sum(ADA_LOC)
ADA_TOT = sum(n for _, _, n in ADA_SEG)


def _cparams(n_axes):
    return pltpu.CompilerParams(dimension_semantics=("arbitrary",) * n_axes, vmem_limit_bytes=VMEM_LIMIT)


def _mesh_pos():
    return lax.axis_index("x"), lax.axis_index("y"), lax.axis_index("c")


def _my_index():
    mx, my, mc = _mesh_pos()
    return 4 * mx + 2 * my + mc


def _peer(k, mx, my, mc):
    px = (1 - mx) if k & 4 else mx
    py = (1 - my) if k & 2 else my
    pc = (1 - mc) if k & 1 else mc
    return (px, py, pc), 4 * px + 2 * py + pc


def _exchange_copies(x_refs, o_refs, sems, gather):
    send_sems, recv_sems, local_sems = sems
    mx, my, mc = _mesh_pos()
    me = 4 * mx + 2 * my + mc
    copies = []
    for a, (x_ref, o_ref) in enumerate(zip(x_refs, o_refs)):
        copies.append(pltpu.make_async_copy(x_ref if gather else x_ref.at[me], o_ref.at[me], local_sems.at[a]))
        for k in range(1, NDEV):
            peer, pidx = _peer(k, mx, my, mc)
            sem = a * (NDEV - 1) + k - 1
            copies.append(pltpu.make_async_remote_copy(
                src_ref=x_ref if gather else x_ref.at[pidx], dst_ref=o_ref.at[me],
                send_sem=send_sems.at[sem], recv_sem=recv_sems.at[sem],
                device_id=peer, device_id_type=pl.DeviceIdType.MESH))
    return copies


def _exchange_shapes(xs, gather):
    n = len(xs)
    hbm = pl.BlockSpec(memory_space=pl.ANY)
    outs = [SDS((NDEV,) + tuple(x.shape if gather else x.shape[1:]), x.dtype) for x in xs]
    sems = [pltpu.SemaphoreType.DMA((n * (NDEV - 1),)), pltpu.SemaphoreType.DMA((n * (NDEV - 1),)), pltpu.SemaphoreType.DMA((n,))]
    return [hbm] * n, [hbm] * n, outs, sems


def _exchange_start(x_refs, o_refs, sems, gather, first):
    @pl.when(first)
    def _():
        for cp in _exchange_copies(x_refs, o_refs, sems, gather):
            cp.start()


def _exchange_wait(x_refs, o_refs, sems, gather, last):
    @pl.when(last)
    def _():
        for cp in _exchange_copies(x_refs, o_refs, sems, gather):
            cp.wait()


def _exchange(xs, gather, name):
    n = len(xs)

    def body(*refs):
        copies = _exchange_copies(refs[:n], refs[n:2 * n], refs[2 * n:], gather)
        for cp in copies:
            cp.start()
        for cp in copies:
            cp.wait()

    in_specs, out_specs, outs, sems = _exchange_shapes(xs, gather)
    return pl.pallas_call(body, out_shape=outs, in_specs=in_specs, out_specs=out_specs, scratch_shapes=sems, name=name)(*xs)


def _row(i):
    return (i, 0)


def _fix(i):
    return (0, 0)


def _normmod(x, g, shift, scale, name):
    T = x.shape[0]
    tm = min(TM, T)

    def body(x_ref, g_ref, sh_ref, sc_ref, h_ref):
        xv = x_ref[...]
        r = lax.rsqrt(jnp.mean(xv * xv, axis=-1, keepdims=True) + EPS)
        hn = (xv * r) * g_ref[...]
        h_ref[...] = (hn * (1.0 + sc_ref[...]) + sh_ref[...]).astype(bf16)

    vec = pl.BlockSpec((1, D), _fix)
    return pl.pallas_call(
        body, grid=(T // tm,), in_specs=[pl.BlockSpec((tm, D), _row), vec, vec, vec],
        out_specs=pl.BlockSpec((tm, D), _row), out_shape=SDS((T, D), bf16),
        compiler_params=_cparams(1), name=name)(x, g, shift, scale)


def _normmod_bwd(dh, x, g, scale, dx_res, name):
    T = x.shape[0]
    tm = min(TM, T)

    def body(dh_ref, x_ref, g_ref, sc_ref, res_ref, dx_ref, acc_ref):
        @pl.when(pl.program_id(0) == 0)
        def _():
            acc_ref[...] = jnp.zeros_like(acc_ref)
        xv = x_ref[...]
        dhv = dh_ref[...]
        gv = g_ref[...]
        r = lax.rsqrt(jnp.mean(xv * xv, axis=-1, keepdims=True) + EPS)
        xn = xv * r
        dhn = dhv * (1.0 + sc_ref[...])
        dxn = dhn * gv
        dx_ref[...] = res_ref[...] + r * (dxn - xn * jnp.mean(dxn * xn, axis=-1, keepdims=True))
        acc_ref[0:1, :] += jnp.sum(dhv, axis=0, keepdims=True)
        acc_ref[1:2, :] += jnp.sum(dhv * (xn * gv), axis=0, keepdims=True)
        acc_ref[2:3, :] += jnp.sum(dhn * xn, axis=0, keepdims=True)

    vec = pl.BlockSpec((1, D), _fix)
    til = pl.BlockSpec((tm, D), _row)
    return pl.pallas_call(
        body, grid=(T // tm,), in_specs=[til, til, vec, vec, til],
        out_specs=[til, pl.BlockSpec((8, D), _fix)], out_shape=[SDS((T, D), f32), SDS((8, D), f32)],
        compiler_params=_cparams(1), name=name)(dh, x, g, scale, dx_res)


def _res_in(dx, y, gate, name):
    T = dx.shape[0]
    tm = min(TM, T)

    def body(dx_ref, y_ref, gt_ref, dy_ref, acc_ref):
        @pl.when(pl.program_id(0) == 0)
        def _():
            acc_ref[...] = jnp.zeros_like(acc_ref)
        dxv = dx_ref[...]
        dy = dxv * gt_ref[...]
        dy_ref[...] = dy.astype(bf16)
        acc_ref[0:1, :] += jnp.sum(dxv * y_ref[...].astype(f32), axis=0, keepdims=True)
        acc_ref[1:2, :] += jnp.sum(dy, axis=0, keepdims=True)

    til = pl.BlockSpec((tm, D), _row)
    return pl.pallas_call(
        body, grid=(T // tm,), in_specs=[til, til, pl.BlockSpec((1, D), _fix)],
        out_specs=[til, pl.BlockSpec((8, D), _fix)], out_shape=[SDS((T, D), bf16), SDS((8, D), f32)],
        compiler_params=_cparams(1), name=name)(dx, y, gate)


def _final_bwd(x, g, tgt, name):
    T = x.shape[0]
    tm = min(TM, T)

    def body(x_ref, g_ref, t_ref, dx_ref, acc_ref):
        @pl.when(pl.program_id(0) == 0)
        def _():
            acc_ref[...] = jnp.zeros_like(acc_ref)
        xv = x_ref[...]
        gv = g_ref[...]
        r = lax.rsqrt(jnp.mean(xv * xv, axis=-1, keepdims=True) + EPS)
        xn = xv * r
        err = xn * gv - t_ref[...]
        dy = err * (1.0 / D)
        dxn = dy * gv
        dx_ref[...] = r * (dxn - xn * jnp.mean(dxn * xn, axis=-1, keepdims=True))
        acc_ref[0:1, :] += jnp.sum(dy * xn, axis=0, keepdims=True)
        acc_ref[1:2, :] += 0.5 * jnp.sum(jnp.mean(err * err, axis=-1, keepdims=True))

    til = pl.BlockSpec((tm, D), _row)
    return pl.pallas_call(
        body, grid=(T // tm,), in_specs=[til, pl.BlockSpec((1, D), _fix), til],
        out_specs=[til, pl.BlockSpec((8, D), _fix)], out_shape=[SDS((T, D), f32), SDS((8, D), f32)],
        compiler_params=_cparams(1), name=name)(x, g, tgt)


def _col_tile(n):
    if n <= 1024:
        return min(n, 512) if n % 512 == 0 else n
    return 1408 if n % 1408 == 0 else 1024


def _mm_gated(h, wa, wb, ba, bb, swiglu, name):
    T, K = h.shape
    N = wa.shape[1]
    tm, tn = min(TM, T), _col_tile(N)

    def body(h_ref, wa_ref, wb_ref, ba_ref, bb_ref, u_ref, w_ref, p_ref):
        hv = h_ref[...]
        u = jnp.dot(hv, wa_ref[...], preferred_element_type=f32) + ba_ref[...]
        w = jnp.dot(hv, wb_ref[...], preferred_element_type=f32) + bb_ref[...]
        u_ref[...] = u
        w_ref[...] = w
        if swiglu:
            p_ref[...] = ((u * jax.nn.sigmoid(u)) * w).astype(p_ref.dtype)
        else:
            p_ref[...] = (u * jax.nn.sigmoid(w)).astype(p_ref.dtype)

    wsp = pl.BlockSpec((K, tn), lambda i, j: (0, j))
    bsp = pl.BlockSpec((1, tn), lambda i, j: (0, j))
    osp = pl.BlockSpec((tm, tn), lambda i, j: (i, j))
    return pl.pallas_call(
        body, grid=(T // tm, N // tn), in_specs=[pl.BlockSpec((tm, K), lambda i, j: (i, 0)), wsp, wsp, bsp, bsp],
        out_specs=[osp, osp, osp], out_shape=[SDS((T, N), f32), SDS((T, N), f32), SDS((T, N), bf16)],
        compiler_params=_cparams(2), name=name)(h, wa, wb, ba, bb)


def _mm_res(a, w, b, x_in, gate, name):
    T, K = a.shape
    N = w.shape[1]
    tm, tn = min(TM, T), _col_tile(N)

    def body(a_ref, w_ref, b_ref, x_ref, gt_ref, xo_ref, y_ref):
        y = jnp.dot(a_ref[...], w_ref[...], preferred_element_type=f32) + b_ref[...]
        y_ref[...] = y.astype(bf16)
        xo_ref[...] = x_ref[...] + gt_ref[...] * y

    vsp = pl.BlockSpec((1, tn), lambda i, j: (0, j))
    osp = pl.BlockSpec((tm, tn), lambda i, j: (i, j))
    return pl.pallas_call(
        body, grid=(T // tm, N // tn),
        in_specs=[pl.BlockSpec((tm, K), lambda i, j: (i, 0)), pl.BlockSpec((K, tn), lambda i, j: (0, j)), vsp, osp, vsp],
        out_specs=[osp, osp], out_shape=[SDS((T, N), f32), SDS((T, N), bf16)],
        compiler_params=_cparams(2), name=name)(a, w, b, x_in, gate)


def _mm(a, w, out_dtype, out_scale, name):
    T, K = a.shape
    N = w.shape[1]
    tm, tn = min(TM, T), _col_tile(N)

    def body(a_ref, w_ref, o_ref):
        y = jnp.dot(a_ref[...], w_ref[...], preferred_element_type=f32)
        if out_scale != 1.0:
            y = y * out_scale
        o_ref[...] = y.astype(out_dtype)

    return pl.pallas_call(
        body, grid=(T // tm, N // tn),
        in_specs=[pl.BlockSpec((tm, K), lambda i, j: (i, 0)), pl.BlockSpec((K, tn), lambda i, j: (0, j))],
        out_specs=pl.BlockSpec((tm, tn), lambda i, j: (i, j)), out_shape=SDS((T, N), out_dtype),
        compiler_params=_cparams(2), name=name)(a, w)


def _dot_nt(a, b):
    return lax.dot_general(a, b, (((1,), (1,)), ((), ())), preferred_element_type=f32)


def _dot_tn(a, b):
    return lax.dot_general(a, b, (((0,), (0,)), ((), ())), preferred_element_type=f32)


def _mm_nt(pairs, out_dtype, name, ride=()):
    T = pairs[0][0].shape[0]
    K = pairs[0][1].shape[0]
    tm, tk = min(TM, T), _col_tile(K)
    n = len(pairs)
    nx = len(ride)
    gi, gj = T // tm, K // tk

    def body(*refs):
        x_refs, o_ref, xo_refs, sems = refs[2 * n:2 * n + nx], refs[2 * n + nx], refs[2 * n + nx + 1:2 * n + 2 * nx + 1], refs[2 * n + 2 * nx + 1:]
        i, j = pl.program_id(0), pl.program_id(1)
        if nx:
            _exchange_start(x_refs, xo_refs, sems, False, (i == 0) & (j == 0))
        acc = None
        for a in range(n):
            part = _dot_nt(refs[2 * a][...].astype(bf16), refs[2 * a + 1][...])
            acc = part if acc is None else acc + part
        o_ref[...] = acc.astype(out_dtype)
        if nx:
            _exchange_wait(x_refs, xo_refs, sems, False, (i == gi - 1) & (j == gj - 1))

    in_specs, args = [], []
    for dy, w in pairs:
        ni = dy.shape[1]
        in_specs += [pl.BlockSpec((tm, ni), lambda i, j: (i, 0)), pl.BlockSpec((tk, ni), lambda i, j: (j, 0))]
        args += [dy, w]
    x_in, x_out, x_shape, x_sems = _exchange_shapes(ride, False) if nx else ([], [], [], [])
    out = pl.pallas_call(
        body, grid=(gi, gj), in_specs=in_specs + x_in,
        out_specs=[pl.BlockSpec((tm, tk), lambda i, j: (i, j))] + x_out, out_shape=[SDS((T, K), out_dtype)] + x_shape,
        scratch_shapes=x_sems, compiler_params=_cparams(2), name=name)(*args, *ride)
    return out if nx else out[0]


def _mm_nt_swiglu(dy, w, ug, uu, name):
    T, N = dy.shape
    K = w.shape[0]
    tm, tk = min(TM, T), _col_tile(K)

    def body(dy_ref, w_ref, ug_ref, uu_ref, dug_ref, duu_ref):
        dact = _dot_nt(dy_ref[...], w_ref[...])
        g = ug_ref[...].astype(f32)
        u = uu_ref[...].astype(f32)
        sg = jax.nn.sigmoid(g)
        duu_ref[...] = (dact * (g * sg)).astype(bf16)
        dug_ref[...] = (dact * u * (sg * (1.0 + g * (1.0 - sg)))).astype(bf16)

    osp = pl.BlockSpec((tm, tk), lambda i, j: (i, j))
    return pl.pallas_call(
        body, grid=(T // tm, K // tk),
        in_specs=[pl.BlockSpec((tm, N), lambda i, j: (i, 0)), pl.BlockSpec((tk, N), lambda i, j: (j, 0)), osp, osp],
        out_specs=[osp, osp], out_shape=[SDS((T, K), bf16), SDS((T, K), bf16)],
        compiler_params=_cparams(2), name=name)(dy, w, ug, uu)


def _mm_tn(a, b, name):
    T, K = a.shape
    N = b.shape[1]
    tt = min(TM, T)
    tk = K if K <= 1024 else _col_tile(K)
    tn = N if N <= 1024 else _col_tile(N)

    def body(a_ref, b_ref, o_ref):
        @pl.when(pl.program_id(2) == 0)
        def _():
            o_ref[...] = jnp.zeros_like(o_ref)
        o_ref[...] += _dot_tn(a_ref[...].astype(bf16), b_ref[...].astype(bf16))

    return pl.pallas_call(
        body, grid=(K // tk, N // tn, T // tt),
        in_specs=[pl.BlockSpec((tt, tk), lambda i, j, t: (t, i)), pl.BlockSpec((tt, tn), lambda i, j, t: (t, j))],
        out_specs=pl.BlockSpec((tk, tn), lambda i, j, t: (i, j)), out_shape=SDS((K, N), f32),
        compiler_params=_cparams(3), name=name)(a, b)


def _mm_tn_shards(a, b, c, name, into=None):
    T, K = a.shape
    half = NDEV // 2
    assert b.shape[1] == half * c
    tt = min(TM, T)
    nt = T // tt
    first = into is None

    def body(a_ref, b_ref, *rest):
        o_ref, acc = rest[-2], rest[-1]
        t = pl.program_id(1)

        @pl.when(t == 0)
        def _():
            acc[...] = jnp.zeros_like(acc)
        acc[...] += _dot_tn(a_ref[...].astype(bf16), b_ref[...].astype(bf16))

        @pl.when(t == nt - 1)
        def _():
            o_ref[...] = acc[...].astype(bf16)

    in_specs = [pl.BlockSpec((tt, K), lambda j, t: (t, 0)), pl.BlockSpec((tt, c), lambda j, t: (t, j))]
    args = [a, b]
    if not first:
        in_specs.append(pl.BlockSpec(memory_space=pl.ANY))
        args.append(into)
    base = 0 if first else half
    return pl.pallas_call(
        body, grid=(half, nt), in_specs=in_specs,
        out_specs=pl.BlockSpec((None, K, c), lambda j, t: (j + base, 0, 0)), out_shape=SDS((NDEV, K, c), bf16),
        scratch_shapes=[pltpu.VMEM((K, c), f32)], input_output_aliases={} if first else {2: 0},
        compiler_params=_cparams(2), name=name)(*args)


def _ffn_in(h, w_sh, name, ride=()):
    T = h.shape[0]
    tm = min(TM, T)
    nx = len(ride)
    gi, gj = T // tm, NDEV // 2

    def body(h_ref, wg_ref, wu_ref, *rest):
        x_refs, (ug_ref, uu_ref, act_ref), xo_refs, sems = rest[:nx], rest[nx:nx + 3], rest[nx + 3:2 * nx + 3], rest[2 * nx + 3:]
        i, j = pl.program_id(0), pl.program_id(1)
        if nx:
            _exchange_start(x_refs, xo_refs, sems, True, (i == 0) & (j == 0))
        hv = h_ref[...]
        ug = jnp.dot(hv, wg_ref[...], preferred_element_type=f32)
        uu = jnp.dot(hv, wu_ref[...], preferred_element_type=f32)
        ug_ref[...] = ug
        uu_ref[...] = uu
        act_ref[...] = ((ug * jax.nn.sigmoid(ug)) * uu).astype(bf16)
        if nx:
            _exchange_wait(x_refs, xo_refs, sems, True, (i == gi - 1) & (j == gj - 1))

    osp = pl.BlockSpec((tm, FSP), lambda i, j: (i, j))
    x_in, x_out, x_shape, x_sems = _exchange_shapes(ride, True) if nx else ([], [], [], [])
    return pl.pallas_call(
        body, grid=(gi, gj),
        in_specs=[pl.BlockSpec((tm, D), lambda i, j: (i, 0)),
                  pl.BlockSpec((None, D, FSP), lambda i, j: (j, 0, 0)),
                  pl.BlockSpec((None, D, FSP), lambda i, j: (j + NDEV // 2, 0, 0))] + x_in,
        out_specs=[osp, osp, osp] + x_out,
        out_shape=[SDS((T, FP), f32), SDS((T, FP), f32), SDS((T, FP), bf16)] + x_shape,
        scratch_shapes=x_sems, compiler_params=_cparams(2), name=name)(h, w_sh, w_sh, *ride)


def _ffn_dh(dug, duu, w_sh, name):
    T = dug.shape[0]
    tm, tk = min(TM, T), 512
    half = NDEV // 2

    def body(dg_ref, du_ref, wg_ref, wu_ref, o_ref):
        acc = None
        for s in range(half):
            cols = slice(s * FSP, (s + 1) * FSP)
            part = _dot_nt(dg_ref[:, cols], wg_ref[s]) + _dot_nt(du_ref[:, cols], wu_ref[s])
            acc = part if acc is None else acc + part
        o_ref[...] = acc

    dsp = pl.BlockSpec((tm, FP), lambda i, j: (i, 0))
    return pl.pallas_call(
        body, grid=(T // tm, D // tk),
        in_specs=[dsp, dsp, pl.BlockSpec((half, tk, FSP), lambda i, j: (0, j, 0)),
                  pl.BlockSpec((half, tk, FSP), lambda i, j: (1, j, 0))],
        out_specs=pl.BlockSpec((tm, tk), lambda i, j: (i, j)), out_shape=SDS((T, D), f32),
        compiler_params=_cparams(2), name=name)(dug, duu, w_sh, w_sh)


def _layernorm_parts(qv, g, b):
    mu = jnp.mean(qv, axis=-1, keepdims=True)
    cen = qv - mu
    rstd = lax.rsqrt(jnp.mean(cen * cen, axis=-1, keepdims=True) + EPS)
    z = cen * rstd
    return z, rstd, z * g + b


def _tap_groups(offsets):
    groups = {}
    for k, o in enumerate(offsets):
        groups.setdefault(o % 8, []).append((k, o - o % 8))
    return groups


def _conv_fwd(p, w_dw, b_dw, ln_g, ln_b, name, ride=()):
    T = p.shape[0]
    tc = min(TC, T)
    n = T // tc
    nx = len(ride)

    def body(p_ref, w_ref, b_ref, g_ref, bb_ref, *rest):
        x_refs, (q_ref, s_ref), xo_refs = rest[:nx], rest[nx:nx + 2], rest[nx + 2:2 * nx + 2]
        (ext, sh), sems = rest[2 * nx + 2:2 * nx + 4], rest[2 * nx + 4:]
        i = pl.program_id(0)
        if nx:
            _exchange_start(x_refs, xo_refs, sems, True, i == 0)

        @pl.when(i == 0)
        def _():
            ext[0:HALO, :] = jnp.zeros((HALO, D), f32)

        @pl.when(i > 0)
        def _():
            ext[0:HALO, :] = ext[tc:tc + HALO, :]

        ext[HALO:HALO + tc, :] = p_ref[...].astype(f32)
        groups = _tap_groups([HALO - (KW - 1) + k for k in range(KW)])
        for cb in range(D // LANES):
            cols = slice(cb * LANES, (cb + 1) * LANES)
            acc = jnp.zeros((tc, LANES), f32)
            for r, taps in groups.items():
                span = max(base for _, base in taps) + tc
                sh[0:span, :] = ext[r:r + span, cols]
                for k, base in taps:
                    acc = acc + w_ref[k:k + 1, cols] * sh[base:base + tc, :]
            q_ref[:, cols] = acc + b_ref[:, cols]
        _, _, l = _layernorm_parts(q_ref[...], g_ref[...], bb_ref[...])
        s_ref[...] = (l * jax.nn.sigmoid(l)).astype(bf16)
        if nx:
            _exchange_wait(x_refs, xo_refs, sems, True, i == n - 1)

    vec = pl.BlockSpec((1, D), _fix)
    til = pl.BlockSpec((tc, D), _row)
    x_in, x_out, x_shape, x_sems = _exchange_shapes(ride, True) if nx else ([], [], [], [])
    return pl.pallas_call(
        body, grid=(n,), in_specs=[til, pl.BlockSpec((HALO, D), _fix), vec, vec, vec] + x_in,
        out_specs=[til, til] + x_out, out_shape=[SDS((T, D), f32), SDS((T, D), bf16)] + x_shape,
        scratch_shapes=[pltpu.VMEM((tc + HALO, D), f32), pltpu.VMEM((tc + HALO, LANES), f32)] + x_sems, compiler_params=_cparams(1),
        name=name)(p, w_dw, b_dw, ln_g, ln_b, *ride)


def _conv_bwd(ds, q, p, a, gl, w_dw, ln_g, ln_b, name, ride=()):
    T = q.shape[0]
    tc = min(TC, T)
    n = T // tc
    nx = len(ride)

    def body(ds_ref, q_ref, p_ref, a_ref, gl_ref, w_ref, g_ref, bb_ref, *rest):
        x_refs, (da_ref, dgl_ref, acc_ref, dw_ref), xo_refs = rest[:nx], rest[nx:nx + 4], rest[nx + 4:2 * nx + 4]
        (ext, sh), sems = rest[2 * nx + 4:2 * nx + 6], rest[2 * nx + 6:]
        i = pl.program_id(0)
        if nx:
            _exchange_start(x_refs, xo_refs, sems, False, i == 0)

        @pl.when(i == 0)
        def _():
            acc_ref[...] = jnp.zeros_like(acc_ref)
            dw_ref[...] = jnp.zeros_like(dw_ref)
            ext[tc:tc + HALO, :] = jnp.zeros((HALO, D), f32)

        @pl.when(i > 0)
        def _():
            ext[tc:tc + HALO, :] = ext[0:HALO, :]

        gv = g_ref[...]
        z, rstd, l = _layernorm_parts(q_ref[...], gv, bb_ref[...])
        sg = jax.nn.sigmoid(l)
        dl = ds_ref[...] * (sg * (1.0 + l * (1.0 - sg)))
        dz = dl * gv
        dq = rstd * (dz - jnp.mean(dz, axis=-1, keepdims=True) - z * jnp.mean(dz * z, axis=-1, keepdims=True))
        ext[0:tc, :] = dq.astype(bf16).astype(f32)
        acc_ref[0:1, :] += jnp.sum(dl * z, axis=0, keepdims=True)
        acc_ref[1:2, :] += jnp.sum(dl, axis=0, keepdims=True)
        acc_ref[2:3, :] += jnp.sum(dq, axis=0, keepdims=True)
        groups = _tap_groups([KW - 1 - k for k in range(KW)])
        for cb in range(D // LANES):
            cols = slice(cb * LANES, (cb + 1) * LANES)
            pc = p_ref[:, cols].astype(f32)
            dp = jnp.zeros((tc, LANES), f32)
            for r, taps in groups.items():
                span = max(base for _, base in taps) + tc
                sh[0:span, :] = ext[r:r + span, cols]
                for k, base in taps:
                    sl = sh[base:base + tc, :]
                    dp = dp + w_ref[k:k + 1, cols] * sl
                    dw_ref[k:k + 1, cols] += jnp.sum(sl * pc, axis=0, keepdims=True)
            av = a_ref[:, cols].astype(f32)
            sgl = jax.nn.sigmoid(gl_ref[:, cols].astype(f32))
            da = dp * sgl
            dgl = dp * av * (sgl * (1.0 - sgl))
            da_ref[:, cols] = da.astype(bf16)
            dgl_ref[:, cols] = dgl.astype(bf16)
            acc_ref[3:4, cols] += jnp.sum(da, axis=0, keepdims=True)
            acc_ref[4:5, cols] += jnp.sum(dgl, axis=0, keepdims=True)
        if nx:
            _exchange_wait(x_refs, xo_refs, sems, False, i == n - 1)

    rev = lambda i: (n - 1 - i, 0)
    til = pl.BlockSpec((tc, D), rev)
    vec = pl.BlockSpec((1, D), _fix)
    x_in, x_out, x_shape, x_sems = _exchange_shapes(ride, False) if nx else ([], [], [], [])
    return pl.pallas_call(
        body, grid=(n,), in_specs=[til, til, til, til, til, pl.BlockSpec((HALO, D), _fix), vec, vec] + x_in,
        out_specs=[til, til, pl.BlockSpec((8, D), _fix), pl.BlockSpec((HALO, D), _fix)] + x_out,
        out_shape=[SDS((T, D), bf16), SDS((T, D), bf16), SDS((8, D), f32), SDS((HALO, D), f32)] + x_shape,
        scratch_shapes=[pltpu.VMEM((tc + HALO, D), f32), pltpu.VMEM((tc + HALO, LANES), f32)] + x_sems, compiler_params=_cparams(1),
        name=name)(ds, q, p, a, gl, w_dw, ln_g, ln_b, *ride)


def _tri(n, upper):
    r = lax.broadcasted_iota(jnp.int32, (n, n), 0)
    c = lax.broadcasted_iota(jnp.int32, (n, n), 1)
    return ((c >= r) if upper else (r >= c)).astype(f32)


def _forget_fwd(fl, fb, name):
    T = fl.shape[0]
    tc = min(TC, T)

    def body(fl_ref, fb_ref, cum_ref, carry):
        @pl.when(pl.program_id(0) == 0)
        def _():
            carry[...] = jnp.zeros_like(carry)
        xv = fl_ref[...] + fb_ref[...]
        lf = jnp.minimum(xv, 0.0) - jnp.log(1.0 + jnp.exp(-jnp.abs(xv)))
        cs = jnp.dot(_tri(tc, False), lf, preferred_element_type=f32, precision=lax.Precision.HIGHEST) + carry[0:1, :]
        cum_ref[...] = cs
        carry[0:1, :] = cs[tc - 1:tc, :]

    til = pl.BlockSpec((tc, LANES), _row)
    return pl.pallas_call(
        body, grid=(T // tc,), in_specs=[til, pl.BlockSpec((1, LANES), _fix)], out_specs=til,
        out_shape=SDS((T, LANES), f32), scratch_shapes=[pltpu.VMEM((8, LANES), f32)],
        compiler_params=_cparams(1), name=name)(fl, fb)


def _forget_bwd(dcum, fl, fb, name):
    T = fl.shape[0]
    tc = min(TC, T)
    n = T // tc

    def body(dc_ref, fl_ref, fb_ref, dfl_ref, acc_ref, carry):
        @pl.when(pl.program_id(0) == 0)
        def _():
            carry[...] = jnp.zeros_like(carry)
            acc_ref[...] = jnp.zeros_like(acc_ref)
        dlf = jnp.dot(_tri(tc, True), dc_ref[...], preferred_element_type=f32, precision=lax.Precision.HIGHEST) + carry[0:1, :]
        carry[0:1, :] = dlf[0:1, :]
        dfl = dlf * (1.0 - jax.nn.sigmoid(fl_ref[...] + fb_ref[...]))
        dfl_ref[...] = dfl.astype(bf16)
        acc_ref[0:1, :] += jnp.sum(dfl, axis=0, keepdims=True)

    rev = lambda i: (n - 1 - i, 0)
    til = pl.BlockSpec((tc, LANES), rev)
    return pl.pallas_call(
        body, grid=(n,), in_specs=[til, til, pl.BlockSpec((1, LANES), _fix)],
        out_specs=[til, pl.BlockSpec((8, LANES), _fix)], out_shape=[SDS((T, LANES), bf16), SDS((8, LANES), f32)],
        scratch_shapes=[pltpu.VMEM((8, LANES), f32)], compiler_params=_cparams(1), name=name)(dcum, fl, fb)


def _causal(s, n):
    r = lax.broadcasted_iota(jnp.int32, (n, n), 0)
    c = lax.broadcasted_iota(jnp.int32, (n, n), 1)
    return jnp.where(c <= r, s, NEG)


def _attn_fwd(q, k, v, cum_t, name, ride=()):
    T = q.shape[0]
    tq = min(TQ, T)
    n = T // tq
    nx = len(ride)

    def body(q_ref, k_ref, v_ref, ck_ref, *rest):
        x_refs, (o_ref, o32_ref, st_ref), xo_refs = rest[:nx], rest[nx:nx + 3], rest[nx + 3:2 * nx + 3]
        (m_sc, l_sc, acc_sc, res_sc), sems = rest[2 * nx + 3:2 * nx + 7], rest[2 * nx + 7:]
        i = pl.program_id(1)
        if nx:
            _exchange_start(x_refs, xo_refs, sems, True, (pl.program_id(0) == 0) & (i == 0))
        lane = lax.broadcasted_iota(jnp.int32, (1, LANES), 1)
        lo = lane < HD
        q2 = q_ref[...]
        zero = jnp.zeros_like(q2)
        qa = (jnp.where(lo, q2, zero), jnp.where(lo, zero, q2))
        m_sc[...] = jnp.full(m_sc.shape, NEG, f32)
        l_sc[...] = jnp.zeros_like(l_sc)
        acc_sc[...] = jnp.zeros_like(acc_sc)
        res_sc[...] = jnp.zeros_like(res_sc)

        def block(j, masked):
            off = pl.multiple_of(j * tq, tq)
            k2 = k_ref[pl.ds(off, tq), :]
            v2 = v_ref[pl.ds(off, tq), :]
            for a in range(2):
                s = _dot_nt(qa[a], k2) - ck_ref[0, a:a + 1, pl.ds(off, tq)]
                if masked:
                    s = _causal(s, tq)
                m_old = m_sc[a]
                m_new = jnp.maximum(m_old, jnp.max(s, axis=1, keepdims=True))
                alpha = jnp.exp(m_old - m_new)
                pm = jnp.exp(s - m_new)
                pb = pm.astype(bf16)
                pr = (pm - pb.astype(f32)).astype(bf16)
                l_sc[a] = alpha * l_sc[a] + jnp.sum(pm, axis=1, keepdims=True)
                acc_sc[a] = alpha * acc_sc[a] + jnp.dot(pb, v2, preferred_element_type=f32)
                res_sc[a] = alpha * res_sc[a] + jnp.dot(pr, v2, preferred_element_type=f32)
                m_sc[a] = m_new

        def step(j, carry):
            block(j, False)
            return carry

        lax.fori_loop(0, i, step, 0)
        block(i, True)
        o_ref[...] = jnp.where(lo, acc_sc[0] / l_sc[0], acc_sc[1] / l_sc[1]).astype(bf16)
        o32_ref[...] = jnp.where(lo, (acc_sc[0] + res_sc[0]) / l_sc[0], (acc_sc[1] + res_sc[1]) / l_sc[1])
        lse0 = m_sc[0] + jnp.log(l_sc[0])
        lse1 = m_sc[1] + jnp.log(l_sc[1])
        st_ref[0] = jnp.where(lane == 0, lse0, jnp.where(lane == 1, lse1, 0.0))
        if nx:
            _exchange_wait(x_refs, xo_refs, sems, True, (pl.program_id(0) == NP - 1) & (i == n - 1))

    full = lambda blk: pl.BlockSpec((T, LANES), blk)
    x_in, x_out, x_shape, x_sems = _exchange_shapes(ride, True) if nx else ([], [], [], [])
    return pl.pallas_call(
        body, grid=(NP, n),
        in_specs=[pl.BlockSpec((tq, LANES), lambda p, i: (i, p)), full(lambda p, i: (0, p)), full(lambda p, i: (0, p)),
                  pl.BlockSpec((1, 2, T), lambda p, i: (p, 0, 0))] + x_in,
        out_specs=[pl.BlockSpec((tq, LANES), lambda p, i: (i, p)), pl.BlockSpec((tq, LANES), lambda p, i: (i, p)),
                   pl.BlockSpec((1, tq, LANES), lambda p, i: (p, i, 0))] + x_out,
        out_shape=[SDS((T, H * HD), bf16), SDS((T, H * HD), f32), SDS((NP, T, LANES), f32)] + x_shape,
        scratch_shapes=[pltpu.VMEM((2, tq, 1), f32), pltpu.VMEM((2, tq, 1), f32), pltpu.VMEM((2, tq, LANES), f32),
                        pltpu.VMEM((2, tq, LANES), f32)] + x_sems,
        compiler_params=_cparams(2), name=name)(q, k, v, cum_t, *ride)


def _attn_stats(do, o, lse, name):
    T = do.shape[0]
    tm = min(TM, T)

    def body(do_ref, o_ref, lse_ref, st_ref):
        lane = lax.broadcasted_iota(jnp.int32, (1, LANES), 1)
        prod = do_ref[...].astype(f32) * o_ref[...].astype(f32)
        d0 = jnp.sum(jnp.where(lane < HD, prod, 0.0), axis=1, keepdims=True)
        d1 = jnp.sum(jnp.where(lane < HD, 0.0, prod), axis=1, keepdims=True)
        st_ref[0] = jnp.where(lane < 2, lse_ref[0], jnp.where(lane == 2, d0, jnp.where(lane == 3, d1, 0.0)))

    til = pl.BlockSpec((tm, LANES), lambda p, i: (i, p))
    stt = pl.BlockSpec((1, tm, LANES), lambda p, i: (p, i, 0))
    return pl.pallas_call(
        body, grid=(NP, T // tm), in_specs=[til, til, stt], out_specs=stt, out_shape=SDS((NP, T, LANES), f32),
        compiler_params=_cparams(2), name=name)(do, o, lse)


def _attn_bwd(q, k, v, do, st, cum_t, name, ride=()):
    T = q.shape[0]
    tq = min(TQ, T)
    n = T // tq
    nx = len(ride)

    def body(q_ref, k_ref, v_ref, do_ref, st_ref, ck_ref, *rest):
        x_refs, (dq_ref, dk_ref, dv_ref, dck_ref), xo_refs = rest[:nx], rest[nx:nx + 4], rest[nx + 4:2 * nx + 4]
        (dk_sc, dv_sc, dck_sc), sems = rest[2 * nx + 4:2 * nx + 7], rest[2 * nx + 7:]
        j = pl.program_id(1)
        if nx:
            _exchange_start(x_refs, xo_refs, sems, False, (pl.program_id(0) == 0) & (j == 0))
        lane = lax.broadcasted_iota(jnp.int32, (1, LANES), 1)
        lo = lane < HD

        @pl.when(j == 0)
        def _():
            dq_ref[...] = jnp.zeros_like(dq_ref)

        k2 = k_ref[...]
        v2 = v_ref[...]
        zero = jnp.zeros_like(k2)
        ka = (jnp.where(lo, k2, zero), jnp.where(lo, zero, k2))
        va = (jnp.where(lo, v2, zero), jnp.where(lo, zero, v2))
        dk_sc[...] = jnp.zeros_like(dk_sc)
        dv_sc[...] = jnp.zeros_like(dv_sc)
        dck_sc[...] = jnp.zeros_like(dck_sc)

        def block(i, masked):
            off = pl.multiple_of(i * tq, tq)
            q2 = q_ref[pl.ds(off, tq), :]
            do2 = do_ref[pl.ds(off, tq), :]
            stt = st_ref[0, pl.ds(off, tq), :]
            parts = []
            for a in range(2):
                s = _dot_nt(q2, ka[a]) - ck_ref[0, a:a + 1, :]
                if masked:
                    s = _causal(s, tq)
                pm = jnp.exp(s - stt[:, a:a + 1])
                dp = _dot_nt(do2, va[a])
                dsm = pm * (dp - stt[:, 2 + a:3 + a])
                dsb = dsm.astype(bf16)
                dv_sc[a] += _dot_tn(pm.astype(bf16), do2)
                dk_sc[a] += _dot_tn(dsb, q2)
                dck_sc[a:a + 1, :] -= jnp.sum(dsm, axis=0, keepdims=True)
                parts.append(jnp.dot(dsb, k2, preferred_element_type=f32))
            dq_ref[pl.ds(off, tq), :] += jnp.where(lo, parts[0], parts[1])

        block(j, True)

        def step(i, carry):
            block(i, False)
            return carry

        lax.fori_loop(j + 1, n, step, 0)
        dk_ref[...] = jnp.where(lo, dk_sc[0], dk_sc[1]).astype(bf16)
        dv_ref[...] = jnp.where(lo, dv_sc[0], dv_sc[1]).astype(bf16)
        dck_ref[0] = dck_sc[0:2, :]

        @pl.when(j == n - 1)
        def _():
            dq_ref[...] = dq_ref[...] * (HD ** -0.5)

        if nx:
            _exchange_wait(x_refs, xo_refs, sems, False, (pl.program_id(0) == NP - 1) & (j == n - 1))

    full = lambda: pl.BlockSpec((T, LANES), lambda p, j: (0, p))
    kvb = lambda: pl.BlockSpec((tq, LANES), lambda p, j: (j, p))
    ckb = lambda: pl.BlockSpec((1, 2, tq), lambda p, j: (p, 0, j))
    x_in, x_out, x_shape, x_sems = _exchange_shapes(ride, False) if nx else ([], [], [], [])
    return pl.pallas_call(
        body, grid=(NP, n),
        in_specs=[full(), kvb(), kvb(), full(), pl.BlockSpec((1, T, LANES), lambda p, j: (p, 0, 0)), ckb()] + x_in,
        out_specs=[full(), kvb(), kvb(), ckb()] + x_out,
        out_shape=[SDS((T, H * HD), f32), SDS((T, H * HD), bf16), SDS((T, H * HD), bf16), SDS((NP, 2, T), f32)] + x_shape,
        scratch_shapes=[pltpu.VMEM((2, tq, LANES), f32), pltpu.VMEM((2, tq, LANES), f32), pltpu.VMEM((8, tq), f32)] + x_sems,
        compiler_params=_cparams(2), name=name)(q, k, v, do, st, cum_t, *ride)


def _ada_fwd(c_all, w_cat, name):
    n = w_cat.shape[1]
    tn = 256

    def body(c_ref, w_ref, o_ref):
        cv = c_ref[...]
        o_ref[...] = jnp.dot((cv * jax.nn.sigmoid(cv)).astype(bf16), w_ref[...].astype(bf16), preferred_element_type=f32)

    return pl.pallas_call(
        body, grid=(n // tn,), in_specs=[pl.BlockSpec((NDEV, D), _fix), pl.BlockSpec((D, tn), lambda i: (0, i))],
        out_specs=pl.BlockSpec((NDEV, tn), lambda i: (0, i)), out_shape=SDS((NDEV, n), f32),
        compiler_params=_cparams(1), name=name)(c_all, w_cat)


def _ada_bwd(c_all_t, dsel, name):
    n = dsel.shape[1]
    tn = 256

    def body(c_ref, d_ref, o_ref):
        cv = c_ref[...]
        ca = cv * jax.nn.sigmoid(cv)
        acc = ca[:, 0:1] * d_ref[0:1, :]
        for b in range(1, NDEV):
            acc = acc + ca[:, b:b + 1] * d_ref[b:b + 1, :]
        o_ref[...] = acc

    return pl.pallas_call(
        body, grid=(n // tn,), in_specs=[pl.BlockSpec((D, NDEV), _fix), pl.BlockSpec((NDEV, tn), lambda i: (0, i))],
        out_specs=pl.BlockSpec((D, tn), lambda i: (0, i)), out_shape=SDS((D, n), f32),
        compiler_params=_cparams(1), name=name)(c_all_t, dsel)


def _sum_parts(parts, name):
    R = parts.shape[1]

    def body(p_ref, o_ref):
        acc = p_ref[0]
        for j in range(1, NDEV):
            acc = acc + p_ref[j]
        o_ref[...] = acc

    return pl.pallas_call(body, out_shape=SDS((R, LANES), f32), name=name)(parts)


def _adamw(g_parts, w, m, v, name):
    n_parts, R, C = g_parts.shape
    tr = next(t for t in (256, 128, 64, 32, 16, 8) if R % t == 0)
    c1 = 1.0 / (1.0 - ADAM_B1 ** ADAM_STEP)
    c2 = 1.0 / (1.0 - ADAM_B2 ** ADAM_STEP)

    def body(g_ref, w_ref, m_ref, v_ref, go_ref, d_ref, mo_ref, vo_ref):
        g = g_ref[0].astype(f32)
        for j in range(1, n_parts):
            g = g + g_ref[j].astype(f32)
        mn = ADAM_B1 * m_ref[...] + (1.0 - ADAM_B1) * g
        vn = ADAM_B2 * v_ref[...] + (1.0 - ADAM_B2) * (g * g)
        go_ref[...] = g
        mo_ref[...] = mn
        vo_ref[...] = vn
        d_ref[...] = -ADAM_LR * ((mn * c1) / (jnp.sqrt(vn * c2) + ADAM_EPS) + ADAM_WD * w_ref[...])

    til = pl.BlockSpec((tr, C), _row)
    out = SDS((R, C), f32)
    return pl.pallas_call(
        body, grid=(R // tr,), in_specs=[pl.BlockSpec((n_parts, tr, C), lambda i: (0, i, 0)), til, til, til],
        out_specs=[til, til, til, til], out_shape=[out, out, out, out],
        compiler_params=_cparams(1), name=name)(g_parts, w, m, v)


def _pad_rows(flat, cols, mult):
    n = flat.shape[-1]
    rows = -(-n // cols)
    rows = -(-rows // mult) * mult
    pad = [(0, 0)] * (flat.ndim - 1) + [(0, rows * cols - n)]
    return jnp.pad(flat, pad).reshape(flat.shape[:-1] + (rows, cols))


def _split_flat(flat, shapes):
    out, off = {}, 0
    for name, shp in shapes:
        n = 1
        for d in shp:
            n *= d
        out[name] = flat[off:off + n].reshape(shp)
        off += n
    return out


def kernel(x, c, mix_norm_g, mix_ada_w, mix_ada_b, ffn_norm_g, ffn_ada_w, ffn_ada_b, ffn_w_in, ffn_w_out, conv_w_in, conv_b_in, conv_w_dw, conv_b_dw, conv_ln_g, conv_ln_b, conv_w_out, conv_b_out, kv_norm_g, kv_ada_w, kv_ada_b, kv_w, forget_b, attn_w_q, attn_w_o, final_norm_g, loss_target, m_mix_norm_g, m_mix_ada_w, m_mix_ada_b, m_ffn_norm_g, m_ffn_ada_w, m_ffn_ada_b, m_ffn_w_in, m_ffn_w_out, m_conv_w_in, m_conv_b_in, m_conv_w_dw, m_conv_b_dw, m_conv_ln_g, m_conv_ln_b, m_conv_w_out, m_conv_b_out, m_kv_norm_g, m_kv_ada_w, m_kv_ada_b, m_kv_w, m_forget_b, m_attn_w_q, m_attn_w_o, m_final_norm_g, v_mix_norm_g, v_mix_ada_w, v_mix_ada_b, v_ffn_norm_g, v_ffn_ada_w, v_ffn_ada_b, v_ffn_w_in, v_ffn_w_out, v_conv_w_in, v_conv_b_in, v_conv_w_dw, v_conv_b_dw, v_conv_ln_g, v_conv_ln_b, v_conv_w_out, v_conv_b_out, v_kv_norm_g, v_kv_ada_w, v_kv_ada_b, v_kv_w, v_forget_b, v_attn_w_q, v_attn_w_o, v_final_norm_g):
    W = dict(mix_norm_g=mix_norm_g, mix_ada_w=mix_ada_w, mix_ada_b=mix_ada_b, ffn_norm_g=ffn_norm_g, ffn_ada_w=ffn_ada_w, ffn_ada_b=ffn_ada_b, ffn_w_in=ffn_w_in, ffn_w_out=ffn_w_out, conv_w_in=conv_w_in, conv_b_in=conv_b_in, conv_w_dw=conv_w_dw, conv_b_dw=conv_b_dw, conv_ln_g=conv_ln_g, conv_ln_b=conv_ln_b, conv_w_out=conv_w_out, conv_b_out=conv_b_out, kv_norm_g=kv_norm_g, kv_ada_w=kv_ada_w, kv_ada_b=kv_ada_b, kv_w=kv_w, forget_b=forget_b, attn_w_q=attn_w_q, attn_w_o=attn_w_o, final_norm_g=final_norm_g)
    M = dict(mix_norm_g=m_mix_norm_g, mix_ada_w=m_mix_ada_w, mix_ada_b=m_mix_ada_b, ffn_norm_g=m_ffn_norm_g, ffn_ada_w=m_ffn_ada_w, ffn_ada_b=m_ffn_ada_b, ffn_w_in=m_ffn_w_in, ffn_w_out=m_ffn_w_out, conv_w_in=m_conv_w_in, conv_b_in=m_conv_b_in, conv_w_dw=m_conv_w_dw, conv_b_dw=m_conv_b_dw, conv_ln_g=m_conv_ln_g, conv_ln_b=m_conv_ln_b, conv_w_out=m_conv_w_out, conv_b_out=m_conv_b_out, kv_norm_g=m_kv_norm_g, kv_ada_w=m_kv_ada_w, kv_ada_b=m_kv_ada_b, kv_w=m_kv_w, forget_b=m_forget_b, attn_w_q=m_attn_w_q, attn_w_o=m_attn_w_o, final_norm_g=m_final_norm_g)
    V = dict(mix_norm_g=v_mix_norm_g, mix_ada_w=v_mix_ada_w, mix_ada_b=v_mix_ada_b, ffn_norm_g=v_ffn_norm_g, ffn_ada_w=v_ffn_ada_w, ffn_ada_b=v_ffn_ada_b, ffn_w_in=v_ffn_w_in, ffn_w_out=v_ffn_w_out, conv_w_in=v_conv_w_in, conv_b_in=v_conv_b_in, conv_w_dw=v_conv_w_dw, conv_b_dw=v_conv_b_dw, conv_ln_g=v_conv_ln_g, conv_ln_b=v_conv_ln_b, conv_w_out=v_conv_w_out, conv_b_out=v_conv_b_out, kv_norm_g=v_kv_norm_g, kv_ada_w=v_kv_ada_w, kv_ada_b=v_kv_ada_b, kv_w=v_kv_w, forget_b=v_forget_b, attn_w_q=v_attn_w_q, attn_w_o=v_attn_w_o, final_norm_g=v_final_norm_g)
    names = list(W)
    T = x.shape[1]
    me = _my_index()
    x0 = x[0]
    tgt = loss_target[0]
    row = lambda vct: vct.reshape(1, -1)

    small_names = ("conv_b_in", "conv_w_dw", "conv_b_dw", "conv_ln_g", "conv_ln_b", "conv_b_out")
    small_loc = jnp.concatenate([c.reshape(-1)] + [W[n].reshape(-1) for n in small_names])
    sg = _exchange([_pad_rows(small_loc, LANES, 8)], True, "gather_small")[0].reshape(NDEV, -1)
    c_all = sg[:, :D]
    off = D
    b_in = sg[:, off:off + 2 * D // NDEV].reshape(1, 2 * D); off += 2 * D // NDEV
    cl = D // NDEV
    w_dw = sg[:, off:off + KW * cl].reshape(NDEV, KW, cl).transpose(1, 0, 2).reshape(KW, D); off += KW * cl
    w_dw = jnp.pad(w_dw, ((0, HALO - KW), (0, 0))).astype(bf16).astype(f32)
    b_dw = sg[:, off:off + cl].reshape(1, D); off += cl
    ln_g = sg[:, off:off + cl].reshape(1, D); off += cl
    ln_b = sg[:, off:off + cl].reshape(1, D); off += cl
    b_out = sg[:, off:off + cl].reshape(1, D)

    cat_ada = lambda s: jnp.concatenate([s["mix_ada_w"][0], s["mix_ada_w"][1], s["ffn_ada_w"][0], s["ffn_ada_w"][1], s["kv_ada_w"]], axis=1)
    w_cat = cat_ada(W)
    ada_loc = _ada_fwd(c_all, w_cat, "ada_fwd")
    ada_all = _exchange([ada_loc], True, "gather_ada")[0]
    ada_me = lax.dynamic_index_in_dim(ada_all, me, axis=1, keepdims=False)
    ada_bias = (mix_ada_b[0], mix_ada_b[1], ffn_ada_b[0], ffn_ada_b[1], kv_ada_b)
    ada, off = [], 0
    for nl, bias in zip(ADA_LOC, ada_bias):
        full = ada_me[:, off:off + nl].reshape(-1) + bias
        ada.append([row(t) for t in jnp.split(full, full.shape[0] // D)])
        off += nl
    (sh_m0, sc_m0, gt_m0), (sh_m1, sc_m1, gt_m1), (sh_f0, sc_f0, gt_f0), (sh_f1, sc_f1, gt_f1), (sh_kv, sc_kv) = ada

    pad_in = lambda src, l: jnp.pad(src["ffn_w_in"][l], ((0, 0), (0, FSP - FS)))
    rows_a = lambda src: jnp.concatenate([src["ffn_w_out"][0], src["attn_w_q"][0]])
    rows_b = lambda src: jnp.concatenate([src["ffn_w_out"][1], src["attn_w_o"][0]])
    as_bf = lambda arrs: [t.astype(bf16) for t in arrs]
    fo, sq_rows = F // NDEV, D // NDEV
    g_ci, g_co = _exchange(as_bf([conv_w_in[0], conv_w_out[0]]), True, "gather_weights")
    soon = as_bf([pad_in(W, 0)])
    next_ = as_bf([rows_a(W), kv_w])
    late = as_bf([pad_in(W, 1), rows_b(W)])

    def w_out_of(g_r):
        t = g_r[:, :fo].reshape(NDEV // 2, FS, D)
        return jnp.pad(t, ((0, 0), (0, FSP - FS), (0, 0))).reshape(FP, D)

    conv_in_full = g_ci.transpose(1, 0, 2).reshape(D, 2 * D)
    wc_a, wc_g = conv_in_full[:, :D], conv_in_full[:, D:]
    wc_o = g_co.reshape(D, D)
    zeros_d = jnp.zeros((1, D), f32)
    fb = jnp.pad(forget_b, (0, LANES - H)).reshape(1, LANES)

    h0 = _normmod(x0, row(mix_norm_g[0]), sh_m0, sc_m0, "norm_mix0")
    a0, gl0, p0 = _mm_gated(h0, wc_a, wc_g, b_in[:, :D], b_in[:, D:], False, "conv_in")
    q0, s0, w_sh0 = _conv_fwd(p0, w_dw, b_dw, ln_g, ln_b, "conv_dw", ride=soon)
    x1, y0 = _mm_res(s0, wc_o, b_out, x0, gt_m0, "conv_out")

    def ffn_fwd(xin, l, sh, sc, gt, w_sh, w_out=None, ride=()):
        h = _normmod(xin, row(ffn_norm_g[l]), sh, sc, f"norm_ffn{l}")
        ug, uu, act, *got = _ffn_in(h, w_sh, f"ffn_in{l}", ride=ride)
        if w_out is None:
            w_out = w_out_of(got[0])
        xo, y = _mm_res(act, w_out, zeros_d, xin, gt, f"ffn_out{l}")
        return xo, (h, ug, uu, act, y, w_sh, w_out), got

    x2, ffn0, (g_ra, g_kv) = ffn_fwd(x1, 0, sh_f0, sc_f0, gt_f0, w_sh0, ride=next_)
    w_q = g_ra[:, fo:].reshape(D, D)
    kv_full = g_kv.transpose(1, 0, 2).reshape(D, -1)
    w_k, w_v = kv_full[:, :D], kv_full[:, D:2 * D]
    w_f = jnp.pad(kv_full[:, 2 * D:], ((0, 0), (0, LANES - H)))

    hk = _normmod(x2, row(kv_norm_g), sh_kv, sc_kv, "norm_kv")
    k_sh = _mm(hk, w_k, bf16, 1.0, "proj_k")
    v_sh = _mm(hk, w_v, bf16, 1.0, "proj_v")
    fl = _mm(hk, w_f, f32, 1.0, "proj_f")
    cum = _forget_fwd(fl, fb, "forget_fwd")
    cum_t = cum[:, :H].T.reshape(NP, 2, T)

    h2 = _normmod(x2, row(mix_norm_g[1]), sh_m1, sc_m1, "norm_mix1")
    qh = _mm(h2, w_q, bf16, HD ** -0.5, "proj_q")
    o, o32, lse, w_sh1, g_rb = _attn_fwd(qh, k_sh, v_sh, cum_t, "attn_fwd", ride=late)
    w_out1, w_o = w_out_of(g_rb), g_rb[:, fo:].reshape(D, D)
    x3, y1 = _mm_res(o, w_o, zeros_d, x2, gt_m1, "attn_out")

    x4, ffn1, _ = ffn_fwd(x3, 1, sh_f1, sc_f1, gt_f1, w_sh1, w_out1)

    dx4, acc_fin = _final_bwd(x4, row(final_norm_g), tgt, "final_bwd")

    d_ada = {}
    by_rows = lambda g: g.reshape(NDEV, sq_rows, D)

    def ffn_bwd(dx_out, xin, l, sc, gt, saved):
        h, ug, uu, act, y, w_sh, w_out = saved
        dyb, acc_r = _res_in(dx_out, y, gt, f"ffn_res_bwd{l}")
        dug, duu = _mm_nt_swiglu(dyb, w_out, ug, uu, f"ffn_dact{l}")
        g_out = _mm_tn(act, dyb, f"ffn_dw_out{l}").reshape(NDEV // 2, FSP, D)[:, :FS].reshape(NDEV, fo, D)
        g_in = _mm_tn_shards(h, duu, FSP, f"ffn_dw_up{l}", into=_mm_tn_shards(h, dug, FSP, f"ffn_dw_gate{l}"))
        dh = _ffn_dh(dug, duu, w_sh, f"ffn_dh{l}")
        dxi, acc_n = _normmod_bwd(dh, xin, row(ffn_norm_g[l]), sc, dx_out, f"norm_ffn_bwd{l}")
        return dxi, g_in, g_out, [acc_n[0:1], acc_n[1:2], acc_r[0:1]], acc_n[2]

    dx3, g_in1, g_out1, d_ada[("ffn", 1)], dg_ffn1 = ffn_bwd(dx4, x3, 1, sc_f1, gt_f1, ffn1)

    dyb, acc_r = _res_in(dx3, y1, gt_m1, "attn_res_bwd")
    do = _mm_nt([(dyb, w_o)], bf16, "attn_do")
    g_wo = _mm_tn(o, dyb, "attn_dw_o")
    st = _attn_stats(do, o32, lse, "attn_stats")
    leave_b = as_bf([g_in1, jnp.concatenate([g_out1, by_rows(g_wo)], axis=1)])
    dq, dk, dv, dck, r_in1, r_rb = _attn_bwd(qh, k_sh, v_sh, do, st, cum_t, "attn_bwd", ride=leave_b)
    g_wq = _mm_tn(h2, dq, "attn_dw_q")
    dh2 = _mm_nt([(dq, w_q)], f32, "attn_dh")
    dx2, acc_n = _normmod_bwd(dh2, x2, row(mix_norm_g[1]), sc_m1, dx3, "norm_mix_bwd1")
    d_ada[("mix", 1)] = [acc_n[0:1], acc_n[1:2], acc_r[0:1]]
    dg_mix1 = acc_n[2]

    dcum = jnp.pad(dck.reshape(H, T).T, ((0, 0), (0, LANES - H)))
    dfl, acc_f = _forget_bwd(dcum, fl, fb, "forget_bwd")
    g_kvw = jnp.concatenate([_mm_tn(hk, dk, "kv_dw_k"), _mm_tn(hk, dv, "kv_dw_v"), _mm_tn(hk, dfl, "kv_dw_f")[:, :H]], axis=1)
    dhk = _mm_nt([(dk, w_k), (dv, w_v), (dfl, w_f)], f32, "kv_dh")
    dx2, acc_n = _normmod_bwd(dhk, x2, row(kv_norm_g), sc_kv, dx2, "norm_kv_bwd")
    d_ada[("kv", 0)] = [acc_n[0:1], acc_n[1:2]]
    dg_kv = acc_n[2]

    dx1, g_in0, g_out0, d_ada[("ffn", 0)], dg_ffn0 = ffn_bwd(dx2, x1, 0, sc_f0, gt_f0, ffn0)

    dyb, acc_r = _res_in(dx1, y0, gt_m0, "conv_res_bwd")
    dsw = _mm_nt([(dyb, wc_o)], f32, "conv_ds")
    g_co_out = _mm_tn(s0, dyb, "conv_dw_out")
    leave_a = as_bf([g_in0, jnp.concatenate([g_out0, by_rows(g_wq)], axis=1), g_kvw.reshape(D, NDEV, -1).transpose(1, 0, 2)])
    da, dgl, acc_c, dw_dw, r_in0, r_ra, r_kv = _conv_bwd(dsw, q0, p0, a0, gl0, w_dw, ln_g, ln_b, "conv_bwd", ride=leave_a)
    cs = 2 * D // NDEV
    g_ci_out = _mm_tn_shards(h0, dgl, cs, "conv_dw_g", into=_mm_tn_shards(h0, da, cs, "conv_dw_a"))
    dh0, r_ci, r_co = _mm_nt([(da, wc_a), (dgl, wc_g)], f32, "conv_dh", ride=as_bf([g_ci_out, by_rows(g_co_out)]))
    dx0, acc_n = _normmod_bwd(dh0, x0, row(mix_norm_g[0]), sc_m0, dx1, "norm_mix_bwd0")
    d_ada[("mix", 0)] = [acc_n[0:1], acc_n[1:2], acc_r[0:1]]
    dg_mix0 = acc_n[2]

    vec = [t.reshape(-1) for key in [(s[0], s[1]) for s in ADA_SEG] for t in d_ada[key]]
    vec += [dg_mix0, dg_mix1, dg_ffn0, dg_ffn1, dg_kv, acc_fin[0]]
    vec += [acc_f[0], acc_fin[1, :LANES]]
    vec += [acc_c[3], acc_c[4], dw_dw[:KW].reshape(-1), acc_c[2], acc_c[0], acc_c[1], acc_r[1]]
    small_parts = _exchange([_pad_rows(jnp.concatenate(vec), LANES, 8)], True, "gather_partials")[0]
    small_sum = _sum_parts(small_parts, "sum_partials").reshape(-1)
    d_ada_all = small_parts.reshape(NDEV, -1)[:, :ADA_TOT]
    off = 0
    gsm = {}
    ada_b_sum = []
    for _, _, n in ADA_SEG:
        ada_b_sum.append(small_sum[off:off + n]); off += n
    gsm["mix_ada_b"] = jnp.stack(ada_b_sum[0:2])
    gsm["ffn_ada_b"] = jnp.stack(ada_b_sum[2:4])
    gsm["kv_ada_b"] = ada_b_sum[4]
    gsm["mix_norm_g"] = small_sum[off:off + 2 * D].reshape(2, D); off += 2 * D
    gsm["ffn_norm_g"] = small_sum[off:off + 2 * D].reshape(2, D); off += 2 * D
    gsm["kv_norm_g"] = small_sum[off:off + D]; off += D
    gsm["final_norm_g"] = small_sum[off:off + D]; off += D
    gsm["forget_b"] = small_sum[off:off + H]; off += LANES
    loss = small_sum[off]; off += LANES
    sl = lambda full, width: lax.dynamic_slice_in_dim(full, me * width, width, axis=full.ndim - 1)
    gsm["conv_b_in"] = sl(small_sum[off:off + 2 * D].reshape(1, 2 * D), 2 * D // NDEV); off += 2 * D
    gsm["conv_w_dw"] = sl(small_sum[off:off + KW * D].reshape(1, KW, D), cl); off += KW * D
    for n in ("conv_b_dw", "conv_ln_g", "conv_ln_b", "conv_b_out"):
        gsm[n] = sl(small_sum[off:off + D].reshape(1, D), cl); off += D

    dsel, off = [], 0
    for (_, _, n), nl in zip(ADA_SEG, ADA_LOC):
        dsel.append(lax.dynamic_slice_in_dim(d_ada_all[:, off:off + n], me * nl, nl, axis=1)); off += n
    g_ada = _ada_bwd(c_all.T, jnp.concatenate(dsel, axis=1), "ada_bwd")
    res_ada = _adamw(g_ada[None], w_cat, cat_ada(M), cat_ada(V), "adamw_ada")

    res_in0 = _adamw(r_in0, *[pad_in(s, 0) for s in (W, M, V)], "adamw_ffn_in0")
    res_in1 = _adamw(r_in1, *[pad_in(s, 1) for s in (W, M, V)], "adamw_ffn_in1")
    res_ra = _adamw(r_ra, *[rows_a(s) for s in (W, M, V)], "adamw_rows_a")
    res_rb = _adamw(r_rb, *[rows_b(s) for s in (W, M, V)], "adamw_rows_b")
    res_ci = _adamw(r_ci, *[s["conv_w_in"][0] for s in (W, M, V)], "adamw_conv_in")
    res_co = _adamw(r_co, *[s["conv_w_out"][0] for s in (W, M, V)], "adamw_conv_out")
    res_kv = _adamw(r_kv, *[s["kv_w"] for s in (W, M, V)], "adamw_kv")
    rest = [n for n in names if n not in MAIN and n not in ("mix_ada_w", "ffn_ada_w", "kv_ada_w")]
    pack_rest = lambda src: _pad_rows(jnp.concatenate([src[n].reshape(-1) for n in rest]), D, 256)
    res_rest = _adamw(pack_rest(gsm)[None], pack_rest(W), pack_rest(M), pack_rest(V), "adamw_rest")

    outs = []
    for k in range(4):
        ur = _split_flat(res_rest[k].reshape(-1), [(n, W[n].shape) for n in rest])
        ra, a0_, a2_ = res_ada[k], ADA_LOC[0], ADA_LOC[2]
        ur["mix_ada_w"] = jnp.stack([ra[:, 0:a0_], ra[:, a0_:2 * a0_]])
        ur["ffn_ada_w"] = jnp.stack([ra[:, 2 * a0_:2 * a0_ + a2_], ra[:, 2 * a0_ + a2_:2 * a0_ + 2 * a2_]])
        ur["kv_ada_w"] = ra[:, 2 * a0_ + 2 * a2_:]
        ur["ffn_w_in"] = jnp.stack([res_in0[k][:, :FS], res_in1[k][:, :FS]])
        ur["conv_w_in"] = res_ci[k][None]
        ur["conv_w_out"] = res_co[k][None]
        ur["kv_w"] = res_kv[k]
        ur["ffn_w_out"] = jnp.stack([res_ra[k][:fo], res_rb[k][:fo]])
        ur["attn_w_q"] = res_ra[k][fo:][None]
        ur["attn_w_o"] = res_rb[k][fo:][None]
        outs.append(ur)
    grads, deltas, new_m, new_v = outs
    return (loss, dx0[None], *[grads[n] for n in names], *[deltas[n] for n in names],
            *[new_m[n] for n in names], *[new_v[n] for n in names])
```

```python
import functools

import jax
import jax.numpy as jnp
from jax import lax
from jax.experimental import pallas as pl
from jax.experimental.pallas import tpu as pltpu

f32, bf16 = jnp.float32, jnp.bfloat16
SDS = jax.ShapeDtypeStruct

D = 1024
F = 2816
H = 16
HD = 64
NP = H // 2
KW = 31
HALO = 32
NDEV = 8
FS = 2 * F // NDEV
FSP = 768
FP = 4 * FSP
EPS = 1e-6
NEG = -1e30
LANES = 128

ADAM_LR, ADAM_B1, ADAM_B2, ADAM_EPS, ADAM_WD, ADAM_STEP = 0.001, 0.9, 0.999, 1e-08, 0.01, 10

TM = 512
TMM = 1024
TC = 256
TQ = 1024
VMEM_LIMIT = 56 << 20

MAIN = ("ffn_w_in", "ffn_w_out", "conv_w_in", "conv_w_out", "kv_w", "attn_w_q", "attn_w_o")
ADA_SEG = (("mix", 0, 3 * D), ("mix", 1, 3 * D), ("ffn", 0, 3 * D), ("ffn", 1, 3 * D), ("kv", 0, 2 * D))
ADA_LOC = tuple(n // NDEV for _, _, n in ADA_SEG)
ADA_COLS = sum(ADA_LOC)
ADA_TOT = sum(n for _, _, n in ADA_SEG)


def _cparams(n_axes):
    return pltpu.CompilerParams(dimension_semantics=("arbitrary",) * n_axes, vmem_limit_bytes=VMEM_LIMIT)


def _mesh_pos():
    return lax.axis_index("x"), lax.axis_index("y"), lax.axis_index("c")


def _my_index():
    mx, my, mc = _mesh_pos()
    return 4 * mx + 2 * my + mc


def _peer(k, mx, my, mc):
    px = (1 - mx) if k & 4 else mx
    py = (1 - my) if k & 2 else my
    pc = (1 - mc) if k & 1 else mc
    return (px, py, pc), 4 * px + 2 * py + pc


def _exchange_copies(x_refs, o_refs, sems, gather):
    send_sems, recv_sems, local_sems = sems
    mx, my, mc = _mesh_pos()
    me = 4 * mx + 2 * my + mc
    copies = []
    for a, (x_ref, o_ref) in enumerate(zip(x_refs, o_refs)):
        copies.append(pltpu.make_async_copy(x_ref if gather else x_ref.at[me], o_ref.at[me], local_sems.at[a]))
        for k in range(1, NDEV):
            peer, pidx = _peer(k, mx, my, mc)
            sem = a * (NDEV - 1) + k - 1
            copies.append(pltpu.make_async_remote_copy(
                src_ref=x_ref if gather else x_ref.at[pidx], dst_ref=o_ref.at[me],
                send_sem=send_sems.at[sem], recv_sem=recv_sems.at[sem],
                device_id=peer, device_id_type=pl.DeviceIdType.MESH))
    return copies


def _exchange_shapes(xs, gather):
    n = len(xs)
    hbm = pl.BlockSpec(memory_space=pl.ANY)
    outs = [SDS((NDEV,) + tuple(x.shape if gather else x.shape[1:]), x.dtype) for x in xs]
    sems = [pltpu.SemaphoreType.DMA((n * (NDEV - 1),)), pltpu.SemaphoreType.DMA((n * (NDEV - 1),)), pltpu.SemaphoreType.DMA((n,))]
    return [hbm] * n, [hbm] * n, outs, sems


def _exchange_start(x_refs, o_refs, sems, gather, first):
    @pl.when(first)
    def _():
        for cp in _exchange_copies(x_refs, o_refs, sems, gather):
            cp.start()


def _exchange_wait(x_refs, o_refs, sems, gather, last):
    @pl.when(last)
    def _():
        for cp in _exchange_copies(x_refs, o_refs, sems, gather):
            cp.wait()


def _exchange(xs, gather, name):
    n = len(xs)

    def body(*refs):
        copies = _exchange_copies(refs[:n], refs[n:2 * n], refs[2 * n:], gather)
        for cp in copies:
            cp.start()
        for cp in copies:
            cp.wait()

    in_specs, out_specs, outs, sems = _exchange_shapes(xs, gather)
    return pl.pallas_call(body, out_shape=outs, in_specs=in_specs, out_specs=out_specs, scratch_shapes=sems, name=name)(*xs)


def _row(i):
    return (i, 0)


def _fix(i):
    return (0, 0)


def _normmod(x, g, shift, scale, name):
    T = x.shape[0]
    tm = min(TM, T)

    def body(x_ref, g_ref, sh_ref, sc_ref, h_ref):
        xv = x_ref[...]
        r = lax.rsqrt(jnp.mean(xv * xv, axis=-1, keepdims=True) + EPS)
        hn = (xv * r) * g_ref[...]
        h_ref[...] = (hn * (1.0 + sc_ref[...]) + sh_ref[...]).astype(bf16)

    vec = pl.BlockSpec((1, D), _fix)
    return pl.pallas_call(
        body, grid=(T // tm,), in_specs=[pl.BlockSpec((tm, D), _row), vec, vec, vec],
        out_specs=pl.BlockSpec((tm, D), _row), out_shape=SDS((T, D), bf16),
        compiler_params=_cparams(1), name=name)(x, g, shift, scale)


def _normmod_bwd(dh, x, g, scale, dx_res, name):
    T = x.shape[0]
    tm = min(TM, T)

    def body(dh_ref, x_ref, g_ref, sc_ref, res_ref, dx_ref, acc_ref):
        @pl.when(pl.program_id(0) == 0)
        def _():
            acc_ref[...] = jnp.zeros_like(acc_ref)
        xv = x_ref[...]
        dhv = dh_ref[...]
        gv = g_ref[...]
        r = lax.rsqrt(jnp.mean(xv * xv, axis=-1, keepdims=True) + EPS)
        xn = xv * r
        dhn = dhv * (1.0 + sc_ref[...])
        dxn = dhn * gv
        dx_ref[...] = res_ref[...] + r * (dxn - xn * jnp.mean(dxn * xn, axis=-1, keepdims=True))
        acc_ref[0:1, :] += jnp.sum(dhv, axis=0, keepdims=True)
        acc_ref[1:2, :] += jnp.sum(dhv * (xn * gv), axis=0, keepdims=True)
        acc_ref[2:3, :] += jnp.sum(dhn * xn, axis=0, keepdims=True)

    vec = pl.BlockSpec((1, D), _fix)
    til = pl.BlockSpec((tm, D), _row)
    return pl.pallas_call(
        body, grid=(T // tm,), in_specs=[til, til, vec, vec, til],
        out_specs=[til, pl.BlockSpec((8, D), _fix)], out_shape=[SDS((T, D), f32), SDS((8, D), f32)],
        compiler_params=_cparams(1), name=name)(dh, x, g, scale, dx_res)


def _res_in(dx, y, gate, name):
    T = dx.shape[0]
    tm = min(TM, T)

    def body(dx_ref, y_ref, gt_ref, dy_ref, acc_ref):
        @pl.when(pl.program_id(0) == 0)
        def _():
            acc_ref[...] = jnp.zeros_like(acc_ref)
        dxv = dx_ref[...]
        dy = dxv * gt_ref[...]
        dy_ref[...] = dy.astype(bf16)
        acc_ref[0:1, :] += jnp.sum(dxv * y_ref[...].astype(f32), axis=0, keepdims=True)
        acc_ref[1:2, :] += jnp.sum(dy, axis=0, keepdims=True)

    til = pl.BlockSpec((tm, D), _row)
    return pl.pallas_call(
        body, grid=(T // tm,), in_specs=[til, til, pl.BlockSpec((1, D), _fix)],
        out_specs=[til, pl.BlockSpec((8, D), _fix)], out_shape=[SDS((T, D), bf16), SDS((8, D), f32)],
        compiler_params=_cparams(1), name=name)(dx, y, gate)


def _final_bwd(x, g, tgt, name):
    T = x.shape[0]
    tm = min(TM, T)

    def body(x_ref, g_ref, t_ref, dx_ref, acc_ref):
        @pl.when(pl.program_id(0) == 0)
        def _():
            acc_ref[...] = jnp.zeros_like(acc_ref)
        xv = x_ref[...]
        gv = g_ref[...]
        r = lax.rsqrt(jnp.mean(xv * xv, axis=-1, keepdims=True) + EPS)
        xn = xv * r
        err = xn * gv - t_ref[...]
        dy = err * (1.0 / D)
        dxn = dy * gv
        dx_ref[...] = r * (dxn - xn * jnp.mean(dxn * xn, axis=-1, keepdims=True))
        acc_ref[0:1, :] += jnp.sum(dy * xn, axis=0, keepdims=True)
        acc_ref[1:2, :] += 0.5 * jnp.sum(jnp.mean(err * err, axis=-1, keepdims=True))

    til = pl.BlockSpec((tm, D), _row)
    return pl.pallas_call(
        body, grid=(T // tm,), in_specs=[til, pl.BlockSpec((1, D), _fix), til],
        out_specs=[til, pl.BlockSpec((8, D), _fix)], out_shape=[SDS((T, D), f32), SDS((8, D), f32)],
        compiler_params=_cparams(1), name=name)(x, g, tgt)


def _col_tile(n):
    if n <= 1024:
        return n
    return 1408 if n % 1408 == 0 else 1024


def _mm_gated(h, wa, wb, ba, bb, swiglu, name):
    T, K = h.shape
    N = wa.shape[1]
    tm, tn = min(TMM, T), _col_tile(N)

    def body(h_ref, wa_ref, wb_ref, ba_ref, bb_ref, u_ref, w_ref, p_ref):
        hv = h_ref[...]
        u = jnp.dot(hv, wa_ref[...], preferred_element_type=f32) + ba_ref[...]
        w = jnp.dot(hv, wb_ref[...], preferred_element_type=f32) + bb_ref[...]
        u_ref[...] = u
        w_ref[...] = w
        if swiglu:
            p_ref[...] = ((u * jax.nn.sigmoid(u)) * w).astype(p_ref.dtype)
        else:
            p_ref[...] = (u * jax.nn.sigmoid(w)).astype(p_ref.dtype)

    wsp = pl.BlockSpec((K, tn), lambda i, j: (0, j))
    bsp = pl.BlockSpec((1, tn), lambda i, j: (0, j))
    osp = pl.BlockSpec((tm, tn), lambda i, j: (i, j))
    return pl.pallas_call(
        body, grid=(T // tm, N // tn), in_specs=[pl.BlockSpec((tm, K), lambda i, j: (i, 0)), wsp, wsp, bsp, bsp],
        out_specs=[osp, osp, osp], out_shape=[SDS((T, N), f32), SDS((T, N), f32), SDS((T, N), bf16)],
        compiler_params=_cparams(2), name=name)(h, wa, wb, ba, bb)


def _mm_res(a, w, b, x_in, gate, name):
    T, K = a.shape
    N = w.shape[1]
    tm, tn = min(TMM, T), _col_tile(N)

    def body(a_ref, w_ref, b_ref, x_ref, gt_ref, xo_ref, y_ref):
        y = jnp.dot(a_ref[...], w_ref[...], preferred_element_type=f32) + b_ref[...]
        y_ref[...] = y.astype(bf16)
        xo_ref[...] = x_ref[...] + gt_ref[...] * y

    vsp = pl.BlockSpec((1, tn), lambda i, j: (0, j))
    osp = pl.BlockSpec((tm, tn), lambda i, j: (i, j))
    return pl.pallas_call(
        body, grid=(T // tm, N // tn),
        in_specs=[pl.BlockSpec((tm, K), lambda i, j: (i, 0)), pl.BlockSpec((K, tn), lambda i, j: (0, j)), vsp, osp, vsp],
        out_specs=[osp, osp], out_shape=[SDS((T, N), f32), SDS((T, N), bf16)],
        compiler_params=_cparams(2), name=name)(a, w, b, x_in, gate)


def _mm(a, w, out_dtype, out_scale, name):
    T, K = a.shape
    N = w.shape[1]
    tm, tn = min(TMM, T), _col_tile(N)

    def body(a_ref, w_ref, o_ref):
        y = jnp.dot(a_ref[...], w_ref[...], preferred_element_type=f32)
        if out_scale != 1.0:
            y = y * out_scale
        o_ref[...] = y.astype(out_dtype)

    return pl.pallas_call(
        body, grid=(T // tm, N // tn),
        in_specs=[pl.BlockSpec((tm, K), lambda i, j: (i, 0)), pl.BlockSpec((K, tn), lambda i, j: (0, j))],
        out_specs=pl.BlockSpec((tm, tn), lambda i, j: (i, j)), out_shape=SDS((T, N), out_dtype),
        compiler_params=_cparams(2), name=name)(a, w)


def _dot_nt(a, b):
    return lax.dot_general(a, b, (((1,), (1,)), ((), ())), preferred_element_type=f32)


def _dot_tn(a, b):
    return lax.dot_general(a, b, (((0,), (0,)), ((), ())), preferred_element_type=f32)


def _mm_nt(pairs, out_dtype, name, ride=()):
    T = pairs[0][0].shape[0]
    K = pairs[0][1].shape[0]
    tm, tk = min(TMM, T), _col_tile(K)
    n = len(pairs)
    nx = len(ride)
    gi, gj = T // tm, K // tk

    def body(*refs):
        x_refs, o_ref, xo_refs, sems = refs[2 * n:2 * n + nx], refs[2 * n + nx], refs[2 * n + nx + 1:2 * n + 2 * nx + 1], refs[2 * n + 2 * nx + 1:]
        i, j = pl.program_id(0), pl.program_id(1)
        if nx:
            _exchange_start(x_refs, xo_refs, sems, False, (i == 0) & (j == 0))
        acc = None
        for a in range(n):
            part = _dot_nt(refs[2 * a][...].astype(bf16), refs[2 * a + 1][...])
            acc = part if acc is None else acc + part
        o_ref[...] = acc.astype(out_dtype)
        if nx:
            _exchange_wait(x_refs, xo_refs, sems, False, (i == gi - 1) & (j == gj - 1))

    in_specs, args = [], []
    for dy, w in pairs:
        ni = dy.shape[1]
        in_specs += [pl.BlockSpec((tm, ni), lambda i, j: (i, 0)), pl.BlockSpec((tk, ni), lambda i, j: (j, 0))]
        args += [dy, w]
    x_in, x_out, x_shape, x_sems = _exchange_shapes(ride, False) if nx else ([], [], [], [])
    out = pl.pallas_call(
        body, grid=(gi, gj), in_specs=in_specs + x_in,
        out_specs=[pl.BlockSpec((tm, tk), lambda i, j: (i, j))] + x_out, out_shape=[SDS((T, K), out_dtype)] + x_shape,
        scratch_shapes=x_sems, compiler_params=_cparams(2), name=name)(*args, *ride)
    return out if nx else out[0]


def _mm_nt_swiglu(dy, w, ug, uu, name):
    T, N = dy.shape
    K = w.shape[0]
    tm, tk = min(TMM, T), _col_tile(K)

    def body(dy_ref, w_ref, ug_ref, uu_ref, dug_ref, duu_ref):
        dact = _dot_nt(dy_ref[...], w_ref[...])
        g = ug_ref[...].astype(f32)
        u = uu_ref[...].astype(f32)
        sg = jax.nn.sigmoid(g)
        duu_ref[...] = (dact * (g * sg)).astype(bf16)
        dug_ref[...] = (dact * u * (sg * (1.0 + g * (1.0 - sg)))).astype(bf16)

    osp = pl.BlockSpec((tm, tk), lambda i, j: (i, j))
    return pl.pallas_call(
        body, grid=(T // tm, K // tk),
        in_specs=[pl.BlockSpec((tm, N), lambda i, j: (i, 0)), pl.BlockSpec((tk, N), lambda i, j: (j, 0)), osp, osp],
        out_specs=[osp, osp], out_shape=[SDS((T, K), bf16), SDS((T, K), bf16)],
        compiler_params=_cparams(2), name=name)(dy, w, ug, uu)


def _mm_tn(a, b, name):
    T, K = a.shape
    N = b.shape[1]
    tt = min(TMM, T)
    tk = K if K <= 1024 else _col_tile(K)
    tn = N if N <= 1024 else _col_tile(N)

    def body(a_ref, b_ref, o_ref):
        @pl.when(pl.program_id(2) == 0)
        def _():
            o_ref[...] = jnp.zeros_like(o_ref)
        o_ref[...] += _dot_tn(a_ref[...].astype(bf16), b_ref[...].astype(bf16))

    return pl.pallas_call(
        body, grid=(K // tk, N // tn, T // tt),
        in_specs=[pl.BlockSpec((tt, tk), lambda i, j, t: (t, i)), pl.BlockSpec((tt, tn), lambda i, j, t: (t, j))],
        out_specs=pl.BlockSpec((tk, tn), lambda i, j, t: (i, j)), out_shape=SDS((K, N), f32),
        compiler_params=_cparams(3), name=name)(a, b)


def _mm_tn_shards(a, b, c, name, into=None):
    T, K = a.shape
    half = NDEV // 2
    assert b.shape[1] == half * c
    tt = min(TMM, T)
    nt = T // tt
    first = into is None

    def body(a_ref, b_ref, *rest):
        o_ref, acc = rest[-2], rest[-1]
        t = pl.program_id(1)

        @pl.when(t == 0)
        def _():
            acc[...] = jnp.zeros_like(acc)
        acc[...] += _dot_tn(a_ref[...].astype(bf16), b_ref[...].astype(bf16))

        @pl.when(t == nt - 1)
        def _():
            o_ref[...] = acc[...].astype(bf16)

    in_specs = [pl.BlockSpec((tt, K), lambda j, t: (t, 0)), pl.BlockSpec((tt, c), lambda j, t: (t, j))]
    args = [a, b]
    if not first:
        in_specs.append(pl.BlockSpec(memory_space=pl.ANY))
        args.append(into)
    base = 0 if first else half
    return pl.pallas_call(
        body, grid=(half, nt), in_specs=in_specs,
        out_specs=pl.BlockSpec((None, K, c), lambda j, t: (j + base, 0, 0)), out_shape=SDS((NDEV, K, c), bf16),
        scratch_shapes=[pltpu.VMEM((K, c), f32)], input_output_aliases={} if first else {2: 0},
        compiler_params=_cparams(2), name=name)(*args)


def _ffn_in(h, w_sh, name, ride=()):
    T = h.shape[0]
    tm = min(TMM, T)
    nx = len(ride)
    gi, gj = T // tm, NDEV // 2

    def body(h_ref, wg_ref, wu_ref, *rest):
        x_refs, (ug_ref, uu_ref, act_ref), xo_refs, sems = rest[:nx], rest[nx:nx + 3], rest[nx + 3:2 * nx + 3], rest[2 * nx + 3:]
        i, j = pl.program_id(0), pl.program_id(1)
        if nx:
            _exchange_start(x_refs, xo_refs, sems, True, (i == 0) & (j == 0))
        hv = h_ref[...]
        ug = jnp.dot(hv, wg_ref[...], preferred_element_type=f32)
        uu = jnp.dot(hv, wu_ref[...], preferred_element_type=f32)
        ug_ref[...] = ug.astype(bf16)
        uu_ref[...] = uu.astype(bf16)
        act_ref[...] = ((ug * jax.nn.sigmoid(ug)) * uu).astype(bf16)
        if nx:
            _exchange_wait(x_refs, xo_refs, sems, True, (i == gi - 1) & (j == gj - 1))

    osp = pl.BlockSpec((tm, FSP), lambda i, j: (i, j))
    x_in, x_out, x_shape, x_sems = _exchange_shapes(ride, True) if nx else ([], [], [], [])
    return pl.pallas_call(
        body, grid=(gi, gj),
        in_specs=[pl.BlockSpec((tm, D), lambda i, j: (i, 0)),
                  pl.BlockSpec((None, D, FSP), lambda i, j: (j, 0, 0)),
                  pl.BlockSpec((None, D, FSP), lambda i, j: (j + NDEV // 2, 0, 0))] + x_in,
        out_specs=[osp, osp, osp] + x_out,
        out_shape=[SDS((T, FP), bf16), SDS((T, FP), bf16), SDS((T, FP), bf16)] + x_shape,
        scratch_shapes=x_sems, compiler_params=_cparams(2), name=name)(h, w_sh, w_sh, *ride)


def _ffn_dh(dug, duu, w_sh, name):
    T = dug.shape[0]
    tm, tk = min(TMM, T), 512
    half = NDEV // 2

    def body(dg_ref, du_ref, wg_ref, wu_ref, o_ref):
        acc = None
        for s in range(half):
            cols = slice(s * FSP, (s + 1) * FSP)
            part = _dot_nt(dg_ref[:, cols], wg_ref[s]) + _dot_nt(du_ref[:, cols], wu_ref[s])
            acc = part if acc is None else acc + part
        o_ref[...] = acc

    dsp = pl.BlockSpec((tm, FP), lambda i, j: (i, 0))
    return pl.pallas_call(
        body, grid=(T // tm, D // tk),
        in_specs=[dsp, dsp, pl.BlockSpec((half, tk, FSP), lambda i, j: (0, j, 0)),
                  pl.BlockSpec((half, tk, FSP), lambda i, j: (1, j, 0))],
        out_specs=pl.BlockSpec((tm, tk), lambda i, j: (i, j)), out_shape=SDS((T, D), f32),
        compiler_params=_cparams(2), name=name)(dug, duu, w_sh, w_sh)


def _layernorm_parts(qv, g, b):
    mu = jnp.mean(qv, axis=-1, keepdims=True)
    cen = qv - mu
    rstd = lax.rsqrt(jnp.mean(cen * cen, axis=-1, keepdims=True) + EPS)
    z = cen * rstd
    return z, rstd, z * g + b


def _tap_groups(offsets):
    groups = {}
    for k, o in enumerate(offsets):
        groups.setdefault(o % 8, []).append((k, o - o % 8))
    return groups


def _conv_fwd(p, w_dw, b_dw, ln_g, ln_b, name, ride=()):
    T = p.shape[0]
    tc = min(TC, T)
    n = T // tc
    nx = len(ride)

    def body(p_ref, w_ref, b_ref, g_ref, bb_ref, *rest):
        x_refs, (q_ref, s_ref), xo_refs = rest[:nx], rest[nx:nx + 2], rest[nx + 2:2 * nx + 2]
        (ext, sh), sems = rest[2 * nx + 2:2 * nx + 4], rest[2 * nx + 4:]
        i = pl.program_id(0)
        if nx:
            _exchange_start(x_refs, xo_refs, sems, True, i == 0)

        @pl.when(i == 0)
        def _():
            ext[0:HALO, :] = jnp.zeros((HALO, D), f32)

        @pl.when(i > 0)
        def _():
            ext[0:HALO, :] = ext[tc:tc + HALO, :]

        ext[HALO:HALO + tc, :] = p_ref[...].astype(f32)
        groups = _tap_groups([HALO - (KW - 1) + k for k in range(KW)])
        for cb in range(D // LANES):
            cols = slice(cb * LANES, (cb + 1) * LANES)
            acc = jnp.zeros((tc, LANES), f32)
            for r, taps in groups.items():
                span = max(base for _, base in taps) + tc
                sh[0:span, :] = ext[r:r + span, cols]
                for k, base in taps:
                    acc = acc + w_ref[k:k + 1, cols] * sh[base:base + tc, :]
            q_ref[:, cols] = acc + b_ref[:, cols]
        _, _, l = _layernorm_parts(q_ref[...], g_ref[...], bb_ref[...])
        s_ref[...] = (l * jax.nn.sigmoid(l)).astype(bf16)
        if nx:
            _exchange_wait(x_refs, xo_refs, sems, True, i == n - 1)

    vec = pl.BlockSpec((1, D), _fix)
    til = pl.BlockSpec((tc, D), _row)
    x_in, x_out, x_shape, x_sems = _exchange_shapes(ride, True) if nx else ([], [], [], [])
    return pl.pallas_call(
        body, grid=(n,), in_specs=[til, pl.BlockSpec((HALO, D), _fix), vec, vec, vec] + x_in,
        out_specs=[til, til] + x_out, out_shape=[SDS((T, D), f32), SDS((T, D), bf16)] + x_shape,
        scratch_shapes=[pltpu.VMEM((tc + HALO, D), f32), pltpu.VMEM((tc + HALO, LANES), f32)] + x_sems, compiler_params=_cparams(1),
        name=name)(p, w_dw, b_dw, ln_g, ln_b, *ride)


def _conv_bwd(ds, q, p, a, gl, w_dw, ln_g, ln_b, name, ride=()):
    T = q.shape[0]
    tc = min(TC, T)
    n = T // tc
    nx = len(ride)

    def body(ds_ref, q_ref, p_ref, a_ref, gl_ref, w_ref, g_ref, bb_ref, *rest):
        x_refs, (da_ref, dgl_ref, acc_ref, dw_ref), xo_refs = rest[:nx], rest[nx:nx + 4], rest[nx + 4:2 * nx + 4]
        (ext, sh), sems = rest[2 * nx + 4:2 * nx + 6], rest[2 * nx + 6:]
        i = pl.program_id(0)
        if nx:
            _exchange_start(x_refs, xo_refs, sems, False, i == 0)

        @pl.when(i == 0)
        def _():
            acc_ref[...] = jnp.zeros_like(acc_ref)
            dw_ref[...] = jnp.zeros_like(dw_ref)
            ext[tc:tc + HALO, :] = jnp.zeros((HALO, D), f32)

        @pl.when(i > 0)
        def _():
            ext[tc:tc + HALO, :] = ext[0:HALO, :]

        gv = g_ref[...]
        z, rstd, l = _layernorm_parts(q_ref[...], gv, bb_ref[...])
        sg = jax.nn.sigmoid(l)
        dl = ds_ref[...] * (sg * (1.0 + l * (1.0 - sg)))
        dz = dl * gv
        dq = rstd * (dz - jnp.mean(dz, axis=-1, keepdims=True) - z * jnp.mean(dz * z, axis=-1, keepdims=True))
        ext[0:tc, :] = dq.astype(bf16).astype(f32)
        acc_ref[0:1, :] += jnp.sum(dl * z, axis=0, keepdims=True)
        acc_ref[1:2, :] += jnp.sum(dl, axis=0, keepdims=True)
        acc_ref[2:3, :] += jnp.sum(dq, axis=0, keepdims=True)
        groups = _tap_groups([KW - 1 - k for k in range(KW)])
        for cb in range(D // LANES):
            cols = slice(cb * LANES, (cb + 1) * LANES)
            pc = p_ref[:, cols].astype(f32)
            dp = jnp.zeros((tc, LANES), f32)
            for r, taps in groups.items():
                span = max(base for _, base in taps) + tc
                sh[0:span, :] = ext[r:r + span, cols]
                for k, base in taps:
                    sl = sh[base:base + tc, :]
                    dp = dp + w_ref[k:k + 1, cols] * sl
                    dw_ref[k:k + 1, cols] += jnp.sum(sl * pc, axis=0, keepdims=True)
            av = a_ref[:, cols].astype(f32)
            sgl = jax.nn.sigmoid(gl_ref[:, cols].astype(f32))
            da = dp * sgl
            dgl = dp * av * (sgl * (1.0 - sgl))
            da_ref[:, cols] = da.astype(bf16)
            dgl_ref[:, cols] = dgl.astype(bf16)
            acc_ref[3:4, cols] += jnp.sum(da, axis=0, keepdims=True)
            acc_ref[4:5, cols] += jnp.sum(dgl, axis=0, keepdims=True)
        if nx:
            _exchange_wait(x_refs, xo_refs, sems, False, i == n - 1)

    rev = lambda i: (n - 1 - i, 0)
    til = pl.BlockSpec((tc, D), rev)
    vec = pl.BlockSpec((1, D), _fix)
    x_in, x_out, x_shape, x_sems = _exchange_shapes(ride, False) if nx else ([], [], [], [])
    return pl.pallas_call(
        body, grid=(n,), in_specs=[til, til, til, til, til, pl.BlockSpec((HALO, D), _fix), vec, vec] + x_in,
        out_specs=[til, til, pl.BlockSpec((8, D), _fix), pl.BlockSpec((HALO, D), _fix)] + x_out,
        out_shape=[SDS((T, D), bf16), SDS((T, D), bf16), SDS((8, D), f32), SDS((HALO, D), f32)] + x_shape,
        scratch_shapes=[pltpu.VMEM((tc + HALO, D), f32), pltpu.VMEM((tc + HALO, LANES), f32)] + x_sems, compiler_params=_cparams(1),
        name=name)(ds, q, p, a, gl, w_dw, ln_g, ln_b, *ride)


def _tri(n, upper):
    r = lax.broadcasted_iota(jnp.int32, (n, n), 0)
    c = lax.broadcasted_iota(jnp.int32, (n, n), 1)
    return ((c >= r) if upper else (r >= c)).astype(f32)


def _forget_fwd(fl, fb, name):
    T = fl.shape[0]
    tc = min(TC, T)

    def body(fl_ref, fb_ref, cum_ref, carry):
        @pl.when(pl.program_id(0) == 0)
        def _():
            carry[...] = jnp.zeros_like(carry)
        xv = fl_ref[...] + fb_ref[...]
        lf = jnp.minimum(xv, 0.0) - jnp.log(1.0 + jnp.exp(-jnp.abs(xv)))
        cs = jnp.dot(_tri(tc, False), lf, preferred_element_type=f32, precision=lax.Precision.HIGHEST) + carry[0:1, :]
        cum_ref[...] = cs
        carry[0:1, :] = cs[tc - 1:tc, :]

    til = pl.BlockSpec((tc, LANES), _row)
    return pl.pallas_call(
        body, grid=(T // tc,), in_specs=[til, pl.BlockSpec((1, LANES), _fix)], out_specs=til,
        out_shape=SDS((T, LANES), f32), scratch_shapes=[pltpu.VMEM((8, LANES), f32)],
        compiler_params=_cparams(1), name=name)(fl, fb)


def _forget_bwd(dcum, fl, fb, name):
    T = fl.shape[0]
    tc = min(TC, T)
    n = T // tc

    def body(dc_ref, fl_ref, fb_ref, dfl_ref, acc_ref, carry):
        @pl.when(pl.program_id(0) == 0)
        def _():
            carry[...] = jnp.zeros_like(carry)
            acc_ref[...] = jnp.zeros_like(acc_ref)
        dlf = jnp.dot(_tri(tc, True), dc_ref[...], preferred_element_type=f32, precision=lax.Precision.HIGHEST) + carry[0:1, :]
        carry[0:1, :] = dlf[0:1, :]
        dfl = dlf * (1.0 - jax.nn.sigmoid(fl_ref[...] + fb_ref[...]))
        dfl_ref[...] = dfl.astype(bf16)
        acc_ref[0:1, :] += jnp.sum(dfl, axis=0, keepdims=True)

    rev = lambda i: (n - 1 - i, 0)
    til = pl.BlockSpec((tc, LANES), rev)
    return pl.pallas_call(
        body, grid=(n,), in_specs=[til, til, pl.BlockSpec((1, LANES), _fix)],
        out_specs=[til, pl.BlockSpec((8, LANES), _fix)], out_shape=[SDS((T, LANES), bf16), SDS((8, LANES), f32)],
        scratch_shapes=[pltpu.VMEM((8, LANES), f32)], compiler_params=_cparams(1), name=name)(dcum, fl, fb)


def _causal(s, n):
    r = lax.broadcasted_iota(jnp.int32, (n, n), 0)
    c = lax.broadcasted_iota(jnp.int32, (n, n), 1)
    return jnp.where(c <= r, s, NEG)


def _attn_fwd(q, k, v, cum_t, name, ride=()):
    T = q.shape[0]
    tq = min(TQ, T)
    n = T // tq
    nx = len(ride)

    def body(q_ref, k_ref, v_ref, ck_ref, *rest):
        x_refs, (o_ref, o32_ref, st_ref), xo_refs = rest[:nx], rest[nx:nx + 3], rest[nx + 3:2 * nx + 3]
        (m_sc, l_sc, acc_sc, res_sc), sems = rest[2 * nx + 3:2 * nx + 7], rest[2 * nx + 7:]
        i = pl.program_id(1)
        if nx:
            _exchange_start(x_refs, xo_refs, sems, True, (pl.program_id(0) == 0) & (i == 0))
        lane = lax.broadcasted_iota(jnp.int32, (1, LANES), 1)
        lo = lane < HD
        q2 = q_ref[...]
        zero = jnp.zeros_like(q2)
        qa = (jnp.where(lo, q2, zero), jnp.where(lo, zero, q2))
        m_sc[...] = jnp.full(m_sc.shape, NEG, f32)
        l_sc[...] = jnp.zeros_like(l_sc)
        acc_sc[...] = jnp.zeros_like(acc_sc)
        res_sc[...] = jnp.zeros_like(res_sc)

        def block(j, masked):
            off = pl.multiple_of(j * tq, tq)
            k2 = k_ref[pl.ds(off, tq), :]
            v2 = v_ref[pl.ds(off, tq), :]
            for a in range(2):
                s = _dot_nt(qa[a], k2) - ck_ref[0, a:a + 1, pl.ds(off, tq)]
                if masked:
                    s = _causal(s, tq)
                m_old = m_sc[a]
                m_new = jnp.maximum(m_old, jnp.max(s, axis=1, keepdims=True))
                alpha = jnp.exp(m_old - m_new)
                pm = jnp.exp(s - m_new)
                pb = pm.astype(bf16)
                pr = (pm - pb.astype(f32)).astype(bf16)
                l_sc[a] = alpha * l_sc[a] + jnp.sum(pm, axis=1, keepdims=True)
                acc_sc[a] = alpha * acc_sc[a] + jnp.dot(pb, v2, preferred_element_type=f32)
                res_sc[a] = alpha * res_sc[a] + jnp.dot(pr, v2, preferred_element_type=f32)
                m_sc[a] = m_new

        def step(j, carry):
            block(j, False)
            return carry

        lax.fori_loop(0, i, step, 0)
        block(i, True)
        o_ref[...] = jnp.where(lo, acc_sc[0] / l_sc[0], acc_sc[1] / l_sc[1]).astype(bf16)
        o32_ref[...] = jnp.where(lo, (acc_sc[0] + res_sc[0]) / l_sc[0], (acc_sc[1] + res_sc[1]) / l_sc[1])
        lse0 = m_sc[0] + jnp.log(l_sc[0])
        lse1 = m_sc[1] + jnp.log(l_sc[1])
        st_ref[0] = jnp.where(lane == 0, lse0, jnp.where(lane == 1, lse1, 0.0))
        if nx:
            _exchange_wait(x_refs, xo_refs, sems, True, (pl.program_id(0) == NP - 1) & (i == n - 1))

    full = lambda blk: pl.BlockSpec((T, LANES), blk)
    x_in, x_out, x_shape, x_sems = _exchange_shapes(ride, True) if nx else ([], [], [], [])
    return pl.pallas_call(
        body, grid=(NP, n),
        in_specs=[pl.BlockSpec((tq, LANES), lambda p, i: (i, p)), full(lambda p, i: (0, p)), full(lambda p, i: (0, p)),
                  pl.BlockSpec((1, 2, T), lambda p, i: (p, 0, 0))] + x_in,
        out_specs=[pl.BlockSpec((tq, LANES), lambda p, i: (i, p)), pl.BlockSpec((tq, LANES), lambda p, i: (i, p)),
                   pl.BlockSpec((1, tq, LANES), lambda p, i: (p, i, 0))] + x_out,
        out_shape=[SDS((T, H * HD), bf16), SDS((T, H * HD), f32), SDS((NP, T, LANES), f32)] + x_shape,
        scratch_shapes=[pltpu.VMEM((2, tq, 1), f32), pltpu.VMEM((2, tq, 1), f32), pltpu.VMEM((2, tq, LANES), f32),
                        pltpu.VMEM((2, tq, LANES), f32)] + x_sems,
        compiler_params=_cparams(2), name=name)(q, k, v, cum_t, *ride)


def _attn_stats(do, o, lse, name):
    T = do.shape[0]
    tm = min(TM, T)

    def body(do_ref, o_ref, lse_ref, st_ref):
        lane = lax.broadcasted_iota(jnp.int32, (1, LANES), 1)
        prod = do_ref[...].astype(f32) * o_ref[...].astype(f32)
        d0 = jnp.sum(jnp.where(lane < HD, prod, 0.0), axis=1, keepdims=True)
        d1 = jnp.sum(jnp.where(lane < HD, 0.0, prod), axis=1, keepdims=True)
        st_ref[0] = jnp.where(lane < 2, lse_ref[0], jnp.where(lane == 2, d0, jnp.where(lane == 3, d1, 0.0)))

    til = pl.BlockSpec((tm, LANES), lambda p, i: (i, p))
    stt = pl.BlockSpec((1, tm, LANES), lambda p, i: (p, i, 0))
    return pl.pallas_call(
        body, grid=(NP, T // tm), in_specs=[til, til, stt], out_specs=stt, out_shape=SDS((NP, T, LANES), f32),
        compiler_params=_cparams(2), name=name)(do, o, lse)


def _attn_bwd(q, k, v, do, st, cum_t, name, ride=()):
    T = q.shape[0]
    tq = min(TQ, T)
    n = T // tq
    nx = len(ride)

    def body(q_ref, k_ref, v_ref, do_ref, st_ref, ck_ref, *rest):
        x_refs, (dq_ref, dk_ref, dv_ref, dck_ref), xo_refs = rest[:nx], rest[nx:nx + 4], rest[nx + 4:2 * nx + 4]
        (dk_sc, dv_sc, dck_sc), sems = rest[2 * nx + 4:2 * nx + 7], rest[2 * nx + 7:]
        j = pl.program_id(1)
        if nx:
            _exchange_start(x_refs, xo_refs, sems, False, (pl.program_id(0) == 0) & (j == 0))
        lane = lax.broadcasted_iota(jnp.int32, (1, LANES), 1)
        lo = lane < HD

        @pl.when(j == 0)
        def _():
            dq_ref[...] = jnp.zeros_like(dq_ref)

        k2 = k_ref[...]
        v2 = v_ref[...]
        zero = jnp.zeros_like(k2)
        ka = (jnp.where(lo, k2, zero), jnp.where(lo, zero, k2))
        va = (jnp.where(lo, v2, zero), jnp.where(lo, zero, v2))
        dk_sc[...] = jnp.zeros_like(dk_sc)
        dv_sc[...] = jnp.zeros_like(dv_sc)
        dck_sc[...] = jnp.zeros_like(dck_sc)

        def block(i, masked):
            off = pl.multiple_of(i * tq, tq)
            q2 = q_ref[pl.ds(off, tq), :]
            do2 = do_ref[pl.ds(off, tq), :]
            stt = st_ref[0, pl.ds(off, tq), :]
            parts = []
            for a in range(2):
                s = _dot_nt(q2, ka[a]) - ck_ref[0, a:a + 1, :]
                if masked:
                    s = _causal(s, tq)
                pm = jnp.exp(s - stt[:, a:a + 1])
                dp = _dot_nt(do2, va[a])
                dsm = pm * (dp - stt[:, 2 + a:3 + a])
                dsb = dsm.astype(bf16)
                dv_sc[a] += _dot_tn(pm.astype(bf16), do2)
                dk_sc[a] += _dot_tn(dsb, q2)
                dck_sc[a:a + 1, :] -= jnp.sum(dsm, axis=0, keepdims=True)
                parts.append(jnp.dot(dsb, k2, preferred_element_type=f32))
            dq_ref[pl.ds(off, tq), :] += jnp.where(lo, parts[0], parts[1])

        block(j, True)

        def step(i, carry):
            block(i, False)
            return carry

        lax.fori_loop(j + 1, n, step, 0)
        dk_ref[...] = jnp.where(lo, dk_sc[0], dk_sc[1]).astype(bf16)
        dv_ref[...] = jnp.where(lo, dv_sc[0], dv_sc[1]).astype(bf16)
        dck_ref[0] = dck_sc[0:2, :]

        @pl.when(j == n - 1)
        def _():
            dq_ref[...] = dq_ref[...] * (HD ** -0.5)

        if nx:
            _exchange_wait(x_refs, xo_refs, sems, False, (pl.program_id(0) == NP - 1) & (j == n - 1))

    full = lambda: pl.BlockSpec((T, LANES), lambda p, j: (0, p))
    kvb = lambda: pl.BlockSpec((tq, LANES), lambda p, j: (j, p))
    ckb = lambda: pl.BlockSpec((1, 2, tq), lambda p, j: (p, 0, j))
    x_in, x_out, x_shape, x_sems = _exchange_shapes(ride, False) if nx else ([], [], [], [])
    return pl.pallas_call(
        body, grid=(NP, n),
        in_specs=[full(), kvb(), kvb(), full(), pl.BlockSpec((1, T, LANES), lambda p, j: (p, 0, 0)), ckb()] + x_in,
        out_specs=[full(), kvb(), kvb(), ckb()] + x_out,
        out_shape=[SDS((T, H * HD), f32), SDS((T, H * HD), bf16), SDS((T, H * HD), bf16), SDS((NP, 2, T), f32)] + x_shape,
        scratch_shapes=[pltpu.VMEM((2, tq, LANES), f32), pltpu.VMEM((2, tq, LANES), f32), pltpu.VMEM((8, tq), f32)] + x_sems,
        compiler_params=_cparams(2), name=name)(q, k, v, do, st, cum_t, *ride)


def _ada_fwd(c_all, w_cat, name):
    n = w_cat.shape[1]
    tn = 256

    def body(c_ref, w_ref, o_ref):
        cv = c_ref[...]
        o_ref[...] = jnp.dot((cv * jax.nn.sigmoid(cv)).astype(bf16), w_ref[...].astype(bf16), preferred_element_type=f32)

    return pl.pallas_call(
        body, grid=(n // tn,), in_specs=[pl.BlockSpec((NDEV, D), _fix), pl.BlockSpec((D, tn), lambda i: (0, i))],
        out_specs=pl.BlockSpec((NDEV, tn), lambda i: (0, i)), out_shape=SDS((NDEV, n), f32),
        compiler_params=_cparams(1), name=name)(c_all, w_cat)


def _ada_bwd(c_all_t, dsel, name):
    n = dsel.shape[1]
    tn = 256

    def body(c_ref, d_ref, o_ref):
        cv = c_ref[...]
        ca = cv * jax.nn.sigmoid(cv)
        acc = ca[:, 0:1] * d_ref[0:1, :]
        for b in range(1, NDEV):
            acc = acc + ca[:, b:b + 1] * d_ref[b:b + 1, :]
        o_ref[...] = acc

    return pl.pallas_call(
        body, grid=(n // tn,), in_specs=[pl.BlockSpec((D, NDEV), _fix), pl.BlockSpec((NDEV, tn), lambda i: (0, i))],
        out_specs=pl.BlockSpec((D, tn), lambda i: (0, i)), out_shape=SDS((D, n), f32),
        compiler_params=_cparams(1), name=name)(c_all_t, dsel)


def _sum_parts(parts, name):
    R = parts.shape[1]

    def body(p_ref, o_ref):
        acc = p_ref[0]
        for j in range(1, NDEV):
            acc = acc + p_ref[j]
        o_ref[...] = acc

    return pl.pallas_call(body, out_shape=SDS((R, LANES), f32), name=name)(parts)


def _adamw(g_parts, w, m, v, name):
    n_parts, R, C = g_parts.shape
    tr = next(t for t in (256, 128, 64, 32, 16, 8) if R % t == 0)
    c1 = 1.0 / (1.0 - ADAM_B1 ** ADAM_STEP)
    c2 = 1.0 / (1.0 - ADAM_B2 ** ADAM_STEP)

    def body(g_ref, w_ref, m_ref, v_ref, go_ref, d_ref, mo_ref, vo_ref):
        g = g_ref[0].astype(f32)
        for j in range(1, n_parts):
            g = g + g_ref[j].astype(f32)
        mn = ADAM_B1 * m_ref[...] + (1.0 - ADAM_B1) * g
        vn = ADAM_B2 * v_ref[...] + (1.0 - ADAM_B2) * (g * g)
        go_ref[...] = g
        mo_ref[...] = mn
        vo_ref[...] = vn
        d_ref[...] = -ADAM_LR * ((mn * c1) / (jnp.sqrt(vn * c2) + ADAM_EPS) + ADAM_WD * w_ref[...])

    til = pl.BlockSpec((tr, C), _row)
    out = SDS((R, C), f32)
    return pl.pallas_call(
        body, grid=(R // tr,), in_specs=[pl.BlockSpec((n_parts, tr, C), lambda i: (0, i, 0)), til, til, til],
        out_specs=[til, til, til, til], out_shape=[out, out, out, out],
        compiler_params=_cparams(1), name=name)(g_parts, w, m, v)


def _pad_rows(flat, cols, mult):
    n = flat.shape[-1]
    rows = -(-n // cols)
    rows = -(-rows // mult) * mult
    pad = [(0, 0)] * (flat.ndim - 1) + [(0, rows * cols - n)]
    return jnp.pad(flat, pad).reshape(flat.shape[:-1] + (rows, cols))


def _split_flat(flat, shapes):
    out, off = {}, 0
    for name, shp in shapes:
        n = 1
        for d in shp:
            n *= d
        out[name] = flat[off:off + n].reshape(shp)
        off += n
    return out


def kernel(x, c, mix_norm_g, mix_ada_w, mix_ada_b, ffn_norm_g, ffn_ada_w, ffn_ada_b, ffn_w_in, ffn_w_out, conv_w_in, conv_b_in, conv_w_dw, conv_b_dw, conv_ln_g, conv_ln_b, conv_w_out, conv_b_out, kv_norm_g, kv_ada_w, kv_ada_b, kv_w, forget_b, attn_w_q, attn_w_o, final_norm_g, loss_target, m_mix_norm_g, m_mix_ada_w, m_mix_ada_b, m_ffn_norm_g, m_ffn_ada_w, m_ffn_ada_b, m_ffn_w_in, m_ffn_w_out, m_conv_w_in, m_conv_b_in, m_conv_w_dw, m_conv_b_dw, m_conv_ln_g, m_conv_ln_b, m_conv_w_out, m_conv_b_out, m_kv_norm_g, m_kv_ada_w, m_kv_ada_b, m_kv_w, m_forget_b, m_attn_w_q, m_attn_w_o, m_final_norm_g, v_mix_norm_g, v_mix_ada_w, v_mix_ada_b, v_ffn_norm_g, v_ffn_ada_w, v_ffn_ada_b, v_ffn_w_in, v_ffn_w_out, v_conv_w_in, v_conv_b_in, v_conv_w_dw, v_conv_b_dw, v_conv_ln_g, v_conv_ln_b, v_conv_w_out, v_conv_b_out, v_kv_norm_g, v_kv_ada_w, v_kv_ada_b, v_kv_w, v_forget_b, v_attn_w_q, v_attn_w_o, v_final_norm_g):
    W = dict(mix_norm_g=mix_norm_g, mix_ada_w=mix_ada_w, mix_ada_b=mix_ada_b, ffn_norm_g=ffn_norm_g, ffn_ada_w=ffn_ada_w, ffn_ada_b=ffn_ada_b, ffn_w_in=ffn_w_in, ffn_w_out=ffn_w_out, conv_w_in=conv_w_in, conv_b_in=conv_b_in, conv_w_dw=conv_w_dw, conv_b_dw=conv_b_dw, conv_ln_g=conv_ln_g, conv_ln_b=conv_ln_b, conv_w_out=conv_w_out, conv_b_out=conv_b_out, kv_norm_g=kv_norm_g, kv_ada_w=kv_ada_w, kv_ada_b=kv_ada_b, kv_w=kv_w, forget_b=forget_b, attn_w_q=attn_w_q, attn_w_o=attn_w_o, final_norm_g=final_norm_g)
    M = dict(mix_norm_g=m_mix_norm_g, mix_ada_w=m_mix_ada_w, mix_ada_b=m_mix_ada_b, ffn_norm_g=m_ffn_norm_g, ffn_ada_w=m_ffn_ada_w, ffn_ada_b=m_ffn_ada_b, ffn_w_in=m_ffn_w_in, ffn_w_out=m_ffn_w_out, conv_w_in=m_conv_w_in, conv_b_in=m_conv_b_in, conv_w_dw=m_conv_w_dw, conv_b_dw=m_conv_b_dw, conv_ln_g=m_conv_ln_g, conv_ln_b=m_conv_ln_b, conv_w_out=m_conv_w_out, conv_b_out=m_conv_b_out, kv_norm_g=m_kv_norm_g, kv_ada_w=m_kv_ada_w, kv_ada_b=m_kv_ada_b, kv_w=m_kv_w, forget_b=m_forget_b, attn_w_q=m_attn_w_q, attn_w_o=m_attn_w_o, final_norm_g=m_final_norm_g)
    V = dict(mix_norm_g=v_mix_norm_g, mix_ada_w=v_mix_ada_w, mix_ada_b=v_mix_ada_b, ffn_norm_g=v_ffn_norm_g, ffn_ada_w=v_ffn_ada_w, ffn_ada_b=v_ffn_ada_b, ffn_w_in=v_ffn_w_in, ffn_w_out=v_ffn_w_out, conv_w_in=v_conv_w_in, conv_b_in=v_conv_b_in, conv_w_dw=v_conv_w_dw, conv_b_dw=v_conv_b_dw, conv_ln_g=v_conv_ln_g, conv_ln_b=v_conv_ln_b, conv_w_out=v_conv_w_out, conv_b_out=v_conv_b_out, kv_norm_g=v_kv_norm_g, kv_ada_w=v_kv_ada_w, kv_ada_b=v_kv_ada_b, kv_w=v_kv_w, forget_b=v_forget_b, attn_w_q=v_attn_w_q, attn_w_o=v_attn_w_o, final_norm_g=v_final_norm_g)
    names = list(W)
    T = x.shape[1]
    me = _my_index()
    x0 = x[0]
    tgt = loss_target[0]
    row = lambda vct: vct.reshape(1, -1)

    small_names = ("conv_b_in", "conv_w_dw", "conv_b_dw", "conv_ln_g", "conv_ln_b", "conv_b_out")
    small_loc = jnp.concatenate([c.reshape(-1)] + [W[n].reshape(-1) for n in small_names])
    sg = _exchange([_pad_rows(small_loc, LANES, 8)], True, "gather_small")[0].reshape(NDEV, -1)
    c_all = sg[:, :D]
    off = D
    b_in = sg[:, off:off + 2 * D // NDEV].reshape(1, 2 * D); off += 2 * D // NDEV
    cl = D // NDEV
    w_dw = sg[:, off:off + KW * cl].reshape(NDEV, KW, cl).transpose(1, 0, 2).reshape(KW, D); off += KW * cl
    w_dw = jnp.pad(w_dw, ((0, HALO - KW), (0, 0))).astype(bf16).astype(f32)
    b_dw = sg[:, off:off + cl].reshape(1, D); off += cl
    ln_g = sg[:, off:off + cl].reshape(1, D); off += cl
    ln_b = sg[:, off:off + cl].reshape(1, D); off += cl
    b_out = sg[:, off:off + cl].reshape(1, D)

    cat_ada = lambda s: jnp.concatenate([s["mix_ada_w"][0], s["mix_ada_w"][1], s["ffn_ada_w"][0], s["ffn_ada_w"][1], s["kv_ada_w"]], axis=1)
    w_cat = cat_ada(W)
    ada_loc = _ada_fwd(c_all, w_cat, "ada_fwd")
    ada_all = _exchange([ada_loc], True, "gather_ada")[0]
    ada_me = lax.dynamic_index_in_dim(ada_all, me, axis=1, keepdims=False)
    ada_bias = (mix_ada_b[0], mix_ada_b[1], ffn_ada_b[0], ffn_ada_b[1], kv_ada_b)
    ada, off = [], 0
    for nl, bias in zip(ADA_LOC, ada_bias):
        full = ada_me[:, off:off + nl].reshape(-1) + bias
        ada.append([row(t) for t in jnp.split(full, full.shape[0] // D)])
        off += nl
    (sh_m0, sc_m0, gt_m0), (sh_m1, sc_m1, gt_m1), (sh_f0, sc_f0, gt_f0), (sh_f1, sc_f1, gt_f1), (sh_kv, sc_kv) = ada

    pad_in = lambda src, l: jnp.pad(src["ffn_w_in"][l], ((0, 0), (0, FSP - FS)))
    rows_a = lambda src: jnp.concatenate([src["ffn_w_out"][0], src["attn_w_q"][0]])
    rows_b = lambda src: jnp.concatenate([src["ffn_w_out"][1], src["attn_w_o"][0]])
    as_bf = lambda arrs: [t.astype(bf16) for t in arrs]
    fo, sq_rows = F // NDEV, D // NDEV
    g_ci, g_co = _exchange(as_bf([conv_w_in[0], conv_w_out[0]]), True, "gather_weights")
    soon = as_bf([pad_in(W, 0)])
    next_ = as_bf([rows_a(W), kv_w])
    late = as_bf([pad_in(W, 1), rows_b(W)])

    def w_out_of(g_r):
        t = g_r[:, :fo].reshape(NDEV // 2, FS, D)
        return jnp.pad(t, ((0, 0), (0, FSP - FS), (0, 0))).reshape(FP, D)

    conv_in_full = g_ci.transpose(1, 0, 2).reshape(D, 2 * D)
    wc_a, wc_g = conv_in_full[:, :D], conv_in_full[:, D:]
    wc_o = g_co.reshape(D, D)
    zeros_d = jnp.zeros((1, D), f32)
    fb = jnp.pad(forget_b, (0, LANES - H)).reshape(1, LANES)

    h0 = _normmod(x0, row(mix_norm_g[0]), sh_m0, sc_m0, "norm_mix0")
    a0, gl0, p0 = _mm_gated(h0, wc_a, wc_g, b_in[:, :D], b_in[:, D:], False, "conv_in")
    q0, s0, w_sh0 = _conv_fwd(p0, w_dw, b_dw, ln_g, ln_b, "conv_dw", ride=soon)
    x1, y0 = _mm_res(s0, wc_o, b_out, x0, gt_m0, "conv_out")

    def ffn_fwd(xin, l, sh, sc, gt, w_sh, w_out=None, ride=()):
        h = _normmod(xin, row(ffn_norm_g[l]), sh, sc, f"norm_ffn{l}")
        ug, uu, act, *got = _ffn_in(h, w_sh, f"ffn_in{l}", ride=ride)
        if w_out is None:
            w_out = w_out_of(got[0])
        xo, y = _mm_res(act, w_out, zeros_d, xin, gt, f"ffn_out{l}")
        return xo, (h, ug, uu, act, y, w_sh, w_out), got

    x2, ffn0, (g_ra, g_kv) = ffn_fwd(x1, 0, sh_f0, sc_f0, gt_f0, w_sh0, ride=next_)
    w_q = g_ra[:, fo:].reshape(D, D)
    kv_full = g_kv.transpose(1, 0, 2).reshape(D, -1)
    w_k, w_v = kv_full[:, :D], kv_full[:, D:2 * D]
    w_f = jnp.pad(kv_full[:, 2 * D:], ((0, 0), (0, LANES - H)))

    hk = _normmod(x2, row(kv_norm_g), sh_kv, sc_kv, "norm_kv")
    k_sh = _mm(hk, w_k, bf16, 1.0, "proj_k")
    v_sh = _mm(hk, w_v, bf16, 1.0, "proj_v")
    fl = _mm(hk, w_f, f32, 1.0, "proj_f")
    cum = _forget_fwd(fl, fb, "forget_fwd")
    cum_t = cum[:, :H].T.reshape(NP, 2, T)

    h2 = _normmod(x2, row(mix_norm_g[1]), sh_m1, sc_m1, "norm_mix1")
    qh = _mm(h2, w_q, bf16, HD ** -0.5, "proj_q")
    o, o32, lse, w_sh1, g_rb = _attn_fwd(qh, k_sh, v_sh, cum_t, "attn_fwd", ride=late)
    w_out1, w_o = w_out_of(g_rb), g_rb[:, fo:].reshape(D, D)
    x3, y1 = _mm_res(o, w_o, zeros_d, x2, gt_m1, "attn_out")

    x4, ffn1, _ = ffn_fwd(x3, 1, sh_f1, sc_f1, gt_f1, w_sh1, w_out1)

    dx4, acc_fin = _final_bwd(x4, row(final_norm_g), tgt, "final_bwd")

    d_ada = {}
    by_rows = lambda g: g.reshape(NDEV, sq_rows, D)

    def ffn_bwd(dx_out, xin, l, sc, gt, saved):
        h, ug, uu, act, y, w_sh, w_out = saved
        dyb, acc_r = _res_in(dx_out, y, gt, f"ffn_res_bwd{l}")
        dug, duu = _mm_nt_swiglu(dyb, w_out, ug, uu, f"ffn_dact{l}")
        g_out = _mm_tn(act, dyb, f"ffn_dw_out{l}").reshape(NDEV // 2, FSP, D)[:, :FS].reshape(NDEV, fo, D)
        g_in = _mm_tn_shards(h, duu, FSP, f"ffn_dw_up{l}", into=_mm_tn_shards(h, dug, FSP, f"ffn_dw_gate{l}"))
        dh = _ffn_dh(dug, duu, w_sh, f"ffn_dh{l}")
        dxi, acc_n = _normmod_bwd(dh, xin, row(ffn_norm_g[l]), sc, dx_out, f"norm_ffn_bwd{l}")
        return dxi, g_in, g_out, [acc_n[0:1], acc_n[1:2], acc_r[0:1]], acc_n[2]

    dx3, g_in1, g_out1, d_ada[("ffn", 1)], dg_ffn1 = ffn_bwd(dx4, x3, 1, sc_f1, gt_f1, ffn1)

    dyb, acc_r = _res_in(dx3, y1, gt_m1, "attn_res_bwd")
    do = _mm_nt([(dyb, w_o)], bf16, "attn_do")
    g_wo = _mm_tn(o, dyb, "attn_dw_o")
    st = _attn_stats(do, o32, lse, "attn_stats")
    leave_b = as_bf([g_in1, jnp.concatenate([g_out1, by_rows(g_wo)], axis=1)])
    dq, dk, dv, dck, r_in1, r_rb = _attn_bwd(qh, k_sh, v_sh, do, st, cum_t, "attn_bwd", ride=leave_b)
    g_wq = _mm_tn(h2, dq, "attn_dw_q")
    dh2 = _mm_nt([(dq, w_q)], f32, "attn_dh")
    dx2, acc_n = _normmod_bwd(dh2, x2, row(mix_norm_g[1]), sc_m1, dx3, "norm_mix_bwd1")
    d_ada[("mix", 1)] = [acc_n[0:1], acc_n[1:2], acc_r[0:1]]
    dg_mix1 = acc_n[2]

    dcum = jnp.pad(dck.reshape(H, T).T, ((0, 0), (0, LANES - H)))
    dfl, acc_f = _forget_bwd(dcum, fl, fb, "forget_bwd")
    g_kvw = jnp.concatenate([_mm_tn(hk, dk, "kv_dw_k"), _mm_tn(hk, dv, "kv_dw_v"), _mm_tn(hk, dfl, "kv_dw_f")[:, :H]], axis=1)
    dhk = _mm_nt([(dk, w_k), (dv, w_v), (dfl, w_f)], f32, "kv_dh")
    dx2, acc_n = _normmod_bwd(dhk, x2, row(kv_norm_g), sc_kv, dx2, "norm_kv_bwd")
    d_ada[("kv", 0)] = [acc_n[0:1], acc_n[1:2]]
    dg_kv = acc_n[2]

    dx1, g_in0, g_out0, d_ada[("ffn", 0)], dg_ffn0 = ffn_bwd(dx2, x1, 0, sc_f0, gt_f0, ffn0)

    dyb, acc_r = _res_in(dx1, y0, gt_m0, "conv_res_bwd")
    dsw = _mm_nt([(dyb, wc_o)], f32, "conv_ds")
    g_co_out = _mm_tn(s0, dyb, "conv_dw_out")
    leave_a = as_bf([g_in0, jnp.concatenate([g_out0, by_rows(g_wq)], axis=1), g_kvw.reshape(D, NDEV, -1).transpose(1, 0, 2)])
    da, dgl, acc_c, dw_dw, r_in0, r_ra, r_kv = _conv_bwd(dsw, q0, p0, a0, gl0, w_dw, ln_g, ln_b, "conv_bwd", ride=leave_a)
    cs = 2 * D // NDEV
    g_ci_out = _mm_tn_shards(h0, dgl, cs, "conv_dw_g", into=_mm_tn_shards(h0, da, cs, "conv_dw_a"))
    dh0, r_ci, r_co = _mm_nt([(da, wc_a), (dgl, wc_g)], f32, "conv_dh", ride=as_bf([g_ci_out, by_rows(g_co_out)]))
    dx0, acc_n = _normmod_bwd(dh0, x0, row(mix_norm_g[0]), sc_m0, dx1, "norm_mix_bwd0")
    d_ada[("mix", 0)] = [acc_n[0:1], acc_n[1:2], acc_r[0:1]]
    dg_mix0 = acc_n[2]

    vec = [t.reshape(-1) for key in [(s[0], s[1]) for s in ADA_SEG] for t in d_ada[key]]
    vec += [dg_mix0, dg_mix1, dg_ffn0, dg_ffn1, dg_kv, acc_fin[0]]
    vec += [acc_f[0], acc_fin[1, :LANES]]
    vec += [acc_c[3], acc_c[4], dw_dw[:KW].reshape(-1), acc_c[2], acc_c[0], acc_c[1], acc_r[1]]
    small_parts = _exchange([_pad_rows(jnp.concatenate(vec), LANES, 8)], True, "gather_partials")[0]
    small_sum = _sum_parts(small_parts, "sum_partials").reshape(-1)
    d_ada_all = small_parts.reshape(NDEV, -1)[:, :ADA_TOT]
    off = 0
    gsm = {}
    ada_b_sum = []
    for _, _, n in ADA_SEG:
        ada_b_sum.append(small_sum[off:off + n]); off += n
    gsm["mix_ada_b"] = jnp.stack(ada_b_sum[0:2])
    gsm["ffn_ada_b"] = jnp.stack(ada_b_sum[2:4])
    gsm["kv_ada_b"] = ada_b_sum[4]
    gsm["mix_norm_g"] = small_sum[off:off + 2 * D].reshape(2, D); off += 2 * D
    gsm["ffn_norm_g"] = small_sum[off:off + 2 * D].reshape(2, D); off += 2 * D
    gsm["kv_norm_g"] = small_sum[off:off + D]; off += D
    gsm["final_norm_g"] = small_sum[off:off + D]; off += D
    gsm["forget_b"] = small_sum[off:off + H]; off += LANES
    loss = small_sum[off]; off += LANES
    sl = lambda full, width: lax.dynamic_slice_in_dim(full, me * width, width, axis=full.ndim - 1)
    gsm["conv_b_in"] = sl(small_sum[off:off + 2 * D].reshape(1, 2 * D), 2 * D // NDEV); off += 2 * D
    gsm["conv_w_dw"] = sl(small_sum[off:off + KW * D].reshape(1, KW, D), cl); off += KW * D
    for n in ("conv_b_dw", "conv_ln_g", "conv_ln_b", "conv_b_out"):
        gsm[n] = sl(small_sum[off:off + D].reshape(1, D), cl); off += D

    dsel, off = [], 0
    for (_, _, n), nl in zip(ADA_SEG, ADA_LOC):
        dsel.append(lax.dynamic_slice_in_dim(d_ada_all[:, off:off + n], me * nl, nl, axis=1)); off += n
    g_ada = _ada_bwd(c_all.T, jnp.concatenate(dsel, axis=1), "ada_bwd")
    res_ada = _adamw(g_ada[None], w_cat, cat_ada(M), cat_ada(V), "adamw_ada")

    res_in0 = _adamw(r_in0, *[pad_in(s, 0) for s in (W, M, V)], "adamw_ffn_in0")
    res_in1 = _adamw(r_in1, *[pad_in(s, 1) for s in (W, M, V)], "adamw_ffn_in1")
    res_ra = _adamw(r_ra, *[rows_a(s) for s in (W, M, V)], "adamw_rows_a")
    res_rb = _adamw(r_rb, *[rows_b(s) for s in (W, M, V)], "adamw_rows_b")
    res_ci = _adamw(r_ci, *[s["conv_w_in"][0] for s in (W, M, V)], "adamw_conv_in")
    res_co = _adamw(r_co, *[s["conv_w_out"][0] for s in (W, M, V)], "adamw_conv_out")
    res_kv = _adamw(r_kv, *[s["kv_w"] for s in (W, M, V)], "adamw_kv")
    rest = [n for n in names if n not in MAIN and n not in ("mix_ada_w", "ffn_ada_w", "kv_ada_w")]
    pack_rest = lambda src: _pad_rows(jnp.concatenate([src[n].reshape(-1) for n in rest]), D, 256)
    res_rest = _adamw(pack_rest(gsm)[None], pack_rest(W), pack_rest(M), pack_rest(V), "adamw_rest")

    outs = []
    for k in range(4):
        ur = _split_flat(res_rest[k].reshape(-1), [(n, W[n].shape) for n in rest])
        ra, a0_, a2_ = res_ada[k], ADA_LOC[0], ADA_LOC[2]
        ur["mix_ada_w"] = jnp.stack([ra[:, 0:a0_], ra[:, a0_:2 * a0_]])
        ur["ffn_ada_w"] = jnp.stack([ra[:, 2 * a0_:2 * a0_ + a2_], ra[:, 2 * a0_ + a2_:2 * a0_ + 2 * a2_]])
        ur["kv_ada_w"] = ra[:, 2 * a0_ + 2 * a2_:]
        ur["ffn_w_in"] = jnp.stack([res_in0[k][:, :FS], res_in1[k][:, :FS]])
        ur["conv_w_in"] = res_ci[k][None]
        ur["conv_w_out"] = res_co[k][None]
        ur["kv_w"] = res_kv[k]
        ur["ffn_w_out"] = jnp.stack([res_ra[k][:fo], res_rb[k][:fo]])
        ur["attn_w_q"] = res_ra[k][fo:][None]
        ur["attn_w_o"] = res_rb[k][fo:][None]
        outs.append(ur)
    grads, deltas, new_m, new_v = outs
    return (loss, dx0[None], *[grads[n] for n in names], *[deltas[n] for n in names],
            *[new_m[n] for n in names], *[new_v[n] for n in names])
```

```python
import functools

import jax
import jax.numpy as jnp
from jax import lax
from jax.experimental import pallas as pl
from jax.experimental.pallas import tpu as pltpu

f32, bf16 = jnp.float32, jnp.bfloat16
SDS = jax.ShapeDtypeStruct

D = 1024
F = 2816
H = 16
HD = 64
NP = H // 2
KW = 31
HALO = 32
NDEV = 8
FS = 2 * F // NDEV
FSP = 768
FP = 4 * FSP
EPS = 1e-6
NEG = -1e30
LANES = 128

ADAM_LR, ADAM_B1, ADAM_B2, ADAM_EPS, ADAM_WD, ADAM_STEP = 0.001, 0.9, 0.999, 1e-08, 0.01, 10

TM = 512
TMM = 1024
TC = 256
TQ = 1024
VMEM_LIMIT = 56 << 20

MAIN = ("ffn_w_in", "ffn_w_out", "conv_w_in", "conv_w_out", "kv_w", "attn_w_q", "attn_w_o")
ADA_SEG = (("mix", 0, 3 * D), ("mix", 1, 3 * D), ("ffn", 0, 3 * D), ("ffn", 1, 3 * D), ("kv", 0, 2 * D))
ADA_LOC = tuple(n // NDEV for _, _, n in ADA_SEG)
ADA_COLS = sum(ADA_LOC)
ADA_TOT = sum(n for _, _, n in ADA_SEG)


def _cparams(n_axes):
    return pltpu.CompilerParams(dimension_semantics=("arbitrary",) * n_axes, vmem_limit_bytes=VMEM_LIMIT)


def _mesh_pos():
    return lax.axis_index("x"), lax.axis_index("y"), lax.axis_index("c")


def _my_index():
    mx, my, mc = _mesh_pos()
    return 4 * mx + 2 * my + mc


def _peer(k, mx, my, mc):
    px = (1 - mx) if k & 4 else mx
    py = (1 - my) if k & 2 else my
    pc = (1 - mc) if k & 1 else mc
    return (px, py, pc), 4 * px + 2 * py + pc


def _exchange_copies(x_refs, o_refs, sems, gather):
    send_sems, recv_sems, local_sems = sems
    mx, my, mc = _mesh_pos()
    me = 4 * mx + 2 * my + mc
    copies = []
    for a, (x_ref, o_ref) in enumerate(zip(x_refs, o_refs)):
        copies.append(pltpu.make_async_copy(x_ref if gather else x_ref.at[me], o_ref.at[me], local_sems.at[a]))
        for k in range(1, NDEV):
            peer, pidx = _peer(k, mx, my, mc)
            sem = a * (NDEV - 1) + k - 1
            copies.append(pltpu.make_async_remote_copy(
                src_ref=x_ref if gather else x_ref.at[pidx], dst_ref=o_ref.at[me],
                send_sem=send_sems.at[sem], recv_sem=recv_sems.at[sem],
                device_id=peer, device_id_type=pl.DeviceIdType.MESH))
    return copies


def _exchange_shapes(xs, gather):
    n = len(xs)
    hbm = pl.BlockSpec(memory_space=pl.ANY)
    outs = [SDS((NDEV,) + tuple(x.shape if gather else x.shape[1:]), x.dtype) for x in xs]
    sems = [pltpu.SemaphoreType.DMA((n * (NDEV - 1),)), pltpu.SemaphoreType.DMA((n * (NDEV - 1),)), pltpu.SemaphoreType.DMA((n,))]
    return [hbm] * n, [hbm] * n, outs, sems


def _exchange_start(x_refs, o_refs, sems, gather, first):
    @pl.when(first)
    def _():
        for cp in _exchange_copies(x_refs, o_refs, sems, gather):
            cp.start()


def _exchange_wait(x_refs, o_refs, sems, gather, last):
    @pl.when(last)
    def _():
        for cp in _exchange_copies(x_refs, o_refs, sems, gather):
            cp.wait()


def _exchange(xs, gather, name):
    n = len(xs)

    def body(*refs):
        copies = _exchange_copies(refs[:n], refs[n:2 * n], refs[2 * n:], gather)
        for cp in copies:
            cp.start()
        for cp in copies:
            cp.wait()

    in_specs, out_specs, outs, sems = _exchange_shapes(xs, gather)
    return pl.pallas_call(body, out_shape=outs, in_specs=in_specs, out_specs=out_specs, scratch_shapes=sems, name=name)(*xs)


def _row(i):
    return (i, 0)


def _fix(i):
    return (0, 0)


def _normmod(x, g, shift, scale, name):
    T = x.shape[0]
    tm = min(TM, T)

    def body(x_ref, g_ref, sh_ref, sc_ref, h_ref):
        xv = x_ref[...]
        r = lax.rsqrt(jnp.mean(xv * xv, axis=-1, keepdims=True) + EPS)
        hn = (xv * r) * g_ref[...]
        h_ref[...] = (hn * (1.0 + sc_ref[...]) + sh_ref[...]).astype(bf16)

    vec = pl.BlockSpec((1, D), _fix)
    return pl.pallas_call(
        body, grid=(T // tm,), in_specs=[pl.BlockSpec((tm, D), _row), vec, vec, vec],
        out_specs=pl.BlockSpec((tm, D), _row), out_shape=SDS((T, D), bf16),
        compiler_params=_cparams(1), name=name)(x, g, shift, scale)


def _normmod_bwd(dh, x, g, scale, dx_res, name):
    T = x.shape[0]
    tm = min(TM, T)

    def body(dh_ref, x_ref, g_ref, sc_ref, res_ref, dx_ref, acc_ref):
        @pl.when(pl.program_id(0) == 0)
        def _():
            acc_ref[...] = jnp.zeros_like(acc_ref)
        xv = x_ref[...]
        dhv = dh_ref[...]
        gv = g_ref[...]
        r = lax.rsqrt(jnp.mean(xv * xv, axis=-1, keepdims=True) + EPS)
        xn = xv * r
        dhn = dhv * (1.0 + sc_ref[...])
        dxn = dhn * gv
        dx_ref[...] = res_ref[...] + r * (dxn - xn * jnp.mean(dxn * xn, axis=-1, keepdims=True))
        acc_ref[0:1, :] += jnp.sum(dhv, axis=0, keepdims=True)
        acc_ref[1:2, :] += jnp.sum(dhv * (xn * gv), axis=0, keepdims=True)
        acc_ref[2:3, :] += jnp.sum(dhn * xn, axis=0, keepdims=True)

    vec = pl.BlockSpec((1, D), _fix)
    til = pl.BlockSpec((tm, D), _row)
    return pl.pallas_call(
        body, grid=(T // tm,), in_specs=[til, til, vec, vec, til],
        out_specs=[til, pl.BlockSpec((8, D), _fix)], out_shape=[SDS((T, D), f32), SDS((8, D), f32)],
        compiler_params=_cparams(1), name=name)(dh, x, g, scale, dx_res)


def _res_in(dx, y, gate, name):
    T = dx.shape[0]
    tm = min(TM, T)

    def body(dx_ref, y_ref, gt_ref, dy_ref, acc_ref):
        @pl.when(pl.program_id(0) == 0)
        def _():
            acc_ref[...] = jnp.zeros_like(acc_ref)
        dxv = dx_ref[...]
        dy = dxv * gt_ref[...]
        dy_ref[...] = dy.astype(bf16)
        acc_ref[0:1, :] += jnp.sum(dxv * y_ref[...].astype(f32), axis=0, keepdims=True)
        acc_ref[1:2, :] += jnp.sum(dy, axis=0, keepdims=True)

    til = pl.BlockSpec((tm, D), _row)
    return pl.pallas_call(
        body, grid=(T // tm,), in_specs=[til, til, pl.BlockSpec((1, D), _fix)],
        out_specs=[til, pl.BlockSpec((8, D), _fix)], out_shape=[SDS((T, D), bf16), SDS((8, D), f32)],
        compiler_params=_cparams(1), name=name)(dx, y, gate)


def _final_bwd(x, g, tgt, name):
    T = x.shape[0]
    tm = min(TM, T)

    def body(x_ref, g_ref, t_ref, dx_ref, acc_ref):
        @pl.when(pl.program_id(0) == 0)
        def _():
            acc_ref[...] = jnp.zeros_like(acc_ref)
        xv = x_ref[...]
        gv = g_ref[...]
        r = lax.rsqrt(jnp.mean(xv * xv, axis=-1, keepdims=True) + EPS)
        xn = xv * r
        err = xn * gv - t_ref[...]
        dy = err * (1.0 / D)
        dxn = dy * gv
        dx_ref[...] = r * (dxn - xn * jnp.mean(dxn * xn, axis=-1, keepdims=True))
        acc_ref[0:1, :] += jnp.sum(dy * xn, axis=0, keepdims=True)
        acc_ref[1:2, :] += 0.5 * jnp.sum(jnp.mean(err * err, axis=-1, keepdims=True))

    til = pl.BlockSpec((tm, D), _row)
    return pl.pallas_call(
        body, grid=(T // tm,), in_specs=[til, pl.BlockSpec((1, D), _fix), til],
        out_specs=[til, pl.BlockSpec((8, D), _fix)], out_shape=[SDS((T, D), f32), SDS((8, D), f32)],
        compiler_params=_cparams(1), name=name)(x, g, tgt)


def _col_tile(n):
    if n <= 1024:
        return n
    return 1408 if n % 1408 == 0 else 1024


def _mm_gated(h, wa, wb, ba, bb, swiglu, name):
    T, K = h.shape
    N = wa.shape[1]
    tm, tn = min(TMM, T), _col_tile(N)

    def body(h_ref, wa_ref, wb_ref, ba_ref, bb_ref, u_ref, w_ref, p_ref):
        hv = h_ref[...]
        u = jnp.dot(hv, wa_ref[...], preferred_element_type=f32) + ba_ref[...]
        w = jnp.dot(hv, wb_ref[...], preferred_element_type=f32) + bb_ref[...]
        u_ref[...] = u
        w_ref[...] = w
        if swiglu:
            p_ref[...] = ((u * jax.nn.sigmoid(u)) * w).astype(p_ref.dtype)
        else:
            p_ref[...] = (u * jax.nn.sigmoid(w)).astype(p_ref.dtype)

    wsp = pl.BlockSpec((K, tn), lambda i, j: (0, j))
    bsp = pl.BlockSpec((1, tn), lambda i, j: (0, j))
    osp = pl.BlockSpec((tm, tn), lambda i, j: (i, j))
    return pl.pallas_call(
        body, grid=(T // tm, N // tn), in_specs=[pl.BlockSpec((tm, K), lambda i, j: (i, 0)), wsp, wsp, bsp, bsp],
        out_specs=[osp, osp, osp], out_shape=[SDS((T, N), f32), SDS((T, N), f32), SDS((T, N), bf16)],
        compiler_params=_cparams(2), name=name)(h, wa, wb, ba, bb)


def _mm_res(a, w, b, x_in, gate, name):
    T, K = a.shape
    N = w.shape[1]
    tm, tn = min(TMM, T), _col_tile(N)

    def body(a_ref, w_ref, b_ref, x_ref, gt_ref, xo_ref, y_ref):
        y = jnp.dot(a_ref[...], w_ref[...], preferred_element_type=f32) + b_ref[...]
        y_ref[...] = y.astype(bf16)
        xo_ref[...] = x_ref[...] + gt_ref[...] * y

    vsp = pl.BlockSpec((1, tn), lambda i, j: (0, j))
    osp = pl.BlockSpec((tm, tn), lambda i, j: (i, j))
    return pl.pallas_call(
        body, grid=(T // tm, N // tn),
        in_specs=[pl.BlockSpec((tm, K), lambda i, j: (i, 0)), pl.BlockSpec((K, tn), lambda i, j: (0, j)), vsp, osp, vsp],
        out_specs=[osp, osp], out_shape=[SDS((T, N), f32), SDS((T, N), bf16)],
        compiler_params=_cparams(2), name=name)(a, w, b, x_in, gate)


def _mm(a, w, out_dtype, out_scale, name):
    T, K = a.shape
    N = w.shape[1]
    tm, tn = min(TMM, T), _col_tile(N)

    def body(a_ref, w_ref, o_ref):
        y = jnp.dot(a_ref[...], w_ref[...], preferred_element_type=f32)
        if out_scale != 1.0:
            y = y * out_scale
        o_ref[...] = y.astype(out_dtype)

    return pl.pallas_call(
        body, grid=(T // tm, N // tn),
        in_specs=[pl.BlockSpec((tm, K), lambda i, j: (i, 0)), pl.BlockSpec((K, tn), lambda i, j: (0, j))],
        out_specs=pl.BlockSpec((tm, tn), lambda i, j: (i, j)), out_shape=SDS((T, N), out_dtype),
        compiler_params=_cparams(2), name=name)(a, w)


def _dot_nt(a, b):
    return lax.dot_general(a, b, (((1,), (1,)), ((), ())), preferred_element_type=f32)


def _dot_tn(a, b):
    return lax.dot_general(a, b, (((0,), (0,)), ((), ())), preferred_element_type=f32)


def _mm_nt(pairs, out_dtype, name, ride=()):
    T = pairs[0][0].shape[0]
    K = pairs[0][1].shape[0]
    tm, tk = min(TMM, T), _col_tile(K)
    n = len(pairs)
    nx = len(ride)
    gi, gj = T // tm, K // tk

    def body(*refs):
        x_refs, o_ref, xo_refs, sems = refs[2 * n:2 * n + nx], refs[2 * n + nx], refs[2 * n + nx + 1:2 * n + 2 * nx + 1], refs[2 * n + 2 * nx + 1:]
        i, j = pl.program_id(0), pl.program_id(1)
        if nx:
            _exchange_start(x_refs, xo_refs, sems, False, (i == 0) & (j == 0))
        acc = None
        for a in range(n):
            part = _dot_nt(refs[2 * a][...].astype(bf16), refs[2 * a + 1][...])
            acc = part if acc is None else acc + part
        o_ref[...] = acc.astype(out_dtype)
        if nx:
            _exchange_wait(x_refs, xo_refs, sems, False, (i == gi - 1) & (j == gj - 1))

    in_specs, args = [], []
    for dy, w in pairs:
        ni = dy.shape[1]
        in_specs += [pl.BlockSpec((tm, ni), lambda i, j: (i, 0)), pl.BlockSpec((tk, ni), lambda i, j: (j, 0))]
        args += [dy, w]
    x_in, x_out, x_shape, x_sems = _exchange_shapes(ride, False) if nx else ([], [], [], [])
    out = pl.pallas_call(
        body, grid=(gi, gj), in_specs=in_specs + x_in,
        out_specs=[pl.BlockSpec((tm, tk), lambda i, j: (i, j))] + x_out, out_shape=[SDS((T, K), out_dtype)] + x_shape,
        scratch_shapes=x_sems, compiler_params=_cparams(2), name=name)(*args, *ride)
    return out if nx else out[0]


def _mm_nt_swiglu(dy, w, ug, uu, name):
    T, N = dy.shape
    K = w.shape[0]
    tm, tk = min(TMM, T), _col_tile(K)

    def body(dy_ref, w_ref, ug_ref, uu_ref, dug_ref, duu_ref):
        dact = _dot_nt(dy_ref[...], w_ref[...])
        g = ug_ref[...].astype(f32)
        u = uu_ref[...].astype(f32)
        sg = jax.nn.sigmoid(g)
        duu_ref[...] = (dact * (g * sg)).astype(bf16)
        dug_ref[...] = (dact * u * (sg * (1.0 + g * (1.0 - sg)))).astype(bf16)

    osp = pl.BlockSpec((tm, tk), lambda i, j: (i, j))
    return pl.pallas_call(
        body, grid=(T // tm, K // tk),
        in_specs=[pl.BlockSpec((tm, N), lambda i, j: (i, 0)), pl.BlockSpec((tk, N), lambda i, j: (j, 0)), osp, osp],
        out_specs=[osp, osp], out_shape=[SDS((T, K), bf16), SDS((T, K), bf16)],
        compiler_params=_cparams(2), name=name)(dy, w, ug, uu)


def _mm_tn(a, b, name):
    T, K = a.shape
    N = b.shape[1]
    tt = min(TMM, T)
    tk = K if K <= 1024 else _col_tile(K)
    tn = N if N <= 1024 else _col_tile(N)

    def body(a_ref, b_ref, o_ref):
        @pl.when(pl.program_id(2) == 0)
        def _():
            o_ref[...] = jnp.zeros_like(o_ref)
        o_ref[...] += _dot_tn(a_ref[...].astype(bf16), b_ref[...].astype(bf16))

    return pl.pallas_call(
        body, grid=(K // tk, N // tn, T // tt),
        in_specs=[pl.BlockSpec((tt, tk), lambda i, j, t: (t, i)), pl.BlockSpec((tt, tn), lambda i, j, t: (t, j))],
        out_specs=pl.BlockSpec((tk, tn), lambda i, j, t: (i, j)), out_shape=SDS((K, N), f32),
        compiler_params=_cparams(3), name=name)(a, b)


def _mm_tn_shards(a, b, c, name, into=None):
    T, K = a.shape
    half = NDEV // 2
    assert b.shape[1] == half * c
    tt = min(TMM, T)
    nt = T // tt
    first = into is None

    def body(a_ref, b_ref, *rest):
        o_ref, acc = rest[-2], rest[-1]
        t = pl.program_id(1)

        @pl.when(t == 0)
        def _():
            acc[...] = jnp.zeros_like(acc)
        acc[...] += _dot_tn(a_ref[...].astype(bf16), b_ref[...].astype(bf16))

        @pl.when(t == nt - 1)
        def _():
            o_ref[...] = acc[...].astype(bf16)

    in_specs = [pl.BlockSpec((tt, K), lambda j, t: (t, 0)), pl.BlockSpec((tt, c), lambda j, t: (t, j))]
    args = [a, b]
    if not first:
        in_specs.append(pl.BlockSpec(memory_space=pl.ANY))
        args.append(into)
    base = 0 if first else half
    return pl.pallas_call(
        body, grid=(half, nt), in_specs=in_specs,
        out_specs=pl.BlockSpec((None, K, c), lambda j, t: (j + base, 0, 0)), out_shape=SDS((NDEV, K, c), bf16),
        scratch_shapes=[pltpu.VMEM((K, c), f32)], input_output_aliases={} if first else {2: 0},
        compiler_params=_cparams(2), name=name)(*args)


def _ffn_in(h, w_sh, name, ride=()):
    T = h.shape[0]
    tm = min(TMM, T)
    nx = len(ride)
    gi, gj = T // tm, NDEV // 2

    def body(h_ref, wg_ref, wu_ref, *rest):
        x_refs, (ug_ref, uu_ref, act_ref), xo_refs, sems = rest[:nx], rest[nx:nx + 3], rest[nx + 3:2 * nx + 3], rest[2 * nx + 3:]
        i, j = pl.program_id(0), pl.program_id(1)
        if nx:
            _exchange_start(x_refs, xo_refs, sems, True, (i == 0) & (j == 0))
        hv = h_ref[...]
        ug = jnp.dot(hv, wg_ref[...], preferred_element_type=f32)
        uu = jnp.dot(hv, wu_ref[...], preferred_element_type=f32)
        ug_ref[...] = ug.astype(bf16)
        uu_ref[...] = uu.astype(bf16)
        act_ref[...] = ((ug * jax.nn.sigmoid(ug)) * uu).astype(bf16)
        if nx:
            _exchange_wait(x_refs, xo_refs, sems, True, (i == gi - 1) & (j == gj - 1))

    osp = pl.BlockSpec((tm, FSP), lambda i, j: (i, j))
    x_in, x_out, x_shape, x_sems = _exchange_shapes(ride, True) if nx else ([], [], [], [])
    return pl.pallas_call(
        body, grid=(gi, gj),
        in_specs=[pl.BlockSpec((tm, D), lambda i, j: (i, 0)),
                  pl.BlockSpec((None, D, FSP), lambda i, j: (j, 0, 0)),
                  pl.BlockSpec((None, D, FSP), lambda i, j: (j + NDEV // 2, 0, 0))] + x_in,
        out_specs=[osp, osp, osp] + x_out,
        out_shape=[SDS((T, FP), bf16), SDS((T, FP), bf16), SDS((T, FP), bf16)] + x_shape,
        scratch_shapes=x_sems, compiler_params=_cparams(2), name=name)(h, w_sh, w_sh, *ride)


def _ffn_dh(dug, duu, w_sh, name):
    T = dug.shape[0]
    tm, tk = min(TMM, T), 512
    half = NDEV // 2

    def body(dg_ref, du_ref, wg_ref, wu_ref, o_ref):
        acc = None
        for s in range(half):
            cols = slice(s * FSP, (s + 1) * FSP)
            part = _dot_nt(dg_ref[:, cols], wg_ref[s]) + _dot_nt(du_ref[:, cols], wu_ref[s])
            acc = part if acc is None else acc + part
        o_ref[...] = acc

    dsp = pl.BlockSpec((tm, FP), lambda i, j: (i, 0))
    return pl.pallas_call(
        body, grid=(T // tm, D // tk),
        in_specs=[dsp, dsp, pl.BlockSpec((half, tk, FSP), lambda i, j: (0, j, 0)),
                  pl.BlockSpec((half, tk, FSP), lambda i, j: (1, j, 0))],
        out_specs=pl.BlockSpec((tm, tk), lambda i, j: (i, j)), out_shape=SDS((T, D), f32),
        compiler_params=_cparams(2), name=name)(dug, duu, w_sh, w_sh)


def _layernorm_parts(qv, g, b):
    mu = jnp.mean(qv, axis=-1, keepdims=True)
    cen = qv - mu
    rstd = lax.rsqrt(jnp.mean(cen * cen, axis=-1, keepdims=True) + EPS)
    z = cen * rstd
    return z, rstd, z * g + b


def _tap_groups(offsets):
    groups = {}
    for k, o in enumerate(offsets):
        groups.setdefault(o % 8, []).append((k, o - o % 8))
    return groups


def _conv_fwd(p, w_dw, b_dw, ln_g, ln_b, name, ride=()):
    T = p.shape[0]
    tc = min(TC, T)
    n = T // tc
    nx = len(ride)

    def body(p_ref, w_ref, b_ref, g_ref, bb_ref, *rest):
        x_refs, (q_ref, s_ref), xo_refs = rest[:nx], rest[nx:nx + 2], rest[nx + 2:2 * nx + 2]
        (ext, sh), sems = rest[2 * nx + 2:2 * nx + 4], rest[2 * nx + 4:]
        i = pl.program_id(0)
        if nx:
            _exchange_start(x_refs, xo_refs, sems, True, i == 0)

        @pl.when(i == 0)
        def _():
            ext[0:HALO, :] = jnp.zeros((HALO, D), f32)

        @pl.when(i > 0)
        def _():
            ext[0:HALO, :] = ext[tc:tc + HALO, :]

        ext[HALO:HALO + tc, :] = p_ref[...].astype(f32)
        groups = _tap_groups([HALO - (KW - 1) + k for k in range(KW)])
        for cb in range(D // LANES):
            cols = slice(cb * LANES, (cb + 1) * LANES)
            acc = jnp.zeros((tc, LANES), f32)
            for r, taps in groups.items():
                span = max(base for _, base in taps) + tc
                sh[0:span, :] = ext[r:r + span, cols]
                for k, base in taps:
                    acc = acc + w_ref[k:k + 1, cols] * sh[base:base + tc, :]
            q_ref[:, cols] = acc + b_ref[:, cols]
        _, _, l = _layernorm_parts(q_ref[...], g_ref[...], bb_ref[...])
        s_ref[...] = (l * jax.nn.sigmoid(l)).astype(bf16)
        if nx:
            _exchange_wait(x_refs, xo_refs, sems, True, i == n - 1)

    vec = pl.BlockSpec((1, D), _fix)
    til = pl.BlockSpec((tc, D), _row)
    x_in, x_out, x_shape, x_sems = _exchange_shapes(ride, True) if nx else ([], [], [], [])
    return pl.pallas_call(
        body, grid=(n,), in_specs=[til, pl.BlockSpec((HALO, D), _fix), vec, vec, vec] + x_in,
        out_specs=[til, til] + x_out, out_shape=[SDS((T, D), f32), SDS((T, D), bf16)] + x_shape,
        scratch_shapes=[pltpu.VMEM((tc + HALO, D), f32), pltpu.VMEM((tc + HALO, LANES), f32)] + x_sems, compiler_params=_cparams(1),
        name=name)(p, w_dw, b_dw, ln_g, ln_b, *ride)


def _conv_bwd(ds, q, p, a, gl, w_dw, ln_g, ln_b, name, ride=()):
    T = q.shape[0]
    tc = min(TC, T)
    n = T // tc
    nx = len(ride)

    def body(ds_ref, q_ref, p_ref, a_ref, gl_ref, w_ref, g_ref, bb_ref, *rest):
        x_refs, (da_ref, dgl_ref, acc_ref, dw_ref), xo_refs = rest[:nx], rest[nx:nx + 4], rest[nx + 4:2 * nx + 4]
        (ext, sh), sems = rest[2 * nx + 4:2 * nx + 6], rest[2 * nx + 6:]
        i = pl.program_id(0)
        if nx:
            _exchange_start(x_refs, xo_refs, sems, False, i == 0)

        @pl.when(i == 0)
        def _():
            acc_ref[...] = jnp.zeros_like(acc_ref)
            dw_ref[...] = jnp.zeros_like(dw_ref)
            ext[tc:tc + HALO, :] = jnp.zeros((HALO, D), f32)

        @pl.when(i > 0)
        def _():
            ext[tc:tc + HALO, :] = ext[0:HALO, :]

        gv = g_ref[...]
        z, rstd, l = _layernorm_parts(q_ref[...], gv, bb_ref[...])
        sg = jax.nn.sigmoid(l)
        dl = ds_ref[...] * (sg * (1.0 + l * (1.0 - sg)))
        dz = dl * gv
        dq = rstd * (dz - jnp.mean(dz, axis=-1, keepdims=True) - z * jnp.mean(dz * z, axis=-1, keepdims=True))
        ext[0:tc, :] = dq.astype(bf16).astype(f32)
        acc_ref[0:1, :] += jnp.sum(dl * z, axis=0, keepdims=True)
        acc_ref[1:2, :] += jnp.sum(dl, axis=0, keepdims=True)
        acc_ref[2:3, :] += jnp.sum(dq, axis=0, keepdims=True)
        groups = _tap_groups([KW - 1 - k for k in range(KW)])
        for cb in range(D // LANES):
            cols = slice(cb * LANES, (cb + 1) * LANES)
            pc = p_ref[:, cols].astype(f32)
            dp = jnp.zeros((tc, LANES), f32)
            for r, taps in groups.items():
                span = max(base for _, base in taps) + tc
                sh[0:span, :] = ext[r:r + span, cols]
                for k, base in taps:
                    sl = sh[base:base + tc, :]
                    dp = dp + w_ref[k:k + 1, cols] * sl
                    dw_ref[k:k + 1, cols] += jnp.sum(sl * pc, axis=0, keepdims=True)
            av = a_ref[:, cols].astype(f32)
            sgl = jax.nn.sigmoid(gl_ref[:, cols].astype(f32))
            da = dp * sgl
            dgl = dp * av * (sgl * (1.0 - sgl))
            da_ref[:, cols] = da.astype(bf16)
            dgl_ref[:, cols] = dgl.astype(bf16)
            acc_ref[3:4, cols] += jnp.sum(da, axis=0, keepdims=True)
            acc_ref[4:5, cols] += jnp.sum(dgl, axis=0, keepdims=True)
        if nx:
            _exchange_wait(x_refs, xo_refs, sems, False, i == n - 1)

    rev = lambda i: (n - 1 - i, 0)
    til = pl.BlockSpec((tc, D), rev)
    vec = pl.BlockSpec((1, D), _fix)
    x_in, x_out, x_shape, x_sems = _exchange_shapes(ride, False) if nx else ([], [], [], [])
    return pl.pallas_call(
        body, grid=(n,), in_specs=[til, til, til, til, til, pl.BlockSpec((HALO, D), _fix), vec, vec] + x_in,
        out_specs=[til, til, pl.BlockSpec((8, D), _fix), pl.BlockSpec((HALO, D), _fix)] + x_out,
        out_shape=[SDS((T, D), bf16), SDS((T, D), bf16), SDS((8, D), f32), SDS((HALO, D), f32)] + x_shape,
        scratch_shapes=[pltpu.VMEM((tc + HALO, D), f32), pltpu.VMEM((tc + HALO, LANES), f32)] + x_sems, compiler_params=_cparams(1),
        name=name)(ds, q, p, a, gl, w_dw, ln_g, ln_b, *ride)


def _tri(n, upper):
    r = lax.broadcasted_iota(jnp.int32, (n, n), 0)
    c = lax.broadcasted_iota(jnp.int32, (n, n), 1)
    return ((c >= r) if upper else (r >= c)).astype(f32)


def _forget_fwd(fl, fb, name):
    T = fl.shape[0]
    tc = min(TC, T)

    def body(fl_ref, fb_ref, cum_ref, carry):
        @pl.when(pl.program_id(0) == 0)
        def _():
            carry[...] = jnp.zeros_like(carry)
        xv = fl_ref[...] + fb_ref[...]
        lf = jnp.minimum(xv, 0.0) - jnp.log(1.0 + jnp.exp(-jnp.abs(xv)))
        cs = jnp.dot(_tri(tc, False), lf, preferred_element_type=f32, precision=lax.Precision.HIGHEST) + carry[0:1, :]
        cum_ref[...] = cs
        carry[0:1, :] = cs[tc - 1:tc, :]

    til = pl.BlockSpec((tc, LANES), _row)
    return pl.pallas_call(
        body, grid=(T // tc,), in_specs=[til, pl.BlockSpec((1, LANES), _fix)], out_specs=til,
        out_shape=SDS((T, LANES), f32), scratch_shapes=[pltpu.VMEM((8, LANES), f32)],
        compiler_params=_cparams(1), name=name)(fl, fb)


def _forget_bwd(dcum, fl, fb, name):
    T = fl.shape[0]
    tc = min(TC, T)
    n = T // tc

    def body(dc_ref, fl_ref, fb_ref, dfl_ref, acc_ref, carry):
        @pl.when(pl.program_id(0) == 0)
        def _():
            carry[...] = jnp.zeros_like(carry)
            acc_ref[...] = jnp.zeros_like(acc_ref)
        dlf = jnp.dot(_tri(tc, True), dc_ref[...], preferred_element_type=f32, precision=lax.Precision.HIGHEST) + carry[0:1, :]
        carry[0:1, :] = dlf[0:1, :]
        dfl = dlf * (1.0 - jax.nn.sigmoid(fl_ref[...] + fb_ref[...]))
        dfl_ref[...] = dfl.astype(bf16)
        acc_ref[0:1, :] += jnp.sum(dfl, axis=0, keepdims=True)

    rev = lambda i: (n - 1 - i, 0)
    til = pl.BlockSpec((tc, LANES), rev)
    return pl.pallas_call(
        body, grid=(n,), in_specs=[til, til, pl.BlockSpec((1, LANES), _fix)],
        out_specs=[til, pl.BlockSpec((8, LANES), _fix)], out_shape=[SDS((T, LANES), bf16), SDS((8, LANES), f32)],
        scratch_shapes=[pltpu.VMEM((8, LANES), f32)], compiler_params=_cparams(1), name=name)(dcum, fl, fb)


def _causal(s, n):
    r = lax.broadcasted_iota(jnp.int32, (n, n), 0)
    c = lax.broadcasted_iota(jnp.int32, (n, n), 1)
    return jnp.where(c <= r, s, NEG)


def _attn_fwd(q, k, v, cum_t, name, ride=()):
    T = q.shape[0]
    tq = min(TQ, T)
    n = T // tq
    nx = len(ride)

    def body(q_ref, k_ref, v_ref, ck_ref, *rest):
        x_refs, (o_ref, o32_ref, st_ref), xo_refs = rest[:nx], rest[nx:nx + 3], rest[nx + 3:2 * nx + 3]
        (m_sc, l_sc, acc_sc, res_sc), sems = rest[2 * nx + 3:2 * nx + 7], rest[2 * nx + 7:]
        i = pl.program_id(1)
        if nx:
            _exchange_start(x_refs, xo_refs, sems, True, (pl.program_id(0) == 0) & (i == 0))
        lane = lax.broadcasted_iota(jnp.int32, (1, LANES), 1)
        lo = lane < HD
        q2 = q_ref[...]
        zero = jnp.zeros_like(q2)
        qa = (jnp.where(lo, q2, zero), jnp.where(lo, zero, q2))
        m_sc[...] = jnp.full(m_sc.shape, NEG, f32)
        l_sc[...] = jnp.zeros_like(l_sc)
        acc_sc[...] = jnp.zeros_like(acc_sc)
        res_sc[...] = jnp.zeros_like(res_sc)

        def block(j, masked):
            off = pl.multiple_of(j * tq, tq)
            k2 = k_ref[pl.ds(off, tq), :]
            v2 = v_ref[pl.ds(off, tq), :]
            for a in range(2):
                s = _dot_nt(qa[a], k2) - ck_ref[0, a:a + 1, pl.ds(off, tq)]
                if masked:
                    s = _causal(s, tq)
                m_old = m_sc[a]
                m_new = jnp.maximum(m_old, jnp.max(s, axis=1, keepdims=True))
                alpha = jnp.exp(m_old - m_new)
                pm = jnp.exp(s - m_new)
                pb = pm.astype(bf16)
                pr = (pm - pb.astype(f32)).astype(bf16)
                l_sc[a] = alpha * l_sc[a] + jnp.sum(pm, axis=1, keepdims=True)
                acc_sc[a] = alpha * acc_sc[a] + jnp.dot(pb, v2, preferred_element_type=f32)
                res_sc[a] = alpha * res_sc[a] + jnp.dot(pr, v2, preferred_element_type=f32)
                m_sc[a] = m_new

        def step(j, carry):
            block(j, False)
            return carry

        lax.fori_loop(0, i, step, 0)
        block(i, True)
        o_ref[...] = jnp.where(lo, acc_sc[0] / l_sc[0], acc_sc[1] / l_sc[1]).astype(bf16)
        o32_ref[...] = jnp.where(lo, (acc_sc[0] + res_sc[0]) / l_sc[0], (acc_sc[1] + res_sc[1]) / l_sc[1])
        lse0 = m_sc[0] + jnp.log(l_sc[0])
        lse1 = m_sc[1] + jnp.log(l_sc[1])
        st_ref[0] = jnp.where(lane == 0, lse0, jnp.where(lane == 1, lse1, 0.0))
        if nx:
            _exchange_wait(x_refs, xo_refs, sems, True, (pl.program_id(0) == NP - 1) & (i == n - 1))

    full = lambda blk: pl.BlockSpec((T, LANES), blk)
    x_in, x_out, x_shape, x_sems = _exchange_shapes(ride, True) if nx else ([], [], [], [])
    return pl.pallas_call(
        body, grid=(NP, n),
        in_specs=[pl.BlockSpec((tq, LANES), lambda p, i: (i, p)), full(lambda p, i: (0, p)), full(lambda p, i: (0, p)),
                  pl.BlockSpec((1, 2, T), lambda p, i: (p, 0, 0))] + x_in,
        out_specs=[pl.BlockSpec((tq, LANES), lambda p, i: (i, p)), pl.BlockSpec((tq, LANES), lambda p, i: (i, p)),
                   pl.BlockSpec((1, tq, LANES), lambda p, i: (p, i, 0))] + x_out,
        out_shape=[SDS((T, H * HD), bf16), SDS((T, H * HD), f32), SDS((NP, T, LANES), f32)] + x_shape,
        scratch_shapes=[pltpu.VMEM((2, tq, 1), f32), pltpu.VMEM((2, tq, 1), f32), pltpu.VMEM((2, tq, LANES), f32),
                        pltpu.VMEM((2, tq, LANES), f32)] + x_sems,
        compiler_params=_cparams(2), name=name)(q, k, v, cum_t, *ride)


def _attn_stats(do, o, lse, name):
    T = do.shape[0]
    tm = min(4 * TM, T)

    def body(do_ref, o_ref, lse_ref, st_ref):
        lane = lax.broadcasted_iota(jnp.int32, (1, LANES), 1)
        prod = do_ref[...].astype(f32) * o_ref[...].astype(f32)
        d0 = jnp.sum(jnp.where(lane < HD, prod, 0.0), axis=1, keepdims=True)
        d1 = jnp.sum(jnp.where(lane < HD, 0.0, prod), axis=1, keepdims=True)
        st_ref[0] = jnp.where(lane < 2, lse_ref[0], jnp.where(lane == 2, d0, jnp.where(lane == 3, d1, 0.0)))

    til = pl.BlockSpec((tm, LANES), lambda p, i: (i, p))
    stt = pl.BlockSpec((1, tm, LANES), lambda p, i: (p, i, 0))
    return pl.pallas_call(
        body, grid=(NP, T // tm), in_specs=[til, til, stt], out_specs=stt, out_shape=SDS((NP, T, LANES), f32),
        compiler_params=_cparams(2), name=name)(do, o, lse)


def _attn_bwd(q, k, v, do, st, cum_t, name, ride=()):
    T = q.shape[0]
    tq = min(TQ, T)
    n = T // tq
    nx = len(ride)

    def body(q_ref, k_ref, v_ref, do_ref, st_ref, ck_ref, *rest):
        x_refs, (dq_ref, dk_ref, dv_ref, dck_ref), xo_refs = rest[:nx], rest[nx:nx + 4], rest[nx + 4:2 * nx + 4]
        (dk_sc, dv_sc, dck_sc), sems = rest[2 * nx + 4:2 * nx + 7], rest[2 * nx + 7:]
        j = pl.program_id(1)
        if nx:
            _exchange_start(x_refs, xo_refs, sems, False, (pl.program_id(0) == 0) & (j == 0))
        lane = lax.broadcasted_iota(jnp.int32, (1, LANES), 1)
        lo = lane < HD

        @pl.when(j == 0)
        def _():
            dq_ref[...] = jnp.zeros_like(dq_ref)

        k2 = k_ref[...]
        v2 = v_ref[...]
        zero = jnp.zeros_like(k2)
        ka = (jnp.where(lo, k2, zero), jnp.where(lo, zero, k2))
        va = (jnp.where(lo, v2, zero), jnp.where(lo, zero, v2))
        dk_sc[...] = jnp.zeros_like(dk_sc)
        dv_sc[...] = jnp.zeros_like(dv_sc)
        dck_sc[...] = jnp.zeros_like(dck_sc)

        def block(i, masked):
            off = pl.multiple_of(i * tq, tq)
            q2 = q_ref[pl.ds(off, tq), :]
            do2 = do_ref[pl.ds(off, tq), :]
            stt = st_ref[0, pl.ds(off, tq), :]
            parts = []
            for a in range(2):
                s = _dot_nt(q2, ka[a]) - ck_ref[0, a:a + 1, :]
                if masked:
                    s = _causal(s, tq)
                pm = jnp.exp(s - stt[:, a:a + 1])
                dp = _dot_nt(do2, va[a])
                dsm = pm * (dp - stt[:, 2 + a:3 + a])
                dsb = dsm.astype(bf16)
                dv_sc[a] += _dot_tn(pm.astype(bf16), do2)
                dk_sc[a] += _dot_tn(dsb, q2)
                dck_sc[a:a + 1, :] -= jnp.sum(dsm, axis=0, keepdims=True)
                parts.append(jnp.dot(dsb, k2, preferred_element_type=f32))
            dq_ref[pl.ds(off, tq), :] += jnp.where(lo, parts[0], parts[1])

        block(j, True)

        def step(i, carry):
            block(i, False)
            return carry

        lax.fori_loop(j + 1, n, step, 0)
        dk_ref[...] = jnp.where(lo, dk_sc[0], dk_sc[1]).astype(bf16)
        dv_ref[...] = jnp.where(lo, dv_sc[0], dv_sc[1]).astype(bf16)
        dck_ref[0] = dck_sc[0:2, :]

        @pl.when(j == n - 1)
        def _():
            dq_ref[...] = dq_ref[...] * (HD ** -0.5)

        if nx:
            _exchange_wait(x_refs, xo_refs, sems, False, (pl.program_id(0) == NP - 1) & (j == n - 1))

    full = lambda: pl.BlockSpec((T, LANES), lambda p, j: (0, p))
    kvb = lambda: pl.BlockSpec((tq, LANES), lambda p, j: (j, p))
    ckb = lambda: pl.BlockSpec((1, 2, tq), lambda p, j: (p, 0, j))
    x_in, x_out, x_shape, x_sems = _exchange_shapes(ride, False) if nx else ([], [], [], [])
    return pl.pallas_call(
        body, grid=(NP, n),
        in_specs=[full(), kvb(), kvb(), full(), pl.BlockSpec((1, T, LANES), lambda p, j: (p, 0, 0)), ckb()] + x_in,
        out_specs=[full(), kvb(), kvb(), ckb()] + x_out,
        out_shape=[SDS((T, H * HD), f32), SDS((T, H * HD), bf16), SDS((T, H * HD), bf16), SDS((NP, 2, T), f32)] + x_shape,
        scratch_shapes=[pltpu.VMEM((2, tq, LANES), f32), pltpu.VMEM((2, tq, LANES), f32), pltpu.VMEM((8, tq), f32)] + x_sems,
        compiler_params=_cparams(2), name=name)(q, k, v, do, st, cum_t, *ride)


def _ada_fwd(c_all, w_cat, name):
    n = w_cat.shape[1]
    tn = 256

    def body(c_ref, w_ref, o_ref):
        cv = c_ref[...]
        o_ref[...] = jnp.dot((cv * jax.nn.sigmoid(cv)).astype(bf16), w_ref[...].astype(bf16), preferred_element_type=f32)

    return pl.pallas_call(
        body, grid=(n // tn,), in_specs=[pl.BlockSpec((NDEV, D), _fix), pl.BlockSpec((D, tn), lambda i: (0, i))],
        out_specs=pl.BlockSpec((NDEV, tn), lambda i: (0, i)), out_shape=SDS((NDEV, n), f32),
        compiler_params=_cparams(1), name=name)(c_all, w_cat)


def _ada_bwd(c_all_t, dsel, name):
    n = dsel.shape[1]
    tn = 256

    def body(c_ref, d_ref, o_ref):
        cv = c_ref[...]
        ca = cv * jax.nn.sigmoid(cv)
        acc = ca[:, 0:1] * d_ref[0:1, :]
        for b in range(1, NDEV):
            acc = acc + ca[:, b:b + 1] * d_ref[b:b + 1, :]
        o_ref[...] = acc

    return pl.pallas_call(
        body, grid=(n // tn,), in_specs=[pl.BlockSpec((D, NDEV), _fix), pl.BlockSpec((NDEV, tn), lambda i: (0, i))],
        out_specs=pl.BlockSpec((D, tn), lambda i: (0, i)), out_shape=SDS((D, n), f32),
        compiler_params=_cparams(1), name=name)(c_all_t, dsel)


def _sum_parts(parts, name):
    R = parts.shape[1]

    def body(p_ref, o_ref):
        acc = p_ref[0]
        for j in range(1, NDEV):
            acc = acc + p_ref[j]
        o_ref[...] = acc

    return pl.pallas_call(body, out_shape=SDS((R, LANES), f32), name=name)(parts)


def _adamw(g_parts, w, m, v, name):
    n_parts, R, C = g_parts.shape
    tr = next(t for t in (256, 128, 64, 32, 16, 8) if R % t == 0)
    c1 = 1.0 / (1.0 - ADAM_B1 ** ADAM_STEP)
    c2 = 1.0 / (1.0 - ADAM_B2 ** ADAM_STEP)

    def body(g_ref, w_ref, m_ref, v_ref, go_ref, d_ref, mo_ref, vo_ref):
        g = g_ref[0].astype(f32)
        for j in range(1, n_parts):
            g = g + g_ref[j].astype(f32)
        mn = ADAM_B1 * m_ref[...] + (1.0 - ADAM_B1) * g
        vn = ADAM_B2 * v_ref[...] + (1.0 - ADAM_B2) * (g * g)
        go_ref[...] = g
        mo_ref[...] = mn
        vo_ref[...] = vn
        d_ref[...] = -ADAM_LR * ((mn * c1) / (jnp.sqrt(vn * c2) + ADAM_EPS) + ADAM_WD * w_ref[...])

    til = pl.BlockSpec((tr, C), _row)
    out = SDS((R, C), f32)
    return pl.pallas_call(
        body, grid=(R // tr,), in_specs=[pl.BlockSpec((n_parts, tr, C), lambda i: (0, i, 0)), til, til, til],
        out_specs=[til, til, til, til], out_shape=[out, out, out, out],
        compiler_params=_cparams(1), name=name)(g_parts, w, m, v)


def _pad_rows(flat, cols, mult):
    n = flat.shape[-1]
    rows = -(-n // cols)
    rows = -(-rows // mult) * mult
    pad = [(0, 0)] * (flat.ndim - 1) + [(0, rows * cols - n)]
    return jnp.pad(flat, pad).reshape(flat.shape[:-1] + (rows, cols))


def _split_flat(flat, shapes):
    out, off = {}, 0
    for name, shp in shapes:
        n = 1
        for d in shp:
            n *= d
        out[name] = flat[off:off + n].reshape(shp)
        off += n
    return out


def kernel(x, c, mix_norm_g, mix_ada_w, mix_ada_b, ffn_norm_g, ffn_ada_w, ffn_ada_b, ffn_w_in, ffn_w_out, conv_w_in, conv_b_in, conv_w_dw, conv_b_dw, conv_ln_g, conv_ln_b, conv_w_out, conv_b_out, kv_norm_g, kv_ada_w, kv_ada_b, kv_w, forget_b, attn_w_q, attn_w_o, final_norm_g, loss_target, m_mix_norm_g, m_mix_ada_w, m_mix_ada_b, m_ffn_norm_g, m_ffn_ada_w, m_ffn_ada_b, m_ffn_w_in, m_ffn_w_out, m_conv_w_in, m_conv_b_in, m_conv_w_dw, m_conv_b_dw, m_conv_ln_g, m_conv_ln_b, m_conv_w_out, m_conv_b_out, m_kv_norm_g, m_kv_ada_w, m_kv_ada_b, m_kv_w, m_forget_b, m_attn_w_q, m_attn_w_o, m_final_norm_g, v_mix_norm_g, v_mix_ada_w, v_mix_ada_b, v_ffn_norm_g, v_ffn_ada_w, v_ffn_ada_b, v_ffn_w_in, v_ffn_w_out, v_conv_w_in, v_conv_b_in, v_conv_w_dw, v_conv_b_dw, v_conv_ln_g, v_conv_ln_b, v_conv_w_out, v_conv_b_out, v_kv_norm_g, v_kv_ada_w, v_kv_ada_b, v_kv_w, v_forget_b, v_attn_w_q, v_attn_w_o, v_final_norm_g):
    W = dict(mix_norm_g=mix_norm_g, mix_ada_w=mix_ada_w, mix_ada_b=mix_ada_b, ffn_norm_g=ffn_norm_g, ffn_ada_w=ffn_ada_w, ffn_ada_b=ffn_ada_b, ffn_w_in=ffn_w_in, ffn_w_out=ffn_w_out, conv_w_in=conv_w_in, conv_b_in=conv_b_in, conv_w_dw=conv_w_dw, conv_b_dw=conv_b_dw, conv_ln_g=conv_ln_g, conv_ln_b=conv_ln_b, conv_w_out=conv_w_out, conv_b_out=conv_b_out, kv_norm_g=kv_norm_g, kv_ada_w=kv_ada_w, kv_ada_b=kv_ada_b, kv_w=kv_w, forget_b=forget_b, attn_w_q=attn_w_q, attn_w_o=attn_w_o, final_norm_g=final_norm_g)
    M = dict(mix_norm_g=m_mix_norm_g, mix_ada_w=m_mix_ada_w, mix_ada_b=m_mix_ada_b, ffn_norm_g=m_ffn_norm_g, ffn_ada_w=m_ffn_ada_w, ffn_ada_b=m_ffn_ada_b, ffn_w_in=m_ffn_w_in, ffn_w_out=m_ffn_w_out, conv_w_in=m_conv_w_in, conv_b_in=m_conv_b_in, conv_w_dw=m_conv_w_dw, conv_b_dw=m_conv_b_dw, conv_ln_g=m_conv_ln_g, conv_ln_b=m_conv_ln_b, conv_w_out=m_conv_w_out, conv_b_out=m_conv_b_out, kv_norm_g=m_kv_norm_g, kv_ada_w=m_kv_ada_w, kv_ada_b=m_kv_ada_b, kv_w=m_kv_w, forget_b=m_forget_b, attn_w_q=m_attn_w_q, attn_w_o=m_attn_w_o, final_norm_g=m_final_norm_g)
    V = dict(mix_norm_g=v_mix_norm_g, mix_ada_w=v_mix_ada_w, mix_ada_b=v_mix_ada_b, ffn_norm_g=v_ffn_norm_g, ffn_ada_w=v_ffn_ada_w, ffn_ada_b=v_ffn_ada_b, ffn_w_in=v_ffn_w_in, ffn_w_out=v_ffn_w_out, conv_w_in=v_conv_w_in, conv_b_in=v_conv_b_in, conv_w_dw=v_conv_w_dw, conv_b_dw=v_conv_b_dw, conv_ln_g=v_conv_ln_g, conv_ln_b=v_conv_ln_b, conv_w_out=v_conv_w_out, conv_b_out=v_conv_b_out, kv_norm_g=v_kv_norm_g, kv_ada_w=v_kv_ada_w, kv_ada_b=v_kv_ada_b, kv_w=v_kv_w, forget_b=v_forget_b, attn_w_q=v_attn_w_q, attn_w_o=v_attn_w_o, final_norm_g=v_final_norm_g)
    names = list(W)
    T = x.shape[1]
    me = _my_index()
    x0 = x[0]
    tgt = loss_target[0]
    row = lambda vct: vct.reshape(1, -1)

    small_names = ("conv_b_in", "conv_w_dw", "conv_b_dw", "conv_ln_g", "conv_ln_b", "conv_b_out")
    small_loc = jnp.concatenate([c.reshape(-1)] + [W[n].reshape(-1) for n in small_names])
    sg = _exchange([_pad_rows(small_loc, LANES, 8)], True, "gather_small")[0].reshape(NDEV, -1)
    c_all = sg[:, :D]
    off = D
    b_in = sg[:, off:off + 2 * D // NDEV].reshape(1, 2 * D); off += 2 * D // NDEV
    cl = D // NDEV
    w_dw = sg[:, off:off + KW * cl].reshape(NDEV, KW, cl).transpose(1, 0, 2).reshape(KW, D); off += KW * cl
    w_dw = jnp.pad(w_dw, ((0, HALO - KW), (0, 0))).astype(bf16).astype(f32)
    b_dw = sg[:, off:off + cl].reshape(1, D); off += cl
    ln_g = sg[:, off:off + cl].reshape(1, D); off += cl
    ln_b = sg[:, off:off + cl].reshape(1, D); off += cl
    b_out = sg[:, off:off + cl].reshape(1, D)

    cat_ada = lambda s: jnp.concatenate([s["mix_ada_w"][0], s["mix_ada_w"][1], s["ffn_ada_w"][0], s["ffn_ada_w"][1], s["kv_ada_w"]], axis=1)
    w_cat = cat_ada(W)
    ada_loc = _ada_fwd(c_all, w_cat, "ada_fwd")
    ada_all = _exchange([ada_loc], True, "gather_ada")[0]
    ada_me = lax.dynamic_index_in_dim(ada_all, me, axis=1, keepdims=False)
    ada_bias = (mix_ada_b[0], mix_ada_b[1], ffn_ada_b[0], ffn_ada_b[1], kv_ada_b)
    ada, off = [], 0
    for nl, bias in zip(ADA_LOC, ada_bias):
        full = ada_me[:, off:off + nl].reshape(-1) + bias
        ada.append([row(t) for t in jnp.split(full, full.shape[0] // D)])
        off += nl
    (sh_m0, sc_m0, gt_m0), (sh_m1, sc_m1, gt_m1), (sh_f0, sc_f0, gt_f0), (sh_f1, sc_f1, gt_f1), (sh_kv, sc_kv) = ada

    pad_in = lambda src, l: jnp.pad(src["ffn_w_in"][l], ((0, 0), (0, FSP - FS)))
    rows_a = lambda src: jnp.concatenate([src["ffn_w_out"][0], src["attn_w_q"][0]])
    rows_b = lambda src: jnp.concatenate([src["ffn_w_out"][1], src["attn_w_o"][0]])
    as_bf = lambda arrs: [t.astype(bf16) for t in arrs]
    fo, sq_rows = F // NDEV, D // NDEV
    g_ci, g_co = _exchange(as_bf([conv_w_in[0], conv_w_out[0]]), True, "gather_weights")
    soon = as_bf([pad_in(W, 0)])
    next_ = as_bf([rows_a(W), kv_w])
    late = as_bf([pad_in(W, 1), rows_b(W)])

    def w_out_of(g_r):
        t = g_r[:, :fo].reshape(NDEV // 2, FS, D)
        return jnp.pad(t, ((0, 0), (0, FSP - FS), (0, 0))).reshape(FP, D)

    conv_in_full = g_ci.transpose(1, 0, 2).reshape(D, 2 * D)
    wc_a, wc_g = conv_in_full[:, :D], conv_in_full[:, D:]
    wc_o = g_co.reshape(D, D)
    zeros_d = jnp.zeros((1, D), f32)
    fb = jnp.pad(forget_b, (0, LANES - H)).reshape(1, LANES)

    h0 = _normmod(x0, row(mix_norm_g[0]), sh_m0, sc_m0, "norm_mix0")
    a0, gl0, p0 = _mm_gated(h0, wc_a, wc_g, b_in[:, :D], b_in[:, D:], False, "conv_in")
    q0, s0, w_sh0 = _conv_fwd(p0, w_dw, b_dw, ln_g, ln_b, "conv_dw", ride=soon)
    x1, y0 = _mm_res(s0, wc_o, b_out, x0, gt_m0, "conv_out")

    def ffn_fwd(xin, l, sh, sc, gt, w_sh, w_out=None, ride=()):
        h = _normmod(xin, row(ffn_norm_g[l]), sh, sc, f"norm_ffn{l}")
        ug, uu, act, *got = _ffn_in(h, w_sh, f"ffn_in{l}", ride=ride)
        if w_out is None:
            w_out = w_out_of(got[0])
        xo, y = _mm_res(act, w_out, zeros_d, xin, gt, f"ffn_out{l}")
        return xo, (h, ug, uu, act, y, w_sh, w_out), got

    x2, ffn0, (g_ra, g_kv) = ffn_fwd(x1, 0, sh_f0, sc_f0, gt_f0, w_sh0, ride=next_)
    w_q = g_ra[:, fo:].reshape(D, D)
    kv_full = g_kv.transpose(1, 0, 2).reshape(D, -1)
    w_k, w_v = kv_full[:, :D], kv_full[:, D:2 * D]
    w_f = jnp.pad(kv_full[:, 2 * D:], ((0, 0), (0, LANES - H)))

    hk = _normmod(x2, row(kv_norm_g), sh_kv, sc_kv, "norm_kv")
    k_sh = _mm(hk, w_k, bf16, 1.0, "proj_k")
    v_sh = _mm(hk, w_v, bf16, 1.0, "proj_v")
    fl = _mm(hk, w_f, f32, 1.0, "proj_f")
    cum = _forget_fwd(fl, fb, "forget_fwd")
    cum_t = cum[:, :H].T.reshape(NP, 2, T)

    h2 = _normmod(x2, row(mix_norm_g[1]), sh_m1, sc_m1, "norm_mix1")
    qh = _mm(h2, w_q, bf16, HD ** -0.5, "proj_q")
    o, o32, lse, w_sh1, g_rb = _attn_fwd(qh, k_sh, v_sh, cum_t, "attn_fwd", ride=late)
    w_out1, w_o = w_out_of(g_rb), g_rb[:, fo:].reshape(D, D)
    x3, y1 = _mm_res(o, w_o, zeros_d, x2, gt_m1, "attn_out")

    x4, ffn1, _ = ffn_fwd(x3, 1, sh_f1, sc_f1, gt_f1, w_sh1, w_out1)

    dx4, acc_fin = _final_bwd(x4, row(final_norm_g), tgt, "final_bwd")

    d_ada = {}
    by_rows = lambda g: g.reshape(NDEV, sq_rows, D)

    def ffn_bwd(dx_out, xin, l, sc, gt, saved):
        h, ug, uu, act, y, w_sh, w_out = saved
        dyb, acc_r = _res_in(dx_out, y, gt, f"ffn_res_bwd{l}")
        dug, duu = _mm_nt_swiglu(dyb, w_out, ug, uu, f"ffn_dact{l}")
        g_out = _mm_tn(act, dyb, f"ffn_dw_out{l}").reshape(NDEV // 2, FSP, D)[:, :FS].reshape(NDEV, fo, D)
        g_in = _mm_tn_shards(h, duu, FSP, f"ffn_dw_up{l}", into=_mm_tn_shards(h, dug, FSP, f"ffn_dw_gate{l}"))
        dh = _ffn_dh(dug, duu, w_sh, f"ffn_dh{l}")
        dxi, acc_n = _normmod_bwd(dh, xin, row(ffn_norm_g[l]), sc, dx_out, f"norm_ffn_bwd{l}")
        return dxi, g_in, g_out, [acc_n[0:1], acc_n[1:2], acc_r[0:1]], acc_n[2]

    dx3, g_in1, g_out1, d_ada[("ffn", 1)], dg_ffn1 = ffn_bwd(dx4, x3, 1, sc_f1, gt_f1, ffn1)

    dyb, acc_r = _res_in(dx3, y1, gt_m1, "attn_res_bwd")
    do = _mm_nt([(dyb, w_o)], bf16, "attn_do")
    g_wo = _mm_tn(o, dyb, "attn_dw_o")
    st = _attn_stats(do, o32, lse, "attn_stats")
    leave_b = as_bf([g_in1, jnp.concatenate([g_out1, by_rows(g_wo)], axis=1)])
    dq, dk, dv, dck, r_in1, r_rb = _attn_bwd(qh, k_sh, v_sh, do, st, cum_t, "attn_bwd", ride=leave_b)
    g_wq = _mm_tn(h2, dq, "attn_dw_q")
    dh2 = _mm_nt([(dq, w_q)], f32, "attn_dh")
    dx2, acc_n = _normmod_bwd(dh2, x2, row(mix_norm_g[1]), sc_m1, dx3, "norm_mix_bwd1")
    d_ada[("mix", 1)] = [acc_n[0:1], acc_n[1:2], acc_r[0:1]]
    dg_mix1 = acc_n[2]

    dcum = jnp.pad(dck.reshape(H, T).T, ((0, 0), (0, LANES - H)))
    dfl, acc_f = _forget_bwd(dcum, fl, fb, "forget_bwd")
    g_kvw = jnp.concatenate([_mm_tn(hk, dk, "kv_dw_k"), _mm_tn(hk, dv, "kv_dw_v"), _mm_tn(hk, dfl, "kv_dw_f")[:, :H]], axis=1)
    dhk = _mm_nt([(dk, w_k), (dv, w_v), (dfl, w_f)], f32, "kv_dh")
    dx2, acc_n = _normmod_bwd(dhk, x2, row(kv_norm_g), sc_kv, dx2, "norm_kv_bwd")
    d_ada[("kv", 0)] = [acc_n[0:1], acc_n[1:2]]
    dg_kv = acc_n[2]

    dx1, g_in0, g_out0, d_ada[("ffn", 0)], dg_ffn0 = ffn_bwd(dx2, x1, 0, sc_f0, gt_f0, ffn0)

    dyb, acc_r = _res_in(dx1, y0, gt_m0, "conv_res_bwd")
    dsw = _mm_nt([(dyb, wc_o)], f32, "conv_ds")
    g_co_out = _mm_tn(s0, dyb, "conv_dw_out")
    leave_a = as_bf([g_in0, jnp.concatenate([g_out0, by_rows(g_wq)], axis=1), g_kvw.reshape(D, NDEV, -1).transpose(1, 0, 2)])
    da, dgl, acc_c, dw_dw, r_in0, r_ra, r_kv = _conv_bwd(dsw, q0, p0, a0, gl0, w_dw, ln_g, ln_b, "conv_bwd", ride=leave_a)
    cs = 2 * D // NDEV
    g_ci_out = _mm_tn_shards(h0, dgl, cs, "conv_dw_g", into=_mm_tn_shards(h0, da, cs, "conv_dw_a"))
    dh0, r_ci, r_co = _mm_nt([(da, wc_a), (dgl, wc_g)], f32, "conv_dh", ride=as_bf([g_ci_out, by_rows(g_co_out)]))
    dx0, acc_n = _normmod_bwd(dh0, x0, row(mix_norm_g[0]), sc_m0, dx1, "norm_mix_bwd0")
    d_ada[("mix", 0)] = [acc_n[0:1], acc_n[1:2], acc_r[0:1]]
    dg_mix0 = acc_n[2]

    vec = [t.reshape(-1) for key in [(s[0], s[1]) for s in ADA_SEG] for t in d_ada[key]]
    vec += [dg_mix0, dg_mix1, dg_ffn0, dg_ffn1, dg_kv, acc_fin[0]]
    vec += [acc_f[0], acc_fin[1, :LANES]]
    vec += [acc_c[3], acc_c[4], dw_dw[:KW].reshape(-1), acc_c[2], acc_c[0], acc_c[1], acc_r[1]]
    small_parts = _exchange([_pad_rows(jnp.concatenate(vec), LANES, 8)], True, "gather_partials")[0]
    small_sum = _sum_parts(small_parts, "sum_partials").reshape(-1)
    d_ada_all = small_parts.reshape(NDEV, -1)[:, :ADA_TOT]
    off = 0
    gsm = {}
    ada_b_sum = []
    for _, _, n in ADA_SEG:
        ada_b_sum.append(small_sum[off:off + n]); off += n
    gsm["mix_ada_b"] = jnp.stack(ada_b_sum[0:2])
    gsm["ffn_ada_b"] = jnp.stack(ada_b_sum[2:4])
    gsm["kv_ada_b"] = ada_b_sum[4]
    gsm["mix_norm_g"] = small_sum[off:off + 2 * D].reshape(2, D); off += 2 * D
    gsm["ffn_norm_g"] = small_sum[off:off + 2 * D].reshape(2, D); off += 2 * D
    gsm["kv_norm_g"] = small_sum[off:off + D]; off += D
    gsm["final_norm_g"] = small_sum[off:off + D]; off += D
    gsm["forget_b"] = small_sum[off:off + H]; off += LANES
    loss = small_sum[off]; off += LANES
    sl = lambda full, width: lax.dynamic_slice_in_dim(full, me * width, width, axis=full.ndim - 1)
    gsm["conv_b_in"] = sl(small_sum[off:off + 2 * D].reshape(1, 2 * D), 2 * D // NDEV); off += 2 * D
    gsm["conv_w_dw"] = sl(small_sum[off:off + KW * D].reshape(1, KW, D), cl); off += KW * D
    for n in ("conv_b_dw", "conv_ln_g", "conv_ln_b", "conv_b_out"):
        gsm[n] = sl(small_sum[off:off + D].reshape(1, D), cl); off += D

    dsel, off = [], 0
    for (_, _, n), nl in zip(ADA_SEG, ADA_LOC):
        dsel.append(lax.dynamic_slice_in_dim(d_ada_all[:, off:off + n], me * nl, nl, axis=1)); off += n
    g_ada = _ada_bwd(c_all.T, jnp.concatenate(dsel, axis=1), "ada_bwd")
    res_ada = _adamw(g_ada[None], w_cat, cat_ada(M), cat_ada(V), "adamw_ada")

    res_in0 = _adamw(r_in0, *[pad_in(s, 0) for s in (W, M, V)], "adamw_ffn_in0")
    res_in1 = _adamw(r_in1, *[pad_in(s, 1) for s in (W, M, V)], "adamw_ffn_in1")
    res_ra = _adamw(r_ra, *[rows_a(s) for s in (W, M, V)], "adamw_rows_a")
    res_rb = _adamw(r_rb, *[rows_b(s) for s in (W, M, V)], "adamw_rows_b")
    res_ci = _adamw(r_ci, *[s["conv_w_in"][0] for s in (W, M, V)], "adamw_conv_in")
    res_co = _adamw(r_co, *[s["conv_w_out"][0] for s in (W, M, V)], "adamw_conv_out")
    res_kv = _adamw(r_kv, *[s["kv_w"] for s in (W, M, V)], "adamw_kv")
    rest = [n for n in names if n not in MAIN and n not in ("mix_ada_w", "ffn_ada_w", "kv_ada_w")]
    pack_rest = lambda src: _pad_rows(jnp.concatenate([src[n].reshape(-1) for n in rest]), D, 256)
    res_rest = _adamw(pack_rest(gsm)[None], pack_rest(W), pack_rest(M), pack_rest(V), "adamw_rest")

    outs = []
    for k in range(4):
        ur = _split_flat(res_rest[k].reshape(-1), [(n, W[n].shape) for n in rest])
        ra, a0_, a2_ = res_ada[k], ADA_LOC[0], ADA_LOC[2]
        ur["mix_ada_w"] = jnp.stack([ra[:, 0:a0_], ra[:, a0_:2 * a0_]])
        ur["ffn_ada_w"] = jnp.stack([ra[:, 2 * a0_:2 * a0_ + a2_], ra[:, 2 * a0_ + a2_:2 * a0_ + 2 * a2_]])
        ur["kv_ada_w"] = ra[:, 2 * a0_ + 2 * a2_:]
        ur["ffn_w_in"] = jnp.stack([res_in0[k][:, :FS], res_in1[k][:, :FS]])
        ur["conv_w_in"] = res_ci[k][None]
        ur["conv_w_out"] = res_co[k][None]
        ur["kv_w"] = res_kv[k]
        ur["ffn_w_out"] = jnp.stack([res_ra[k][:fo], res_rb[k][:fo]])
        ur["attn_w_q"] = res_ra[k][fo:][None]
        ur["attn_w_o"] = res_rb[k][fo:][None]
        outs.append(ur)
    grads, deltas, new_m, new_v = outs
    return (loss, dx0[None], *[grads[n] for n in names], *[deltas[n] for n in names],
            *[new_m[n] for n in names], *[new_v[n] for n in names])
```

```python
import functools

import jax
import jax.numpy as jnp
from jax import lax
from jax.experimental import pallas as pl
from jax.experimental.pallas import tpu as pltpu

f32, bf16 = jnp.float32, jnp.bfloat16
SDS = jax.ShapeDtypeStruct

D = 1024
F = 2816
H = 16
HD = 64
NP = H // 2
KW = 31
HALO = 32
NDEV = 8
FS = 2 * F // NDEV
FSP = 768
FP = 4 * FSP
EPS = 1e-6
NEG = -1e30
LANES = 128

ADAM_LR, ADAM_B1, ADAM_B2, ADAM_EPS, ADAM_WD, ADAM_STEP = 0.001, 0.9, 0.999, 1e-08, 0.01, 10

TM = 512
TMM = 1024
TC = 256
TQ = 1024
VMEM_LIMIT = 56 << 20

MAIN = ("ffn_w_in", "ffn_w_out", "conv_w_in", "conv_w_out", "kv_w", "attn_w_q", "attn_w_o")
ADA_SEG = (("mix", 0, 3 * D), ("mix", 1, 3 * D), ("ffn", 0, 3 * D), ("ffn", 1, 3 * D), ("kv", 0, 2 * D))
ADA_LOC = tuple(n // NDEV for _, _, n in ADA_SEG)
ADA_COLS = sum(ADA_LOC)
ADA_TOT = sum(n for _, _, n in ADA_SEG)


def _cparams(n_axes):
    return pltpu.CompilerParams(dimension_semantics=("arbitrary",) * n_axes, vmem_limit_bytes=VMEM_LIMIT)


def _mesh_pos():
    return lax.axis_index("x"), lax.axis_index("y"), lax.axis_index("c")


def _my_index():
    mx, my, mc = _mesh_pos()
    return 4 * mx + 2 * my + mc


def _peer(k, mx, my, mc):
    px = (1 - mx) if k & 4 else mx
    py = (1 - my) if k & 2 else my
    pc = (1 - mc) if k & 1 else mc
    return (px, py, pc), 4 * px + 2 * py + pc


def _exchange_copies(x_refs, o_refs, sems, gather):
    send_sems, recv_sems, local_sems = sems
    mx, my, mc = _mesh_pos()
    me = 4 * mx + 2 * my + mc
    copies = []
    for a, (x_ref, o_ref) in enumerate(zip(x_refs, o_refs)):
        copies.append(pltpu.make_async_copy(x_ref if gather else x_ref.at[me], o_ref.at[me], local_sems.at[a]))
        for k in range(1, NDEV):
            peer, pidx = _peer(k, mx, my, mc)
            sem = a * (NDEV - 1) + k - 1
            copies.append(pltpu.make_async_remote_copy(
                src_ref=x_ref if gather else x_ref.at[pidx], dst_ref=o_ref.at[me],
                send_sem=send_sems.at[sem], recv_sem=recv_sems.at[sem],
                device_id=peer, device_id_type=pl.DeviceIdType.MESH))
    return copies


def _exchange_shapes(xs, gather):
    n = len(xs)
    hbm = pl.BlockSpec(memory_space=pl.ANY)
    outs = [SDS((NDEV,) + tuple(x.shape if gather else x.shape[1:]), x.dtype) for x in xs]
    sems = [pltpu.SemaphoreType.DMA((n * (NDEV - 1),)), pltpu.SemaphoreType.DMA((n * (NDEV - 1),)), pltpu.SemaphoreType.DMA((n,))]
    return [hbm] * n, [hbm] * n, outs, sems


def _exchange_start(x_refs, o_refs, sems, gather, first):
    @pl.when(first)
    def _():
        for cp in _exchange_copies(x_refs, o_refs, sems, gather):
            cp.start()


def _exchange_wait(x_refs, o_refs, sems, gather, last):
    @pl.when(last)
    def _():
        for cp in _exchange_copies(x_refs, o_refs, sems, gather):
            cp.wait()


def _exchange(xs, gather, name):
    n = len(xs)

    def body(*refs):
        copies = _exchange_copies(refs[:n], refs[n:2 * n], refs[2 * n:], gather)
        for cp in copies:
            cp.start()
        for cp in copies:
            cp.wait()

    in_specs, out_specs, outs, sems = _exchange_shapes(xs, gather)
    return pl.pallas_call(body, out_shape=outs, in_specs=in_specs, out_specs=out_specs, scratch_shapes=sems, name=name)(*xs)


def _row(i):
    return (i, 0)


def _fix(i):
    return (0, 0)


def _normmod(x, g, shift, scale, name):
    T = x.shape[0]
    tm = min(TM, T)

    def body(x_ref, g_ref, sh_ref, sc_ref, h_ref):
        xv = x_ref[...]
        r = lax.rsqrt(jnp.mean(xv * xv, axis=-1, keepdims=True) + EPS)
        hn = (xv * r) * g_ref[...]
        h_ref[...] = (hn * (1.0 + sc_ref[...]) + sh_ref[...]).astype(bf16)

    vec = pl.BlockSpec((1, D), _fix)
    return pl.pallas_call(
        body, grid=(T // tm,), in_specs=[pl.BlockSpec((tm, D), _row), vec, vec, vec],
        out_specs=pl.BlockSpec((tm, D), _row), out_shape=SDS((T, D), bf16),
        compiler_params=_cparams(1), name=name)(x, g, shift, scale)


def _normmod_bwd(dh, x, g, scale, dx_res, name):
    T = x.shape[0]
    tm = min(TM, T)

    def body(dh_ref, x_ref, g_ref, sc_ref, res_ref, dx_ref, acc_ref):
        @pl.when(pl.program_id(0) == 0)
        def _():
            acc_ref[...] = jnp.zeros_like(acc_ref)
        xv = x_ref[...]
        dhv = dh_ref[...]
        gv = g_ref[...]
        r = lax.rsqrt(jnp.mean(xv * xv, axis=-1, keepdims=True) + EPS)
        xn = xv * r
        dhn = dhv * (1.0 + sc_ref[...])
        dxn = dhn * gv
        dx_ref[...] = res_ref[...] + r * (dxn - xn * jnp.mean(dxn * xn, axis=-1, keepdims=True))
        acc_ref[0:1, :] += jnp.sum(dhv, axis=0, keepdims=True)
        acc_ref[1:2, :] += jnp.sum(dhv * (xn * gv), axis=0, keepdims=True)
        acc_ref[2:3, :] += jnp.sum(dhn * xn, axis=0, keepdims=True)

    vec = pl.BlockSpec((1, D), _fix)
    til = pl.BlockSpec((tm, D), _row)
    return pl.pallas_call(
        body, grid=(T // tm,), in_specs=[til, til, vec, vec, til],
        out_specs=[til, pl.BlockSpec((8, D), _fix)], out_shape=[SDS((T, D), f32), SDS((8, D), f32)],
        compiler_params=_cparams(1), name=name)(dh, x, g, scale, dx_res)


def _res_in(dx, y, gate, name):
    T = dx.shape[0]
    tm = min(TM, T)

    def body(dx_ref, y_ref, gt_ref, dy_ref, acc_ref):
        @pl.when(pl.program_id(0) == 0)
        def _():
            acc_ref[...] = jnp.zeros_like(acc_ref)
        dxv = dx_ref[...]
        dy = dxv * gt_ref[...]
        dy_ref[...] = dy.astype(bf16)
        acc_ref[0:1, :] += jnp.sum(dxv * y_ref[...].astype(f32), axis=0, keepdims=True)
        acc_ref[1:2, :] += jnp.sum(dy, axis=0, keepdims=True)

    til = pl.BlockSpec((tm, D), _row)
    return pl.pallas_call(
        body, grid=(T // tm,), in_specs=[til, til, pl.BlockSpec((1, D), _fix)],
        out_specs=[til, pl.BlockSpec((8, D), _fix)], out_shape=[SDS((T, D), bf16), SDS((8, D), f32)],
        compiler_params=_cparams(1), name=name)(dx, y, gate)


def _final_bwd(x, g, tgt, name):
    T = x.shape[0]
    tm = min(TM, T)

    def body(x_ref, g_ref, t_ref, dx_ref, acc_ref):
        @pl.when(pl.program_id(0) == 0)
        def _():
            acc_ref[...] = jnp.zeros_like(acc_ref)
        xv = x_ref[...]
        gv = g_ref[...]
        r = lax.rsqrt(jnp.mean(xv * xv, axis=-1, keepdims=True) + EPS)
        xn = xv * r
        err = xn * gv - t_ref[...]
        dy = err * (1.0 / D)
        dxn = dy * gv
        dx_ref[...] = r * (dxn - xn * jnp.mean(dxn * xn, axis=-1, keepdims=True))
        acc_ref[0:1, :] += jnp.sum(dy * xn, axis=0, keepdims=True)
        acc_ref[1:2, :] += 0.5 * jnp.sum(jnp.mean(err * err, axis=-1, keepdims=True))

    til = pl.BlockSpec((tm, D), _row)
    return pl.pallas_call(
        body, grid=(T // tm,), in_specs=[til, pl.BlockSpec((1, D), _fix), til],
        out_specs=[til, pl.BlockSpec((8, D), _fix)], out_shape=[SDS((T, D), f32), SDS((8, D), f32)],
        compiler_params=_cparams(1), name=name)(x, g, tgt)


def _col_tile(n):
    if n <= 1024:
        return n
    return 1408 if n % 1408 == 0 else 1024


def _mm_gated(h, wa, wb, ba, bb, swiglu, name):
    T, K = h.shape
    N = wa.shape[1]
    tm, tn = min(TMM, T), _col_tile(N)

    def body(h_ref, wa_ref, wb_ref, ba_ref, bb_ref, u_ref, w_ref, p_ref):
        hv = h_ref[...]
        u = jnp.dot(hv, wa_ref[...], preferred_element_type=f32) + ba_ref[...]
        w = jnp.dot(hv, wb_ref[...], preferred_element_type=f32) + bb_ref[...]
        u_ref[...] = u
        w_ref[...] = w
        if swiglu:
            p_ref[...] = ((u * jax.nn.sigmoid(u)) * w).astype(p_ref.dtype)
        else:
            p_ref[...] = (u * jax.nn.sigmoid(w)).astype(p_ref.dtype)

    wsp = pl.BlockSpec((K, tn), lambda i, j: (0, j))
    bsp = pl.BlockSpec((1, tn), lambda i, j: (0, j))
    osp = pl.BlockSpec((tm, tn), lambda i, j: (i, j))
    return pl.pallas_call(
        body, grid=(T // tm, N // tn), in_specs=[pl.BlockSpec((tm, K), lambda i, j: (i, 0)), wsp, wsp, bsp, bsp],
        out_specs=[osp, osp, osp], out_shape=[SDS((T, N), f32), SDS((T, N), f32), SDS((T, N), bf16)],
        compiler_params=_cparams(2), name=name)(h, wa, wb, ba, bb)


def _mm_res(a, w, b, x_in, gate, name):
    T, K = a.shape
    N = w.shape[1]
    tm, tn = min(TMM, T), _col_tile(N)

    def body(a_ref, w_ref, b_ref, x_ref, gt_ref, xo_ref, y_ref):
        y = jnp.dot(a_ref[...], w_ref[...], preferred_element_type=f32) + b_ref[...]
        y_ref[...] = y.astype(bf16)
        xo_ref[...] = x_ref[...] + gt_ref[...] * y

    vsp = pl.BlockSpec((1, tn), lambda i, j: (0, j))
    osp = pl.BlockSpec((tm, tn), lambda i, j: (i, j))
    return pl.pallas_call(
        body, grid=(T // tm, N // tn),
        in_specs=[pl.BlockSpec((tm, K), lambda i, j: (i, 0)), pl.BlockSpec((K, tn), lambda i, j: (0, j)), vsp, osp, vsp],
        out_specs=[osp, osp], out_shape=[SDS((T, N), f32), SDS((T, N), bf16)],
        compiler_params=_cparams(2), name=name)(a, w, b, x_in, gate)


def _mm(a, w, out_dtype, out_scale, name):
    T, K = a.shape
    N = w.shape[1]
    tm, tn = min(TMM, T), _col_tile(N)

    def body(a_ref, w_ref, o_ref):
        y = jnp.dot(a_ref[...], w_ref[...], preferred_element_type=f32)
        if out_scale != 1.0:
            y = y * out_scale
        o_ref[...] = y.astype(out_dtype)

    return pl.pallas_call(
        body, grid=(T // tm, N // tn),
        in_specs=[pl.BlockSpec((tm, K), lambda i, j: (i, 0)), pl.BlockSpec((K, tn), lambda i, j: (0, j))],
        out_specs=pl.BlockSpec((tm, tn), lambda i, j: (i, j)), out_shape=SDS((T, N), out_dtype),
        compiler_params=_cparams(2), name=name)(a, w)


def _dot_nt(a, b):
    return lax.dot_general(a, b, (((1,), (1,)), ((), ())), preferred_element_type=f32)


def _dot_tn(a, b):
    return lax.dot_general(a, b, (((0,), (0,)), ((), ())), preferred_element_type=f32)


def _mm_nt(pairs, out_dtype, name, ride=()):
    T = pairs[0][0].shape[0]
    K = pairs[0][1].shape[0]
    tm, tk = min(TMM, T), _col_tile(K)
    n = len(pairs)
    nx = len(ride)
    gi, gj = T // tm, K // tk

    def body(*refs):
        x_refs, o_ref, xo_refs, sems = refs[2 * n:2 * n + nx], refs[2 * n + nx], refs[2 * n + nx + 1:2 * n + 2 * nx + 1], refs[2 * n + 2 * nx + 1:]
        i, j = pl.program_id(0), pl.program_id(1)
        if nx:
            _exchange_start(x_refs, xo_refs, sems, False, (i == 0) & (j == 0))
        acc = None
        for a in range(n):
            part = _dot_nt(refs[2 * a][...].astype(bf16), refs[2 * a + 1][...])
            acc = part if acc is None else acc + part
        o_ref[...] = acc.astype(out_dtype)
        if nx:
            _exchange_wait(x_refs, xo_refs, sems, False, (i == gi - 1) & (j == gj - 1))

    in_specs, args = [], []
    for dy, w in pairs:
        ni = dy.shape[1]
        in_specs += [pl.BlockSpec((tm, ni), lambda i, j: (i, 0)), pl.BlockSpec((tk, ni), lambda i, j: (j, 0))]
        args += [dy, w]
    x_in, x_out, x_shape, x_sems = _exchange_shapes(ride, False) if nx else ([], [], [], [])
    out = pl.pallas_call(
        body, grid=(gi, gj), in_specs=in_specs + x_in,
        out_specs=[pl.BlockSpec((tm, tk), lambda i, j: (i, j))] + x_out, out_shape=[SDS((T, K), out_dtype)] + x_shape,
        scratch_shapes=x_sems, compiler_params=_cparams(2), name=name)(*args, *ride)
    return out if nx else out[0]


def _mm_nt_swiglu(dy, w, ug, uu, name):
    T, N = dy.shape
    K = w.shape[0]
    tm, tk = min(TMM, T), _col_tile(K)

    def body(dy_ref, w_ref, ug_ref, uu_ref, dug_ref, duu_ref):
        dact = _dot_nt(dy_ref[...], w_ref[...])
        g = ug_ref[...].astype(f32)
        u = uu_ref[...].astype(f32)
        sg = jax.nn.sigmoid(g)
        duu_ref[...] = (dact * (g * sg)).astype(bf16)
        dug_ref[...] = (dact * u * (sg * (1.0 + g * (1.0 - sg)))).astype(bf16)

    osp = pl.BlockSpec((tm, tk), lambda i, j: (i, j))
    return pl.pallas_call(
        body, grid=(T // tm, K // tk),
        in_specs=[pl.BlockSpec((tm, N), lambda i, j: (i, 0)), pl.BlockSpec((tk, N), lambda i, j: (j, 0)), osp, osp],
        out_specs=[osp, osp], out_shape=[SDS((T, K), bf16), SDS((T, K), bf16)],
        compiler_params=_cparams(2), name=name)(dy, w, ug, uu)


def _mm_tn(a, b, name):
    T, K = a.shape
    N = b.shape[1]
    tt = min(TMM, T)
    tk = K if K <= 1024 else _col_tile(K)
    tn = N if N <= 1024 else _col_tile(N)

    def body(a_ref, b_ref, o_ref):
        @pl.when(pl.program_id(2) == 0)
        def _():
            o_ref[...] = jnp.zeros_like(o_ref)
        o_ref[...] += _dot_tn(a_ref[...].astype(bf16), b_ref[...].astype(bf16))

    return pl.pallas_call(
        body, grid=(K // tk, N // tn, T // tt),
        in_specs=[pl.BlockSpec((tt, tk), lambda i, j, t: (t, i)), pl.BlockSpec((tt, tn), lambda i, j, t: (t, j))],
        out_specs=pl.BlockSpec((tk, tn), lambda i, j, t: (i, j)), out_shape=SDS((K, N), f32),
        compiler_params=_cparams(3), name=name)(a, b)


def _mm_tn_shards(a, b, c, name, into=None):
    T, K = a.shape
    half = NDEV // 2
    assert b.shape[1] == half * c
    tt = min(TMM, T)
    nt = T // tt
    first = into is None

    def body(a_ref, b_ref, *rest):
        o_ref, acc = rest[-2], rest[-1]
        t = pl.program_id(1)

        @pl.when(t == 0)
        def _():
            acc[...] = jnp.zeros_like(acc)
        acc[...] += _dot_tn(a_ref[...].astype(bf16), b_ref[...].astype(bf16))

        @pl.when(t == nt - 1)
        def _():
            o_ref[...] = acc[...].astype(bf16)

    in_specs = [pl.BlockSpec((tt, K), lambda j, t: (t, 0)), pl.BlockSpec((tt, c), lambda j, t: (t, j))]
    args = [a, b]
    if not first:
        in_specs.append(pl.BlockSpec(memory_space=pl.ANY))
        args.append(into)
    base = 0 if first else half
    return pl.pallas_call(
        body, grid=(half, nt), in_specs=in_specs,
        out_specs=pl.BlockSpec((None, K, c), lambda j, t: (j + base, 0, 0)), out_shape=SDS((NDEV, K, c), bf16),
        scratch_shapes=[pltpu.VMEM((K, c), f32)], input_output_aliases={} if first else {2: 0},
        compiler_params=_cparams(2), name=name)(*args)


def _ffn_in(h, w_sh, name, ride=()):
    T = h.shape[0]
    tm = min(TMM, T)
    nx = len(ride)
    gi, gj = T // tm, NDEV // 2

    def body(h_ref, wg_ref, wu_ref, *rest):
        x_refs, (ug_ref, uu_ref, act_ref), xo_refs, sems = rest[:nx], rest[nx:nx + 3], rest[nx + 3:2 * nx + 3], rest[2 * nx + 3:]
        i, j = pl.program_id(0), pl.program_id(1)
        if nx:
            _exchange_start(x_refs, xo_refs, sems, True, (i == 0) & (j == 0))
        hv = h_ref[...]
        ug = jnp.dot(hv, wg_ref[...], preferred_element_type=f32)
        uu = jnp.dot(hv, wu_ref[...], preferred_element_type=f32)
        ug_ref[...] = ug.astype(bf16)
        uu_ref[...] = uu.astype(bf16)
        act_ref[...] = ((ug * jax.nn.sigmoid(ug)) * uu).astype(bf16)
        if nx:
            _exchange_wait(x_refs, xo_refs, sems, True, (i == gi - 1) & (j == gj - 1))

    osp = pl.BlockSpec((tm, FSP), lambda i, j: (i, j))
    x_in, x_out, x_shape, x_sems = _exchange_shapes(ride, True) if nx else ([], [], [], [])
    return pl.pallas_call(
        body, grid=(gi, gj),
        in_specs=[pl.BlockSpec((tm, D), lambda i, j: (i, 0)),
                  pl.BlockSpec((None, D, FSP), lambda i, j: (j, 0, 0)),
                  pl.BlockSpec((None, D, FSP), lambda i, j: (j + NDEV // 2, 0, 0))] + x_in,
        out_specs=[osp, osp, osp] + x_out,
        out_shape=[SDS((T, FP), bf16), SDS((T, FP), bf16), SDS((T, FP), bf16)] + x_shape,
        scratch_shapes=x_sems, compiler_params=_cparams(2), name=name)(h, w_sh, w_sh, *ride)


def _ffn_dh(dug, duu, w_sh, name):
    T = dug.shape[0]
    tm, tk = min(TMM, T), 512
    half = NDEV // 2

    def body(dg_ref, du_ref, wg_ref, wu_ref, o_ref):
        acc = None
        for s in range(half):
            cols = slice(s * FSP, (s + 1) * FSP)
            part = _dot_nt(dg_ref[:, cols], wg_ref[s]) + _dot_nt(du_ref[:, cols], wu_ref[s])
            acc = part if acc is None else acc + part
        o_ref[...] = acc

    dsp = pl.BlockSpec((tm, FP), lambda i, j: (i, 0))
    return pl.pallas_call(
        body, grid=(T // tm, D // tk),
        in_specs=[dsp, dsp, pl.BlockSpec((half, tk, FSP), lambda i, j: (0, j, 0)),
                  pl.BlockSpec((half, tk, FSP), lambda i, j: (1, j, 0))],
        out_specs=pl.BlockSpec((tm, tk), lambda i, j: (i, j)), out_shape=SDS((T, D), f32),
        compiler_params=_cparams(2), name=name)(dug, duu, w_sh, w_sh)


def _layernorm_parts(qv, g, b):
    mu = jnp.mean(qv, axis=-1, keepdims=True)
    cen = qv - mu
    rstd = lax.rsqrt(jnp.mean(cen * cen, axis=-1, keepdims=True) + EPS)
    z = cen * rstd
    return z, rstd, z * g + b


def _tap_groups(offsets):
    groups = {}
    for k, o in enumerate(offsets):
        groups.setdefault(o % 8, []).append((k, o - o % 8))
    return groups


def _conv_fwd(p, w_dw, b_dw, ln_g, ln_b, name, ride=()):
    T = p.shape[0]
    tc = min(TC, T)
    n = T // tc
    nx = len(ride)

    def body(p_ref, w_ref, b_ref, g_ref, bb_ref, *rest):
        x_refs, (q_ref, s_ref), xo_refs = rest[:nx], rest[nx:nx + 2], rest[nx + 2:2 * nx + 2]
        (ext, sh), sems = rest[2 * nx + 2:2 * nx + 4], rest[2 * nx + 4:]
        i = pl.program_id(0)
        if nx:
            _exchange_start(x_refs, xo_refs, sems, True, i == 0)

        @pl.when(i == 0)
        def _():
            ext[0:HALO, :] = jnp.zeros((HALO, D), f32)

        @pl.when(i > 0)
        def _():
            ext[0:HALO, :] = ext[tc:tc + HALO, :]

        ext[HALO:HALO + tc, :] = p_ref[...].astype(f32)
        groups = _tap_groups([HALO - (KW - 1) + k for k in range(KW)])
        for cb in range(D // LANES):
            cols = slice(cb * LANES, (cb + 1) * LANES)
            acc = jnp.zeros((tc, LANES), f32)
            for r, taps in groups.items():
                span = max(base for _, base in taps) + tc
                sh[0:span, :] = ext[r:r + span, cols]
                for k, base in taps:
                    acc = acc + w_ref[k:k + 1, cols] * sh[base:base + tc, :]
            q_ref[:, cols] = acc + b_ref[:, cols]
        _, _, l = _layernorm_parts(q_ref[...], g_ref[...], bb_ref[...])
        s_ref[...] = (l * jax.nn.sigmoid(l)).astype(bf16)
        if nx:
            _exchange_wait(x_refs, xo_refs, sems, True, i == n - 1)

    vec = pl.BlockSpec((1, D), _fix)
    til = pl.BlockSpec((tc, D), _row)
    x_in, x_out, x_shape, x_sems = _exchange_shapes(ride, True) if nx else ([], [], [], [])
    return pl.pallas_call(
        body, grid=(n,), in_specs=[til, pl.BlockSpec((HALO, D), _fix), vec, vec, vec] + x_in,
        out_specs=[til, til] + x_out, out_shape=[SDS((T, D), f32), SDS((T, D), bf16)] + x_shape,
        scratch_shapes=[pltpu.VMEM((tc + HALO, D), f32), pltpu.VMEM((tc + HALO, LANES), f32)] + x_sems, compiler_params=_cparams(1),
        name=name)(p, w_dw, b_dw, ln_g, ln_b, *ride)


def _conv_bwd(ds, q, p, a, gl, w_dw, ln_g, ln_b, name, ride=()):
    T = q.shape[0]
    tc = min(TC, T)
    n = T // tc
    nx = len(ride)

    def body(ds_ref, q_ref, p_ref, a_ref, gl_ref, w_ref, g_ref, bb_ref, *rest):
        x_refs, (da_ref, dgl_ref, acc_ref, dw_ref), xo_refs = rest[:nx], rest[nx:nx + 4], rest[nx + 4:2 * nx + 4]
        (ext, sh), sems = rest[2 * nx + 4:2 * nx + 6], rest[2 * nx + 6:]
        i = pl.program_id(0)
        if nx:
            _exchange_start(x_refs, xo_refs, sems, False, i == 0)

        @pl.when(i == 0)
        def _():
            acc_ref[...] = jnp.zeros_like(acc_ref)
            dw_ref[...] = jnp.zeros_like(dw_ref)
            ext[tc:tc + HALO, :] = jnp.zeros((HALO, D), f32)

        @pl.when(i > 0)
        def _():
            ext[tc:tc + HALO, :] = ext[0:HALO, :]

        gv = g_ref[...]
        z, rstd, l = _layernorm_parts(q_ref[...], gv, bb_ref[...])
        sg = jax.nn.sigmoid(l)
        dl = ds_ref[...] * (sg * (1.0 + l * (1.0 - sg)))
        dz = dl * gv
        dq = rstd * (dz - jnp.mean(dz, axis=-1, keepdims=True) - z * jnp.mean(dz * z, axis=-1, keepdims=True))
        ext[0:tc, :] = dq.astype(bf16).astype(f32)
        acc_ref[0:1, :] += jnp.sum(dl * z, axis=0, keepdims=True)
        acc_ref[1:2, :] += jnp.sum(dl, axis=0, keepdims=True)
        acc_ref[2:3, :] += jnp.sum(dq, axis=0, keepdims=True)
        groups = _tap_groups([KW - 1 - k for k in range(KW)])
        for cb in range(D // LANES):
            cols = slice(cb * LANES, (cb + 1) * LANES)
            pc = p_ref[:, cols].astype(f32)
            dp = jnp.zeros((tc, LANES), f32)
            for r, taps in groups.items():
                span = max(base for _, base in taps) + tc
                sh[0:span, :] = ext[r:r + span, cols]
                for k, base in taps:
                    sl = sh[base:base + tc, :]
                    dp = dp + w_ref[k:k + 1, cols] * sl
                    dw_ref[k:k + 1, cols] += jnp.sum(sl * pc, axis=0, keepdims=True)
            av = a_ref[:, cols].astype(f32)
            sgl = jax.nn.sigmoid(gl_ref[:, cols].astype(f32))
            da = dp * sgl
            dgl = dp * av * (sgl * (1.0 - sgl))
            da_ref[:, cols] = da.astype(bf16)
            dgl_ref[:, cols] = dgl.astype(bf16)
            acc_ref[3:4, cols] += jnp.sum(da, axis=0, keepdims=True)
            acc_ref[4:5, cols] += jnp.sum(dgl, axis=0, keepdims=True)
        if nx:
            _exchange_wait(x_refs, xo_refs, sems, False, i == n - 1)

    rev = lambda i: (n - 1 - i, 0)
    til = pl.BlockSpec((tc, D), rev)
    vec = pl.BlockSpec((1, D), _fix)
    x_in, x_out, x_shape, x_sems = _exchange_shapes(ride, False) if nx else ([], [], [], [])
    return pl.pallas_call(
        body, grid=(n,), in_specs=[til, til, til, til, til, pl.BlockSpec((HALO, D), _fix), vec, vec] + x_in,
        out_specs=[til, til, pl.BlockSpec((8, D), _fix), pl.BlockSpec((HALO, D), _fix)] + x_out,
        out_shape=[SDS((T, D), bf16), SDS((T, D), bf16), SDS((8, D), f32), SDS((HALO, D), f32)] + x_shape,
        scratch_shapes=[pltpu.VMEM((tc + HALO, D), f32), pltpu.VMEM((tc + HALO, LANES), f32)] + x_sems, compiler_params=_cparams(1),
        name=name)(ds, q, p, a, gl, w_dw, ln_g, ln_b, *ride)


def _tri(n, upper):
    r = lax.broadcasted_iota(jnp.int32, (n, n), 0)
    c = lax.broadcasted_iota(jnp.int32, (n, n), 1)
    return ((c >= r) if upper else (r >= c)).astype(f32)


def _forget_fwd(fl, fb, name):
    T = fl.shape[0]
    tc = min(TC, T)

    def body(fl_ref, fb_ref, cum_ref, carry):
        @pl.when(pl.program_id(0) == 0)
        def _():
            carry[...] = jnp.zeros_like(carry)
        xv = fl_ref[...] + fb_ref[...]
        lf = jnp.minimum(xv, 0.0) - jnp.log(1.0 + jnp.exp(-jnp.abs(xv)))
        cs = jnp.dot(_tri(tc, False), lf, preferred_element_type=f32, precision=lax.Precision.HIGHEST) + carry[0:1, :]
        cum_ref[...] = cs
        carry[0:1, :] = cs[tc - 1:tc, :]

    til = pl.BlockSpec((tc, LANES), _row)
    return pl.pallas_call(
        body, grid=(T // tc,), in_specs=[til, pl.BlockSpec((1, LANES), _fix)], out_specs=til,
        out_shape=SDS((T, LANES), f32), scratch_shapes=[pltpu.VMEM((8, LANES), f32)],
        compiler_params=_cparams(1), name=name)(fl, fb)


def _forget_bwd(dcum, fl, fb, name):
    T = fl.shape[0]
    tc = min(TC, T)
    n = T // tc

    def body(dc_ref, fl_ref, fb_ref, dfl_ref, acc_ref, carry):
        @pl.when(pl.program_id(0) == 0)
        def _():
            carry[...] = jnp.zeros_like(carry)
            acc_ref[...] = jnp.zeros_like(acc_ref)
        dlf = jnp.dot(_tri(tc, True), dc_ref[...], preferred_element_type=f32, precision=lax.Precision.HIGHEST) + carry[0:1, :]
        carry[0:1, :] = dlf[0:1, :]
        dfl = dlf * (1.0 - jax.nn.sigmoid(fl_ref[...] + fb_ref[...]))
        dfl_ref[...] = dfl.astype(bf16)
        acc_ref[0:1, :] += jnp.sum(dfl, axis=0, keepdims=True)

    rev = lambda i: (n - 1 - i, 0)
    til = pl.BlockSpec((tc, LANES), rev)
    return pl.pallas_call(
        body, grid=(n,), in_specs=[til, til, pl.BlockSpec((1, LANES), _fix)],
        out_specs=[til, pl.BlockSpec((8, LANES), _fix)], out_shape=[SDS((T, LANES), bf16), SDS((8, LANES), f32)],
        scratch_shapes=[pltpu.VMEM((8, LANES), f32)], compiler_params=_cparams(1), name=name)(dcum, fl, fb)


def _causal(s, n):
    r = lax.broadcasted_iota(jnp.int32, (n, n), 0)
    c = lax.broadcasted_iota(jnp.int32, (n, n), 1)
    return jnp.where(c <= r, s, NEG)


def _attn_fwd(q, k, v, cum_t, name, ride=()):
    T = q.shape[0]
    tq = min(TQ, T)
    n = T // tq
    nx = len(ride)

    def body(q_ref, k_ref, v_ref, ck_ref, *rest):
        x_refs, (o_ref, o32_ref, st_ref), xo_refs = rest[:nx], rest[nx:nx + 3], rest[nx + 3:2 * nx + 3]
        (m_sc, l_sc, acc_sc, res_sc), sems = rest[2 * nx + 3:2 * nx + 7], rest[2 * nx + 7:]
        i = pl.program_id(1)
        if nx:
            _exchange_start(x_refs, xo_refs, sems, True, (pl.program_id(0) == 0) & (i == 0))
        lane = lax.broadcasted_iota(jnp.int32, (1, LANES), 1)
        lo = lane < HD
        q2 = q_ref[...]
        zero = jnp.zeros_like(q2)
        qa = (jnp.where(lo, q2, zero), jnp.where(lo, zero, q2))
        m_sc[...] = jnp.full(m_sc.shape, NEG, f32)
        l_sc[...] = jnp.zeros_like(l_sc)
        acc_sc[...] = jnp.zeros_like(acc_sc)
        res_sc[...] = jnp.zeros_like(res_sc)

        def block(j, masked):
            off = pl.multiple_of(j * tq, tq)
            k2 = k_ref[pl.ds(off, tq), :]
            v2 = v_ref[pl.ds(off, tq), :]
            ss = [_dot_nt(qa[a], k2) - ck_ref[0, a:a + 1, pl.ds(off, tq)] for a in range(2)]
            if masked:
                ss = [_causal(t, tq) for t in ss]
            pbs, prs, alphas = [], [], []
            for a in range(2):
                m_old = m_sc[a]
                m_new = jnp.maximum(m_old, jnp.max(ss[a], axis=1, keepdims=True))
                alpha = jnp.exp(m_old - m_new)
                pm = jnp.exp(ss[a] - m_new)
                pb = pm.astype(bf16)
                prs.append((pm - pb.astype(f32)).astype(bf16))
                pbs.append(pb)
                alphas.append(alpha)
                l_sc[a] = alpha * l_sc[a] + jnp.sum(pm, axis=1, keepdims=True)
                m_sc[a] = m_new
            for a in range(2):
                acc_sc[a] = alphas[a] * acc_sc[a] + jnp.dot(pbs[a], v2, preferred_element_type=f32)
                res_sc[a] = alphas[a] * res_sc[a] + jnp.dot(prs[a], v2, preferred_element_type=f32)

        def step(j, carry):
            block(j, False)
            return carry

        lax.fori_loop(0, i, step, 0)
        block(i, True)
        o_ref[...] = jnp.where(lo, acc_sc[0] / l_sc[0], acc_sc[1] / l_sc[1]).astype(bf16)
        o32_ref[...] = jnp.where(lo, (acc_sc[0] + res_sc[0]) / l_sc[0], (acc_sc[1] + res_sc[1]) / l_sc[1])
        lse0 = m_sc[0] + jnp.log(l_sc[0])
        lse1 = m_sc[1] + jnp.log(l_sc[1])
        st_ref[0] = jnp.where(lane == 0, lse0, jnp.where(lane == 1, lse1, 0.0))
        if nx:
            _exchange_wait(x_refs, xo_refs, sems, True, (pl.program_id(0) == NP - 1) & (i == n - 1))

    full = lambda blk: pl.BlockSpec((T, LANES), blk)
    x_in, x_out, x_shape, x_sems = _exchange_shapes(ride, True) if nx else ([], [], [], [])
    return pl.pallas_call(
        body, grid=(NP, n),
        in_specs=[pl.BlockSpec((tq, LANES), lambda p, i: (i, p)), full(lambda p, i: (0, p)), full(lambda p, i: (0, p)),
                  pl.BlockSpec((1, 2, T), lambda p, i: (p, 0, 0))] + x_in,
        out_specs=[pl.BlockSpec((tq, LANES), lambda p, i: (i, p)), pl.BlockSpec((tq, LANES), lambda p, i: (i, p)),
                   pl.BlockSpec((1, tq, LANES), lambda p, i: (p, i, 0))] + x_out,
        out_shape=[SDS((T, H * HD), bf16), SDS((T, H * HD), f32), SDS((NP, T, LANES), f32)] + x_shape,
        scratch_shapes=[pltpu.VMEM((2, tq, 1), f32), pltpu.VMEM((2, tq, 1), f32), pltpu.VMEM((2, tq, LANES), f32),
                        pltpu.VMEM((2, tq, LANES), f32)] + x_sems,
        compiler_params=_cparams(2), name=name)(q, k, v, cum_t, *ride)


def _attn_stats(do, o, lse, name):
    T = do.shape[0]
    tm = min(4 * TM, T)

    def body(do_ref, o_ref, lse_ref, st_ref):
        lane = lax.broadcasted_iota(jnp.int32, (1, LANES), 1)
        prod = do_ref[...].astype(f32) * o_ref[...].astype(f32)
        d0 = jnp.sum(jnp.where(lane < HD, prod, 0.0), axis=1, keepdims=True)
        d1 = jnp.sum(jnp.where(lane < HD, 0.0, prod), axis=1, keepdims=True)
        st_ref[0] = jnp.where(lane < 2, lse_ref[0], jnp.where(lane == 2, d0, jnp.where(lane == 3, d1, 0.0)))

    til = pl.BlockSpec((tm, LANES), lambda p, i: (i, p))
    stt = pl.BlockSpec((1, tm, LANES), lambda p, i: (p, i, 0))
    return pl.pallas_call(
        body, grid=(NP, T // tm), in_specs=[til, til, stt], out_specs=stt, out_shape=SDS((NP, T, LANES), f32),
        compiler_params=_cparams(2), name=name)(do, o, lse)


def _attn_bwd(q, k, v, do, st, cum_t, name, ride=()):
    T = q.shape[0]
    tq = min(TQ, T)
    n = T // tq
    nx = len(ride)

    def body(q_ref, k_ref, v_ref, do_ref, st_ref, ck_ref, *rest):
        x_refs, (dq_ref, dk_ref, dv_ref, dck_ref), xo_refs = rest[:nx], rest[nx:nx + 4], rest[nx + 4:2 * nx + 4]
        (dk_sc, dv_sc, dck_sc), sems = rest[2 * nx + 4:2 * nx + 7], rest[2 * nx + 7:]
        j = pl.program_id(1)
        if nx:
            _exchange_start(x_refs, xo_refs, sems, False, (pl.program_id(0) == 0) & (j == 0))
        lane = lax.broadcasted_iota(jnp.int32, (1, LANES), 1)
        lo = lane < HD

        @pl.when(j == 0)
        def _():
            dq_ref[...] = jnp.zeros_like(dq_ref)

        k2 = k_ref[...]
        v2 = v_ref[...]
        zero = jnp.zeros_like(k2)
        ka = (jnp.where(lo, k2, zero), jnp.where(lo, zero, k2))
        va = (jnp.where(lo, v2, zero), jnp.where(lo, zero, v2))
        dk_sc[...] = jnp.zeros_like(dk_sc)
        dv_sc[...] = jnp.zeros_like(dv_sc)
        dck_sc[...] = jnp.zeros_like(dck_sc)

        def block(i, masked):
            off = pl.multiple_of(i * tq, tq)
            q2 = q_ref[pl.ds(off, tq), :]
            do2 = do_ref[pl.ds(off, tq), :]
            stt = st_ref[0, pl.ds(off, tq), :]
            parts = []
            for a in range(2):
                s = _dot_nt(q2, ka[a]) - ck_ref[0, a:a + 1, :]
                if masked:
                    s = _causal(s, tq)
                pm = jnp.exp(s - stt[:, a:a + 1])
                dp = _dot_nt(do2, va[a])
                dsm = pm * (dp - stt[:, 2 + a:3 + a])
                dsb = dsm.astype(bf16)
                dv_sc[a] += _dot_tn(pm.astype(bf16), do2)
                dk_sc[a] += _dot_tn(dsb, q2)
                dck_sc[a:a + 1, :] -= jnp.sum(dsm, axis=0, keepdims=True)
                parts.append(jnp.dot(dsb, k2, preferred_element_type=f32))
            dq_ref[pl.ds(off, tq), :] += jnp.where(lo, parts[0], parts[1])

        block(j, True)

        def step(i, carry):
            block(i, False)
            return carry

        lax.fori_loop(j + 1, n, step, 0)
        dk_ref[...] = jnp.where(lo, dk_sc[0], dk_sc[1]).astype(bf16)
        dv_ref[...] = jnp.where(lo, dv_sc[0], dv_sc[1]).astype(bf16)
        dck_ref[0] = dck_sc[0:2, :]

        @pl.when(j == n - 1)
        def _():
            dq_ref[...] = dq_ref[...] * (HD ** -0.5)

        if nx:
            _exchange_wait(x_refs, xo_refs, sems, False, (pl.program_id(0) == NP - 1) & (j == n - 1))

    full = lambda: pl.BlockSpec((T, LANES), lambda p, j: (0, p))
    kvb = lambda: pl.BlockSpec((tq, LANES), lambda p, j: (j, p))
    ckb = lambda: pl.BlockSpec((1, 2, tq), lambda p, j: (p, 0, j))
    x_in, x_out, x_shape, x_sems = _exchange_shapes(ride, False) if nx else ([], [], [], [])
    return pl.pallas_call(
        body, grid=(NP, n),
        in_specs=[full(), kvb(), kvb(), full(), pl.BlockSpec((1, T, LANES), lambda p, j: (p, 0, 0)), ckb()] + x_in,
        out_specs=[full(), kvb(), kvb(), ckb()] + x_out,
        out_shape=[SDS((T, H * HD), f32), SDS((T, H * HD), bf16), SDS((T, H * HD), bf16), SDS((NP, 2, T), f32)] + x_shape,
        scratch_shapes=[pltpu.VMEM((2, tq, LANES), f32), pltpu.VMEM((2, tq, LANES), f32), pltpu.VMEM((8, tq), f32)] + x_sems,
        compiler_params=_cparams(2), name=name)(q, k, v, do, st, cum_t, *ride)


def _ada_fwd(c_all, w_cat, name):
    n = w_cat.shape[1]
    tn = 256

    def body(c_ref, w_ref, o_ref):
        cv = c_ref[...]
        o_ref[...] = jnp.dot((cv * jax.nn.sigmoid(cv)).astype(bf16), w_ref[...].astype(bf16), preferred_element_type=f32)

    return pl.pallas_call(
        body, grid=(n // tn,), in_specs=[pl.BlockSpec((NDEV, D), _fix), pl.BlockSpec((D, tn), lambda i: (0, i))],
        out_specs=pl.BlockSpec((NDEV, tn), lambda i: (0, i)), out_shape=SDS((NDEV, n), f32),
        compiler_params=_cparams(1), name=name)(c_all, w_cat)


def _ada_bwd(c_all_t, dsel, name):
    n = dsel.shape[1]
    tn = 256

    def body(c_ref, d_ref, o_ref):
        cv = c_ref[...]
        ca = cv * jax.nn.sigmoid(cv)
        acc = ca[:, 0:1] * d_ref[0:1, :]
        for b in range(1, NDEV):
            acc = acc + ca[:, b:b + 1] * d_ref[b:b + 1, :]
        o_ref[...] = acc

    return pl.pallas_call(
        body, grid=(n // tn,), in_specs=[pl.BlockSpec((D, NDEV), _fix), pl.BlockSpec((NDEV, tn), lambda i: (0, i))],
        out_specs=pl.BlockSpec((D, tn), lambda i: (0, i)), out_shape=SDS((D, n), f32),
        compiler_params=_cparams(1), name=name)(c_all_t, dsel)


def _sum_parts(parts, name):
    R = parts.shape[1]

    def body(p_ref, o_ref):
        acc = p_ref[0]
        for j in range(1, NDEV):
            acc = acc + p_ref[j]
        o_ref[...] = acc

    return pl.pallas_call(body, out_shape=SDS((R, LANES), f32), name=name)(parts)


def _adamw(g_parts, w, m, v, name):
    n_parts, R, C = g_parts.shape
    tr = next(t for t in (256, 128, 64, 32, 16, 8) if R % t == 0)
    c1 = 1.0 / (1.0 - ADAM_B1 ** ADAM_STEP)
    c2 = 1.0 / (1.0 - ADAM_B2 ** ADAM_STEP)

    def body(g_ref, w_ref, m_ref, v_ref, go_ref, d_ref, mo_ref, vo_ref):
        g = g_ref[0].astype(f32)
        for j in range(1, n_parts):
            g = g + g_ref[j].astype(f32)
        mn = ADAM_B1 * m_ref[...] + (1.0 - ADAM_B1) * g
        vn = ADAM_B2 * v_ref[...] + (1.0 - ADAM_B2) * (g * g)
        go_ref[...] = g
        mo_ref[...] = mn
        vo_ref[...] = vn
        d_ref[...] = -ADAM_LR * ((mn * c1) / (jnp.sqrt(vn * c2) + ADAM_EPS) + ADAM_WD * w_ref[...])

    til = pl.BlockSpec((tr, C), _row)
    out = SDS((R, C), f32)
    return pl.pallas_call(
        body, grid=(R // tr,), in_specs=[pl.BlockSpec((n_parts, tr, C), lambda i: (0, i, 0)), til, til, til],
        out_specs=[til, til, til, til], out_shape=[out, out, out, out],
        compiler_params=_cparams(1), name=name)(g_parts, w, m, v)


def _pad_rows(flat, cols, mult):
    n = flat.shape[-1]
    rows = -(-n // cols)
    rows = -(-rows // mult) * mult
    pad = [(0, 0)] * (flat.ndim - 1) + [(0, rows * cols - n)]
    return jnp.pad(flat, pad).reshape(flat.shape[:-1] + (rows, cols))


def _split_flat(flat, shapes):
    out, off = {}, 0
    for name, shp in shapes:
        n = 1
        for d in shp:
            n *= d
        out[name] = flat[off:off + n].reshape(shp)
        off += n
    return out


def kernel(x, c, mix_norm_g, mix_ada_w, mix_ada_b, ffn_norm_g, ffn_ada_w, ffn_ada_b, ffn_w_in, ffn_w_out, conv_w_in, conv_b_in, conv_w_dw, conv_b_dw, conv_ln_g, conv_ln_b, conv_w_out, conv_b_out, kv_norm_g, kv_ada_w, kv_ada_b, kv_w, forget_b, attn_w_q, attn_w_o, final_norm_g, loss_target, m_mix_norm_g, m_mix_ada_w, m_mix_ada_b, m_ffn_norm_g, m_ffn_ada_w, m_ffn_ada_b, m_ffn_w_in, m_ffn_w_out, m_conv_w_in, m_conv_b_in, m_conv_w_dw, m_conv_b_dw, m_conv_ln_g, m_conv_ln_b, m_conv_w_out, m_conv_b_out, m_kv_norm_g, m_kv_ada_w, m_kv_ada_b, m_kv_w, m_forget_b, m_attn_w_q, m_attn_w_o, m_final_norm_g, v_mix_norm_g, v_mix_ada_w, v_mix_ada_b, v_ffn_norm_g, v_ffn_ada_w, v_ffn_ada_b, v_ffn_w_in, v_ffn_w_out, v_conv_w_in, v_conv_b_in, v_conv_w_dw, v_conv_b_dw, v_conv_ln_g, v_conv_ln_b, v_conv_w_out, v_conv_b_out, v_kv_norm_g, v_kv_ada_w, v_kv_ada_b, v_kv_w, v_forget_b, v_attn_w_q, v_attn_w_o, v_final_norm_g):
    W = dict(mix_norm_g=mix_norm_g, mix_ada_w=mix_ada_w, mix_ada_b=mix_ada_b, ffn_norm_g=ffn_norm_g, ffn_ada_w=ffn_ada_w, ffn_ada_b=ffn_ada_b, ffn_w_in=ffn_w_in, ffn_w_out=ffn_w_out, conv_w_in=conv_w_in, conv_b_in=conv_b_in, conv_w_dw=conv_w_dw, conv_b_dw=conv_b_dw, conv_ln_g=conv_ln_g, conv_ln_b=conv_ln_b, conv_w_out=conv_w_out, conv_b_out=conv_b_out, kv_norm_g=kv_norm_g, kv_ada_w=kv_ada_w, kv_ada_b=kv_ada_b, kv_w=kv_w, forget_b=forget_b, attn_w_q=attn_w_q, attn_w_o=attn_w_o, final_norm_g=final_norm_g)
    M = dict(mix_norm_g=m_mix_norm_g, mix_ada_w=m_mix_ada_w, mix_ada_b=m_mix_ada_b, ffn_norm_g=m_ffn_norm_g, ffn_ada_w=m_ffn_ada_w, ffn_ada_b=m_ffn_ada_b, ffn_w_in=m_ffn_w_in, ffn_w_out=m_ffn_w_out, conv_w_in=m_conv_w_in, conv_b_in=m_conv_b_in, conv_w_dw=m_conv_w_dw, conv_b_dw=m_conv_b_dw, conv_ln_g=m_conv_ln_g, conv_ln_b=m_conv_ln_b, conv_w_out=m_conv_w_out, conv_b_out=m_conv_b_out, kv_norm_g=m_kv_norm_g, kv_ada_w=m_kv_ada_w, kv_ada_b=m_kv_ada_b, kv_w=m_kv_w, forget_b=m_forget_b, attn_w_q=m_attn_w_q, attn_w_o=m_attn_w_o, final_norm_g=m_final_norm_g)
    V = dict(mix_norm_g=v_mix_norm_g, mix_ada_w=v_mix_ada_w, mix_ada_b=v_mix_ada_b, ffn_norm_g=v_ffn_norm_g, ffn_ada_w=v_ffn_ada_w, ffn_ada_b=v_ffn_ada_b, ffn_w_in=v_ffn_w_in, ffn_w_out=v_ffn_w_out, conv_w_in=v_conv_w_in, conv_b_in=v_conv_b_in, conv_w_dw=v_conv_w_dw, conv_b_dw=v_conv_b_dw, conv_ln_g=v_conv_ln_g, conv_ln_b=v_conv_ln_b, conv_w_out=v_conv_w_out, conv_b_out=v_conv_b_out, kv_norm_g=v_kv_norm_g, kv_ada_w=v_kv_ada_w, kv_ada_b=v_kv_ada_b, kv_w=v_kv_w, forget_b=v_forget_b, attn_w_q=v_attn_w_q, attn_w_o=v_attn_w_o, final_norm_g=v_final_norm_g)
    names = list(W)
    T = x.shape[1]
    me = _my_index()
    x0 = x[0]
    tgt = loss_target[0]
    row = lambda vct: vct.reshape(1, -1)

    small_names = ("conv_b_in", "conv_w_dw", "conv_b_dw", "conv_ln_g", "conv_ln_b", "conv_b_out")
    small_loc = jnp.concatenate([c.reshape(-1)] + [W[n].reshape(-1) for n in small_names])
    sg = _exchange([_pad_rows(small_loc, LANES, 8)], True, "gather_small")[0].reshape(NDEV, -1)
    c_all = sg[:, :D]
    off = D
    b_in = sg[:, off:off + 2 * D // NDEV].reshape(1, 2 * D); off += 2 * D // NDEV
    cl = D // NDEV
    w_dw = sg[:, off:off + KW * cl].reshape(NDEV, KW, cl).transpose(1, 0, 2).reshape(KW, D); off += KW * cl
    w_dw = jnp.pad(w_dw, ((0, HALO - KW), (0, 0))).astype(bf16).astype(f32)
    b_dw = sg[:, off:off + cl].reshape(1, D); off += cl
    ln_g = sg[:, off:off + cl].reshape(1, D); off += cl
    ln_b = sg[:, off:off + cl].reshape(1, D); off += cl
    b_out = sg[:, off:off + cl].reshape(1, D)

    cat_ada = lambda s: jnp.concatenate([s["mix_ada_w"][0], s["mix_ada_w"][1], s["ffn_ada_w"][0], s["ffn_ada_w"][1], s["kv_ada_w"]], axis=1)
    w_cat = cat_ada(W)
    ada_loc = _ada_fwd(c_all, w_cat, "ada_fwd")
    ada_all = _exchange([ada_loc], True, "gather_ada")[0]
    ada_me = lax.dynamic_index_in_dim(ada_all, me, axis=1, keepdims=False)
    ada_bias = (mix_ada_b[0], mix_ada_b[1], ffn_ada_b[0], ffn_ada_b[1], kv_ada_b)
    ada, off = [], 0
    for nl, bias in zip(ADA_LOC, ada_bias):
        full = ada_me[:, off:off + nl].reshape(-1) + bias
        ada.append([row(t) for t in jnp.split(full, full.shape[0] // D)])
        off += nl
    (sh_m0, sc_m0, gt_m0), (sh_m1, sc_m1, gt_m1), (sh_f0, sc_f0, gt_f0), (sh_f1, sc_f1, gt_f1), (sh_kv, sc_kv) = ada

    pad_in = lambda src, l: jnp.pad(src["ffn_w_in"][l], ((0, 0), (0, FSP - FS)))
    rows_a = lambda src: jnp.concatenate([src["ffn_w_out"][0], src["attn_w_q"][0]])
    rows_b = lambda src: jnp.concatenate([src["ffn_w_out"][1], src["attn_w_o"][0]])
    as_bf = lambda arrs: [t.astype(bf16) for t in arrs]
    fo, sq_rows = F // NDEV, D // NDEV
    g_ci, g_co = _exchange(as_bf([conv_w_in[0], conv_w_out[0]]), True, "gather_weights")
    soon = as_bf([pad_in(W, 0)])
    next_ = as_bf([rows_a(W), kv_w])
    late = as_bf([pad_in(W, 1), rows_b(W)])

    def w_out_of(g_r):
        t = g_r[:, :fo].reshape(NDEV // 2, FS, D)
        return jnp.pad(t, ((0, 0), (0, FSP - FS), (0, 0))).reshape(FP, D)

    conv_in_full = g_ci.transpose(1, 0, 2).reshape(D, 2 * D)
    wc_a, wc_g = conv_in_full[:, :D], conv_in_full[:, D:]
    wc_o = g_co.reshape(D, D)
    zeros_d = jnp.zeros((1, D), f32)
    fb = jnp.pad(forget_b, (0, LANES - H)).reshape(1, LANES)

    h0 = _normmod(x0, row(mix_norm_g[0]), sh_m0, sc_m0, "norm_mix0")
    a0, gl0, p0 = _mm_gated(h0, wc_a, wc_g, b_in[:, :D], b_in[:, D:], False, "conv_in")
    q0, s0, w_sh0 = _conv_fwd(p0, w_dw, b_dw, ln_g, ln_b, "conv_dw", ride=soon)
    x1, y0 = _mm_res(s0, wc_o, b_out, x0, gt_m0, "conv_out")

    def ffn_fwd(xin, l, sh, sc, gt, w_sh, w_out=None, ride=()):
        h = _normmod(xin, row(ffn_norm_g[l]), sh, sc, f"norm_ffn{l}")
        ug, uu, act, *got = _ffn_in(h, w_sh, f"ffn_in{l}", ride=ride)
        if w_out is None:
            w_out = w_out_of(got[0])
        xo, y = _mm_res(act, w_out, zeros_d, xin, gt, f"ffn_out{l}")
        return xo, (h, ug, uu, act, y, w_sh, w_out), got

    x2, ffn0, (g_ra, g_kv) = ffn_fwd(x1, 0, sh_f0, sc_f0, gt_f0, w_sh0, ride=next_)
    w_q = g_ra[:, fo:].reshape(D, D)
    kv_full = g_kv.transpose(1, 0, 2).reshape(D, -1)
    w_k, w_v = kv_full[:, :D], kv_full[:, D:2 * D]
    w_f = jnp.pad(kv_full[:, 2 * D:], ((0, 0), (0, LANES - H)))

    hk = _normmod(x2, row(kv_norm_g), sh_kv, sc_kv, "norm_kv")
    k_sh = _mm(hk, w_k, bf16, 1.0, "proj_k")
    v_sh = _mm(hk, w_v, bf16, 1.0, "proj_v")
    fl = _mm(hk, w_f, f32, 1.0, "proj_f")
    cum = _forget_fwd(fl, fb, "forget_fwd")
    cum_t = cum[:, :H].T.reshape(NP, 2, T)

    h2 = _normmod(x2, row(mix_norm_g[1]), sh_m1, sc_m1, "norm_mix1")
    qh = _mm(h2, w_q, bf16, HD ** -0.5, "proj_q")
    o, o32, lse, w_sh1, g_rb = _attn_fwd(qh, k_sh, v_sh, cum_t, "attn_fwd", ride=late)
    w_out1, w_o = w_out_of(g_rb), g_rb[:, fo:].reshape(D, D)
    x3, y1 = _mm_res(o, w_o, zeros_d, x2, gt_m1, "attn_out")

    x4, ffn1, _ = ffn_fwd(x3, 1, sh_f1, sc_f1, gt_f1, w_sh1, w_out1)

    dx4, acc_fin = _final_bwd(x4, row(final_norm_g), tgt, "final_bwd")

    d_ada = {}
    by_rows = lambda g: g.reshape(NDEV, sq_rows, D)

    def ffn_bwd(dx_out, xin, l, sc, gt, saved):
        h, ug, uu, act, y, w_sh, w_out = saved
        dyb, acc_r = _res_in(dx_out, y, gt, f"ffn_res_bwd{l}")
        dug, duu = _mm_nt_swiglu(dyb, w_out, ug, uu, f"ffn_dact{l}")
        g_out = _mm_tn(act, dyb, f"ffn_dw_out{l}").reshape(NDEV // 2, FSP, D)[:, :FS].reshape(NDEV, fo, D)
        g_in = _mm_tn_shards(h, duu, FSP, f"ffn_dw_up{l}", into=_mm_tn_shards(h, dug, FSP, f"ffn_dw_gate{l}"))
        dh = _ffn_dh(dug, duu, w_sh, f"ffn_dh{l}")
        dxi, acc_n = _normmod_bwd(dh, xin, row(ffn_norm_g[l]), sc, dx_out, f"norm_ffn_bwd{l}")
        return dxi, g_in, g_out, [acc_n[0:1], acc_n[1:2], acc_r[0:1]], acc_n[2]

    dx3, g_in1, g_out1, d_ada[("ffn", 1)], dg_ffn1 = ffn_bwd(dx4, x3, 1, sc_f1, gt_f1, ffn1)

    dyb, acc_r = _res_in(dx3, y1, gt_m1, "attn_res_bwd")
    do = _mm_nt([(dyb, w_o)], bf16, "attn_do")
    g_wo = _mm_tn(o, dyb, "attn_dw_o")
    st = _attn_stats(do, o32, lse, "attn_stats")
    leave_b = as_bf([g_in1, jnp.concatenate([g_out1, by_rows(g_wo)], axis=1)])
    dq, dk, dv, dck, r_in1, r_rb = _attn_bwd(qh, k_sh, v_sh, do, st, cum_t, "attn_bwd", ride=leave_b)
    g_wq = _mm_tn(h2, dq, "attn_dw_q")
    dh2 = _mm_nt([(dq, w_q)], f32, "attn_dh")
    dx2, acc_n = _normmod_bwd(dh2, x2, row(mix_norm_g[1]), sc_m1, dx3, "norm_mix_bwd1")
    d_ada[("mix", 1)] = [acc_n[0:1], acc_n[1:2], acc_r[0:1]]
    dg_mix1 = acc_n[2]

    dcum = jnp.pad(dck.reshape(H, T).T, ((0, 0), (0, LANES - H)))
    dfl, acc_f = _forget_bwd(dcum, fl, fb, "forget_bwd")
    g_kvw = jnp.concatenate([_mm_tn(hk, dk, "kv_dw_k"), _mm_tn(hk, dv, "kv_dw_v"), _mm_tn(hk, dfl, "kv_dw_f")[:, :H]], axis=1)
    dhk = _mm_nt([(dk, w_k), (dv, w_v), (dfl, w_f)], f32, "kv_dh")
    dx2, acc_n = _normmod_bwd(dhk, x2, row(kv_norm_g), sc_kv, dx2, "norm_kv_bwd")
    d_ada[("kv", 0)] = [acc_n[0:1], acc_n[1:2]]
    dg_kv = acc_n[2]

    dx1, g_in0, g_out0, d_ada[("ffn", 0)], dg_ffn0 = ffn_bwd(dx2, x1, 0, sc_f0, gt_f0, ffn0)

    dyb, acc_r = _res_in(dx1, y0, gt_m0, "conv_res_bwd")
    dsw = _mm_nt([(dyb, wc_o)], f32, "conv_ds")
    g_co_out = _mm_tn(s0, dyb, "conv_dw_out")
    leave_a = as_bf([g_in0, jnp.concatenate([g_out0, by_rows(g_wq)], axis=1), g_kvw.reshape(D, NDEV, -1).transpose(1, 0, 2)])
    da, dgl, acc_c, dw_dw, r_in0, r_ra, r_kv = _conv_bwd(dsw, q0, p0, a0, gl0, w_dw, ln_g, ln_b, "conv_bwd", ride=leave_a)
    cs = 2 * D // NDEV
    g_ci_out = _mm_tn_shards(h0, dgl, cs, "conv_dw_g", into=_mm_tn_shards(h0, da, cs, "conv_dw_a"))
    dh0, r_ci, r_co = _mm_nt([(da, wc_a), (dgl, wc_g)], f32, "conv_dh", ride=as_bf([g_ci_out, by_rows(g_co_out)]))
    dx0, acc_n = _normmod_bwd(dh0, x0, row(mix_norm_g[0]), sc_m0, dx1, "norm_mix_bwd0")
    d_ada[("mix", 0)] = [acc_n[0:1], acc_n[1:2], acc_r[0:1]]
    dg_mix0 = acc_n[2]

    vec = [t.reshape(-1) for key in [(s[0], s[1]) for s in ADA_SEG] for t in d_ada[key]]
    vec += [dg_mix0, dg_mix1, dg_ffn0, dg_ffn1, dg_kv, acc_fin[0]]
    vec += [acc_f[0], acc_fin[1, :LANES]]
    vec += [acc_c[3], acc_c[4], dw_dw[:KW].reshape(-1), acc_c[2], acc_c[0], acc_c[1], acc_r[1]]
    small_parts = _exchange([_pad_rows(jnp.concatenate(vec), LANES, 8)], True, "gather_partials")[0]
    small_sum = _sum_parts(small_parts, "sum_partials").reshape(-1)
    d_ada_all = small_parts.reshape(NDEV, -1)[:, :ADA_TOT]
    off = 0
    gsm = {}
    ada_b_sum = []
    for _, _, n in ADA_SEG:
        ada_b_sum.append(small_sum[off:off + n]); off += n
    gsm["mix_ada_b"] = jnp.stack(ada_b_sum[0:2])
    gsm["ffn_ada_b"] = jnp.stack(ada_b_sum[2:4])
    gsm["kv_ada_b"] = ada_b_sum[4]
    gsm["mix_norm_g"] = small_sum[off:off + 2 * D].reshape(2, D); off += 2 * D
    gsm["ffn_norm_g"] = small_sum[off:off + 2 * D].reshape(2, D); off += 2 * D
    gsm["kv_norm_g"] = small_sum[off:off + D]; off += D
    gsm["final_norm_g"] = small_sum[off:off + D]; off += D
    gsm["forget_b"] = small_sum[off:off + H]; off += LANES
    loss = small_sum[off]; off += LANES
    sl = lambda full, width: lax.dynamic_slice_in_dim(full, me * width, width, axis=full.ndim - 1)
    gsm["conv_b_in"] = sl(small_sum[off:off + 2 * D].reshape(1, 2 * D), 2 * D // NDEV); off += 2 * D
    gsm["conv_w_dw"] = sl(small_sum[off:off + KW * D].reshape(1, KW, D), cl); off += KW * D
    for n in ("conv_b_dw", "conv_ln_g", "conv_ln_b", "conv_b_out"):
        gsm[n] = sl(small_sum[off:off + D].reshape(1, D), cl); off += D

    dsel, off = [], 0
    for (_, _, n), nl in zip(ADA_SEG, ADA_LOC):
        dsel.append(lax.dynamic_slice_in_dim(d_ada_all[:, off:off + n], me * nl, nl, axis=1)); off += n
    g_ada = _ada_bwd(c_all.T, jnp.concatenate(dsel, axis=1), "ada_bwd")
    res_ada = _adamw(g_ada[None], w_cat, cat_ada(M), cat_ada(V), "adamw_ada")

    res_in0 = _adamw(r_in0, *[pad_in(s, 0) for s in (W, M, V)], "adamw_ffn_in0")
    res_in1 = _adamw(r_in1, *[pad_in(s, 1) for s in (W, M, V)], "adamw_ffn_in1")
    res_ra = _adamw(r_ra, *[rows_a(s) for s in (W, M, V)], "adamw_rows_a")
    res_rb = _adamw(r_rb, *[rows_b(s) for s in (W, M, V)], "adamw_rows_b")
    res_ci = _adamw(r_ci, *[s["conv_w_in"][0] for s in (W, M, V)], "adamw_conv_in")
    res_co = _adamw(r_co, *[s["conv_w_out"][0] for s in (W, M, V)], "adamw_conv_out")
    res_kv = _adamw(r_kv, *[s["kv_w"] for s in (W, M, V)], "adamw_kv")
    rest = [n for n in names if n not in MAIN and n not in ("mix_ada_w", "ffn_ada_w", "kv_ada_w")]
    pack_rest = lambda src: _pad_rows(jnp.concatenate([src[n].reshape(-1) for n in rest]), D, 256)
    res_rest = _adamw(pack_rest(gsm)[None], pack_rest(W), pack_rest(M), pack_rest(V), "adamw_rest")

    outs = []
    for k in range(4):
        ur = _split_flat(res_rest[k].reshape(-1), [(n, W[n].shape) for n in rest])
        ra, a0_, a2_ = res_ada[k], ADA_LOC[0], ADA_LOC[2]
        ur["mix_ada_w"] = jnp.stack([ra[:, 0:a0_], ra[:, a0_:2 * a0_]])
        ur["ffn_ada_w"] = jnp.stack([ra[:, 2 * a0_:2 * a0_ + a2_], ra[:, 2 * a0_ + a2_:2 * a0_ + 2 * a2_]])
        ur["kv_ada_w"] = ra[:, 2 * a0_ + 2 * a2_:]
        ur["ffn_w_in"] = jnp.stack([res_in0[k][:, :FS], res_in1[k][:, :FS]])
        ur["conv_w_in"] = res_ci[k][None]
        ur["conv_w_out"] = res_co[k][None]
        ur["kv_w"] = res_kv[k]
        ur["ffn_w_out"] = jnp.stack([res_ra[k][:fo], res_rb[k][:fo]])
        ur["attn_w_q"] = res_ra[k][fo:][None]
        ur["attn_w_o"] = res_rb[k][fo:][None]
        outs.append(ur)
    grads, deltas, new_m, new_v = outs
    return (loss, dx0[None], *[grads[n] for n in names], *[deltas[n] for n in names],
            *[new_m[n] for n in names], *[new_v[n] for n in names])
```

```python
import functools

import jax
import jax.numpy as jnp
from jax import lax
from jax.experimental import pallas as pl
from jax.experimental.pallas import tpu as pltpu

f32, bf16 = jnp.float32, jnp.bfloat16
SDS = jax.ShapeDtypeStruct

D = 1024
F = 2816
H = 16
HD = 64
NP = H // 2
KW = 31
HALO = 32
NDEV = 8
FS = 2 * F // NDEV
FSP = 768
FP = 4 * FSP
EPS = 1e-6
NEG = -1e30
LANES = 128

ADAM_LR, ADAM_B1, ADAM_B2, ADAM_EPS, ADAM_WD, ADAM_STEP = 0.001, 0.9, 0.999, 1e-08, 0.01, 10

TM = 512
TMM = 1024
TC = 256
TQ = 1024
VMEM_LIMIT = 56 << 20

MAIN = ("ffn_w_in", "ffn_w_out", "conv_w_in", "conv_w_out", "kv_w", "attn_w_q", "attn_w_o")
ADA_SEG = (("mix", 0, 3 * D), ("mix", 1, 3 * D), ("ffn", 0, 3 * D), ("ffn", 1, 3 * D), ("kv", 0, 2 * D))
ADA_LOC = tuple(n // NDEV for _, _, n in ADA_SEG)
ADA_COLS = sum(ADA_LOC)
ADA_TOT = sum(n for _, _, n in ADA_SEG)


def _cparams(n_axes):
    return pltpu.CompilerParams(dimension_semantics=("arbitrary",) * n_axes, vmem_limit_bytes=VMEM_LIMIT)


def _mesh_pos():
    return lax.axis_index("x"), lax.axis_index("y"), lax.axis_index("c")


def _my_index():
    mx, my, mc = _mesh_pos()
    return 4 * mx + 2 * my + mc


def _peer(k, mx, my, mc):
    px = (1 - mx) if k & 4 else mx
    py = (1 - my) if k & 2 else my
    pc = (1 - mc) if k & 1 else mc
    return (px, py, pc), 4 * px + 2 * py + pc


def _exchange_copies(x_refs, o_refs, sems, gather):
    send_sems, recv_sems, local_sems = sems
    mx, my, mc = _mesh_pos()
    me = 4 * mx + 2 * my + mc
    copies = []
    for a, (x_ref, o_ref) in enumerate(zip(x_refs, o_refs)):
        copies.append(pltpu.make_async_copy(x_ref if gather else x_ref.at[me], o_ref.at[me], local_sems.at[a]))
        for k in range(1, NDEV):
            peer, pidx = _peer(k, mx, my, mc)
            sem = a * (NDEV - 1) + k - 1
            copies.append(pltpu.make_async_remote_copy(
                src_ref=x_ref if gather else x_ref.at[pidx], dst_ref=o_ref.at[me],
                send_sem=send_sems.at[sem], recv_sem=recv_sems.at[sem],
                device_id=peer, device_id_type=pl.DeviceIdType.MESH))
    return copies


def _exchange_shapes(xs, gather):
    n = len(xs)
    hbm = pl.BlockSpec(memory_space=pl.ANY)
    outs = [SDS((NDEV,) + tuple(x.shape if gather else x.shape[1:]), x.dtype) for x in xs]
    sems = [pltpu.SemaphoreType.DMA((n * (NDEV - 1),)), pltpu.SemaphoreType.DMA((n * (NDEV - 1),)), pltpu.SemaphoreType.DMA((n,))]
    return [hbm] * n, [hbm] * n, outs, sems


def _exchange_start(x_refs, o_refs, sems, gather, first):
    @pl.when(first)
    def _():
        for cp in _exchange_copies(x_refs, o_refs, sems, gather):
            cp.start()


def _exchange_wait(x_refs, o_refs, sems, gather, last):
    @pl.when(last)
    def _():
        for cp in _exchange_copies(x_refs, o_refs, sems, gather):
            cp.wait()


def _exchange(xs, gather, name):
    n = len(xs)

    def body(*refs):
        copies = _exchange_copies(refs[:n], refs[n:2 * n], refs[2 * n:], gather)
        for cp in copies:
            cp.start()
        for cp in copies:
            cp.wait()

    in_specs, out_specs, outs, sems = _exchange_shapes(xs, gather)
    return pl.pallas_call(body, out_shape=outs, in_specs=in_specs, out_specs=out_specs, scratch_shapes=sems, name=name)(*xs)


def _row(i):
    return (i, 0)


def _fix(i):
    return (0, 0)


def _normmod(x, g, shift, scale, name):
    T = x.shape[0]
    tm = min(TM, T)

    def body(x_ref, g_ref, sh_ref, sc_ref, h_ref):
        xv = x_ref[...]
        r = lax.rsqrt(jnp.mean(xv * xv, axis=-1, keepdims=True) + EPS)
        hn = (xv * r) * g_ref[...]
        h_ref[...] = (hn * (1.0 + sc_ref[...]) + sh_ref[...]).astype(bf16)

    vec = pl.BlockSpec((1, D), _fix)
    return pl.pallas_call(
        body, grid=(T // tm,), in_specs=[pl.BlockSpec((tm, D), _row), vec, vec, vec],
        out_specs=pl.BlockSpec((tm, D), _row), out_shape=SDS((T, D), bf16),
        compiler_params=_cparams(1), name=name)(x, g, shift, scale)


def _normmod_bwd(dh, x, g, scale, dx_res, name):
    T = x.shape[0]
    tm = min(TM, T)

    def body(dh_ref, x_ref, g_ref, sc_ref, res_ref, dx_ref, acc_ref):
        @pl.when(pl.program_id(0) == 0)
        def _():
            acc_ref[...] = jnp.zeros_like(acc_ref)
        xv = x_ref[...]
        dhv = dh_ref[...]
        gv = g_ref[...]
        r = lax.rsqrt(jnp.mean(xv * xv, axis=-1, keepdims=True) + EPS)
        xn = xv * r
        dhn = dhv * (1.0 + sc_ref[...])
        dxn = dhn * gv
        dx_ref[...] = res_ref[...] + r * (dxn - xn * jnp.mean(dxn * xn, axis=-1, keepdims=True))
        acc_ref[0:1, :] += jnp.sum(dhv, axis=0, keepdims=True)
        acc_ref[1:2, :] += jnp.sum(dhv * (xn * gv), axis=0, keepdims=True)
        acc_ref[2:3, :] += jnp.sum(dhn * xn, axis=0, keepdims=True)

    vec = pl.BlockSpec((1, D), _fix)
    til = pl.BlockSpec((tm, D), _row)
    return pl.pallas_call(
        body, grid=(T // tm,), in_specs=[til, til, vec, vec, til],
        out_specs=[til, pl.BlockSpec((8, D), _fix)], out_shape=[SDS((T, D), f32), SDS((8, D), f32)],
        compiler_params=_cparams(1), name=name)(dh, x, g, scale, dx_res)


def _res_in(dx, y, gate, name):
    T = dx.shape[0]
    tm = min(TM, T)

    def body(dx_ref, y_ref, gt_ref, dy_ref, acc_ref):
        @pl.when(pl.program_id(0) == 0)
        def _():
            acc_ref[...] = jnp.zeros_like(acc_ref)
        dxv = dx_ref[...]
        dy = dxv * gt_ref[...]
        dy_ref[...] = dy.astype(bf16)
        acc_ref[0:1, :] += jnp.sum(dxv * y_ref[...].astype(f32), axis=0, keepdims=True)
        acc_ref[1:2, :] += jnp.sum(dy, axis=0, keepdims=True)

    til = pl.BlockSpec((tm, D), _row)
    return pl.pallas_call(
        body, grid=(T // tm,), in_specs=[til, til, pl.BlockSpec((1, D), _fix)],
        out_specs=[til, pl.BlockSpec((8, D), _fix)], out_shape=[SDS((T, D), bf16), SDS((8, D), f32)],
        compiler_params=_cparams(1), name=name)(dx, y, gate)


def _final_bwd(x, g, tgt, name):
    T = x.shape[0]
    tm = min(TM, T)

    def body(x_ref, g_ref, t_ref, dx_ref, acc_ref):
        @pl.when(pl.program_id(0) == 0)
        def _():
            acc_ref[...] = jnp.zeros_like(acc_ref)
        xv = x_ref[...]
        gv = g_ref[...]
        r = lax.rsqrt(jnp.mean(xv * xv, axis=-1, keepdims=True) + EPS)
        xn = xv * r
        err = xn * gv - t_ref[...]
        dy = err * (1.0 / D)
        dxn = dy * gv
        dx_ref[...] = r * (dxn - xn * jnp.mean(dxn * xn, axis=-1, keepdims=True))
        acc_ref[0:1, :] += jnp.sum(dy * xn, axis=0, keepdims=True)
        acc_ref[1:2, :] += 0.5 * jnp.sum(jnp.mean(err * err, axis=-1, keepdims=True))

    til = pl.BlockSpec((tm, D), _row)
    return pl.pallas_call(
        body, grid=(T // tm,), in_specs=[til, pl.BlockSpec((1, D), _fix), til],
        out_specs=[til, pl.BlockSpec((8, D), _fix)], out_shape=[SDS((T, D), f32), SDS((8, D), f32)],
        compiler_params=_cparams(1), name=name)(x, g, tgt)


def _col_tile(n):
    if n <= 1024:
        return n
    return 1408 if n % 1408 == 0 else 1024


def _mm_gated(h, wa, wb, ba, bb, swiglu, name):
    T, K = h.shape
    N = wa.shape[1]
    tm, tn = min(TMM, T), _col_tile(N)

    def body(h_ref, wa_ref, wb_ref, ba_ref, bb_ref, u_ref, w_ref, p_ref):
        hv = h_ref[...]
        u = jnp.dot(hv, wa_ref[...], preferred_element_type=f32) + ba_ref[...]
        w = jnp.dot(hv, wb_ref[...], preferred_element_type=f32) + bb_ref[...]
        u_ref[...] = u
        w_ref[...] = w
        if swiglu:
            p_ref[...] = ((u * jax.nn.sigmoid(u)) * w).astype(p_ref.dtype)
        else:
            p_ref[...] = (u * jax.nn.sigmoid(w)).astype(p_ref.dtype)

    wsp = pl.BlockSpec((K, tn), lambda i, j: (0, j))
    bsp = pl.BlockSpec((1, tn), lambda i, j: (0, j))
    osp = pl.BlockSpec((tm, tn), lambda i, j: (i, j))
    return pl.pallas_call(
        body, grid=(T // tm, N // tn), in_specs=[pl.BlockSpec((tm, K), lambda i, j: (i, 0)), wsp, wsp, bsp, bsp],
        out_specs=[osp, osp, osp], out_shape=[SDS((T, N), f32), SDS((T, N), f32), SDS((T, N), bf16)],
        compiler_params=_cparams(2), name=name)(h, wa, wb, ba, bb)


def _mm_res(a, w, b, x_in, gate, name, norms=()):
    T, K = a.shape
    N = w.shape[1]
    nn = len(norms)
    tm, tn = min(TM if (nn and K > 1024) else TMM, T), _col_tile(N)
    assert tn == N or not nn

    def body(a_ref, w_ref, b_ref, x_ref, gt_ref, *rest):
        vecs, (xo_ref, y_ref), h_refs = rest[:3 * nn], rest[3 * nn:3 * nn + 2], rest[3 * nn + 2:]
        y = jnp.dot(a_ref[...], w_ref[...], preferred_element_type=f32) + b_ref[...]
        y_ref[...] = y.astype(bf16)
        xv = x_ref[...] + gt_ref[...] * y
        xo_ref[...] = xv
        if nn:
            r = lax.rsqrt(jnp.mean(xv * xv, axis=-1, keepdims=True) + EPS)
            xn = xv * r
            for k in range(nn):
                g_ref, sh_ref, sc_ref = vecs[3 * k:3 * k + 3]
                h_refs[k][...] = ((xn * g_ref[...]) * (1.0 + sc_ref[...]) + sh_ref[...]).astype(bf16)

    vsp = pl.BlockSpec((1, tn), lambda i, j: (0, j))
    osp = pl.BlockSpec((tm, tn), lambda i, j: (i, j))
    flat = [v for trio in norms for v in trio]
    return pl.pallas_call(
        body, grid=(T // tm, N // tn),
        in_specs=[pl.BlockSpec((tm, K), lambda i, j: (i, 0)), pl.BlockSpec((K, tn), lambda i, j: (0, j)), vsp, osp, vsp] + [vsp] * (3 * nn),
        out_specs=[osp, osp] + [osp] * nn, out_shape=[SDS((T, N), f32), SDS((T, N), bf16)] + [SDS((T, N), bf16)] * nn,
        compiler_params=_cparams(2), name=name)(a, w, b, x_in, gate, *flat)


def _mm(a, w, out_dtype, out_scale, name):
    T, K = a.shape
    N = w.shape[1]
    tm, tn = min(TMM, T), _col_tile(N)

    def body(a_ref, w_ref, o_ref):
        y = jnp.dot(a_ref[...], w_ref[...], preferred_element_type=f32)
        if out_scale != 1.0:
            y = y * out_scale
        o_ref[...] = y.astype(out_dtype)

    return pl.pallas_call(
        body, grid=(T // tm, N // tn),
        in_specs=[pl.BlockSpec((tm, K), lambda i, j: (i, 0)), pl.BlockSpec((K, tn), lambda i, j: (0, j))],
        out_specs=pl.BlockSpec((tm, tn), lambda i, j: (i, j)), out_shape=SDS((T, N), out_dtype),
        compiler_params=_cparams(2), name=name)(a, w)


def _dot_nt(a, b):
    return lax.dot_general(a, b, (((1,), (1,)), ((), ())), preferred_element_type=f32)


def _dot_tn(a, b):
    return lax.dot_general(a, b, (((0,), (0,)), ((), ())), preferred_element_type=f32)


def _mm_nt(pairs, out_dtype, name, ride=()):
    T = pairs[0][0].shape[0]
    K = pairs[0][1].shape[0]
    tm, tk = min(TMM, T), _col_tile(K)
    n = len(pairs)
    nx = len(ride)
    gi, gj = T // tm, K // tk

    def body(*refs):
        x_refs, o_ref, xo_refs, sems = refs[2 * n:2 * n + nx], refs[2 * n + nx], refs[2 * n + nx + 1:2 * n + 2 * nx + 1], refs[2 * n + 2 * nx + 1:]
        i, j = pl.program_id(0), pl.program_id(1)
        if nx:
            _exchange_start(x_refs, xo_refs, sems, False, (i == 0) & (j == 0))
        acc = None
        for a in range(n):
            part = _dot_nt(refs[2 * a][...].astype(bf16), refs[2 * a + 1][...])
            acc = part if acc is None else acc + part
        o_ref[...] = acc.astype(out_dtype)
        if nx:
            _exchange_wait(x_refs, xo_refs, sems, False, (i == gi - 1) & (j == gj - 1))

    in_specs, args = [], []
    for dy, w in pairs:
        ni = dy.shape[1]
        in_specs += [pl.BlockSpec((tm, ni), lambda i, j: (i, 0)), pl.BlockSpec((tk, ni), lambda i, j: (j, 0))]
        args += [dy, w]
    x_in, x_out, x_shape, x_sems = _exchange_shapes(ride, False) if nx else ([], [], [], [])
    out = pl.pallas_call(
        body, grid=(gi, gj), in_specs=in_specs + x_in,
        out_specs=[pl.BlockSpec((tm, tk), lambda i, j: (i, j))] + x_out, out_shape=[SDS((T, K), out_dtype)] + x_shape,
        scratch_shapes=x_sems, compiler_params=_cparams(2), name=name)(*args, *ride)
    return out if nx else out[0]


def _mm_nt_swiglu(dy, w, ug, uu, name):
    T, N = dy.shape
    K = w.shape[0]
    tm, tk = min(TMM, T), _col_tile(K)

    def body(dy_ref, w_ref, ug_ref, uu_ref, dug_ref, duu_ref):
        dact = _dot_nt(dy_ref[...], w_ref[...])
        g = ug_ref[...].astype(f32)
        u = uu_ref[...].astype(f32)
        sg = jax.nn.sigmoid(g)
        duu_ref[...] = (dact * (g * sg)).astype(bf16)
        dug_ref[...] = (dact * u * (sg * (1.0 + g * (1.0 - sg)))).astype(bf16)

    osp = pl.BlockSpec((tm, tk), lambda i, j: (i, j))
    return pl.pallas_call(
        body, grid=(T // tm, K // tk),
        in_specs=[pl.BlockSpec((tm, N), lambda i, j: (i, 0)), pl.BlockSpec((tk, N), lambda i, j: (j, 0)), osp, osp],
        out_specs=[osp, osp], out_shape=[SDS((T, K), bf16), SDS((T, K), bf16)],
        compiler_params=_cparams(2), name=name)(dy, w, ug, uu)


def _mm_tn(a, b, name):
    T, K = a.shape
    N = b.shape[1]
    tt = min(TMM, T)
    tk = K if K <= 1024 else _col_tile(K)
    tn = N if N <= 1024 else _col_tile(N)

    def body(a_ref, b_ref, o_ref):
        @pl.when(pl.program_id(2) == 0)
        def _():
            o_ref[...] = jnp.zeros_like(o_ref)
        o_ref[...] += _dot_tn(a_ref[...].astype(bf16), b_ref[...].astype(bf16))

    return pl.pallas_call(
        body, grid=(K // tk, N // tn, T // tt),
        in_specs=[pl.BlockSpec((tt, tk), lambda i, j, t: (t, i)), pl.BlockSpec((tt, tn), lambda i, j, t: (t, j))],
        out_specs=pl.BlockSpec((tk, tn), lambda i, j, t: (i, j)), out_shape=SDS((K, N), f32),
        compiler_params=_cparams(3), name=name)(a, b)


def _mm_tn_shards(a, b, c, name, into=None):
    T, K = a.shape
    half = NDEV // 2
    assert b.shape[1] == half * c
    tt = min(TMM, T)
    nt = T // tt
    first = into is None

    def body(a_ref, b_ref, *rest):
        o_ref, acc = rest[-2], rest[-1]
        t = pl.program_id(1)

        @pl.when(t == 0)
        def _():
            acc[...] = jnp.zeros_like(acc)
        acc[...] += _dot_tn(a_ref[...].astype(bf16), b_ref[...].astype(bf16))

        @pl.when(t == nt - 1)
        def _():
            o_ref[...] = acc[...].astype(bf16)

    in_specs = [pl.BlockSpec((tt, K), lambda j, t: (t, 0)), pl.BlockSpec((tt, c), lambda j, t: (t, j))]
    args = [a, b]
    if not first:
        in_specs.append(pl.BlockSpec(memory_space=pl.ANY))
        args.append(into)
    base = 0 if first else half
    return pl.pallas_call(
        body, grid=(half, nt), in_specs=in_specs,
        out_specs=pl.BlockSpec((None, K, c), lambda j, t: (j + base, 0, 0)), out_shape=SDS((NDEV, K, c), bf16),
        scratch_shapes=[pltpu.VMEM((K, c), f32)], input_output_aliases={} if first else {2: 0},
        compiler_params=_cparams(2), name=name)(*args)


def _ffn_in(h, w_sh, name, ride=()):
    T = h.shape[0]
    tm = min(TMM, T)
    nx = len(ride)
    gi, gj = T // tm, NDEV // 2

    def body(h_ref, wg_ref, wu_ref, *rest):
        x_refs, (ug_ref, uu_ref, act_ref), xo_refs, sems = rest[:nx], rest[nx:nx + 3], rest[nx + 3:2 * nx + 3], rest[2 * nx + 3:]
        i, j = pl.program_id(0), pl.program_id(1)
        if nx:
            _exchange_start(x_refs, xo_refs, sems, True, (i == 0) & (j == 0))
        hv = h_ref[...]
        ug = jnp.dot(hv, wg_ref[...], preferred_element_type=f32)
        uu = jnp.dot(hv, wu_ref[...], preferred_element_type=f32)
        ug_ref[...] = ug.astype(bf16)
        uu_ref[...] = uu.astype(bf16)
        act_ref[...] = ((ug * jax.nn.sigmoid(ug)) * uu).astype(bf16)
        if nx:
            _exchange_wait(x_refs, xo_refs, sems, True, (i == gi - 1) & (j == gj - 1))

    osp = pl.BlockSpec((tm, FSP), lambda i, j: (i, j))
    x_in, x_out, x_shape, x_sems = _exchange_shapes(ride, True) if nx else ([], [], [], [])
    return pl.pallas_call(
        body, grid=(gi, gj),
        in_specs=[pl.BlockSpec((tm, D), lambda i, j: (i, 0)),
                  pl.BlockSpec((None, D, FSP), lambda i, j: (j, 0, 0)),
                  pl.BlockSpec((None, D, FSP), lambda i, j: (j + NDEV // 2, 0, 0))] + x_in,
        out_specs=[osp, osp, osp] + x_out,
        out_shape=[SDS((T, FP), bf16), SDS((T, FP), bf16), SDS((T, FP), bf16)] + x_shape,
        scratch_shapes=x_sems, compiler_params=_cparams(2), name=name)(h, w_sh, w_sh, *ride)


def _ffn_dh(dug, duu, w_sh, name):
    T = dug.shape[0]
    tm, tk = min(TMM, T), 512
    half = NDEV // 2

    def body(dg_ref, du_ref, wg_ref, wu_ref, o_ref):
        acc = None
        for s in range(half):
            cols = slice(s * FSP, (s + 1) * FSP)
            part = _dot_nt(dg_ref[:, cols], wg_ref[s]) + _dot_nt(du_ref[:, cols], wu_ref[s])
            acc = part if acc is None else acc + part
        o_ref[...] = acc

    dsp = pl.BlockSpec((tm, FP), lambda i, j: (i, 0))
    return pl.pallas_call(
        body, grid=(T // tm, D // tk),
        in_specs=[dsp, dsp, pl.BlockSpec((half, tk, FSP), lambda i, j: (0, j, 0)),
                  pl.BlockSpec((half, tk, FSP), lambda i, j: (1, j, 0))],
        out_specs=pl.BlockSpec((tm, tk), lambda i, j: (i, j)), out_shape=SDS((T, D), f32),
        compiler_params=_cparams(2), name=name)(dug, duu, w_sh, w_sh)


def _layernorm_parts(qv, g, b):
    mu = jnp.mean(qv, axis=-1, keepdims=True)
    cen = qv - mu
    rstd = lax.rsqrt(jnp.mean(cen * cen, axis=-1, keepdims=True) + EPS)
    z = cen * rstd
    return z, rstd, z * g + b


def _tap_groups(offsets):
    groups = {}
    for k, o in enumerate(offsets):
        groups.setdefault(o % 8, []).append((k, o - o % 8))
    return groups


def _conv_fwd(p, w_dw, b_dw, ln_g, ln_b, name, ride=()):
    T = p.shape[0]
    tc = min(TC, T)
    n = T // tc
    nx = len(ride)

    def body(p_ref, w_ref, b_ref, g_ref, bb_ref, *rest):
        x_refs, (q_ref, s_ref), xo_refs = rest[:nx], rest[nx:nx + 2], rest[nx + 2:2 * nx + 2]
        (ext, sh), sems = rest[2 * nx + 2:2 * nx + 4], rest[2 * nx + 4:]
        i = pl.program_id(0)
        if nx:
            _exchange_start(x_refs, xo_refs, sems, True, i == 0)

        @pl.when(i == 0)
        def _():
            ext[0:HALO, :] = jnp.zeros((HALO, D), f32)

        @pl.when(i > 0)
        def _():
            ext[0:HALO, :] = ext[tc:tc + HALO, :]

        ext[HALO:HALO + tc, :] = p_ref[...].astype(f32)
        groups = _tap_groups([HALO - (KW - 1) + k for k in range(KW)])
        for cb in range(D // LANES):
            cols = slice(cb * LANES, (cb + 1) * LANES)
            acc = jnp.zeros((tc, LANES), f32)
            for r, taps in groups.items():
                span = max(base for _, base in taps) + tc
                sh[0:span, :] = ext[r:r + span, cols]
                for k, base in taps:
                    acc = acc + w_ref[k:k + 1, cols] * sh[base:base + tc, :]
            q_ref[:, cols] = acc + b_ref[:, cols]
        _, _, l = _layernorm_parts(q_ref[...], g_ref[...], bb_ref[...])
        s_ref[...] = (l * jax.nn.sigmoid(l)).astype(bf16)
        if nx:
            _exchange_wait(x_refs, xo_refs, sems, True, i == n - 1)

    vec = pl.BlockSpec((1, D), _fix)
    til = pl.BlockSpec((tc, D), _row)
    x_in, x_out, x_shape, x_sems = _exchange_shapes(ride, True) if nx else ([], [], [], [])
    return pl.pallas_call(
        body, grid=(n,), in_specs=[til, pl.BlockSpec((HALO, D), _fix), vec, vec, vec] + x_in,
        out_specs=[til, til] + x_out, out_shape=[SDS((T, D), f32), SDS((T, D), bf16)] + x_shape,
        scratch_shapes=[pltpu.VMEM((tc + HALO, D), f32), pltpu.VMEM((tc + HALO, LANES), f32)] + x_sems, compiler_params=_cparams(1),
        name=name)(p, w_dw, b_dw, ln_g, ln_b, *ride)


def _conv_bwd(ds, q, p, a, gl, w_dw, ln_g, ln_b, name, ride=()):
    T = q.shape[0]
    tc = min(TC, T)
    n = T // tc
    nx = len(ride)

    def body(ds_ref, q_ref, p_ref, a_ref, gl_ref, w_ref, g_ref, bb_ref, *rest):
        x_refs, (da_ref, dgl_ref, acc_ref, dw_ref), xo_refs = rest[:nx], rest[nx:nx + 4], rest[nx + 4:2 * nx + 4]
        (ext, sh), sems = rest[2 * nx + 4:2 * nx + 6], rest[2 * nx + 6:]
        i = pl.program_id(0)
        if nx:
            _exchange_start(x_refs, xo_refs, sems, False, i == 0)

        @pl.when(i == 0)
        def _():
            acc_ref[...] = jnp.zeros_like(acc_ref)
            dw_ref[...] = jnp.zeros_like(dw_ref)
            ext[tc:tc + HALO, :] = jnp.zeros((HALO, D), f32)

        @pl.when(i > 0)
        def _():
            ext[tc:tc + HALO, :] = ext[0:HALO, :]

        gv = g_ref[...]
        z, rstd, l = _layernorm_parts(q_ref[...], gv, bb_ref[...])
        sg = jax.nn.sigmoid(l)
        dl = ds_ref[...] * (sg * (1.0 + l * (1.0 - sg)))
        dz = dl * gv
        dq = rstd * (dz - jnp.mean(dz, axis=-1, keepdims=True) - z * jnp.mean(dz * z, axis=-1, keepdims=True))
        ext[0:tc, :] = dq.astype(bf16).astype(f32)
        acc_ref[0:1, :] += jnp.sum(dl * z, axis=0, keepdims=True)
        acc_ref[1:2, :] += jnp.sum(dl, axis=0, keepdims=True)
        acc_ref[2:3, :] += jnp.sum(dq, axis=0, keepdims=True)
        groups = _tap_groups([KW - 1 - k for k in range(KW)])
        for cb in range(D // LANES):
            cols = slice(cb * LANES, (cb + 1) * LANES)
            pc = p_ref[:, cols].astype(f32)
            dp = jnp.zeros((tc, LANES), f32)
            for r, taps in groups.items():
                span = max(base for _, base in taps) + tc
                sh[0:span, :] = ext[r:r + span, cols]
                for k, base in taps:
                    sl = sh[base:base + tc, :]
                    dp = dp + w_ref[k:k + 1, cols] * sl
                    dw_ref[k:k + 1, cols] += jnp.sum(sl * pc, axis=0, keepdims=True)
            av = a_ref[:, cols].astype(f32)
            sgl = jax.nn.sigmoid(gl_ref[:, cols].astype(f32))
            da = dp * sgl
            dgl = dp * av * (sgl * (1.0 - sgl))
            da_ref[:, cols] = da.astype(bf16)
            dgl_ref[:, cols] = dgl.astype(bf16)
            acc_ref[3:4, cols] += jnp.sum(da, axis=0, keepdims=True)
            acc_ref[4:5, cols] += jnp.sum(dgl, axis=0, keepdims=True)
        if nx:
            _exchange_wait(x_refs, xo_refs, sems, False, i == n - 1)

    rev = lambda i: (n - 1 - i, 0)
    til = pl.BlockSpec((tc, D), rev)
    vec = pl.BlockSpec((1, D), _fix)
    x_in, x_out, x_shape, x_sems = _exchange_shapes(ride, False) if nx else ([], [], [], [])
    return pl.pallas_call(
        body, grid=(n,), in_specs=[til, til, til, til, til, pl.BlockSpec((HALO, D), _fix), vec, vec] + x_in,
        out_specs=[til, til, pl.BlockSpec((8, D), _fix), pl.BlockSpec((HALO, D), _fix)] + x_out,
        out_shape=[SDS((T, D), bf16), SDS((T, D), bf16), SDS((8, D), f32), SDS((HALO, D), f32)] + x_shape,
        scratch_shapes=[pltpu.VMEM((tc + HALO, D), f32), pltpu.VMEM((tc + HALO, LANES), f32)] + x_sems, compiler_params=_cparams(1),
        name=name)(ds, q, p, a, gl, w_dw, ln_g, ln_b, *ride)


def _tri(n, upper):
    r = lax.broadcasted_iota(jnp.int32, (n, n), 0)
    c = lax.broadcasted_iota(jnp.int32, (n, n), 1)
    return ((c >= r) if upper else (r >= c)).astype(f32)


def _forget_fwd(fl, fb, name):
    T = fl.shape[0]
    tc = min(TC, T)

    def body(fl_ref, fb_ref, cum_ref, carry):
        @pl.when(pl.program_id(0) == 0)
        def _():
            carry[...] = jnp.zeros_like(carry)
        xv = fl_ref[...] + fb_ref[...]
        lf = jnp.minimum(xv, 0.0) - jnp.log(1.0 + jnp.exp(-jnp.abs(xv)))
        cs = jnp.dot(_tri(tc, False), lf, preferred_element_type=f32, precision=lax.Precision.HIGHEST) + carry[0:1, :]
        cum_ref[...] = cs
        carry[0:1, :] = cs[tc - 1:tc, :]

    til = pl.BlockSpec((tc, LANES), _row)
    return pl.pallas_call(
        body, grid=(T // tc,), in_specs=[til, pl.BlockSpec((1, LANES), _fix)], out_specs=til,
        out_shape=SDS((T, LANES), f32), scratch_shapes=[pltpu.VMEM((8, LANES), f32)],
        compiler_params=_cparams(1), name=name)(fl, fb)


def _forget_bwd(dcum, fl, fb, name):
    T = fl.shape[0]
    tc = min(TC, T)
    n = T // tc

    def body(dc_ref, fl_ref, fb_ref, dfl_ref, acc_ref, carry):
        @pl.when(pl.program_id(0) == 0)
        def _():
            carry[...] = jnp.zeros_like(carry)
            acc_ref[...] = jnp.zeros_like(acc_ref)
        dlf = jnp.dot(_tri(tc, True), dc_ref[...], preferred_element_type=f32, precision=lax.Precision.HIGHEST) + carry[0:1, :]
        carry[0:1, :] = dlf[0:1, :]
        dfl = dlf * (1.0 - jax.nn.sigmoid(fl_ref[...] + fb_ref[...]))
        dfl_ref[...] = dfl.astype(bf16)
        acc_ref[0:1, :] += jnp.sum(dfl, axis=0, keepdims=True)

    rev = lambda i: (n - 1 - i, 0)
    til = pl.BlockSpec((tc, LANES), rev)
    return pl.pallas_call(
        body, grid=(n,), in_specs=[til, til, pl.BlockSpec((1, LANES), _fix)],
        out_specs=[til, pl.BlockSpec((8, LANES), _fix)], out_shape=[SDS((T, LANES), bf16), SDS((8, LANES), f32)],
        scratch_shapes=[pltpu.VMEM((8, LANES), f32)], compiler_params=_cparams(1), name=name)(dcum, fl, fb)


def _causal(s, n):
    r = lax.broadcasted_iota(jnp.int32, (n, n), 0)
    c = lax.broadcasted_iota(jnp.int32, (n, n), 1)
    return jnp.where(c <= r, s, NEG)


def _attn_fwd(q, k, v, cum_t, name, ride=()):
    T = q.shape[0]
    tq = min(TQ, T)
    n = T // tq
    nx = len(ride)

    def body(q_ref, k_ref, v_ref, ck_ref, *rest):
        x_refs, (o_ref, o32_ref, st_ref), xo_refs = rest[:nx], rest[nx:nx + 3], rest[nx + 3:2 * nx + 3]
        (m_sc, l_sc, acc_sc, res_sc), sems = rest[2 * nx + 3:2 * nx + 7], rest[2 * nx + 7:]
        i = pl.program_id(1)
        if nx:
            _exchange_start(x_refs, xo_refs, sems, True, (pl.program_id(0) == 0) & (i == 0))
        lane = lax.broadcasted_iota(jnp.int32, (1, LANES), 1)
        lo = lane < HD
        q2 = q_ref[...]
        zero = jnp.zeros_like(q2)
        qa = (jnp.where(lo, q2, zero), jnp.where(lo, zero, q2))
        m_sc[...] = jnp.full(m_sc.shape, NEG, f32)
        l_sc[...] = jnp.zeros_like(l_sc)
        acc_sc[...] = jnp.zeros_like(acc_sc)
        res_sc[...] = jnp.zeros_like(res_sc)

        def block(j, masked):
            off = pl.multiple_of(j * tq, tq)
            k2 = k_ref[pl.ds(off, tq), :]
            v2 = v_ref[pl.ds(off, tq), :]
            ss = [_dot_nt(qa[a], k2) - ck_ref[0, a:a + 1, pl.ds(off, tq)] for a in range(2)]
            if masked:
                ss = [_causal(t, tq) for t in ss]
            pbs, prs, alphas = [], [], []
            for a in range(2):
                m_old = m_sc[a]
                m_new = jnp.maximum(m_old, jnp.max(ss[a], axis=1, keepdims=True))
                alpha = jnp.exp(m_old - m_new)
                pm = jnp.exp(ss[a] - m_new)
                pb = pm.astype(bf16)
                prs.append((pm - pb.astype(f32)).astype(bf16))
                pbs.append(pb)
                alphas.append(alpha)
                l_sc[a] = alpha * l_sc[a] + jnp.sum(pm, axis=1, keepdims=True)
                m_sc[a] = m_new
            for a in range(2):
                acc_sc[a] = alphas[a] * acc_sc[a] + jnp.dot(pbs[a], v2, preferred_element_type=f32)
                res_sc[a] = alphas[a] * res_sc[a] + jnp.dot(prs[a], v2, preferred_element_type=f32)

        def step(j, carry):
            block(j, False)
            return carry

        lax.fori_loop(0, i, step, 0)
        block(i, True)
        o_ref[...] = jnp.where(lo, acc_sc[0] / l_sc[0], acc_sc[1] / l_sc[1]).astype(bf16)
        o32_ref[...] = jnp.where(lo, (acc_sc[0] + res_sc[0]) / l_sc[0], (acc_sc[1] + res_sc[1]) / l_sc[1])
        lse0 = m_sc[0] + jnp.log(l_sc[0])
        lse1 = m_sc[1] + jnp.log(l_sc[1])
        st_ref[0] = jnp.where(lane == 0, lse0, jnp.where(lane == 1, lse1, 0.0))
        if nx:
            _exchange_wait(x_refs, xo_refs, sems, True, (pl.program_id(0) == NP - 1) & (i == n - 1))

    full = lambda blk: pl.BlockSpec((T, LANES), blk)
    x_in, x_out, x_shape, x_sems = _exchange_shapes(ride, True) if nx else ([], [], [], [])
    return pl.pallas_call(
        body, grid=(NP, n),
        in_specs=[pl.BlockSpec((tq, LANES), lambda p, i: (i, p)), full(lambda p, i: (0, p)), full(lambda p, i: (0, p)),
                  pl.BlockSpec((1, 2, T), lambda p, i: (p, 0, 0))] + x_in,
        out_specs=[pl.BlockSpec((tq, LANES), lambda p, i: (i, p)), pl.BlockSpec((tq, LANES), lambda p, i: (i, p)),
                   pl.BlockSpec((1, tq, LANES), lambda p, i: (p, i, 0))] + x_out,
        out_shape=[SDS((T, H * HD), bf16), SDS((T, H * HD), f32), SDS((NP, T, LANES), f32)] + x_shape,
        scratch_shapes=[pltpu.VMEM((2, tq, 1), f32), pltpu.VMEM((2, tq, 1), f32), pltpu.VMEM((2, tq, LANES), f32),
                        pltpu.VMEM((2, tq, LANES), f32)] + x_sems,
        compiler_params=_cparams(2), name=name)(q, k, v, cum_t, *ride)


def _attn_stats(do, o, lse, name):
    T = do.shape[0]
    tm = min(4 * TM, T)

    def body(do_ref, o_ref, lse_ref, st_ref):
        lane = lax.broadcasted_iota(jnp.int32, (1, LANES), 1)
        prod = do_ref[...].astype(f32) * o_ref[...].astype(f32)
        d0 = jnp.sum(jnp.where(lane < HD, prod, 0.0), axis=1, keepdims=True)
        d1 = jnp.sum(jnp.where(lane < HD, 0.0, prod), axis=1, keepdims=True)
        st_ref[0] = jnp.where(lane < 2, lse_ref[0], jnp.where(lane == 2, d0, jnp.where(lane == 3, d1, 0.0)))

    til = pl.BlockSpec((tm, LANES), lambda p, i: (i, p))
    stt = pl.BlockSpec((1, tm, LANES), lambda p, i: (p, i, 0))
    return pl.pallas_call(
        body, grid=(NP, T // tm), in_specs=[til, til, stt], out_specs=stt, out_shape=SDS((NP, T, LANES), f32),
        compiler_params=_cparams(2), name=name)(do, o, lse)


def _attn_bwd(q, k, v, do, st, cum_t, name, ride=()):
    T = q.shape[0]
    tq = min(TQ, T)
    n = T // tq
    nx = len(ride)

    def body(q_ref, k_ref, v_ref, do_ref, st_ref, ck_ref, *rest):
        x_refs, (dq_ref, dk_ref, dv_ref, dck_ref), xo_refs = rest[:nx], rest[nx:nx + 4], rest[nx + 4:2 * nx + 4]
        (dk_sc, dv_sc, dck_sc), sems = rest[2 * nx + 4:2 * nx + 7], rest[2 * nx + 7:]
        j = pl.program_id(1)
        if nx:
            _exchange_start(x_refs, xo_refs, sems, False, (pl.program_id(0) == 0) & (j == 0))
        lane = lax.broadcasted_iota(jnp.int32, (1, LANES), 1)
        lo = lane < HD

        @pl.when(j == 0)
        def _():
            dq_ref[...] = jnp.zeros_like(dq_ref)

        k2 = k_ref[...]
        v2 = v_ref[...]
        zero = jnp.zeros_like(k2)
        ka = (jnp.where(lo, k2, zero), jnp.where(lo, zero, k2))
        va = (jnp.where(lo, v2, zero), jnp.where(lo, zero, v2))
        dk_sc[...] = jnp.zeros_like(dk_sc)
        dv_sc[...] = jnp.zeros_like(dv_sc)
        dck_sc[...] = jnp.zeros_like(dck_sc)

        def block(i, masked):
            off = pl.multiple_of(i * tq, tq)
            q2 = q_ref[pl.ds(off, tq), :]
            do2 = do_ref[pl.ds(off, tq), :]
            stt = st_ref[0, pl.ds(off, tq), :]
            parts = []
            for a in range(2):
                s = _dot_nt(q2, ka[a]) - ck_ref[0, a:a + 1, :]
                if masked:
                    s = _causal(s, tq)
                pm = jnp.exp(s - stt[:, a:a + 1])
                dp = _dot_nt(do2, va[a])
                dsm = pm * (dp - stt[:, 2 + a:3 + a])
                dsb = dsm.astype(bf16)
                dv_sc[a] += _dot_tn(pm.astype(bf16), do2)
                dk_sc[a] += _dot_tn(dsb, q2)
                dck_sc[a:a + 1, :] -= jnp.sum(dsm, axis=0, keepdims=True)
                parts.append(jnp.dot(dsb, k2, preferred_element_type=f32))
            dq_ref[pl.ds(off, tq), :] += jnp.where(lo, parts[0], parts[1])

        block(j, True)

        def step(i, carry):
            block(i, False)
            return carry

        lax.fori_loop(j + 1, n, step, 0)
        dk_ref[...] = jnp.where(lo, dk_sc[0], dk_sc[1]).astype(bf16)
        dv_ref[...] = jnp.where(lo, dv_sc[0], dv_sc[1]).astype(bf16)
        dck_ref[0] = dck_sc[0:2, :]

        @pl.when(j == n - 1)
        def _():
            dq_ref[...] = dq_ref[...] * (HD ** -0.5)

        if nx:
            _exchange_wait(x_refs, xo_refs, sems, False, (pl.program_id(0) == NP - 1) & (j == n - 1))

    full = lambda: pl.BlockSpec((T, LANES), lambda p, j: (0, p))
    kvb = lambda: pl.BlockSpec((tq, LANES), lambda p, j: (j, p))
    ckb = lambda: pl.BlockSpec((1, 2, tq), lambda p, j: (p, 0, j))
    x_in, x_out, x_shape, x_sems = _exchange_shapes(ride, False) if nx else ([], [], [], [])
    return pl.pallas_call(
        body, grid=(NP, n),
        in_specs=[full(), kvb(), kvb(), full(), pl.BlockSpec((1, T, LANES), lambda p, j: (p, 0, 0)), ckb()] + x_in,
        out_specs=[full(), kvb(), kvb(), ckb()] + x_out,
        out_shape=[SDS((T, H * HD), f32), SDS((T, H * HD), bf16), SDS((T, H * HD), bf16), SDS((NP, 2, T), f32)] + x_shape,
        scratch_shapes=[pltpu.VMEM((2, tq, LANES), f32), pltpu.VMEM((2, tq, LANES), f32), pltpu.VMEM((8, tq), f32)] + x_sems,
        compiler_params=_cparams(2), name=name)(q, k, v, do, st, cum_t, *ride)


def _ada_fwd(c_all, w_cat, name):
    n = w_cat.shape[1]
    tn = 256

    def body(c_ref, w_ref, o_ref):
        cv = c_ref[...]
        o_ref[...] = jnp.dot((cv * jax.nn.sigmoid(cv)).astype(bf16), w_ref[...].astype(bf16), preferred_element_type=f32)

    return pl.pallas_call(
        body, grid=(n // tn,), in_specs=[pl.BlockSpec((NDEV, D), _fix), pl.BlockSpec((D, tn), lambda i: (0, i))],
        out_specs=pl.BlockSpec((NDEV, tn), lambda i: (0, i)), out_shape=SDS((NDEV, n), f32),
        compiler_params=_cparams(1), name=name)(c_all, w_cat)


def _ada_bwd(c_all_t, dsel, name):
    n = dsel.shape[1]
    tn = 256

    def body(c_ref, d_ref, o_ref):
        cv = c_ref[...]
        ca = cv * jax.nn.sigmoid(cv)
        acc = ca[:, 0:1] * d_ref[0:1, :]
        for b in range(1, NDEV):
            acc = acc + ca[:, b:b + 1] * d_ref[b:b + 1, :]
        o_ref[...] = acc

    return pl.pallas_call(
        body, grid=(n // tn,), in_specs=[pl.BlockSpec((D, NDEV), _fix), pl.BlockSpec((NDEV, tn), lambda i: (0, i))],
        out_specs=pl.BlockSpec((D, tn), lambda i: (0, i)), out_shape=SDS((D, n), f32),
        compiler_params=_cparams(1), name=name)(c_all_t, dsel)


def _sum_parts(parts, name):
    R = parts.shape[1]

    def body(p_ref, o_ref):
        acc = p_ref[0]
        for j in range(1, NDEV):
            acc = acc + p_ref[j]
        o_ref[...] = acc

    return pl.pallas_call(body, out_shape=SDS((R, LANES), f32), name=name)(parts)


def _adamw(g_parts, w, m, v, name):
    n_parts, R, C = g_parts.shape
    tr = next(t for t in (256, 128, 64, 32, 16, 8) if R % t == 0)
    c1 = 1.0 / (1.0 - ADAM_B1 ** ADAM_STEP)
    c2 = 1.0 / (1.0 - ADAM_B2 ** ADAM_STEP)

    def body(g_ref, w_ref, m_ref, v_ref, go_ref, d_ref, mo_ref, vo_ref):
        g = g_ref[0].astype(f32)
        for j in range(1, n_parts):
            g = g + g_ref[j].astype(f32)
        mn = ADAM_B1 * m_ref[...] + (1.0 - ADAM_B1) * g
        vn = ADAM_B2 * v_ref[...] + (1.0 - ADAM_B2) * (g * g)
        go_ref[...] = g
        mo_ref[...] = mn
        vo_ref[...] = vn
        d_ref[...] = -ADAM_LR * ((mn * c1) / (jnp.sqrt(vn * c2) + ADAM_EPS) + ADAM_WD * w_ref[...])

    til = pl.BlockSpec((tr, C), _row)
    out = SDS((R, C), f32)
    return pl.pallas_call(
        body, grid=(R // tr,), in_specs=[pl.BlockSpec((n_parts, tr, C), lambda i: (0, i, 0)), til, til, til],
        out_specs=[til, til, til, til], out_shape=[out, out, out, out],
        compiler_params=_cparams(1), name=name)(g_parts, w, m, v)


def _pad_rows(flat, cols, mult):
    n = flat.shape[-1]
    rows = -(-n // cols)
    rows = -(-rows // mult) * mult
    pad = [(0, 0)] * (flat.ndim - 1) + [(0, rows * cols - n)]
    return jnp.pad(flat, pad).reshape(flat.shape[:-1] + (rows, cols))


def _split_flat(flat, shapes):
    out, off = {}, 0
    for name, shp in shapes:
        n = 1
        for d in shp:
            n *= d
        out[name] = flat[off:off + n].reshape(shp)
        off += n
    return out


def kernel(x, c, mix_norm_g, mix_ada_w, mix_ada_b, ffn_norm_g, ffn_ada_w, ffn_ada_b, ffn_w_in, ffn_w_out, conv_w_in, conv_b_in, conv_w_dw, conv_b_dw, conv_ln_g, conv_ln_b, conv_w_out, conv_b_out, kv_norm_g, kv_ada_w, kv_ada_b, kv_w, forget_b, attn_w_q, attn_w_o, final_norm_g, loss_target, m_mix_norm_g, m_mix_ada_w, m_mix_ada_b, m_ffn_norm_g, m_ffn_ada_w, m_ffn_ada_b, m_ffn_w_in, m_ffn_w_out, m_conv_w_in, m_conv_b_in, m_conv_w_dw, m_conv_b_dw, m_conv_ln_g, m_conv_ln_b, m_conv_w_out, m_conv_b_out, m_kv_norm_g, m_kv_ada_w, m_kv_ada_b, m_kv_w, m_forget_b, m_attn_w_q, m_attn_w_o, m_final_norm_g, v_mix_norm_g, v_mix_ada_w, v_mix_ada_b, v_ffn_norm_g, v_ffn_ada_w, v_ffn_ada_b, v_ffn_w_in, v_ffn_w_out, v_conv_w_in, v_conv_b_in, v_conv_w_dw, v_conv_b_dw, v_conv_ln_g, v_conv_ln_b, v_conv_w_out, v_conv_b_out, v_kv_norm_g, v_kv_ada_w, v_kv_ada_b, v_kv_w, v_forget_b, v_attn_w_q, v_attn_w_o, v_final_norm_g):
    W = dict(mix_norm_g=mix_norm_g, mix_ada_w=mix_ada_w, mix_ada_b=mix_ada_b, ffn_norm_g=ffn_norm_g, ffn_ada_w=ffn_ada_w, ffn_ada_b=ffn_ada_b, ffn_w_in=ffn_w_in, ffn_w_out=ffn_w_out, conv_w_in=conv_w_in, conv_b_in=conv_b_in, conv_w_dw=conv_w_dw, conv_b_dw=conv_b_dw, conv_ln_g=conv_ln_g, conv_ln_b=conv_ln_b, conv_w_out=conv_w_out, conv_b_out=conv_b_out, kv_norm_g=kv_norm_g, kv_ada_w=kv_ada_w, kv_ada_b=kv_ada_b, kv_w=kv_w, forget_b=forget_b, attn_w_q=attn_w_q, attn_w_o=attn_w_o, final_norm_g=final_norm_g)
    M = dict(mix_norm_g=m_mix_norm_g, mix_ada_w=m_mix_ada_w, mix_ada_b=m_mix_ada_b, ffn_norm_g=m_ffn_norm_g, ffn_ada_w=m_ffn_ada_w, ffn_ada_b=m_ffn_ada_b, ffn_w_in=m_ffn_w_in, ffn_w_out=m_ffn_w_out, conv_w_in=m_conv_w_in, conv_b_in=m_conv_b_in, conv_w_dw=m_conv_w_dw, conv_b_dw=m_conv_b_dw, conv_ln_g=m_conv_ln_g, conv_ln_b=m_conv_ln_b, conv_w_out=m_conv_w_out, conv_b_out=m_conv_b_out, kv_norm_g=m_kv_norm_g, kv_ada_w=m_kv_ada_w, kv_ada_b=m_kv_ada_b, kv_w=m_kv_w, forget_b=m_forget_b, attn_w_q=m_attn_w_q, attn_w_o=m_attn_w_o, final_norm_g=m_final_norm_g)
    V = dict(mix_norm_g=v_mix_norm_g, mix_ada_w=v_mix_ada_w, mix_ada_b=v_mix_ada_b, ffn_norm_g=v_ffn_norm_g, ffn_ada_w=v_ffn_ada_w, ffn_ada_b=v_ffn_ada_b, ffn_w_in=v_ffn_w_in, ffn_w_out=v_ffn_w_out, conv_w_in=v_conv_w_in, conv_b_in=v_conv_b_in, conv_w_dw=v_conv_w_dw, conv_b_dw=v_conv_b_dw, conv_ln_g=v_conv_ln_g, conv_ln_b=v_conv_ln_b, conv_w_out=v_conv_w_out, conv_b_out=v_conv_b_out, kv_norm_g=v_kv_norm_g, kv_ada_w=v_kv_ada_w, kv_ada_b=v_kv_ada_b, kv_w=v_kv_w, forget_b=v_forget_b, attn_w_q=v_attn_w_q, attn_w_o=v_attn_w_o, final_norm_g=v_final_norm_g)
    names = list(W)
    T = x.shape[1]
    me = _my_index()
    x0 = x[0]
    tgt = loss_target[0]
    row = lambda vct: vct.reshape(1, -1)

    small_names = ("conv_b_in", "conv_w_dw", "conv_b_dw", "conv_ln_g", "conv_ln_b", "conv_b_out")
    small_loc = jnp.concatenate([c.reshape(-1)] + [W[n].reshape(-1) for n in small_names])
    sg = _exchange([_pad_rows(small_loc, LANES, 8)], True, "gather_small")[0].reshape(NDEV, -1)
    c_all = sg[:, :D]
    off = D
    b_in = sg[:, off:off + 2 * D // NDEV].reshape(1, 2 * D); off += 2 * D // NDEV
    cl = D // NDEV
    w_dw = sg[:, off:off + KW * cl].reshape(NDEV, KW, cl).transpose(1, 0, 2).reshape(KW, D); off += KW * cl
    w_dw = jnp.pad(w_dw, ((0, HALO - KW), (0, 0))).astype(bf16).astype(f32)
    b_dw = sg[:, off:off + cl].reshape(1, D); off += cl
    ln_g = sg[:, off:off + cl].reshape(1, D); off += cl
    ln_b = sg[:, off:off + cl].reshape(1, D); off += cl
    b_out = sg[:, off:off + cl].reshape(1, D)

    cat_ada = lambda s: jnp.concatenate([s["mix_ada_w"][0], s["mix_ada_w"][1], s["ffn_ada_w"][0], s["ffn_ada_w"][1], s["kv_ada_w"]], axis=1)
    w_cat = cat_ada(W)
    ada_loc = _ada_fwd(c_all, w_cat, "ada_fwd")
    ada_all = _exchange([ada_loc], True, "gather_ada")[0]
    ada_me = lax.dynamic_index_in_dim(ada_all, me, axis=1, keepdims=False)
    ada_bias = (mix_ada_b[0], mix_ada_b[1], ffn_ada_b[0], ffn_ada_b[1], kv_ada_b)
    ada, off = [], 0
    for nl, bias in zip(ADA_LOC, ada_bias):
        full = ada_me[:, off:off + nl].reshape(-1) + bias
        ada.append([row(t) for t in jnp.split(full, full.shape[0] // D)])
        off += nl
    (sh_m0, sc_m0, gt_m0), (sh_m1, sc_m1, gt_m1), (sh_f0, sc_f0, gt_f0), (sh_f1, sc_f1, gt_f1), (sh_kv, sc_kv) = ada

    pad_in = lambda src, l: jnp.pad(src["ffn_w_in"][l], ((0, 0), (0, FSP - FS)))
    rows_a = lambda src: jnp.concatenate([src["ffn_w_out"][0], src["attn_w_q"][0]])
    rows_b = lambda src: jnp.concatenate([src["ffn_w_out"][1], src["attn_w_o"][0]])
    as_bf = lambda arrs: [t.astype(bf16) for t in arrs]
    fo, sq_rows = F // NDEV, D // NDEV
    g_ci, g_co = _exchange(as_bf([conv_w_in[0], conv_w_out[0]]), True, "gather_weights")
    soon = as_bf([pad_in(W, 0)])
    next_ = as_bf([rows_a(W), kv_w])
    late = as_bf([pad_in(W, 1), rows_b(W)])

    def w_out_of(g_r):
        t = g_r[:, :fo].reshape(NDEV // 2, FS, D)
        return jnp.pad(t, ((0, 0), (0, FSP - FS), (0, 0))).reshape(FP, D)

    conv_in_full = g_ci.transpose(1, 0, 2).reshape(D, 2 * D)
    wc_a, wc_g = conv_in_full[:, :D], conv_in_full[:, D:]
    wc_o = g_co.reshape(D, D)
    zeros_d = jnp.zeros((1, D), f32)
    fb = jnp.pad(forget_b, (0, LANES - H)).reshape(1, LANES)

    h0 = _normmod(x0, row(mix_norm_g[0]), sh_m0, sc_m0, "norm_mix0")
    a0, gl0, p0 = _mm_gated(h0, wc_a, wc_g, b_in[:, :D], b_in[:, D:], False, "conv_in")
    q0, s0, w_sh0 = _conv_fwd(p0, w_dw, b_dw, ln_g, ln_b, "conv_dw", ride=soon)
    x1, y0, h_f0 = _mm_res(s0, wc_o, b_out, x0, gt_m0, "conv_out", norms=[(row(ffn_norm_g[0]), sh_f0, sc_f0)])

    def ffn_fwd(xin, h, l, gt, w_sh, w_out=None, ride=(), norms=()):
        ug, uu, act, *got = _ffn_in(h, w_sh, f"ffn_in{l}", ride=ride)
        if w_out is None:
            w_out = w_out_of(got[0])
        xo, y, *hs = _mm_res(act, w_out, zeros_d, xin, gt, f"ffn_out{l}", norms=norms)
        return xo, (h, ug, uu, act, y, w_sh, w_out), got, hs

    x2, ffn0, (g_ra, g_kv), (hk, h2) = ffn_fwd(
        x1, h_f0, 0, gt_f0, w_sh0, ride=next_,
        norms=[(row(kv_norm_g), sh_kv, sc_kv), (row(mix_norm_g[1]), sh_m1, sc_m1)])
    w_q = g_ra[:, fo:].reshape(D, D)
    kv_full = g_kv.transpose(1, 0, 2).reshape(D, -1)
    w_k, w_v = kv_full[:, :D], kv_full[:, D:2 * D]
    w_f = jnp.pad(kv_full[:, 2 * D:], ((0, 0), (0, LANES - H)))

    k_sh = _mm(hk, w_k, bf16, 1.0, "proj_k")
    v_sh = _mm(hk, w_v, bf16, 1.0, "proj_v")
    fl = _mm(hk, w_f, f32, 1.0, "proj_f")
    cum = _forget_fwd(fl, fb, "forget_fwd")
    cum_t = cum[:, :H].T.reshape(NP, 2, T)

    qh = _mm(h2, w_q, bf16, HD ** -0.5, "proj_q")
    o, o32, lse, w_sh1, g_rb = _attn_fwd(qh, k_sh, v_sh, cum_t, "attn_fwd", ride=late)
    w_out1, w_o = w_out_of(g_rb), g_rb[:, fo:].reshape(D, D)
    x3, y1, h_f1 = _mm_res(o, w_o, zeros_d, x2, gt_m1, "attn_out", norms=[(row(ffn_norm_g[1]), sh_f1, sc_f1)])

    x4, ffn1, _, _ = ffn_fwd(x3, h_f1, 1, gt_f1, w_sh1, w_out1)

    dx4, acc_fin = _final_bwd(x4, row(final_norm_g), tgt, "final_bwd")

    d_ada = {}
    by_rows = lambda g: g.reshape(NDEV, sq_rows, D)

    def ffn_bwd(dx_out, xin, l, sc, gt, saved):
        h, ug, uu, act, y, w_sh, w_out = saved
        dyb, acc_r = _res_in(dx_out, y, gt, f"ffn_res_bwd{l}")
        dug, duu = _mm_nt_swiglu(dyb, w_out, ug, uu, f"ffn_dact{l}")
        g_out = _mm_tn(act, dyb, f"ffn_dw_out{l}").reshape(NDEV // 2, FSP, D)[:, :FS].reshape(NDEV, fo, D)
        g_in = _mm_tn_shards(h, duu, FSP, f"ffn_dw_up{l}", into=_mm_tn_shards(h, dug, FSP, f"ffn_dw_gate{l}"))
        dh = _ffn_dh(dug, duu, w_sh, f"ffn_dh{l}")
        dxi, acc_n = _normmod_bwd(dh, xin, row(ffn_norm_g[l]), sc, dx_out, f"norm_ffn_bwd{l}")
        return dxi, g_in, g_out, [acc_n[0:1], acc_n[1:2], acc_r[0:1]], acc_n[2]

    dx3, g_in1, g_out1, d_ada[("ffn", 1)], dg_ffn1 = ffn_bwd(dx4, x3, 1, sc_f1, gt_f1, ffn1)

    dyb, acc_r = _res_in(dx3, y1, gt_m1, "attn_res_bwd")
    do = _mm_nt([(dyb, w_o)], bf16, "attn_do")
    g_wo = _mm_tn(o, dyb, "attn_dw_o")
    st = _attn_stats(do, o32, lse, "attn_stats")
    leave_b = as_bf([g_in1, jnp.concatenate([g_out1, by_rows(g_wo)], axis=1)])
    dq, dk, dv, dck, r_in1, r_rb = _attn_bwd(qh, k_sh, v_sh, do, st, cum_t, "attn_bwd", ride=leave_b)
    g_wq = _mm_tn(h2, dq, "attn_dw_q")
    dh2 = _mm_nt([(dq, w_q)], f32, "attn_dh")
    dx2, acc_n = _normmod_bwd(dh2, x2, row(mix_norm_g[1]), sc_m1, dx3, "norm_mix_bwd1")
    d_ada[("mix", 1)] = [acc_n[0:1], acc_n[1:2], acc_r[0:1]]
    dg_mix1 = acc_n[2]

    dcum = jnp.pad(dck.reshape(H, T).T, ((0, 0), (0, LANES - H)))
    dfl, acc_f = _forget_bwd(dcum, fl, fb, "forget_bwd")
    g_kvw = jnp.concatenate([_mm_tn(hk, dk, "kv_dw_k"), _mm_tn(hk, dv, "kv_dw_v"), _mm_tn(hk, dfl, "kv_dw_f")[:, :H]], axis=1)
    dhk = _mm_nt([(dk, w_k), (dv, w_v), (dfl, w_f)], f32, "kv_dh")
    dx2, acc_n = _normmod_bwd(dhk, x2, row(kv_norm_g), sc_kv, dx2, "norm_kv_bwd")
    d_ada[("kv", 0)] = [acc_n[0:1], acc_n[1:2]]
    dg_kv = acc_n[2]

    dx1, g_in0, g_out0, d_ada[("ffn", 0)], dg_ffn0 = ffn_bwd(dx2, x1, 0, sc_f0, gt_f0, ffn0)

    dyb, acc_r = _res_in(dx1, y0, gt_m0, "conv_res_bwd")
    dsw = _mm_nt([(dyb, wc_o)], f32, "conv_ds")
    g_co_out = _mm_tn(s0, dyb, "conv_dw_out")
    leave_a = as_bf([g_in0, jnp.concatenate([g_out0, by_rows(g_wq)], axis=1), g_kvw.reshape(D, NDEV, -1).transpose(1, 0, 2)])
    da, dgl, acc_c, dw_dw, r_in0, r_ra, r_kv = _conv_bwd(dsw, q0, p0, a0, gl0, w_dw, ln_g, ln_b, "conv_bwd", ride=leave_a)
    cs = 2 * D // NDEV
    g_ci_out = _mm_tn_shards(h0, dgl, cs, "conv_dw_g", into=_mm_tn_shards(h0, da, cs, "conv_dw_a"))
    dh0, r_ci, r_co = _mm_nt([(da, wc_a), (dgl, wc_g)], f32, "conv_dh", ride=as_bf([g_ci_out, by_rows(g_co_out)]))
    dx0, acc_n = _normmod_bwd(dh0, x0, row(mix_norm_g[0]), sc_m0, dx1, "norm_mix_bwd0")
    d_ada[("mix", 0)] = [acc_n[0:1], acc_n[1:2], acc_r[0:1]]
    dg_mix0 = acc_n[2]

    vec = [t.reshape(-1) for key in [(s[0], s[1]) for s in ADA_SEG] for t in d_ada[key]]
    vec += [dg_mix0, dg_mix1, dg_ffn0, dg_ffn1, dg_kv, acc_fin[0]]
    vec += [acc_f[0], acc_fin[1, :LANES]]
    vec += [acc_c[3], acc_c[4], dw_dw[:KW].reshape(-1), acc_c[2], acc_c[0], acc_c[1], acc_r[1]]
    small_parts = _exchange([_pad_rows(jnp.concatenate(vec), LANES, 8)], True, "gather_partials")[0]
    small_sum = _sum_parts(small_parts, "sum_partials").reshape(-1)
    d_ada_all = small_parts.reshape(NDEV, -1)[:, :ADA_TOT]
    off = 0
    gsm = {}
    ada_b_sum = []
    for _, _, n in ADA_SEG:
        ada_b_sum.append(small_sum[off:off + n]); off += n
    gsm["mix_ada_b"] = jnp.stack(ada_b_sum[0:2])
    gsm["ffn_ada_b"] = jnp.stack(ada_b_sum[2:4])
    gsm["kv_ada_b"] = ada_b_sum[4]
    gsm["mix_norm_g"] = small_sum[off:off + 2 * D].reshape(2, D); off += 2 * D
    gsm["ffn_norm_g"] = small_sum[off:off + 2 * D].reshape(2, D); off += 2 * D
    gsm["kv_norm_g"] = small_sum[off:off + D]; off += D
    gsm["final_norm_g"] = small_sum[off:off + D]; off += D
    gsm["forget_b"] = small_sum[off:off + H]; off += LANES
    loss = small_sum[off]; off += LANES
    sl = lambda full, width: lax.dynamic_slice_in_dim(full, me * width, width, axis=full.ndim - 1)
    gsm["conv_b_in"] = sl(small_sum[off:off + 2 * D].reshape(1, 2 * D), 2 * D // NDEV); off += 2 * D
    gsm["conv_w_dw"] = sl(small_sum[off:off + KW * D].reshape(1, KW, D), cl); off += KW * D
    for n in ("conv_b_dw", "conv_ln_g", "conv_ln_b", "conv_b_out"):
        gsm[n] = sl(small_sum[off:off + D].reshape(1, D), cl); off += D

    dsel, off = [], 0
    for (_, _, n), nl in zip(ADA_SEG, ADA_LOC):
        dsel.append(lax.dynamic_slice_in_dim(d_ada_all[:, off:off + n], me * nl, nl, axis=1)); off += n
    g_ada = _ada_bwd(c_all.T, jnp.concatenate(dsel, axis=1), "ada_bwd")
    res_ada = _adamw(g_ada[None], w_cat, cat_ada(M), cat_ada(V), "adamw_ada")

    res_in0 = _adamw(r_in0, *[pad_in(s, 0) for s in (W, M, V)], "adamw_ffn_in0")
    res_in1 = _adamw(r_in1, *[pad_in(s, 1) for s in (W, M, V)], "adamw_ffn_in1")
    res_ra = _adamw(r_ra, *[rows_a(s) for s in (W, M, V)], "adamw_rows_a")
    res_rb = _adamw(r_rb, *[rows_b(s) for s in (W, M, V)], "adamw_rows_b")
    res_ci = _adamw(r_ci, *[s["conv_w_in"][0] for s in (W, M, V)], "adamw_conv_in")
    res_co = _adamw(r_co, *[s["conv_w_out"][0] for s in (W, M, V)], "adamw_conv_out")
    res_kv = _adamw(r_kv, *[s["kv_w"] for s in (W, M, V)], "adamw_kv")
    rest = [n for n in names if n not in MAIN and n not in ("mix_ada_w", "ffn_ada_w", "kv_ada_w")]
    pack_rest = lambda src: _pad_rows(jnp.concatenate([src[n].reshape(-1) for n in rest]), D, 256)
    res_rest = _adamw(pack_rest(gsm)[None], pack_rest(W), pack_rest(M), pack_rest(V), "adamw_rest")

    outs = []
    for k in range(4):
        ur = _split_flat(res_rest[k].reshape(-1), [(n, W[n].shape) for n in rest])
        ra, a0_, a2_ = res_ada[k], ADA_LOC[0], ADA_LOC[2]
        ur["mix_ada_w"] = jnp.stack([ra[:, 0:a0_], ra[:, a0_:2 * a0_]])
        ur["ffn_ada_w"] = jnp.stack([ra[:, 2 * a0_:2 * a0_ + a2_], ra[:, 2 * a0_ + a2_:2 * a0_ + 2 * a2_]])
        ur["kv_ada_w"] = ra[:, 2 * a0_ + 2 * a2_:]
        ur["ffn_w_in"] = jnp.stack([res_in0[k][:, :FS], res_in1[k][:, :FS]])
        ur["conv_w_in"] = res_ci[k][None]
        ur["conv_w_out"] = res_co[k][None]
        ur["kv_w"] = res_kv[k]
        ur["ffn_w_out"] = jnp.stack([res_ra[k][:fo], res_rb[k][:fo]])
        ur["attn_w_q"] = res_ra[k][fo:][None]
        ur["attn_w_o"] = res_rb[k][fo:][None]
        outs.append(ur)
    grads, deltas, new_m, new_v = outs
    return (loss, dx0[None], *[grads[n] for n in names], *[deltas[n] for n in names],
            *[new_m[n] for n in names], *[new_v[n] for n in names])
```

```python
import functools

import jax
import jax.numpy as jnp
from jax import lax
from jax.experimental import pallas as pl
from jax.experimental.pallas import tpu as pltpu

f32, bf16 = jnp.float32, jnp.bfloat16
SDS = jax.ShapeDtypeStruct

D = 1024
F = 2816
H = 16
HD = 64
NP = H // 2
KW = 31
HALO = 32
NDEV = 8
FS = 2 * F // NDEV
FSP = 768
FP = 4 * FSP
EPS = 1e-6
NEG = -1e30
LANES = 128

ADAM_LR, ADAM_B1, ADAM_B2, ADAM_EPS, ADAM_WD, ADAM_STEP = 0.001, 0.9, 0.999, 1e-08, 0.01, 10

TM = 512
TMM = 1024
TC = 256
TQ = 1024
VMEM_LIMIT = 56 << 20

MAIN = ("ffn_w_in", "ffn_w_out", "conv_w_in", "conv_w_out", "kv_w", "attn_w_q", "attn_w_o")
ADA_SEG = (("mix", 0, 3 * D), ("mix", 1, 3 * D), ("ffn", 0, 3 * D), ("ffn", 1, 3 * D), ("kv", 0, 2 * D))
ADA_LOC = tuple(n // NDEV for _, _, n in ADA_SEG)
ADA_COLS = sum(ADA_LOC)
ADA_TOT = sum(n for _, _, n in ADA_SEG)


def _cparams(n_axes):
    return pltpu.CompilerParams(dimension_semantics=("arbitrary",) * n_axes, vmem_limit_bytes=VMEM_LIMIT)


def _mesh_pos():
    return lax.axis_index("x"), lax.axis_index("y"), lax.axis_index("c")


def _my_index():
    mx, my, mc = _mesh_pos()
    return 4 * mx + 2 * my + mc


def _peer(k, mx, my, mc):
    px = (1 - mx) if k & 4 else mx
    py = (1 - my) if k & 2 else my
    pc = (1 - mc) if k & 1 else mc
    return (px, py, pc), 4 * px + 2 * py + pc


def _exchange_copies(x_refs, o_refs, sems, gather):
    send_sems, recv_sems, local_sems = sems
    mx, my, mc = _mesh_pos()
    me = 4 * mx + 2 * my + mc
    copies = []
    for a, (x_ref, o_ref) in enumerate(zip(x_refs, o_refs)):
        copies.append(pltpu.make_async_copy(x_ref if gather else x_ref.at[me], o_ref.at[me], local_sems.at[a]))
        for k in range(1, NDEV):
            peer, pidx = _peer(k, mx, my, mc)
            sem = a * (NDEV - 1) + k - 1
            copies.append(pltpu.make_async_remote_copy(
                src_ref=x_ref if gather else x_ref.at[pidx], dst_ref=o_ref.at[me],
                send_sem=send_sems.at[sem], recv_sem=recv_sems.at[sem],
                device_id=peer, device_id_type=pl.DeviceIdType.MESH))
    return copies


def _exchange_shapes(xs, gather):
    n = len(xs)
    hbm = pl.BlockSpec(memory_space=pl.ANY)
    outs = [SDS((NDEV,) + tuple(x.shape if gather else x.shape[1:]), x.dtype) for x in xs]
    sems = [pltpu.SemaphoreType.DMA((n * (NDEV - 1),)), pltpu.SemaphoreType.DMA((n * (NDEV - 1),)), pltpu.SemaphoreType.DMA((n,))]
    return [hbm] * n, [hbm] * n, outs, sems


def _exchange_start(x_refs, o_refs, sems, gather, first):
    @pl.when(first)
    def _():
        for cp in _exchange_copies(x_refs, o_refs, sems, gather):
            cp.start()


def _exchange_wait(x_refs, o_refs, sems, gather, last):
    @pl.when(last)
    def _():
        for cp in _exchange_copies(x_refs, o_refs, sems, gather):
            cp.wait()


def _exchange(xs, gather, name):
    n = len(xs)

    def body(*refs):
        copies = _exchange_copies(refs[:n], refs[n:2 * n], refs[2 * n:], gather)
        for cp in copies:
            cp.start()
        for cp in copies:
            cp.wait()

    in_specs, out_specs, outs, sems = _exchange_shapes(xs, gather)
    return pl.pallas_call(body, out_shape=outs, in_specs=in_specs, out_specs=out_specs, scratch_shapes=sems, name=name)(*xs)


def _row(i):
    return (i, 0)


def _fix(i):
    return (0, 0)


def _normmod(x, g, shift, scale, name):
    T = x.shape[0]
    tm = min(TM, T)

    def body(x_ref, g_ref, sh_ref, sc_ref, h_ref):
        xv = x_ref[...]
        r = lax.rsqrt(jnp.mean(xv * xv, axis=-1, keepdims=True) + EPS)
        hn = (xv * r) * g_ref[...]
        h_ref[...] = (hn * (1.0 + sc_ref[...]) + sh_ref[...]).astype(bf16)

    vec = pl.BlockSpec((1, D), _fix)
    return pl.pallas_call(
        body, grid=(T // tm,), in_specs=[pl.BlockSpec((tm, D), _row), vec, vec, vec],
        out_specs=pl.BlockSpec((tm, D), _row), out_shape=SDS((T, D), bf16),
        compiler_params=_cparams(1), name=name)(x, g, shift, scale)


def _normmod_bwd(dh, x, g, scale, dx_res, name):
    T = x.shape[0]
    tm = min(TM, T)

    def body(dh_ref, x_ref, g_ref, sc_ref, res_ref, dx_ref, acc_ref):
        @pl.when(pl.program_id(0) == 0)
        def _():
            acc_ref[...] = jnp.zeros_like(acc_ref)
        xv = x_ref[...]
        dhv = dh_ref[...]
        gv = g_ref[...]
        r = lax.rsqrt(jnp.mean(xv * xv, axis=-1, keepdims=True) + EPS)
        xn = xv * r
        dhn = dhv * (1.0 + sc_ref[...])
        dxn = dhn * gv
        dx_ref[...] = res_ref[...] + r * (dxn - xn * jnp.mean(dxn * xn, axis=-1, keepdims=True))
        acc_ref[0:1, :] += jnp.sum(dhv, axis=0, keepdims=True)
        acc_ref[1:2, :] += jnp.sum(dhv * (xn * gv), axis=0, keepdims=True)
        acc_ref[2:3, :] += jnp.sum(dhn * xn, axis=0, keepdims=True)

    vec = pl.BlockSpec((1, D), _fix)
    til = pl.BlockSpec((tm, D), _row)
    return pl.pallas_call(
        body, grid=(T // tm,), in_specs=[til, til, vec, vec, til],
        out_specs=[til, pl.BlockSpec((8, D), _fix)], out_shape=[SDS((T, D), f32), SDS((8, D), f32)],
        compiler_params=_cparams(1), name=name)(dh, x, g, scale, dx_res)


def _res_in(dx, y, gate, name):
    T = dx.shape[0]
    tm = min(TM, T)

    def body(dx_ref, y_ref, gt_ref, dy_ref, acc_ref):
        @pl.when(pl.program_id(0) == 0)
        def _():
            acc_ref[...] = jnp.zeros_like(acc_ref)
        dxv = dx_ref[...]
        dy = dxv * gt_ref[...]
        dy_ref[...] = dy.astype(bf16)
        acc_ref[0:1, :] += jnp.sum(dxv * y_ref[...].astype(f32), axis=0, keepdims=True)
        acc_ref[1:2, :] += jnp.sum(dy, axis=0, keepdims=True)

    til = pl.BlockSpec((tm, D), _row)
    return pl.pallas_call(
        body, grid=(T // tm,), in_specs=[til, til, pl.BlockSpec((1, D), _fix)],
        out_specs=[til, pl.BlockSpec((8, D), _fix)], out_shape=[SDS((T, D), bf16), SDS((8, D), f32)],
        compiler_params=_cparams(1), name=name)(dx, y, gate)


def _final_bwd(x, g, tgt, name):
    T = x.shape[0]
    tm = min(TM, T)

    def body(x_ref, g_ref, t_ref, dx_ref, acc_ref):
        @pl.when(pl.program_id(0) == 0)
        def _():
            acc_ref[...] = jnp.zeros_like(acc_ref)
        xv = x_ref[...]
        gv = g_ref[...]
        r = lax.rsqrt(jnp.mean(xv * xv, axis=-1, keepdims=True) + EPS)
        xn = xv * r
        err = xn * gv - t_ref[...]
        dy = err * (1.0 / D)
        dxn = dy * gv
        dx_ref[...] = r * (dxn - xn * jnp.mean(dxn * xn, axis=-1, keepdims=True))
        acc_ref[0:1, :] += jnp.sum(dy * xn, axis=0, keepdims=True)
        acc_ref[1:2, :] += 0.5 * jnp.sum(jnp.mean(err * err, axis=-1, keepdims=True))

    til = pl.BlockSpec((tm, D), _row)
    return pl.pallas_call(
        body, grid=(T // tm,), in_specs=[til, pl.BlockSpec((1, D), _fix), til],
        out_specs=[til, pl.BlockSpec((8, D), _fix)], out_shape=[SDS((T, D), f32), SDS((8, D), f32)],
        compiler_params=_cparams(1), name=name)(x, g, tgt)


def _col_tile(n):
    if n <= 1024:
        return n
    return 1408 if n % 1408 == 0 else 1024


def _mm_gated(h, wa, wb, ba, bb, swiglu, name):
    T, K = h.shape
    N = wa.shape[1]
    tm, tn = min(TMM, T), _col_tile(N)

    def body(h_ref, wa_ref, wb_ref, ba_ref, bb_ref, u_ref, w_ref, p_ref):
        hv = h_ref[...]
        u = jnp.dot(hv, wa_ref[...], preferred_element_type=f32) + ba_ref[...]
        w = jnp.dot(hv, wb_ref[...], preferred_element_type=f32) + bb_ref[...]
        u_ref[...] = u
        w_ref[...] = w
        if swiglu:
            p_ref[...] = ((u * jax.nn.sigmoid(u)) * w).astype(p_ref.dtype)
        else:
            p_ref[...] = (u * jax.nn.sigmoid(w)).astype(p_ref.dtype)

    wsp = pl.BlockSpec((K, tn), lambda i, j: (0, j))
    bsp = pl.BlockSpec((1, tn), lambda i, j: (0, j))
    osp = pl.BlockSpec((tm, tn), lambda i, j: (i, j))
    return pl.pallas_call(
        body, grid=(T // tm, N // tn), in_specs=[pl.BlockSpec((tm, K), lambda i, j: (i, 0)), wsp, wsp, bsp, bsp],
        out_specs=[osp, osp, osp], out_shape=[SDS((T, N), f32), SDS((T, N), f32), SDS((T, N), bf16)],
        compiler_params=_cparams(2), name=name)(h, wa, wb, ba, bb)


def _mm_res(a, w, b, x_in, gate, name, norms=()):
    T, K = a.shape
    N = w.shape[1]
    nn = len(norms)
    tm, tn = min(TM if (nn and K > 1024) else TMM, T), _col_tile(N)
    assert tn == N or not nn

    def body(a_ref, w_ref, b_ref, x_ref, gt_ref, *rest):
        vecs, (xo_ref, y_ref), h_refs = rest[:3 * nn], rest[3 * nn:3 * nn + 2], rest[3 * nn + 2:]
        y = jnp.dot(a_ref[...], w_ref[...], preferred_element_type=f32) + b_ref[...]
        y_ref[...] = y.astype(bf16)
        xv = x_ref[...] + gt_ref[...] * y
        xo_ref[...] = xv
        if nn:
            r = lax.rsqrt(jnp.mean(xv * xv, axis=-1, keepdims=True) + EPS)
            xn = xv * r
            for k in range(nn):
                g_ref, sh_ref, sc_ref = vecs[3 * k:3 * k + 3]
                h_refs[k][...] = ((xn * g_ref[...]) * (1.0 + sc_ref[...]) + sh_ref[...]).astype(bf16)

    vsp = pl.BlockSpec((1, tn), lambda i, j: (0, j))
    osp = pl.BlockSpec((tm, tn), lambda i, j: (i, j))
    flat = [v for trio in norms for v in trio]
    return pl.pallas_call(
        body, grid=(T // tm, N // tn),
        in_specs=[pl.BlockSpec((tm, K), lambda i, j: (i, 0)), pl.BlockSpec((K, tn), lambda i, j: (0, j)), vsp, osp, vsp] + [vsp] * (3 * nn),
        out_specs=[osp, osp] + [osp] * nn, out_shape=[SDS((T, N), f32), SDS((T, N), bf16)] + [SDS((T, N), bf16)] * nn,
        compiler_params=_cparams(2), name=name)(a, w, b, x_in, gate, *flat)


def _mm(a, w, out_dtype, out_scale, name):
    T, K = a.shape
    N = w.shape[1]
    tm, tn = min(TMM, T), _col_tile(N)

    def body(a_ref, w_ref, o_ref):
        y = jnp.dot(a_ref[...], w_ref[...], preferred_element_type=f32)
        if out_scale != 1.0:
            y = y * out_scale
        o_ref[...] = y.astype(out_dtype)

    return pl.pallas_call(
        body, grid=(T // tm, N // tn),
        in_specs=[pl.BlockSpec((tm, K), lambda i, j: (i, 0)), pl.BlockSpec((K, tn), lambda i, j: (0, j))],
        out_specs=pl.BlockSpec((tm, tn), lambda i, j: (i, j)), out_shape=SDS((T, N), out_dtype),
        compiler_params=_cparams(2), name=name)(a, w)


def _dot_nt(a, b):
    return lax.dot_general(a, b, (((1,), (1,)), ((), ())), preferred_element_type=f32)


def _dot_tn(a, b):
    return lax.dot_general(a, b, (((0,), (0,)), ((), ())), preferred_element_type=f32)


def _mm_nt(pairs, out_dtype, name, ride=()):
    T = pairs[0][0].shape[0]
    K = pairs[0][1].shape[0]
    tm, tk = min(TMM, T), _col_tile(K)
    n = len(pairs)
    nx = len(ride)
    gi, gj = T // tm, K // tk

    def body(*refs):
        x_refs, o_ref, xo_refs, sems = refs[2 * n:2 * n + nx], refs[2 * n + nx], refs[2 * n + nx + 1:2 * n + 2 * nx + 1], refs[2 * n + 2 * nx + 1:]
        i, j = pl.program_id(0), pl.program_id(1)
        if nx:
            _exchange_start(x_refs, xo_refs, sems, False, (i == 0) & (j == 0))
        acc = None
        for a in range(n):
            part = _dot_nt(refs[2 * a][...].astype(bf16), refs[2 * a + 1][...])
            acc = part if acc is None else acc + part
        o_ref[...] = acc.astype(out_dtype)
        if nx:
            _exchange_wait(x_refs, xo_refs, sems, False, (i == gi - 1) & (j == gj - 1))

    in_specs, args = [], []
    for dy, w in pairs:
        ni = dy.shape[1]
        in_specs += [pl.BlockSpec((tm, ni), lambda i, j: (i, 0)), pl.BlockSpec((tk, ni), lambda i, j: (j, 0))]
        args += [dy, w]
    x_in, x_out, x_shape, x_sems = _exchange_shapes(ride, False) if nx else ([], [], [], [])
    out = pl.pallas_call(
        body, grid=(gi, gj), in_specs=in_specs + x_in,
        out_specs=[pl.BlockSpec((tm, tk), lambda i, j: (i, j))] + x_out, out_shape=[SDS((T, K), out_dtype)] + x_shape,
        scratch_shapes=x_sems, compiler_params=_cparams(2), name=name)(*args, *ride)
    return out if nx else out[0]


def _mm_nt_swiglu(dy, w, ug, uu, name):
    T, N = dy.shape
    K = w.shape[0]
    tm, tk = min(TMM, T), _col_tile(K)

    def body(dy_ref, w_ref, ug_ref, uu_ref, dug_ref, duu_ref):
        dact = _dot_nt(dy_ref[...], w_ref[...])
        g = ug_ref[...].astype(f32)
        u = uu_ref[...].astype(f32)
        sg = jax.nn.sigmoid(g)
        duu_ref[...] = (dact * (g * sg)).astype(bf16)
        dug_ref[...] = (dact * u * (sg * (1.0 + g * (1.0 - sg)))).astype(bf16)

    osp = pl.BlockSpec((tm, tk), lambda i, j: (i, j))
    return pl.pallas_call(
        body, grid=(T // tm, K // tk),
        in_specs=[pl.BlockSpec((tm, N), lambda i, j: (i, 0)), pl.BlockSpec((tk, N), lambda i, j: (j, 0)), osp, osp],
        out_specs=[osp, osp], out_shape=[SDS((T, K), bf16), SDS((T, K), bf16)],
        compiler_params=_cparams(2), name=name)(dy, w, ug, uu)


def _mm_tn(a, b, name):
    T, K = a.shape
    N = b.shape[1]
    tt = min(TMM, T)
    tk = K if K <= 1024 else _col_tile(K)
    tn = N if N <= 1024 else _col_tile(N)

    def body(a_ref, b_ref, o_ref):
        @pl.when(pl.program_id(2) == 0)
        def _():
            o_ref[...] = jnp.zeros_like(o_ref)
        o_ref[...] += _dot_tn(a_ref[...].astype(bf16), b_ref[...].astype(bf16))

    return pl.pallas_call(
        body, grid=(K // tk, N // tn, T // tt),
        in_specs=[pl.BlockSpec((tt, tk), lambda i, j, t: (t, i)), pl.BlockSpec((tt, tn), lambda i, j, t: (t, j))],
        out_specs=pl.BlockSpec((tk, tn), lambda i, j, t: (i, j)), out_shape=SDS((K, N), f32),
        compiler_params=_cparams(3), name=name)(a, b)


def _mm_tn_shards(a, b, c, name, into=None):
    T, K = a.shape
    half = NDEV // 2
    assert b.shape[1] == half * c
    tt = min(TMM, T)
    nt = T // tt
    first = into is None

    def body(a_ref, b_ref, *rest):
        o_ref, acc = rest[-2], rest[-1]
        t = pl.program_id(1)

        @pl.when(t == 0)
        def _():
            acc[...] = jnp.zeros_like(acc)
        acc[...] += _dot_tn(a_ref[...].astype(bf16), b_ref[...].astype(bf16))

        @pl.when(t == nt - 1)
        def _():
            o_ref[...] = acc[...].astype(bf16)

    in_specs = [pl.BlockSpec((tt, K), lambda j, t: (t, 0)), pl.BlockSpec((tt, c), lambda j, t: (t, j))]
    args = [a, b]
    if not first:
        in_specs.append(pl.BlockSpec(memory_space=pl.ANY))
        args.append(into)
    base = 0 if first else half
    return pl.pallas_call(
        body, grid=(half, nt), in_specs=in_specs,
        out_specs=pl.BlockSpec((None, K, c), lambda j, t: (j + base, 0, 0)), out_shape=SDS((NDEV, K, c), bf16),
        scratch_shapes=[pltpu.VMEM((K, c), f32)], input_output_aliases={} if first else {2: 0},
        compiler_params=_cparams(2), name=name)(*args)


def _ffn_in(h, w_sh, name, ride=()):
    T = h.shape[0]
    tm = min(TMM, T)
    nx = len(ride)
    gi, gj = T // tm, NDEV // 2

    def body(h_ref, wg_ref, wu_ref, *rest):
        x_refs, (ug_ref, uu_ref, act_ref), xo_refs, sems = rest[:nx], rest[nx:nx + 3], rest[nx + 3:2 * nx + 3], rest[2 * nx + 3:]
        i, j = pl.program_id(0), pl.program_id(1)
        if nx:
            _exchange_start(x_refs, xo_refs, sems, True, (i == 0) & (j == 0))
        hv = h_ref[...]
        ug = jnp.dot(hv, wg_ref[...], preferred_element_type=f32)
        uu = jnp.dot(hv, wu_ref[...], preferred_element_type=f32)
        ug_ref[...] = ug.astype(bf16)
        uu_ref[...] = uu.astype(bf16)
        act_ref[...] = ((ug * jax.nn.sigmoid(ug)) * uu).astype(bf16)
        if nx:
            _exchange_wait(x_refs, xo_refs, sems, True, (i == gi - 1) & (j == gj - 1))

    osp = pl.BlockSpec((tm, FSP), lambda i, j: (i, j))
    x_in, x_out, x_shape, x_sems = _exchange_shapes(ride, True) if nx else ([], [], [], [])
    return pl.pallas_call(
        body, grid=(gi, gj),
        in_specs=[pl.BlockSpec((tm, D), lambda i, j: (i, 0)),
                  pl.BlockSpec((None, D, FSP), lambda i, j: (j, 0, 0)),
                  pl.BlockSpec((None, D, FSP), lambda i, j: (j + NDEV // 2, 0, 0))] + x_in,
        out_specs=[osp, osp, osp] + x_out,
        out_shape=[SDS((T, FP), bf16), SDS((T, FP), bf16), SDS((T, FP), bf16)] + x_shape,
        scratch_shapes=x_sems, compiler_params=_cparams(2), name=name)(h, w_sh, w_sh, *ride)


def _ffn_dh(dug, duu, w_sh, name):
    T = dug.shape[0]
    tm, tk = min(TMM, T), 512
    half = NDEV // 2

    def body(dg_ref, du_ref, wg_ref, wu_ref, o_ref):
        acc = None
        for s in range(half):
            cols = slice(s * FSP, (s + 1) * FSP)
            part = _dot_nt(dg_ref[:, cols], wg_ref[s]) + _dot_nt(du_ref[:, cols], wu_ref[s])
            acc = part if acc is None else acc + part
        o_ref[...] = acc

    dsp = pl.BlockSpec((tm, FP), lambda i, j: (i, 0))
    return pl.pallas_call(
        body, grid=(T // tm, D // tk),
        in_specs=[dsp, dsp, pl.BlockSpec((half, tk, FSP), lambda i, j: (0, j, 0)),
                  pl.BlockSpec((half, tk, FSP), lambda i, j: (1, j, 0))],
        out_specs=pl.BlockSpec((tm, tk), lambda i, j: (i, j)), out_shape=SDS((T, D), f32),
        compiler_params=_cparams(2), name=name)(dug, duu, w_sh, w_sh)


def _layernorm_parts(qv, g, b):
    mu = jnp.mean(qv, axis=-1, keepdims=True)
    cen = qv - mu
    rstd = lax.rsqrt(jnp.mean(cen * cen, axis=-1, keepdims=True) + EPS)
    z = cen * rstd
    return z, rstd, z * g + b


def _tap_groups(offsets):
    groups = {}
    for k, o in enumerate(offsets):
        groups.setdefault(o % 8, []).append((k, o - o % 8))
    return groups


def _conv_fwd(p, w_dw, b_dw, ln_g, ln_b, name, ride=()):
    T = p.shape[0]
    tc = min(TC, T)
    n = T // tc
    nx = len(ride)

    def body(p_ref, w_ref, b_ref, g_ref, bb_ref, *rest):
        x_refs, (q_ref, s_ref), xo_refs = rest[:nx], rest[nx:nx + 2], rest[nx + 2:2 * nx + 2]
        (ext, sh), sems = rest[2 * nx + 2:2 * nx + 4], rest[2 * nx + 4:]
        i = pl.program_id(0)
        if nx:
            _exchange_start(x_refs, xo_refs, sems, True, i == 0)

        @pl.when(i == 0)
        def _():
            ext[0:HALO, :] = jnp.zeros((HALO, D), f32)

        @pl.when(i > 0)
        def _():
            ext[0:HALO, :] = ext[tc:tc + HALO, :]

        ext[HALO:HALO + tc, :] = p_ref[...].astype(f32)
        groups = _tap_groups([HALO - (KW - 1) + k for k in range(KW)])
        for cb in range(D // LANES):
            cols = slice(cb * LANES, (cb + 1) * LANES)
            acc = jnp.zeros((tc, LANES), f32)
            for r, taps in groups.items():
                span = max(base for _, base in taps) + tc
                sh[0:span, :] = ext[r:r + span, cols]
                for k, base in taps:
                    acc = acc + w_ref[k:k + 1, cols] * sh[base:base + tc, :]
            q_ref[:, cols] = acc + b_ref[:, cols]
        _, _, l = _layernorm_parts(q_ref[...], g_ref[...], bb_ref[...])
        s_ref[...] = (l * jax.nn.sigmoid(l)).astype(bf16)
        if nx:
            _exchange_wait(x_refs, xo_refs, sems, True, i == n - 1)

    vec = pl.BlockSpec((1, D), _fix)
    til = pl.BlockSpec((tc, D), _row)
    x_in, x_out, x_shape, x_sems = _exchange_shapes(ride, True) if nx else ([], [], [], [])
    return pl.pallas_call(
        body, grid=(n,), in_specs=[til, pl.BlockSpec((HALO, D), _fix), vec, vec, vec] + x_in,
        out_specs=[til, til] + x_out, out_shape=[SDS((T, D), f32), SDS((T, D), bf16)] + x_shape,
        scratch_shapes=[pltpu.VMEM((tc + HALO, D), f32), pltpu.VMEM((tc + HALO, LANES), f32)] + x_sems, compiler_params=_cparams(1),
        name=name)(p, w_dw, b_dw, ln_g, ln_b, *ride)


def _conv_bwd(ds, q, p, a, gl, w_dw, ln_g, ln_b, name, ride=()):
    T = q.shape[0]
    tc = min(TC, T)
    n = T // tc
    nx = len(ride)

    def body(ds_ref, q_ref, p_ref, a_ref, gl_ref, w_ref, g_ref, bb_ref, *rest):
        x_refs, (da_ref, dgl_ref, acc_ref, dw_ref), xo_refs = rest[:nx], rest[nx:nx + 4], rest[nx + 4:2 * nx + 4]
        (ext, sh), sems = rest[2 * nx + 4:2 * nx + 6], rest[2 * nx + 6:]
        i = pl.program_id(0)
        if nx:
            _exchange_start(x_refs, xo_refs, sems, False, i == 0)

        @pl.when(i == 0)
        def _():
            acc_ref[...] = jnp.zeros_like(acc_ref)
            dw_ref[...] = jnp.zeros_like(dw_ref)
            ext[tc:tc + HALO, :] = jnp.zeros((HALO, D), f32)

        @pl.when(i > 0)
        def _():
            ext[tc:tc + HALO, :] = ext[0:HALO, :]

        gv = g_ref[...]
        z, rstd, l = _layernorm_parts(q_ref[...], gv, bb_ref[...])
        sg = jax.nn.sigmoid(l)
        dl = ds_ref[...] * (sg * (1.0 + l * (1.0 - sg)))
        dz = dl * gv
        dq = rstd * (dz - jnp.mean(dz, axis=-1, keepdims=True) - z * jnp.mean(dz * z, axis=-1, keepdims=True))
        ext[0:tc, :] = dq.astype(bf16).astype(f32)
        acc_ref[0:1, :] += jnp.sum(dl * z, axis=0, keepdims=True)
        acc_ref[1:2, :] += jnp.sum(dl, axis=0, keepdims=True)
        acc_ref[2:3, :] += jnp.sum(dq, axis=0, keepdims=True)
        groups = _tap_groups([KW - 1 - k for k in range(KW)])
        for cb in range(D // LANES):
            cols = slice(cb * LANES, (cb + 1) * LANES)
            pc = p_ref[:, cols].astype(f32)
            dp = jnp.zeros((tc, LANES), f32)
            for r, taps in groups.items():
                span = max(base for _, base in taps) + tc
                sh[0:span, :] = ext[r:r + span, cols]
                for k, base in taps:
                    sl = sh[base:base + tc, :]
                    dp = dp + w_ref[k:k + 1, cols] * sl
                    dw_ref[k:k + 1, cols] += jnp.sum(sl * pc, axis=0, keepdims=True)
            av = a_ref[:, cols].astype(f32)
            sgl = jax.nn.sigmoid(gl_ref[:, cols].astype(f32))
            da = dp * sgl
            dgl = dp * av * (sgl * (1.0 - sgl))
            da_ref[:, cols] = da.astype(bf16)
            dgl_ref[:, cols] = dgl.astype(bf16)
            acc_ref[3:4, cols] += jnp.sum(da, axis=0, keepdims=True)
            acc_ref[4:5, cols] += jnp.sum(dgl, axis=0, keepdims=True)
        if nx:
            _exchange_wait(x_refs, xo_refs, sems, False, i == n - 1)

    rev = lambda i: (n - 1 - i, 0)
    til = pl.BlockSpec((tc, D), rev)
    vec = pl.BlockSpec((1, D), _fix)
    x_in, x_out, x_shape, x_sems = _exchange_shapes(ride, False) if nx else ([], [], [], [])
    return pl.pallas_call(
        body, grid=(n,), in_specs=[til, til, til, til, til, pl.BlockSpec((HALO, D), _fix), vec, vec] + x_in,
        out_specs=[til, til, pl.BlockSpec((8, D), _fix), pl.BlockSpec((HALO, D), _fix)] + x_out,
        out_shape=[SDS((T, D), bf16), SDS((T, D), bf16), SDS((8, D), f32), SDS((HALO, D), f32)] + x_shape,
        scratch_shapes=[pltpu.VMEM((tc + HALO, D), f32), pltpu.VMEM((tc + HALO, LANES), f32)] + x_sems, compiler_params=_cparams(1),
        name=name)(ds, q, p, a, gl, w_dw, ln_g, ln_b, *ride)


def _tri(n, upper):
    r = lax.broadcasted_iota(jnp.int32, (n, n), 0)
    c = lax.broadcasted_iota(jnp.int32, (n, n), 1)
    return ((c >= r) if upper else (r >= c)).astype(f32)


def _forget_fwd(fl, fb, name):
    T = fl.shape[0]
    tc = min(TC, T)

    def body(fl_ref, fb_ref, cum_ref, carry):
        @pl.when(pl.program_id(0) == 0)
        def _():
            carry[...] = jnp.zeros_like(carry)
        xv = fl_ref[...] + fb_ref[...]
        lf = jnp.minimum(xv, 0.0) - jnp.log(1.0 + jnp.exp(-jnp.abs(xv)))
        cs = jnp.dot(_tri(tc, False), lf, preferred_element_type=f32, precision=lax.Precision.HIGHEST) + carry[0:1, :]
        cum_ref[...] = cs
        carry[0:1, :] = cs[tc - 1:tc, :]

    til = pl.BlockSpec((tc, LANES), _row)
    return pl.pallas_call(
        body, grid=(T // tc,), in_specs=[til, pl.BlockSpec((1, LANES), _fix)], out_specs=til,
        out_shape=SDS((T, LANES), f32), scratch_shapes=[pltpu.VMEM((8, LANES), f32)],
        compiler_params=_cparams(1), name=name)(fl, fb)


def _forget_bwd(dcum, fl, fb, name):
    T = fl.shape[0]
    tc = min(TC, T)
    n = T // tc

    def body(dc_ref, fl_ref, fb_ref, dfl_ref, acc_ref, carry):
        @pl.when(pl.program_id(0) == 0)
        def _():
            carry[...] = jnp.zeros_like(carry)
            acc_ref[...] = jnp.zeros_like(acc_ref)
        dlf = jnp.dot(_tri(tc, True), dc_ref[...], preferred_element_type=f32, precision=lax.Precision.HIGHEST) + carry[0:1, :]
        carry[0:1, :] = dlf[0:1, :]
        dfl = dlf * (1.0 - jax.nn.sigmoid(fl_ref[...] + fb_ref[...]))
        dfl_ref[...] = dfl.astype(bf16)
        acc_ref[0:1, :] += jnp.sum(dfl, axis=0, keepdims=True)

    rev = lambda i: (n - 1 - i, 0)
    til = pl.BlockSpec((tc, LANES), rev)
    return pl.pallas_call(
        body, grid=(n,), in_specs=[til, til, pl.BlockSpec((1, LANES), _fix)],
        out_specs=[til, pl.BlockSpec((8, LANES), _fix)], out_shape=[SDS((T, LANES), bf16), SDS((8, LANES), f32)],
        scratch_shapes=[pltpu.VMEM((8, LANES), f32)], compiler_params=_cparams(1), name=name)(dcum, fl, fb)


def _causal(s, n):
    r = lax.broadcasted_iota(jnp.int32, (n, n), 0)
    c = lax.broadcasted_iota(jnp.int32, (n, n), 1)
    return jnp.where(c <= r, s, NEG)


def _attn_fwd(q, k, v, cum_t, name, ride=()):
    T = q.shape[0]
    tq = min(TQ, T)
    n = T // tq
    nx = len(ride)

    def body(q_ref, k_ref, v_ref, ck_ref, *rest):
        x_refs, (o_ref, o32_ref, st_ref), xo_refs = rest[:nx], rest[nx:nx + 3], rest[nx + 3:2 * nx + 3]
        (m_sc, l_sc, acc_sc, res_sc), sems = rest[2 * nx + 3:2 * nx + 7], rest[2 * nx + 7:]
        i = pl.program_id(1)
        if nx:
            _exchange_start(x_refs, xo_refs, sems, True, (pl.program_id(0) == 0) & (i == 0))
        lane = lax.broadcasted_iota(jnp.int32, (1, LANES), 1)
        lo = lane < HD
        q2 = q_ref[...]
        zero = jnp.zeros_like(q2)
        qa = (jnp.where(lo, q2, zero), jnp.where(lo, zero, q2))
        m_sc[...] = jnp.full(m_sc.shape, NEG, f32)
        l_sc[...] = jnp.zeros_like(l_sc)
        acc_sc[...] = jnp.zeros_like(acc_sc)
        res_sc[...] = jnp.zeros_like(res_sc)

        def block(j, masked):
            off = pl.multiple_of(j * tq, tq)
            k2 = k_ref[pl.ds(off, tq), :]
            v2 = v_ref[pl.ds(off, tq), :]
            ss = [_dot_nt(qa[a], k2) - ck_ref[0, a:a + 1, pl.ds(off, tq)] for a in range(2)]
            if masked:
                ss = [_causal(t, tq) for t in ss]
            pbs, prs, alphas = [], [], []
            for a in range(2):
                m_old = m_sc[a]
                m_new = jnp.maximum(m_old, jnp.max(ss[a], axis=1, keepdims=True))
                alpha = jnp.exp(m_old - m_new)
                pm = jnp.exp(ss[a] - m_new)
                pb = pm.astype(bf16)
                prs.append((pm - pb.astype(f32)).astype(bf16))
                pbs.append(pb)
                alphas.append(alpha)
                l_sc[a] = alpha * l_sc[a] + jnp.sum(pm, axis=1, keepdims=True)
                m_sc[a] = m_new
            for a in range(2):
                acc_sc[a] = alphas[a] * acc_sc[a] + jnp.dot(pbs[a], v2, preferred_element_type=f32)
                res_sc[a] = alphas[a] * res_sc[a] + jnp.dot(prs[a], v2, preferred_element_type=f32)

        def step(j, carry):
            block(j, False)
            return carry

        lax.fori_loop(0, i, step, 0)
        block(i, True)
        o_ref[...] = jnp.where(lo, acc_sc[0] / l_sc[0], acc_sc[1] / l_sc[1]).astype(bf16)
        o32_ref[...] = jnp.where(lo, (acc_sc[0] + res_sc[0]) / l_sc[0], (acc_sc[1] + res_sc[1]) / l_sc[1])
        lse0 = m_sc[0] + jnp.log(l_sc[0])
        lse1 = m_sc[1] + jnp.log(l_sc[1])
        st_ref[0] = jnp.where(lane == 0, lse0, jnp.where(lane == 1, lse1, 0.0))
        if nx:
            _exchange_wait(x_refs, xo_refs, sems, True, (pl.program_id(0) == NP - 1) & (i == n - 1))

    full = lambda blk: pl.BlockSpec((T, LANES), blk)
    x_in, x_out, x_shape, x_sems = _exchange_shapes(ride, True) if nx else ([], [], [], [])
    return pl.pallas_call(
        body, grid=(NP, n),
        in_specs=[pl.BlockSpec((tq, LANES), lambda p, i: (i, p)), full(lambda p, i: (0, p)), full(lambda p, i: (0, p)),
                  pl.BlockSpec((1, 2, T), lambda p, i: (p, 0, 0))] + x_in,
        out_specs=[pl.BlockSpec((tq, LANES), lambda p, i: (i, p)), pl.BlockSpec((tq, LANES), lambda p, i: (i, p)),
                   pl.BlockSpec((1, tq, LANES), lambda p, i: (p, i, 0))] + x_out,
        out_shape=[SDS((T, H * HD), bf16), SDS((T, H * HD), f32), SDS((NP, T, LANES), f32)] + x_shape,
        scratch_shapes=[pltpu.VMEM((2, tq, 1), f32), pltpu.VMEM((2, tq, 1), f32), pltpu.VMEM((2, tq, LANES), f32),
                        pltpu.VMEM((2, tq, LANES), f32)] + x_sems,
        compiler_params=_cparams(2), name=name)(q, k, v, cum_t, *ride)


def _attn_stats(do, o, lse, name):
    T = do.shape[0]
    tm = min(4 * TM, T)

    def body(do_ref, o_ref, lse_ref, st_ref):
        lane = lax.broadcasted_iota(jnp.int32, (1, LANES), 1)
        prod = do_ref[...].astype(f32) * o_ref[...].astype(f32)
        d0 = jnp.sum(jnp.where(lane < HD, prod, 0.0), axis=1, keepdims=True)
        d1 = jnp.sum(jnp.where(lane < HD, 0.0, prod), axis=1, keepdims=True)
        st_ref[0] = jnp.where(lane < 2, lse_ref[0], jnp.where(lane == 2, d0, jnp.where(lane == 3, d1, 0.0)))

    til = pl.BlockSpec((tm, LANES), lambda p, i: (i, p))
    stt = pl.BlockSpec((1, tm, LANES), lambda p, i: (p, i, 0))
    return pl.pallas_call(
        body, grid=(NP, T // tm), in_specs=[til, til, stt], out_specs=stt, out_shape=SDS((NP, T, LANES), f32),
        compiler_params=_cparams(2), name=name)(do, o, lse)


def _attn_bwd(q, k, v, do, st, cum_t, name, ride=()):
    T = q.shape[0]
    tq = min(TQ, T)
    n = T // tq
    nx = len(ride)

    def body(q_ref, k_ref, v_ref, do_ref, st_ref, ck_ref, *rest):
        x_refs, (dq_ref, dk_ref, dv_ref, dck_ref), xo_refs = rest[:nx], rest[nx:nx + 4], rest[nx + 4:2 * nx + 4]
        (dk_sc, dv_sc, dck_sc), sems = rest[2 * nx + 4:2 * nx + 7], rest[2 * nx + 7:]
        j = pl.program_id(1)
        if nx:
            _exchange_start(x_refs, xo_refs, sems, False, (pl.program_id(0) == 0) & (j == 0))
        lane = lax.broadcasted_iota(jnp.int32, (1, LANES), 1)
        lo = lane < HD

        @pl.when(j == 0)
        def _():
            dq_ref[...] = jnp.zeros_like(dq_ref)

        k2 = k_ref[...]
        v2 = v_ref[...]
        zero = jnp.zeros_like(k2)
        ka = (jnp.where(lo, k2, zero), jnp.where(lo, zero, k2))
        va = (jnp.where(lo, v2, zero), jnp.where(lo, zero, v2))
        dk_sc[...] = jnp.zeros_like(dk_sc)
        dv_sc[...] = jnp.zeros_like(dv_sc)
        dck_sc[...] = jnp.zeros_like(dck_sc)

        def block(i, masked):
            off = pl.multiple_of(i * tq, tq)
            q2 = q_ref[pl.ds(off, tq), :]
            do2 = do_ref[pl.ds(off, tq), :]
            stt = st_ref[0, pl.ds(off, tq), :]
            pbs, dsbs = [], []
            for a in range(2):
                s = _dot_nt(q2, ka[a]) - ck_ref[0, a:a + 1, :]
                if masked:
                    s = _causal(s, tq)
                pm = jnp.exp(s - stt[:, a:a + 1])
                dsm = pm * (_dot_nt(do2, va[a]) - stt[:, 2 + a:3 + a])
                dck_sc[a:a + 1, :] -= jnp.sum(dsm, axis=0, keepdims=True)
                pbs.append(pm.astype(bf16))
                dsbs.append(dsm.astype(bf16))
            parts = []
            for a in range(2):
                dv_sc[a] += _dot_tn(pbs[a], do2)
                dk_sc[a] += _dot_tn(dsbs[a], q2)
                parts.append(jnp.dot(dsbs[a], k2, preferred_element_type=f32))
            dq_ref[pl.ds(off, tq), :] += jnp.where(lo, parts[0], parts[1])

        block(j, True)

        def step(i, carry):
            block(i, False)
            return carry

        lax.fori_loop(j + 1, n, step, 0)
        dk_ref[...] = jnp.where(lo, dk_sc[0], dk_sc[1]).astype(bf16)
        dv_ref[...] = jnp.where(lo, dv_sc[0], dv_sc[1]).astype(bf16)
        dck_ref[0] = dck_sc[0:2, :]

        @pl.when(j == n - 1)
        def _():
            dq_ref[...] = dq_ref[...] * (HD ** -0.5)

        if nx:
            _exchange_wait(x_refs, xo_refs, sems, False, (pl.program_id(0) == NP - 1) & (j == n - 1))

    full = lambda: pl.BlockSpec((T, LANES), lambda p, j: (0, p))
    kvb = lambda: pl.BlockSpec((tq, LANES), lambda p, j: (j, p))
    ckb = lambda: pl.BlockSpec((1, 2, tq), lambda p, j: (p, 0, j))
    x_in, x_out, x_shape, x_sems = _exchange_shapes(ride, False) if nx else ([], [], [], [])
    return pl.pallas_call(
        body, grid=(NP, n),
        in_specs=[full(), kvb(), kvb(), full(), pl.BlockSpec((1, T, LANES), lambda p, j: (p, 0, 0)), ckb()] + x_in,
        out_specs=[full(), kvb(), kvb(), ckb()] + x_out,
        out_shape=[SDS((T, H * HD), f32), SDS((T, H * HD), bf16), SDS((T, H * HD), bf16), SDS((NP, 2, T), f32)] + x_shape,
        scratch_shapes=[pltpu.VMEM((2, tq, LANES), f32), pltpu.VMEM((2, tq, LANES), f32), pltpu.VMEM((8, tq), f32)] + x_sems,
        compiler_params=_cparams(2), name=name)(q, k, v, do, st, cum_t, *ride)


def _ada_fwd(c_all, w_cat, name):
    n = w_cat.shape[1]
    tn = 256

    def body(c_ref, w_ref, o_ref):
        cv = c_ref[...]
        o_ref[...] = jnp.dot((cv * jax.nn.sigmoid(cv)).astype(bf16), w_ref[...].astype(bf16), preferred_element_type=f32)

    return pl.pallas_call(
        body, grid=(n // tn,), in_specs=[pl.BlockSpec((NDEV, D), _fix), pl.BlockSpec((D, tn), lambda i: (0, i))],
        out_specs=pl.BlockSpec((NDEV, tn), lambda i: (0, i)), out_shape=SDS((NDEV, n), f32),
        compiler_params=_cparams(1), name=name)(c_all, w_cat)


def _ada_bwd(c_all_t, dsel, name):
    n = dsel.shape[1]
    tn = 256

    def body(c_ref, d_ref, o_ref):
        cv = c_ref[...]
        ca = cv * jax.nn.sigmoid(cv)
        acc = ca[:, 0:1] * d_ref[0:1, :]
        for b in range(1, NDEV):
            acc = acc + ca[:, b:b + 1] * d_ref[b:b + 1, :]
        o_ref[...] = acc

    return pl.pallas_call(
        body, grid=(n // tn,), in_specs=[pl.BlockSpec((D, NDEV), _fix), pl.BlockSpec((NDEV, tn), lambda i: (0, i))],
        out_specs=pl.BlockSpec((D, tn), lambda i: (0, i)), out_shape=SDS((D, n), f32),
        compiler_params=_cparams(1), name=name)(c_all_t, dsel)


def _sum_parts(parts, name):
    R = parts.shape[1]

    def body(p_ref, o_ref):
        acc = p_ref[0]
        for j in range(1, NDEV):
            acc = acc + p_ref[j]
        o_ref[...] = acc

    return pl.pallas_call(body, out_shape=SDS((R, LANES), f32), name=name)(parts)


def _adamw(g_parts, w, m, v, name):
    n_parts, R, C = g_parts.shape
    tr = next(t for t in (256, 128, 64, 32, 16, 8) if R % t == 0)
    c1 = 1.0 / (1.0 - ADAM_B1 ** ADAM_STEP)
    c2 = 1.0 / (1.0 - ADAM_B2 ** ADAM_STEP)

    def body(g_ref, w_ref, m_ref, v_ref, go_ref, d_ref, mo_ref, vo_ref):
        g = g_ref[0].astype(f32)
        for j in range(1, n_parts):
            g = g + g_ref[j].astype(f32)
        mn = ADAM_B1 * m_ref[...] + (1.0 - ADAM_B1) * g
        vn = ADAM_B2 * v_ref[...] + (1.0 - ADAM_B2) * (g * g)
        go_ref[...] = g
        mo_ref[...] = mn
        vo_ref[...] = vn
        d_ref[...] = -ADAM_LR * ((mn * c1) / (jnp.sqrt(vn * c2) + ADAM_EPS) + ADAM_WD * w_ref[...])

    til = pl.BlockSpec((tr, C), _row)
    out = SDS((R, C), f32)
    return pl.pallas_call(
        body, grid=(R // tr,), in_specs=[pl.BlockSpec((n_parts, tr, C), lambda i: (0, i, 0)), til, til, til],
        out_specs=[til, til, til, til], out_shape=[out, out, out, out],
        compiler_params=_cparams(1), name=name)(g_parts, w, m, v)


def _pad_rows(flat, cols, mult):
    n = flat.shape[-1]
    rows = -(-n // cols)
    rows = -(-rows // mult) * mult
    pad = [(0, 0)] * (flat.ndim - 1) + [(0, rows * cols - n)]
    return jnp.pad(flat, pad).reshape(flat.shape[:-1] + (rows, cols))


def _split_flat(flat, shapes):
    out, off = {}, 0
    for name, shp in shapes:
        n = 1
        for d in shp:
            n *= d
        out[name] = flat[off:off + n].reshape(shp)
        off += n
    return out


def kernel(x, c, mix_norm_g, mix_ada_w, mix_ada_b, ffn_norm_g, ffn_ada_w, ffn_ada_b, ffn_w_in, ffn_w_out, conv_w_in, conv_b_in, conv_w_dw, conv_b_dw, conv_ln_g, conv_ln_b, conv_w_out, conv_b_out, kv_norm_g, kv_ada_w, kv_ada_b, kv_w, forget_b, attn_w_q, attn_w_o, final_norm_g, loss_target, m_mix_norm_g, m_mix_ada_w, m_mix_ada_b, m_ffn_norm_g, m_ffn_ada_w, m_ffn_ada_b, m_ffn_w_in, m_ffn_w_out, m_conv_w_in, m_conv_b_in, m_conv_w_dw, m_conv_b_dw, m_conv_ln_g, m_conv_ln_b, m_conv_w_out, m_conv_b_out, m_kv_norm_g, m_kv_ada_w, m_kv_ada_b, m_kv_w, m_forget_b, m_attn_w_q, m_attn_w_o, m_final_norm_g, v_mix_norm_g, v_mix_ada_w, v_mix_ada_b, v_ffn_norm_g, v_ffn_ada_w, v_ffn_ada_b, v_ffn_w_in, v_ffn_w_out, v_conv_w_in, v_conv_b_in, v_conv_w_dw, v_conv_b_dw, v_conv_ln_g, v_conv_ln_b, v_conv_w_out, v_conv_b_out, v_kv_norm_g, v_kv_ada_w, v_kv_ada_b, v_kv_w, v_forget_b, v_attn_w_q, v_attn_w_o, v_final_norm_g):
    W = dict(mix_norm_g=mix_norm_g, mix_ada_w=mix_ada_w, mix_ada_b=mix_ada_b, ffn_norm_g=ffn_norm_g, ffn_ada_w=ffn_ada_w, ffn_ada_b=ffn_ada_b, ffn_w_in=ffn_w_in, ffn_w_out=ffn_w_out, conv_w_in=conv_w_in, conv_b_in=conv_b_in, conv_w_dw=conv_w_dw, conv_b_dw=conv_b_dw, conv_ln_g=conv_ln_g, conv_ln_b=conv_ln_b, conv_w_out=conv_w_out, conv_b_out=conv_b_out, kv_norm_g=kv_norm_g, kv_ada_w=kv_ada_w, kv_ada_b=kv_ada_b, kv_w=kv_w, forget_b=forget_b, attn_w_q=attn_w_q, attn_w_o=attn_w_o, final_norm_g=final_norm_g)
    M = dict(mix_norm_g=m_mix_norm_g, mix_ada_w=m_mix_ada_w, mix_ada_b=m_mix_ada_b, ffn_norm_g=m_ffn_norm_g, ffn_ada_w=m_ffn_ada_w, ffn_ada_b=m_ffn_ada_b, ffn_w_in=m_ffn_w_in, ffn_w_out=m_ffn_w_out, conv_w_in=m_conv_w_in, conv_b_in=m_conv_b_in, conv_w_dw=m_conv_w_dw, conv_b_dw=m_conv_b_dw, conv_ln_g=m_conv_ln_g, conv_ln_b=m_conv_ln_b, conv_w_out=m_conv_w_out, conv_b_out=m_conv_b_out, kv_norm_g=m_kv_norm_g, kv_ada_w=m_kv_ada_w, kv_ada_b=m_kv_ada_b, kv_w=m_kv_w, forget_b=m_forget_b, attn_w_q=m_attn_w_q, attn_w_o=m_attn_w_o, final_norm_g=m_final_norm_g)
    V = dict(mix_norm_g=v_mix_norm_g, mix_ada_w=v_mix_ada_w, mix_ada_b=v_mix_ada_b, ffn_norm_g=v_ffn_norm_g, ffn_ada_w=v_ffn_ada_w, ffn_ada_b=v_ffn_ada_b, ffn_w_in=v_ffn_w_in, ffn_w_out=v_ffn_w_out, conv_w_in=v_conv_w_in, conv_b_in=v_conv_b_in, conv_w_dw=v_conv_w_dw, conv_b_dw=v_conv_b_dw, conv_ln_g=v_conv_ln_g, conv_ln_b=v_conv_ln_b, conv_w_out=v_conv_w_out, conv_b_out=v_conv_b_out, kv_norm_g=v_kv_norm_g, kv_ada_w=v_kv_ada_w, kv_ada_b=v_kv_ada_b, kv_w=v_kv_w, forget_b=v_forget_b, attn_w_q=v_attn_w_q, attn_w_o=v_attn_w_o, final_norm_g=v_final_norm_g)
    names = list(W)
    T = x.shape[1]
    me = _my_index()
    x0 = x[0]
    tgt = loss_target[0]
    row = lambda vct: vct.reshape(1, -1)

    small_names = ("conv_b_in", "conv_w_dw", "conv_b_dw", "conv_ln_g", "conv_ln_b", "conv_b_out")
    small_loc = jnp.concatenate([c.reshape(-1)] + [W[n].reshape(-1) for n in small_names])
    sg = _exchange([_pad_rows(small_loc, LANES, 8)], True, "gather_small")[0].reshape(NDEV, -1)
    c_all = sg[:, :D]
    off = D
    b_in = sg[:, off:off + 2 * D // NDEV].reshape(1, 2 * D); off += 2 * D // NDEV
    cl = D // NDEV
    w_dw = sg[:, off:off + KW * cl].reshape(NDEV, KW, cl).transpose(1, 0, 2).reshape(KW, D); off += KW * cl
    w_dw = jnp.pad(w_dw, ((0, HALO - KW), (0, 0))).astype(bf16).astype(f32)
    b_dw = sg[:, off:off + cl].reshape(1, D); off += cl
    ln_g = sg[:, off:off + cl].reshape(1, D); off += cl
    ln_b = sg[:, off:off + cl].reshape(1, D); off += cl
    b_out = sg[:, off:off + cl].reshape(1, D)

    cat_ada = lambda s: jnp.concatenate([s["mix_ada_w"][0], s["mix_ada_w"][1], s["ffn_ada_w"][0], s["ffn_ada_w"][1], s["kv_ada_w"]], axis=1)
    w_cat = cat_ada(W)
    ada_loc = _ada_fwd(c_all, w_cat, "ada_fwd")
    ada_all = _exchange([ada_loc], True, "gather_ada")[0]
    ada_me = lax.dynamic_index_in_dim(ada_all, me, axis=1, keepdims=False)
    ada_bias = (mix_ada_b[0], mix_ada_b[1], ffn_ada_b[0], ffn_ada_b[1], kv_ada_b)
    ada, off = [], 0
    for nl, bias in zip(ADA_LOC, ada_bias):
        full = ada_me[:, off:off + nl].reshape(-1) + bias
        ada.append([row(t) for t in jnp.split(full, full.shape[0] // D)])
        off += nl
    (sh_m0, sc_m0, gt_m0), (sh_m1, sc_m1, gt_m1), (sh_f0, sc_f0, gt_f0), (sh_f1, sc_f1, gt_f1), (sh_kv, sc_kv) = ada

    pad_in = lambda src, l: jnp.pad(src["ffn_w_in"][l], ((0, 0), (0, FSP - FS)))
    rows_a = lambda src: jnp.concatenate([src["ffn_w_out"][0], src["attn_w_q"][0]])
    rows_b = lambda src: jnp.concatenate([src["ffn_w_out"][1], src["attn_w_o"][0]])
    as_bf = lambda arrs: [t.astype(bf16) for t in arrs]
    fo, sq_rows = F // NDEV, D // NDEV
    g_ci, g_co = _exchange(as_bf([conv_w_in[0], conv_w_out[0]]), True, "gather_weights")
    soon = as_bf([pad_in(W, 0)])
    next_ = as_bf([rows_a(W), kv_w])
    late = as_bf([pad_in(W, 1), rows_b(W)])

    def w_out_of(g_r):
        t = g_r[:, :fo].reshape(NDEV // 2, FS, D)
        return jnp.pad(t, ((0, 0), (0, FSP - FS), (0, 0))).reshape(FP, D)

    conv_in_full = g_ci.transpose(1, 0, 2).reshape(D, 2 * D)
    wc_a, wc_g = conv_in_full[:, :D], conv_in_full[:, D:]
    wc_o = g_co.reshape(D, D)
    zeros_d = jnp.zeros((1, D), f32)
    fb = jnp.pad(forget_b, (0, LANES - H)).reshape(1, LANES)

    h0 = _normmod(x0, row(mix_norm_g[0]), sh_m0, sc_m0, "norm_mix0")
    a0, gl0, p0 = _mm_gated(h0, wc_a, wc_g, b_in[:, :D], b_in[:, D:], False, "conv_in")
    q0, s0, w_sh0 = _conv_fwd(p0, w_dw, b_dw, ln_g, ln_b, "conv_dw", ride=soon)
    x1, y0, h_f0 = _mm_res(s0, wc_o, b_out, x0, gt_m0, "conv_out", norms=[(row(ffn_norm_g[0]), sh_f0, sc_f0)])

    def ffn_fwd(xin, h, l, gt, w_sh, w_out=None, ride=(), norms=()):
        ug, uu, act, *got = _ffn_in(h, w_sh, f"ffn_in{l}", ride=ride)
        if w_out is None:
            w_out = w_out_of(got[0])
        xo, y, *hs = _mm_res(act, w_out, zeros_d, xin, gt, f"ffn_out{l}", norms=norms)
        return xo, (h, ug, uu, act, y, w_sh, w_out), got, hs

    x2, ffn0, (g_ra, g_kv), (hk, h2) = ffn_fwd(
        x1, h_f0, 0, gt_f0, w_sh0, ride=next_,
        norms=[(row(kv_norm_g), sh_kv, sc_kv), (row(mix_norm_g[1]), sh_m1, sc_m1)])
    w_q = g_ra[:, fo:].reshape(D, D)
    kv_full = g_kv.transpose(1, 0, 2).reshape(D, -1)
    w_k, w_v = kv_full[:, :D], kv_full[:, D:2 * D]
    w_f = jnp.pad(kv_full[:, 2 * D:], ((0, 0), (0, LANES - H)))

    k_sh = _mm(hk, w_k, bf16, 1.0, "proj_k")
    v_sh = _mm(hk, w_v, bf16, 1.0, "proj_v")
    fl = _mm(hk, w_f, f32, 1.0, "proj_f")
    cum = _forget_fwd(fl, fb, "forget_fwd")
    cum_t = cum[:, :H].T.reshape(NP, 2, T)

    qh = _mm(h2, w_q, bf16, HD ** -0.5, "proj_q")
    o, o32, lse, w_sh1, g_rb = _attn_fwd(qh, k_sh, v_sh, cum_t, "attn_fwd", ride=late)
    w_out1, w_o = w_out_of(g_rb), g_rb[:, fo:].reshape(D, D)
    x3, y1, h_f1 = _mm_res(o, w_o, zeros_d, x2, gt_m1, "attn_out", norms=[(row(ffn_norm_g[1]), sh_f1, sc_f1)])

    x4, ffn1, _, _ = ffn_fwd(x3, h_f1, 1, gt_f1, w_sh1, w_out1)

    dx4, acc_fin = _final_bwd(x4, row(final_norm_g), tgt, "final_bwd")

    d_ada = {}
    by_rows = lambda g: g.reshape(NDEV, sq_rows, D)

    def ffn_bwd(dx_out, xin, l, sc, gt, saved):
        h, ug, uu, act, y, w_sh, w_out = saved
        dyb, acc_r = _res_in(dx_out, y, gt, f"ffn_res_bwd{l}")
        dug, duu = _mm_nt_swiglu(dyb, w_out, ug, uu, f"ffn_dact{l}")
        g_out = _mm_tn(act, dyb, f"ffn_dw_out{l}").reshape(NDEV // 2, FSP, D)[:, :FS].reshape(NDEV, fo, D)
        g_in = _mm_tn_shards(h, duu, FSP, f"ffn_dw_up{l}", into=_mm_tn_shards(h, dug, FSP, f"ffn_dw_gate{l}"))
        dh = _ffn_dh(dug, duu, w_sh, f"ffn_dh{l}")
        dxi, acc_n = _normmod_bwd(dh, xin, row(ffn_norm_g[l]), sc, dx_out, f"norm_ffn_bwd{l}")
        return dxi, g_in, g_out, [acc_n[0:1], acc_n[1:2], acc_r[0:1]], acc_n[2]

    dx3, g_in1, g_out1, d_ada[("ffn", 1)], dg_ffn1 = ffn_bwd(dx4, x3, 1, sc_f1, gt_f1, ffn1)

    dyb, acc_r = _res_in(dx3, y1, gt_m1, "attn_res_bwd")
    do = _mm_nt([(dyb, w_o)], bf16, "attn_do")
    g_wo = _mm_tn(o, dyb, "attn_dw_o")
    st = _attn_stats(do, o32, lse, "attn_stats")
    leave_b = as_bf([g_in1, jnp.concatenate([g_out1, by_rows(g_wo)], axis=1)])
    dq, dk, dv, dck, r_in1, r_rb = _attn_bwd(qh, k_sh, v_sh, do, st, cum_t, "attn_bwd", ride=leave_b)
    g_wq = _mm_tn(h2, dq, "attn_dw_q")
    dh2 = _mm_nt([(dq, w_q)], f32, "attn_dh")
    dx2, acc_n = _normmod_bwd(dh2, x2, row(mix_norm_g[1]), sc_m1, dx3, "norm_mix_bwd1")
    d_ada[("mix", 1)] = [acc_n[0:1], acc_n[1:2], acc_r[0:1]]
    dg_mix1 = acc_n[2]

    dcum = jnp.pad(dck.reshape(H, T).T, ((0, 0), (0, LANES - H)))
    dfl, acc_f = _forget_bwd(dcum, fl, fb, "forget_bwd")
    g_kvw = jnp.concatenate([_mm_tn(hk, dk, "kv_dw_k"), _mm_tn(hk, dv, "kv_dw_v"), _mm_tn(hk, dfl, "kv_dw_f")[:, :H]], axis=1)
    dhk = _mm_nt([(dk, w_k), (dv, w_v), (dfl, w_f)], f32, "kv_dh")
    dx2, acc_n = _normmod_bwd(dhk, x2, row(kv_norm_g), sc_kv, dx2, "norm_kv_bwd")
    d_ada[("kv", 0)] = [acc_n[0:1], acc_n[1:2]]
    dg_kv = acc_n[2]

    dx1, g_in0, g_out0, d_ada[("ffn", 0)], dg_ffn0 = ffn_bwd(dx2, x1, 0, sc_f0, gt_f0, ffn0)

    dyb, acc_r = _res_in(dx1, y0, gt_m0, "conv_res_bwd")
    dsw = _mm_nt([(dyb, wc_o)], f32, "conv_ds")
    g_co_out = _mm_tn(s0, dyb, "conv_dw_out")
    leave_a = as_bf([g_in0, jnp.concatenate([g_out0, by_rows(g_wq)], axis=1), g_kvw.reshape(D, NDEV, -1).transpose(1, 0, 2)])
    da, dgl, acc_c, dw_dw, r_in0, r_ra, r_kv = _conv_bwd(dsw, q0, p0, a0, gl0, w_dw, ln_g, ln_b, "conv_bwd", ride=leave_a)
    cs = 2 * D // NDEV
    g_ci_out = _mm_tn_shards(h0, dgl, cs, "conv_dw_g", into=_mm_tn_shards(h0, da, cs, "conv_dw_a"))
    dh0, r_ci, r_co = _mm_nt([(da, wc_a), (dgl, wc_g)], f32, "conv_dh", ride=as_bf([g_ci_out, by_rows(g_co_out)]))
    dx0, acc_n = _normmod_bwd(dh0, x0, row(mix_norm_g[0]), sc_m0, dx1, "norm_mix_bwd0")
    d_ada[("mix", 0)] = [acc_n[0:1], acc_n[1:2], acc_r[0:1]]
    dg_mix0 = acc_n[2]

    vec = [t.reshape(-1) for key in [(s[0], s[1]) for s in ADA_SEG] for t in d_ada[key]]
    vec += [dg_mix0, dg_mix1, dg_ffn0, dg_ffn1, dg_kv, acc_fin[0]]
    vec += [acc_f[0], acc_fin[1, :LANES]]
    vec += [acc_c[3], acc_c[4], dw_dw[:KW].reshape(-1), acc_c[2], acc_c[0], acc_c[1], acc_r[1]]
    small_parts = _exchange([_pad_rows(jnp.concatenate(vec), LANES, 8)], True, "gather_partials")[0]
    small_sum = _sum_parts(small_parts, "sum_partials").reshape(-1)
    d_ada_all = small_parts.reshape(NDEV, -1)[:, :ADA_TOT]
    off = 0
    gsm = {}
    ada_b_sum = []
    for _, _, n in ADA_SEG:
        ada_b_sum.append(small_sum[off:off + n]); off += n
    gsm["mix_ada_b"] = jnp.stack(ada_b_sum[0:2])
    gsm["ffn_ada_b"] = jnp.stack(ada_b_sum[2:4])
    gsm["kv_ada_b"] = ada_b_sum[4]
    gsm["mix_norm_g"] = small_sum[off:off + 2 * D].reshape(2, D); off += 2 * D
    gsm["ffn_norm_g"] = small_sum[off:off + 2 * D].reshape(2, D); off += 2 * D
    gsm["kv_norm_g"] = small_sum[off:off + D]; off += D
    gsm["final_norm_g"] = small_sum[off:off + D]; off += D
    gsm["forget_b"] = small_sum[off:off + H]; off += LANES
    loss = small_sum[off]; off += LANES
    sl = lambda full, width: lax.dynamic_slice_in_dim(full, me * width, width, axis=full.ndim - 1)
    gsm["conv_b_in"] = sl(small_sum[off:off + 2 * D].reshape(1, 2 * D), 2 * D // NDEV); off += 2 * D
    gsm["conv_w_dw"] = sl(small_sum[off:off + KW * D].reshape(1, KW, D), cl); off += KW * D
    for n in ("conv_b_dw", "conv_ln_g", "conv_ln_b", "conv_b_out"):
        gsm[n] = sl(small_sum[off:off + D].reshape(1, D), cl); off += D

    dsel, off = [], 0
    for (_, _, n), nl in zip(ADA_SEG, ADA_LOC):
        dsel.append(lax.dynamic_slice_in_dim(d_ada_all[:, off:off + n], me * nl, nl, axis=1)); off += n
    g_ada = _ada_bwd(c_all.T, jnp.concatenate(dsel, axis=1), "ada_bwd")
    res_ada = _adamw(g_ada[None], w_cat, cat_ada(M), cat_ada(V), "adamw_ada")

    res_in0 = _adamw(r_in0, *[pad_in(s, 0) for s in (W, M, V)], "adamw_ffn_in0")
    res_in1 = _adamw(r_in1, *[pad_in(s, 1) for s in (W, M, V)], "adamw_ffn_in1")
    res_ra = _adamw(r_ra, *[rows_a(s) for s in (W, M, V)], "adamw_rows_a")
    res_rb = _adamw(r_rb, *[rows_b(s) for s in (W, M, V)], "adamw_rows_b")
    res_ci = _adamw(r_ci, *[s["conv_w_in"][0] for s in (W, M, V)], "adamw_conv_in")
    res_co = _adamw(r_co, *[s["conv_w_out"][0] for s in (W, M, V)], "adamw_conv_out")
    res_kv = _adamw(r_kv, *[s["kv_w"] for s in (W, M, V)], "adamw_kv")
    rest = [n for n in names if n not in MAIN and n not in ("mix_ada_w", "ffn_ada_w", "kv_ada_w")]
    pack_rest = lambda src: _pad_rows(jnp.concatenate([src[n].reshape(-1) for n in rest]), D, 256)
    res_rest = _adamw(pack_rest(gsm)[None], pack_rest(W), pack_rest(M), pack_rest(V), "adamw_rest")

    outs = []
    for k in range(4):
        ur = _split_flat(res_rest[k].reshape(-1), [(n, W[n].shape) for n in rest])
        ra, a0_, a2_ = res_ada[k], ADA_LOC[0], ADA_LOC[2]
        ur["mix_ada_w"] = jnp.stack([ra[:, 0:a0_], ra[:, a0_:2 * a0_]])
        ur["ffn_ada_w"] = jnp.stack([ra[:, 2 * a0_:2 * a0_ + a2_], ra[:, 2 * a0_ + a2_:2 * a0_ + 2 * a2_]])
        ur["kv_ada_w"] = ra[:, 2 * a0_ + 2 * a2_:]
        ur["ffn_w_in"] = jnp.stack([res_in0[k][:, :FS], res_in1[k][:, :FS]])
        ur["conv_w_in"] = res_ci[k][None]
        ur["conv_w_out"] = res_co[k][None]
        ur["kv_w"] = res_kv[k]
        ur["ffn_w_out"] = jnp.stack([res_ra[k][:fo], res_rb[k][:fo]])
        ur["attn_w_q"] = res_ra[k][fo:][None]
        ur["attn_w_o"] = res_rb[k][fo:][None]
        outs.append(ur)
    grads, deltas, new_m, new_v = outs
    return (loss, dx0[None], *[grads[n] for n in names], *[deltas[n] for n in names],
            *[new_m[n] for n in names], *[new_v[n] for n in names])
```
